```python
import math
import jax, jax.numpy as jnp
from jax import lax
import numpy as np

D_MODEL = 2048
BATCH = 1
SEQ = 8192
DEPTH = 1
DEC_BATCH = 8
DEC_SEQ = 32
PAST_LEN = 2048

CHUNK = 64
H_A = 8
DK_A = 64
DV_A = 2 * DK_A
QK_A = H_A * 2 * DK_A
W_A = H_A * DV_A
H_B = 8
DH_B = 128
W_B = H_B * DH_B
BAND_CHUNKS = 8
BAND_PAST = BAND_CHUNKS * CHUNK
REL_CLIP = 128
T5_BUCKETS = 32
T5_MAX_DIST = 128
Q_BLOCK = 128
EPS = 1e-6

kernel_name = "diff_chunkband_hybrid_stream_step"


def _split_points():
    sizes = (QK_A, QK_A, W_A, W_A, W_B, W_B, W_B, W_B, D_MODEL, D_MODEL)
    pts, acc = [], 0
    for s in sizes[:-1]:
        acc += s
        pts.append(acc)
    return pts


def _in_width():
    return 2 * QK_A + 2 * W_A + 4 * W_B + 2 * D_MODEL


def _rms_norm(x, g):
    xf = x.astype(jnp.float32)
    y = xf * lax.rsqrt(jnp.mean(xf * xf, axis=-1, keepdims=True) + EPS)
    return (y * g.astype(jnp.float32)).astype(x.dtype)


def _t5_bucket(rel):
    half = T5_BUCKETS // 2
    max_exact = half // 2
    ret = jnp.where(rel > 0, half, 0)
    n = jnp.abs(rel)
    nf = jnp.maximum(n, 1).astype(jnp.float32)
    large = max_exact + (jnp.log(nf / max_exact) / math.log(T5_MAX_DIST / max_exact)
                         * (half - max_exact)).astype(jnp.int32)
    large = jnp.minimum(large, half - 1)
    return ret + jnp.where(n < max_exact, n, large)


def _in_proj(h, w_in):
    B, S = h.shape[0], h.shape[1]
    y = jnp.einsum('bsd,de->bse', h, w_in)
    q_a, k_a, v_a, z_a, q_b, k_b, v_b, z_b, g_a, g_b = jnp.split(y, _split_points(), axis=-1)
    return (q_a.reshape(B, S, H_A, 2, DK_A), k_a.reshape(B, S, H_A, 2, DK_A),
            v_a.reshape(B, S, H_A, DV_A), z_a,
            q_b.reshape(B, S, H_B, DH_B), k_b.reshape(B, S, H_B, DH_B),
            v_b.reshape(B, S, H_B, DH_B), z_b, g_a, g_b)


def _diff_core(q, k, v, qpos, kpos, t5_bias, lam, subln_g, lam_init):
    s = jnp.einsum('bqhmd,bkhmd->bhmqk', q, k, preferred_element_type=jnp.float32) * (DK_A ** -0.5)
    bias = jnp.transpose(t5_bias[_t5_bucket(kpos[None, :] - qpos[:, None])], (2, 0, 1))
    mask = (kpos[None, :] // CHUNK) <= (qpos[:, None] // CHUNK)
    s = jnp.where(mask, s + bias[None, :, None].astype(jnp.float32), -jnp.inf)
    p = jax.nn.softmax(s, axis=-1)
    attn = p[:, :, 0] - lam * p[:, :, 1]
    o = jnp.einsum('bhqk,bkhd->bqhd', attn.astype(v.dtype), v, preferred_element_type=jnp.float32)
    o = _rms_norm(o, subln_g) * (1.0 - lam_init)
    return o.astype(q.dtype)


def _band_core(q, k, v, qpos, kpos, rel_bias):
    s = jnp.einsum('bqhd,bkhd->bhqk', q, k, preferred_element_type=jnp.float32) * (DH_B ** -0.5)
    rel = jnp.clip(kpos[None, :] - qpos[:, None], -REL_CLIP, REL_CLIP) + REL_CLIP
    bias = rel_bias[:, rel].astype(jnp.float32)
    qc = qpos // CHUNK
    kc = kpos // CHUNK
    mask = ((kpos[None, :] >= 0) & (kc[None, :] <= qc[:, None])
            & (kc[None, :] >= qc[:, None] - BAND_CHUNKS))
    p = jax.nn.softmax(jnp.where(mask, s + bias[None], -jnp.inf), axis=-1)
    o = jnp.einsum('bhqk,bkhd->bqhd', p.astype(v.dtype), v, preferred_element_type=jnp.float32)
    return o.astype(q.dtype)


def _merge(o_a, z_a, o_b, z_b, g_a, g_b, w_o_a, w_o_b, w_out):
    B, S = o_a.shape[0], o_a.shape[1]
    y_a = jnp.einsum('bse,ed->bsd', o_a.reshape(B, S, W_A) * jax.nn.silu(z_a), w_o_a)
    y_b = jnp.einsum('bse,ed->bsd', o_b.reshape(B, S, W_B) * jax.nn.silu(z_b), w_o_b)
    m = jax.nn.sigmoid(g_a) * y_a + jax.nn.sigmoid(g_b) * y_b
    return jnp.einsum('bsd,de->bse', m, w_out)


def _lambda(lq1, lk1, lq2, lk2, lam_init):
    f = jnp.float32
    return (jnp.exp(jnp.sum(lq1.astype(f) * lk1.astype(f)))
            - jnp.exp(jnp.sum(lq2.astype(f) * lk2.astype(f))) + lam_init)


def _layer_prompt(x, pre_g, post_g, w_in, t5_bias, lam, lam_init, subln_g, rel_bias, w_o_a, w_o_b, w_out):
    B, S = x.shape[0], x.shape[1]
    h = _rms_norm(x, pre_g)
    q_a, k_a, v_a, z_a, q_b, k_b, v_b, z_b, g_a, g_b = _in_proj(h, w_in)
    nb = S // Q_BLOCK
    qa_blocks = jnp.moveaxis(q_a.reshape(B, nb, Q_BLOCK, H_A, 2, DK_A), 1, 0)
    kpos = jnp.arange(S, dtype=jnp.int32)

    def a_block(args):
        qi, i = args
        qpos = i * Q_BLOCK + jnp.arange(Q_BLOCK, dtype=jnp.int32)
        return _diff_core(qi, k_a, v_a, qpos, kpos, t5_bias, lam, subln_g, lam_init)

    o_a = lax.map(a_block, (qa_blocks, jnp.arange(nb, dtype=jnp.int32)))
    o_a = jnp.moveaxis(o_a, 0, 1).reshape(B, S, H_A, DV_A)
    nc = S // CHUNK
    band = BAND_PAST + CHUNK
    kp = jnp.pad(k_b, ((0, 0), (BAND_PAST, 0), (0, 0), (0, 0)))
    vp = jnp.pad(v_b, ((0, 0), (BAND_PAST, 0), (0, 0), (0, 0)))
    qb_chunks = jnp.moveaxis(q_b.reshape(B, nc, CHUNK, H_B, DH_B), 1, 0)

    def b_chunk(args):
        qi, n = args
        start = n * CHUNK
        kb = lax.dynamic_slice_in_dim(kp, start, band, axis=1)
        vb = lax.dynamic_slice_in_dim(vp, start, band, axis=1)
        qpos = start + jnp.arange(CHUNK, dtype=jnp.int32)
        kpos_b = start - BAND_PAST + jnp.arange(band, dtype=jnp.int32)
        return _band_core(qi, kb, vb, qpos, kpos_b, rel_bias)

    o_b = lax.map(b_chunk, (qb_chunks, jnp.arange(nc, dtype=jnp.int32)))
    o_b = jnp.moveaxis(o_b, 0, 1).reshape(B, S, H_B, DH_B)
    y = x + _rms_norm(_merge(o_a, z_a, o_b, z_b, g_a, g_b, w_o_a, w_o_b, w_out), post_g)
    tail = min(BAND_PAST, S)
    return y, k_a.reshape(B, S, 2 * H_A, DK_A), v_a, k_b[:, S - tail:], v_b[:, S - tail:]


def _layer_sample(x, ck_a, cv_a, ck_b, cv_b, pre_g, post_g, w_in, t5_bias, lam, lam_init, subln_g,
                  rel_bias, w_o_a, w_o_b, w_out):
    B, S = x.shape[0], x.shape[1]
    past = ck_a.shape[1]
    win = ck_b.shape[1]
    h = _rms_norm(x, pre_g)
    q_a, k_a, v_a, z_a, q_b, k_b, v_b, z_b, g_a, g_b = _in_proj(h, w_in)
    qpos = past + jnp.arange(S, dtype=jnp.int32)
    k_full = jnp.concatenate([ck_a.reshape(B, past, H_A, 2, DK_A), k_a], axis=1)
    v_full = jnp.concatenate([cv_a, v_a], axis=1)
    kpos = jnp.arange(past + S, dtype=jnp.int32)
    o_a = _diff_core(q_a, k_full, v_full, qpos, kpos, t5_bias, lam, subln_g, lam_init)
    kb = jnp.concatenate([ck_b, k_b], axis=1)
    vb = jnp.concatenate([cv_b, v_b], axis=1)
    kpos_b = past - win + jnp.arange(win + S, dtype=jnp.int32)
    o_b = _band_core(q_b, kb, vb, qpos, kpos_b, rel_bias)
    y = x + _rms_norm(_merge(o_a, z_a, o_b, z_b, g_a, g_b, w_o_a, w_o_b, w_out), post_g)
    return y, k_a.reshape(B, S, 2 * H_A, DK_A), v_a, kb[:, S:], vb[:, S:]


def setup_inputs(seed: int = 0) -> dict:
    key = jax.random.key(seed)
    ks = jax.random.split(key, 20)
    f = jnp.float32
    wb = min(BAND_PAST, PAST_LEN)
    nrm = lambda k, shp, sc: jax.random.normal(k, shp, f) * sc
    return {
        "x_prompt": nrm(ks[0], (BATCH, SEQ, D_MODEL), 1.0),
        "x_sample": nrm(ks[1], (DEC_BATCH, DEC_SEQ, D_MODEL), 1.0),
        "cache_k_a": nrm(ks[2], (DEPTH, DEC_BATCH, PAST_LEN, 2 * H_A, DK_A), 1.0),
        "cache_v_a": nrm(ks[3], (DEPTH, DEC_BATCH, PAST_LEN, H_A, DV_A), 1.0),
        "cache_k_b": nrm(ks[4], (DEPTH, DEC_BATCH, wb, H_B, DH_B), 1.0),
        "cache_v_b": nrm(ks[5], (DEPTH, DEC_BATCH, wb, H_B, DH_B), 1.0),
        "t5_bias": nrm(ks[6], (T5_BUCKETS, H_A), 0.5),
        "pre_norm": 1.0 + nrm(ks[7], (DEPTH, D_MODEL), 0.02),
        "post_norm": 1.0 + nrm(ks[8], (DEPTH, D_MODEL), 0.02),
        "w_in": nrm(ks[9], (DEPTH, D_MODEL, _in_width()), D_MODEL ** -0.5),
        "lambda_q1": nrm(ks[10], (DEPTH, DK_A), 0.1),
        "lambda_k1": nrm(ks[11], (DEPTH, DK_A), 0.1),
        "lambda_q2": nrm(ks[12], (DEPTH, DK_A), 0.1),
        "lambda_k2": nrm(ks[13], (DEPTH, DK_A), 0.1),
        "subln_a": 1.0 + nrm(ks[14], (DEPTH, DV_A), 0.02),
        "rel_bias_b": nrm(ks[15], (DEPTH, H_B, 2 * REL_CLIP + 1), 0.5),
        "w_o_a": nrm(ks[16], (DEPTH, W_A, D_MODEL), W_A ** -0.5),
        "w_o_b": nrm(ks[17], (DEPTH, W_B, D_MODEL), W_B ** -0.5),
        "w_out": nrm(ks[18], (DEPTH, D_MODEL, D_MODEL), D_MODEL ** -0.5),
    }


def reference(x_prompt, x_sample, cache_k_a, cache_v_a, cache_k_b, cache_v_b, t5_bias, pre_norm,
              post_norm, w_in, lambda_q1, lambda_k1, lambda_q2, lambda_k2, subln_a, rel_bias_b,
              w_o_a, w_o_b, w_out):
    yp, ys = x_prompt, x_sample
    kap, vap, kbp, vbp, kas, vas, kbs, vbs = [], [], [], [], [], [], [], []
    for l in range(DEPTH):
        lam_init = 0.8 - 0.6 * math.exp(-0.3 * l)
        lam = _lambda(lambda_q1[l], lambda_k1[l], lambda_q2[l], lambda_k2[l], lam_init)
        yp, ka, va, kb, vb = _layer_prompt(yp, pre_norm[l], post_norm[l], w_in[l], t5_bias, lam, lam_init,
                                           subln_a[l], rel_bias_b[l], w_o_a[l], w_o_b[l], w_out[l])
        kap.append(ka); vap.append(va); kbp.append(kb); vbp.append(vb)
        ys, ka, va, kb, vb = _layer_sample(ys, cache_k_a[l], cache_v_a[l], cache_k_b[l], cache_v_b[l],
                                           pre_norm[l], post_norm[l], w_in[l], t5_bias, lam, lam_init,
                                           subln_a[l], rel_bias_b[l], w_o_a[l], w_o_b[l], w_out[l])
        kas.append(ka); vas.append(va); kbs.append(kb); vbs.append(vb)
    return (yp, ys, jnp.stack(kap), jnp.stack(vap), jnp.stack(kbp), jnp.stack(vbp),
            jnp.stack(kas), jnp.stack(vas), jnp.stack(kbs), jnp.stack(vbs))
```

```python
import functools
import math

import numpy as np
import jax
import jax.numpy as jnp
from jax import lax
from jax.experimental import pallas as pl
from jax.experimental.pallas import tpu as pltpu

F32 = jnp.float32
BF16 = jnp.bfloat16

CHUNK = 64
H_A = 8
DK_A = 64
DV_A = 2 * DK_A
H_B = 8
DH_B = 128
BAND_CHUNKS = 8
BAND_PAST = BAND_CHUNKS * CHUNK
REL_CLIP = 128
T5_BUCKETS = 32
T5_MAX_DIST = 128
EPS = 1e-6

HEAD_W = 128
COL_BLOCK = 1024
ATT_TILE = 512
NEAR = max(REL_CLIP, T5_MAX_DIST)
VMEM_LIMIT = 60 * 1024 * 1024

QA, KA, VA, ZA, QB, KB, VB, ZB, GA0, GA1, GB0, GB1 = range(12)


def _t5_bucket_int(rel):
    half = T5_BUCKETS // 2
    max_exact = half // 2
    n = abs(rel)
    ret = half if rel > 0 else 0
    if n < max_exact:
        return ret + n
    assert (T5_MAX_DIST // max_exact) ** 2 == 2 ** (half - max_exact)
    j = 0
    while n * n >= (max_exact * max_exact) * 2 ** (j + 1):
        j += 1
    return ret + min(max_exact + j, half - 1)


def _t5_runs(lo, hi):
    runs = []
    for r in range(lo, hi + 1):
        b = _t5_bucket_int(r)
        if not runs or runs[-1][1] != b:
            runs.append((r, b))
    return runs


T5_FAR_BUCKET = _t5_bucket_int(-T5_MAX_DIST)
assert all(_t5_bucket_int(-n) == T5_FAR_BUCKET for n in range(T5_MAX_DIST, 4 * T5_MAX_DIST))


def _t5_bias_tile(rel, lo, hi, t5_ref, h):
    runs = _t5_runs(lo, hi)
    val = jnp.full(rel.shape, t5_ref[runs[0][1], h], F32)
    for start, b in runs[1:]:
        val = jnp.where(rel >= start, t5_ref[b, h], val)
    return val - t5_ref[T5_FAR_BUCKET, h]


def _rel_bias_tile(rel, lo, hi, rb_ref, h):
    lo = max(lo, -REL_CLIP)
    hi = min(hi, REL_CLIP)
    val = jnp.full(rel.shape, rb_ref[h, lo + REL_CLIP], F32)
    for d in range(lo + 1, hi + 1):
        val = jnp.where(rel >= d, rb_ref[h, d + REL_CLIP], val)
    return val


def _sigmoid(x):
    return 1.0 / (1.0 + jnp.exp(-x))


def _lambda(lq1, lk1, lq2, lk2, lam_init):
    a = jnp.sum(lq1[...] * lk1[...], axis=-1, keepdims=True)
    b = jnp.sum(lq2[...] * lk2[...], axis=-1, keepdims=True)
    return jnp.exp(a) - jnp.exp(b) + lam_init


def _in_proj_kernel(x_ref, g_ref, w_ref, o_ref, h_ref):
    @pl.when(pl.program_id(1) == 0)
    def _():
        x = x_ref[...]
        ms = jnp.mean(x * x, axis=-1, keepdims=True)
        h_ref[...] = (x * lax.rsqrt(ms + EPS) * g_ref[...]).astype(BF16)

    o_ref[...] = jnp.dot(h_ref[...], w_ref[...], preferred_element_type=F32)


def _in_proj(x2d, pre_g, w_bf16, tm):
    m, d = x2d.shape
    n = w_bf16.shape[1]
    assert m % tm == 0 and n % COL_BLOCK == 0
    return pl.pallas_call(
        _in_proj_kernel,
        grid=(m // tm, n // COL_BLOCK),
        in_specs=[
            pl.BlockSpec((tm, d), lambda i, j: (i, 0)),
            pl.BlockSpec((1, d), lambda i, j: (0, 0)),
            pl.BlockSpec((d, COL_BLOCK), lambda i, j: (0, j)),
        ],
        out_specs=pl.BlockSpec((None, tm, COL_BLOCK), lambda i, j: (j, i, 0)),
        out_shape=jax.ShapeDtypeStruct((n // COL_BLOCK, m, COL_BLOCK), F32),
        scratch_shapes=[pltpu.VMEM((tm, d), BF16)],
        compiler_params=pltpu.CompilerParams(
            dimension_semantics=("arbitrary", "arbitrary"), vmem_limit_bytes=VMEM_LIMIT),
        name="in_proj",
    )(x2d, pre_g.reshape(1, d), w_bf16)


def _online_softmax_step(s, vt, m_ref, l_ref, acc_ref):
    m_old = m_ref[...]
    m_new = jnp.maximum(m_old, jnp.max(s, axis=0, keepdims=True))
    alpha = jnp.exp(m_old - m_new)
    p = jnp.exp(s - m_new)
    l_ref[...] = alpha * l_ref[...] + jnp.sum(p, axis=0, keepdims=True)
    acc_ref[...] = alpha * acc_ref[...] + jnp.dot(vt, p.astype(BF16), preferred_element_type=F32)
    m_ref[...] = m_new


def _stage_keys_values(k_ref, v_ref, kbf_ref, vt_ref, t):
    def body(j, c):
        r = pl.multiple_of(j * t, t)
        kbf_ref[pl.ds(r, t), :] = k_ref[pl.ds(r, t), :].astype(BF16)
        vt_ref[j] = v_ref[pl.ds(r, t), :].T.astype(BF16)
        return c
    lax.fori_loop(0, vt_ref.shape[0], body, 0)


def _attn_a_kernel(t5_ref, lq1, lk1, lq2, lk2, g_ref, q_ref, k_ref, v_ref, o_ref,
                   kbf_ref, vt_ref, bias_ref, qt_ref, m_ref, l_ref, acc_ref, *, lam_init):
    t = ATT_TILE
    h = pl.program_id(0)
    i = pl.program_id(1)

    @pl.when(i == 0)
    def _():
        _stage_keys_values(k_ref, v_ref, kbf_ref, vt_ref, t)
        key = lax.broadcasted_iota(jnp.int32, (t, t), 0)
        qry = lax.broadcasted_iota(jnp.int32, (t, t), 1)
        rel = key - qry
        diag = _t5_bias_tile(jnp.minimum(rel, CHUNK - 1), -(t - 1), CHUNK - 1, t5_ref, h)
        bias_ref[0] = jnp.where((key // CHUNK) <= (qry // CHUNK), diag, -jnp.inf)
        bias_ref[1] = _t5_bias_tile(rel - t, -(2 * t - 1), -1, t5_ref, h)

    qt = (q_ref[...] * (DK_A ** -0.5)).T
    sub = lax.broadcasted_iota(jnp.int32, (HEAD_W, t), 0)
    qt_ref[:, :t] = jnp.where(sub < DK_A, qt, 0.0).astype(BF16)
    qt_ref[:, t:] = jnp.where(sub >= DK_A, qt, 0.0).astype(BF16)
    m_ref[...] = jnp.full(m_ref.shape, -jnp.inf, F32)
    l_ref[...] = jnp.zeros(l_ref.shape, F32)
    acc_ref[...] = jnp.zeros(acc_ref.shape, F32)

    def tile(j, bias_idx):
        r = pl.multiple_of(j * t, t)
        s = jnp.dot(kbf_ref[pl.ds(r, t), :], qt_ref[...], preferred_element_type=F32)
        if bias_idx is not None:
            b = bias_ref[bias_idx]
            s = jnp.concatenate([s[:, :t] + b, s[:, t:] + b], axis=1)
        _online_softmax_step(s, vt_ref[j], m_ref, l_ref, acc_ref)

    def far(j, c):
        tile(j, None)
        return c
    lax.fori_loop(0, jnp.maximum(i - 1, 0), far, 0)

    @pl.when(i >= 1)
    def _():
        tile(i - 1, 1)

    tile(i, 0)

    inv = 1.0 / l_ref[...]
    acc = acc_ref[...]
    lam = _lambda(lq1, lk1, lq2, lk2, lam_init)
    o = acc[:, :t] * inv[:, :t] - lam * (acc[:, t:] * inv[:, t:])
    ms = jnp.mean(o * o, axis=0, keepdims=True)
    y = (o * lax.rsqrt(ms + EPS) * g_ref[...]) * (1.0 - lam_init)
    o_ref[...] = y.T


def _attn_a_prompt(proj, t5_bias, lq1, lk1, lq2, lk2, subln_g, lam_init):
    s = proj.shape[1]
    t = ATT_TILE
    assert s % t == 0 and t % CHUNK == 0 and t >= T5_MAX_DIST
    vec = lambda: pl.BlockSpec((1, DK_A), lambda h, i: (0, 0))
    return pl.pallas_call(
        functools.partial(_attn_a_kernel, lam_init=lam_init),
        grid=(H_A, s // t),
        in_specs=[
            pl.BlockSpec(memory_space=pltpu.SMEM),
            vec(), vec(), vec(), vec(),
            pl.BlockSpec((DV_A, 1), lambda h, i: (0, 0)),
            pl.BlockSpec((None, t, HEAD_W), lambda h, i: (QA, i, h)),
            pl.BlockSpec((None, s, HEAD_W), lambda h, i: (KA, 0, h)),
            pl.BlockSpec((None, s, HEAD_W), lambda h, i: (VA, 0, h)),
        ],
        out_specs=pl.BlockSpec((t, HEAD_W), lambda h, i: (i, h)),
        out_shape=jax.ShapeDtypeStruct((s, H_A * DV_A), F32),
        scratch_shapes=[
            pltpu.VMEM((s, HEAD_W), BF16),
            pltpu.VMEM((s // t, DV_A, t), BF16),
            pltpu.VMEM((2, t, t), F32),
            pltpu.VMEM((HEAD_W, 2 * t), BF16),
            pltpu.VMEM((1, 2 * t), F32),
            pltpu.VMEM((1, 2 * t), F32),
            pltpu.VMEM((DV_A, 2 * t), F32),
        ],
        compiler_params=pltpu.CompilerParams(
            dimension_semantics=("arbitrary", "arbitrary"), vmem_limit_bytes=VMEM_LIMIT),
        name="attn_a_prompt",
    )(t5_bias, lq1.reshape(1, DK_A), lk1.reshape(1, DK_A), lq2.reshape(1, DK_A), lk2.reshape(1, DK_A),
      subln_g.reshape(DV_A, 1), proj, proj, proj)


def _attn_b_kernel(rb_ref, q_ref, k_ref, v_ref, o_ref,
                   kbf_ref, vt_ref, bias_ref, qt_ref, m_ref, l_ref, acc_ref):
    t = ATT_TILE
    blk = REL_CLIP
    nb = t // blk
    h = pl.program_id(0)
    i = pl.program_id(1)

    @pl.when(i == 0)
    def _():
        _stage_keys_values(k_ref, v_ref, kbf_ref, vt_ref, t)
        kk = lax.broadcasted_iota(jnp.int32, (blk, blk), 0)
        qq = lax.broadcasted_iota(jnp.int32, (blk, blk), 1)
        rel = kk - qq
        lo = jnp.full((blk, blk), rb_ref[h, 0], F32)
        same = _rel_bias_tile(rel, -(blk - 1), blk - 1, rb_ref, h)
        prev = _rel_bias_tile(rel - blk, -(2 * blk - 1), -1, rb_ref, h)
        ninf = jnp.full((blk, blk), -jnp.inf, F32)
        kc = kk // CHUNK
        qc = qq // CHUNK
        for a in range(nb):
            for b in range(nb):
                if a > b:
                    own = ninf
                elif a == b:
                    own = jnp.where(kc <= qc, same, -jnp.inf)
                elif a == b - 1:
                    own = prev
                else:
                    own = lo
                bias_ref[0, a * blk:(a + 1) * blk, b * blk:(b + 1) * blk] = own
                if a < b:
                    old = ninf
                elif a == b:
                    old = jnp.where(kc >= qc, lo, -jnp.inf)
                elif a - b == nb - 1:
                    old = prev
                else:
                    old = lo
                bias_ref[1, a * blk:(a + 1) * blk, b * blk:(b + 1) * blk] = old

    qt_ref[...] = q_ref[...].T.astype(BF16)
    m_ref[...] = jnp.full(m_ref.shape, -jnp.inf, F32)
    l_ref[...] = jnp.zeros(l_ref.shape, F32)
    acc_ref[...] = jnp.zeros(acc_ref.shape, F32)

    def tile(j, bias_idx):
        r = pl.multiple_of(j * t, t)
        s = jnp.dot(kbf_ref[pl.ds(r, t), :], qt_ref[...], preferred_element_type=F32)
        s = s * (DH_B ** -0.5) + bias_ref[bias_idx]
        _online_softmax_step(s, vt_ref[j], m_ref, l_ref, acc_ref)

    tile(i, 0)

    @pl.when(i >= 1)
    def _():
        tile(i - 1, 1)

    o_ref[...] = (acc_ref[...] * (1.0 / l_ref[...])).T


def _attn_b_prompt(proj, rel_bias):
    s = proj.shape[1]
    t = ATT_TILE
    assert s % t == 0 and t == BAND_PAST and t % REL_CLIP == 0 and REL_CLIP % CHUNK == 0
    return pl.pallas_call(
        _attn_b_kernel,
        grid=(H_B, s // t),
        in_specs=[
            pl.BlockSpec(memory_space=pltpu.SMEM),
            pl.BlockSpec((None, t, HEAD_W), lambda h, i: (QB, i, h)),
            pl.BlockSpec((None, s, HEAD_W), lambda h, i: (KB, 0, h)),
            pl.BlockSpec((None, s, HEAD_W), lambda h, i: (VB, 0, h)),
        ],
        out_specs=pl.BlockSpec((t, HEAD_W), lambda h, i: (i, h)),
        out_shape=jax.ShapeDtypeStruct((s, H_B * DH_B), F32),
        scratch_shapes=[
            pltpu.VMEM((s, HEAD_W), BF16),
            pltpu.VMEM((s // t, DH_B, t), BF16),
            pltpu.VMEM((2, t, t), F32),
            pltpu.VMEM((HEAD_W, t), BF16),
            pltpu.VMEM((1, t), F32),
            pltpu.VMEM((1, t), F32),
            pltpu.VMEM((DH_B, t), F32),
        ],
        compiler_params=pltpu.CompilerParams(
            dimension_semantics=("arbitrary", "arbitrary"), vmem_limit_bytes=VMEM_LIMIT),
        name="attn_b_prompt",
    )(rel_bias, proj, proj, proj)


def _dot_nt(a, b):
    return lax.dot_general(a, b, (((1,), (1,)), ((), ())), preferred_element_type=F32)


def _attn_sample_kernel(t5_ref, rb_ref, lq1, lk1, lq2, lk2, g_ref,
                        qa_ref, ka_ref, va_ref, cka_ref, cva_ref,
                        qb_ref, kb_ref, vb_ref, ckb_ref, cvb_ref,
                        oa_ref, ob_ref, kroll_ref, vroll_ref,
                        ba_ref, bb_ref, *, lam_init, past):
    n = qa_ref.shape[0]
    win = ckb_ref.shape[0]
    near = NEAR
    h = pl.program_id(0)

    @pl.when(pl.program_id(1) == 0)
    def _():
        qry = lax.broadcasted_iota(jnp.int32, (n, near + n), 0)
        key = lax.broadcasted_iota(jnp.int32, (n, near + n), 1)
        rel = key - near - qry
        ba_ref[...] = _t5_bias_tile(rel, -(near + n - 1), n - 1, t5_ref, h)
        bb_ref[...] = _rel_bias_tile(rel, -(near + n - 1), n - 1, rb_ref, h)

    q = qa_ref[...] * (DK_A ** -0.5)
    lane = lax.broadcasted_iota(jnp.int32, (n, HEAD_W), 1)
    q2 = jnp.concatenate([jnp.where(lane < DK_A, q, 0.0), jnp.where(lane >= DK_A, q, 0.0)], axis=0).astype(BF16)
    ba = ba_ref[...]
    ba2 = jnp.concatenate([ba, ba], axis=0)
    s_c = _dot_nt(q2, cka_ref[...].astype(BF16))
    s_c = jnp.concatenate([s_c[:, :past - near], s_c[:, past - near:] + ba2[:, :near]], axis=1)
    s_n = _dot_nt(q2, ka_ref[...].astype(BF16)) + ba2[:, near:]
    m = jnp.maximum(jnp.max(s_c, axis=-1, keepdims=True), jnp.max(s_n, axis=-1, keepdims=True))
    p_c = jnp.exp(s_c - m)
    p_n = jnp.exp(s_n - m)
    l = jnp.sum(p_c, axis=-1, keepdims=True) + jnp.sum(p_n, axis=-1, keepdims=True)
    o2 = (jnp.dot(p_c.astype(BF16), cva_ref[...].astype(BF16), preferred_element_type=F32)
          + jnp.dot(p_n.astype(BF16), va_ref[...].astype(BF16), preferred_element_type=F32)) * (1.0 / l)
    lam = _lambda(lq1, lk1, lq2, lk2, lam_init)
    o = o2[:n] - lam * o2[n:]
    ms = jnp.mean(o * o, axis=-1, keepdims=True)
    oa_ref[...] = (o * lax.rsqrt(ms + EPS) * g_ref[...]) * (1.0 - lam_init)

    qb = qb_ref[...].astype(BF16)
    bb = bb_ref[...]
    s_c = _dot_nt(qb, ckb_ref[...].astype(BF16)) * (DH_B ** -0.5)
    s_c = jnp.concatenate([s_c[:, :win - near] + rb_ref[h, 0], s_c[:, win - near:] + bb[:, :near]], axis=1)
    s_n = _dot_nt(qb, kb_ref[...].astype(BF16)) * (DH_B ** -0.5) + bb[:, near:]
    m = jnp.maximum(jnp.max(s_c, axis=-1, keepdims=True), jnp.max(s_n, axis=-1, keepdims=True))
    p_c = jnp.exp(s_c - m)
    p_n = jnp.exp(s_n - m)
    l = jnp.sum(p_c, axis=-1, keepdims=True) + jnp.sum(p_n, axis=-1, keepdims=True)
    ob_ref[...] = (jnp.dot(p_c.astype(BF16), cvb_ref[...].astype(BF16), preferred_element_type=F32)
                   + jnp.dot(p_n.astype(BF16), vb_ref[...].astype(BF16), preferred_element_type=F32)) * (1.0 / l)

    kroll_ref[:win - n, :] = ckb_ref[n:, :]
    kroll_ref[win - n:, :] = kb_ref[...]
    vroll_ref[:win - n, :] = cvb_ref[n:, :]
    vroll_ref[win - n:, :] = vb_ref[...]


def _attn_sample(proj, ck_a, cv_a, ck_b, cv_b, t5_bias, rel_bias, lq1, lk1, lq2, lk2, subln_g, lam_init):
    nb, past = ck_a.shape[0], ck_a.shape[1]
    win = ck_b.shape[1]
    n = proj.shape[1] // nb
    near = NEAR
    assert past % CHUNK == 0 and n <= CHUNK and win <= BAND_PAST and win <= past
    assert near % HEAD_W == 0 and near <= win and near <= past and n % 8 == 0 and T5_MAX_DIST <= REL_CLIP
    cka = ck_a.reshape(nb, past, H_A * 2 * DK_A)
    cva = cv_a.reshape(nb, past, H_A * DV_A)
    ckb = ck_b.reshape(nb, win, H_B * DH_B)
    cvb = cv_b.reshape(nb, win, H_B * DH_B)
    vec = lambda: pl.BlockSpec((1, DK_A), lambda h, b: (0, 0))
    new = lambda c: pl.BlockSpec((None, n, HEAD_W), lambda h, b: (c, b, h))
    cache = lambda rows: pl.BlockSpec((None, rows, HEAD_W), lambda h, b: (b, 0, h))
    out = pl.BlockSpec((n, HEAD_W), lambda h, b: (b, h))
    return pl.pallas_call(
        functools.partial(_attn_sample_kernel, lam_init=lam_init, past=past),
        grid=(H_A, nb),
        in_specs=[
            pl.BlockSpec(memory_space=pltpu.SMEM),
            pl.BlockSpec(memory_space=pltpu.SMEM),
            vec(), vec(), vec(), vec(),
            pl.BlockSpec((1, DV_A), lambda h, b: (0, 0)),
            new(QA), new(KA), new(VA), cache(past), cache(past),
            new(QB), new(KB), new(VB), cache(win), cache(win),
        ],
        out_specs=[out, out, cache(win), cache(win)],
        out_shape=[
            jax.ShapeDtypeStruct((nb * n, H_A * DV_A), F32),
            jax.ShapeDtypeStruct((nb * n, H_B * DH_B), F32),
            jax.ShapeDtypeStruct((nb, win, H_B * DH_B), F32),
            jax.ShapeDtypeStruct((nb, win, H_B * DH_B), F32),
        ],
        scratch_shapes=[pltpu.VMEM((n, near + n), F32), pltpu.VMEM((n, near + n), F32)],
        compiler_params=pltpu.CompilerParams(
            dimension_semantics=("arbitrary", "arbitrary"), vmem_limit_bytes=VMEM_LIMIT),
        name="attn_sample",
    )(t5_bias, rel_bias, lq1.reshape(1, DK_A), lk1.reshape(1, DK_A), lq2.reshape(1, DK_A),
      lk2.reshape(1, DK_A), subln_g.reshape(1, DV_A),
      proj, proj, proj, cka, cva, proj, proj, proj, ckb, cvb)


def _merge_kernel(x_ref, oa_ref, ob_ref, za_ref, zb_ref, ga0_ref, ga1_ref, gb0_ref, gb1_ref,
                  woa_ref, wob_ref, wout_ref, pg_ref, y_ref):
    za = za_ref[...]
    zb = zb_ref[...]
    a = (oa_ref[...] * (za * _sigmoid(za))).astype(BF16)
    b = (ob_ref[...] * (zb * _sigmoid(zb))).astype(BF16)
    ya = jnp.dot(a, woa_ref[...], preferred_element_type=F32)
    yb = jnp.dot(b, wob_ref[...], preferred_element_type=F32)
    ga = jnp.concatenate([ga0_ref[...], ga1_ref[...]], axis=1)
    gb = jnp.concatenate([gb0_ref[...], gb1_ref[...]], axis=1)
    mix = (_sigmoid(ga) * ya + _sigmoid(gb) * yb).astype(BF16)
    y = jnp.dot(mix, wout_ref[...], preferred_element_type=F32)
    ms = jnp.mean(y * y, axis=-1, keepdims=True)
    y_ref[...] = x_ref[...] + y * lax.rsqrt(ms + EPS) * pg_ref[...]


def _merge(x2d, o_a, o_b, proj, woa, wob, wout, post_g, tm):
    m, d = x2d.shape
    wa = o_a.shape[1]
    wb = o_b.shape[1]
    assert m % tm == 0 and wa == COL_BLOCK and wb == COL_BLOCK and d == 2 * COL_BLOCK
    row = lambda w: pl.BlockSpec((tm, w), lambda i: (i, 0))
    col = lambda c: pl.BlockSpec((None, tm, COL_BLOCK), lambda i: (c, i, 0))
    resident = lambda r, c: pl.BlockSpec((r, c), lambda i: (0, 0), pipeline_mode=pl.Buffered(1))
    return pl.pallas_call(
        _merge_kernel,
        grid=(m // tm,),
        in_specs=[row(d), row(wa), row(wb), col(ZA), col(ZB), col(GA0), col(GA1), col(GB0), col(GB1),
                  resident(wa, d), resident(wb, d), resident(d, d), resident(1, d)],
        out_specs=row(d),
        out_shape=jax.ShapeDtypeStruct((m, d), F32),
        compiler_params=pltpu.CompilerParams(
            dimension_semantics=("arbitrary",), vmem_limit_bytes=VMEM_LIMIT),
        name="merge",
    )(x2d, o_a, o_b, proj, proj, proj, proj, proj, proj, woa, wob, wout, post_g.reshape(1, d))


def kernel(x_prompt, x_sample, cache_k_a, cache_v_a, cache_k_b, cache_v_b, t5_bias, pre_norm, post_norm,
           w_in, lambda_q1, lambda_k1, lambda_q2, lambda_k2, subln_a, rel_bias_b, w_o_a, w_o_b, w_out):
    depth = w_in.shape[0]
    bp, sp, d = x_prompt.shape
    bs, ss, _ = x_sample.shape
    assert bp == 1 and w_in.shape[2] == 12 * COL_BLOCK
    yp = x_prompt.reshape(sp, d)
    ys = x_sample.reshape(bs * ss, d)
    tail = min(BAND_PAST, sp)
    outs = [[] for _ in range(8)]
    for l in range(depth):
        lam_init = 0.8 - 0.6 * math.exp(-0.3 * l)
        w = w_in[l].astype(BF16)
        woa = w_o_a[l].astype(BF16)
        wob = w_o_b[l].astype(BF16)
        wout = w_out[l].astype(BF16)
        lam_args = (lambda_q1[l], lambda_k1[l], lambda_q2[l], lambda_k2[l], subln_a[l], lam_init)

        pp = _in_proj(yp, pre_norm[l], w, tm=1024)
        oa = _attn_a_prompt(pp, t5_bias, *lam_args)
        ob = _attn_b_prompt(pp, rel_bias_b[l])
        yp = _merge(yp, oa, ob, pp, woa, wob, wout, post_norm[l], tm=256)
        outs[0].append(pp[KA].reshape(bp, sp, 2 * H_A, DK_A))
        outs[1].append(pp[VA].reshape(bp, sp, H_A, DV_A))
        outs[2].append(pp[KB, sp - tail:].reshape(bp, tail, H_B, DH_B))
        outs[3].append(pp[VB, sp - tail:].reshape(bp, tail, H_B, DH_B))

        ps = _in_proj(ys, pre_norm[l], w, tm=bs * ss)
        oas, obs, kroll, vroll = _attn_sample(ps, cache_k_a[l], cache_v_a[l], cache_k_b[l], cache_v_b[l],
                                              t5_bias, rel_bias_b[l], *lam_args)
        ys = _merge(ys, oas, obs, ps, woa, wob, wout, post_norm[l], tm=bs * ss)
        outs[4].append(ps[KA].reshape(bs, ss, 2 * H_A, DK_A))
        outs[5].append(ps[VA].reshape(bs, ss, H_A, DV_A))
        outs[6].append(kroll.reshape(bs, -1, H_B, DH_B))
        outs[7].append(vroll.reshape(bs, -1, H_B, DH_B))
    return (yp.reshape(bp, sp, d), ys.reshape(bs, ss, d)) + tuple(jnp.stack(o) for o in outs)
```

```python
import functools
import math

import numpy as np
import jax
import jax.numpy as jnp
from jax import lax
from jax.experimental import pallas as pl
from jax.experimental.pallas import tpu as pltpu

F32 = jnp.float32
BF16 = jnp.bfloat16

CHUNK = 64
H_A = 8
DK_A = 64
DV_A = 2 * DK_A
H_B = 8
DH_B = 128
BAND_CHUNKS = 8
BAND_PAST = BAND_CHUNKS * CHUNK
REL_CLIP = 128
T5_BUCKETS = 32
T5_MAX_DIST = 128
EPS = 1e-6

HEAD_W = 128
COL_BLOCK = 1024
ATT_TILE = 512
A_TQ, A_TK = 512, 256
LOG2E = math.log2(math.e)
NEAR = max(REL_CLIP, T5_MAX_DIST)
VMEM_LIMIT = 60 * 1024 * 1024

QA, KA, VA, ZA, QB, KB, VB, ZB, GA0, GA1, GB0, GB1 = range(12)


def _t5_bucket_int(rel):
    half = T5_BUCKETS // 2
    max_exact = half // 2
    n = abs(rel)
    ret = half if rel > 0 else 0
    if n < max_exact:
        return ret + n
    assert (T5_MAX_DIST // max_exact) ** 2 == 2 ** (half - max_exact)
    j = 0
    while n * n >= (max_exact * max_exact) * 2 ** (j + 1):
        j += 1
    return ret + min(max_exact + j, half - 1)


def _t5_runs(lo, hi):
    runs = []
    for r in range(lo, hi + 1):
        b = _t5_bucket_int(r)
        if not runs or runs[-1][1] != b:
            runs.append((r, b))
    return runs


T5_FAR_BUCKET = _t5_bucket_int(-T5_MAX_DIST)
assert all(_t5_bucket_int(-n) == T5_FAR_BUCKET for n in range(T5_MAX_DIST, 4 * T5_MAX_DIST))


def _t5_bias_tile(rel, lo, hi, t5_ref, h):
    runs = _t5_runs(lo, hi)
    val = jnp.full(rel.shape, t5_ref[runs[0][1], h], F32)
    for start, b in runs[1:]:
        val = jnp.where(rel >= start, t5_ref[b, h], val)
    return val - t5_ref[T5_FAR_BUCKET, h]


def _rel_bias_tile(rel, lo, hi, rb_ref, h):
    lo = max(lo, -REL_CLIP)
    hi = min(hi, REL_CLIP)
    val = jnp.full(rel.shape, rb_ref[h, lo + REL_CLIP], F32)
    for d in range(lo + 1, hi + 1):
        val = jnp.where(rel >= d, rb_ref[h, d + REL_CLIP], val)
    return val


def _sigmoid(x):
    return 1.0 / (1.0 + jnp.exp(-x))


def _lambda(lq1, lk1, lq2, lk2, lam_init):
    a = jnp.sum(lq1[...] * lk1[...], axis=-1, keepdims=True)
    b = jnp.sum(lq2[...] * lk2[...], axis=-1, keepdims=True)
    return jnp.exp(a) - jnp.exp(b) + lam_init


def _in_proj_kernel(x_ref, g_ref, w_ref, o_ref, h_ref):
    @pl.when(pl.program_id(1) == 0)
    def _():
        x = x_ref[...]
        ms = jnp.mean(x * x, axis=-1, keepdims=True)
        h_ref[...] = (x * lax.rsqrt(ms + EPS) * g_ref[...]).astype(BF16)

    o_ref[...] = jnp.dot(h_ref[...], w_ref[...], preferred_element_type=F32)


def _in_proj(x2d, pre_g, w_bf16, tm):
    m, d = x2d.shape
    n = w_bf16.shape[1]
    assert m % tm == 0 and n % COL_BLOCK == 0
    return pl.pallas_call(
        _in_proj_kernel,
        grid=(m // tm, n // COL_BLOCK),
        in_specs=[
            pl.BlockSpec((tm, d), lambda i, j: (i, 0)),
            pl.BlockSpec((1, d), lambda i, j: (0, 0)),
            pl.BlockSpec((d, COL_BLOCK), lambda i, j: (0, j)),
        ],
        out_specs=pl.BlockSpec((None, tm, COL_BLOCK), lambda i, j: (j, i, 0)),
        out_shape=jax.ShapeDtypeStruct((n // COL_BLOCK, m, COL_BLOCK), F32),
        scratch_shapes=[pltpu.VMEM((tm, d), BF16)],
        compiler_params=pltpu.CompilerParams(
            dimension_semantics=("arbitrary", "arbitrary"), vmem_limit_bytes=VMEM_LIMIT),
        name="in_proj",
    )(x2d, pre_g.reshape(1, d), w_bf16)


def _online_softmax_step(s, vt, m_ref, l_ref, acc_ref):
    m_old = m_ref[...]
    m_new = jnp.maximum(m_old, jnp.max(s, axis=0, keepdims=True))
    alpha = jnp.exp2(m_old - m_new)
    p = jnp.exp2(s - m_new)
    l_ref[...] = alpha * l_ref[...] + jnp.sum(p, axis=0, keepdims=True)
    acc_ref[...] = alpha * acc_ref[...] + jnp.dot(vt, p.astype(BF16), preferred_element_type=F32)
    m_ref[...] = m_new


def _stage_keys_values(k_ref, v_ref, kbf_ref, vt_ref, t):
    def body(j, c):
        r = pl.multiple_of(j * t, t)
        kbf_ref[pl.ds(r, t), :] = k_ref[pl.ds(r, t), :].astype(BF16)
        vt_ref[j] = v_ref[pl.ds(r, t), :].T.astype(BF16)
        return c
    lax.fori_loop(0, vt_ref.shape[0], body, 0)


def _attn_a_kernel(t5_ref, lq1, lk1, lq2, lk2, g_ref, q_ref, k_ref, v_ref, o_ref,
                   kbf_ref, vt_ref, bias_ref, qt_ref, m_ref, l_ref, acc_ref, s0_ref, s1_ref, *, lam_init):
    tq, tk = A_TQ, A_TK
    h = pl.program_id(0)
    i = pl.program_id(1)

    @pl.when(i == 0)
    def _():
        _stage_keys_values(k_ref, v_ref, kbf_ref, vt_ref, tk)
        key = lax.broadcasted_iota(jnp.int32, (tk, tq), 0)
        qry = lax.broadcasted_iota(jnp.int32, (tk, tq), 1)
        for n in range(3):
            rel = key + (n - 1) * tk - qry
            lo, hi = (n - 1) * tk - (tq - 1), min(n * tk - 1, CHUNK - 1)
            b = _t5_bias_tile(jnp.minimum(rel, hi), lo, hi, t5_ref, h) * LOG2E
            if n >= 1:
                b = jnp.where((key + (n - 1) * tk) // CHUNK <= qry // CHUNK, b, -jnp.inf)
            bias_ref[n] = b

    qt = (q_ref[...] * (DK_A ** -0.5 * LOG2E)).T
    sub = lax.broadcasted_iota(jnp.int32, (HEAD_W, tq), 0)
    qt_ref[:, :tq] = jnp.where(sub < DK_A, qt, 0.0).astype(BF16)
    qt_ref[:, tq:] = jnp.where(sub >= DK_A, qt, 0.0).astype(BF16)
    m_ref[...] = jnp.full(m_ref.shape, -jnp.inf, F32)
    l_ref[...] = jnp.zeros(l_ref.shape, F32)
    acc_ref[...] = jnp.zeros(acc_ref.shape, F32)

    def scores(j, s_ref):
        r = pl.multiple_of(j * tk, tk)
        s_ref[...] = jnp.dot(kbf_ref[pl.ds(r, tk), :], qt_ref[...], preferred_element_type=F32)

    def update(j, s_ref, bias_idx):
        s = s_ref[...]
        if bias_idx is not None:
            b = bias_ref[bias_idx]
            s = jnp.concatenate([s[:, :tq] + b, s[:, tq:] + b], axis=1)
        _online_softmax_step(s, vt_ref[j], m_ref, l_ref, acc_ref)

    def pair(j, bias0, bias1, last):
        scores(j + 1, s1_ref)
        update(j, s0_ref, bias0)
        if not last:
            scores(j + 2, s0_ref)
        update(j + 1, s1_ref, bias1)

    scores(0, s0_ref)

    def far(u, c):
        pair(2 * u, None, None, False)
        return c
    lax.fori_loop(0, i - 1, far, 0)

    @pl.when(i >= 1)
    def _():
        pair(2 * i - 2, None, 0, False)

    pair(2 * i, 1, 2, True)

    inv = 1.0 / l_ref[...]
    acc = acc_ref[...]
    lam = _lambda(lq1, lk1, lq2, lk2, lam_init)
    o = acc[:, :tq] * inv[:, :tq] - lam * (acc[:, tq:] * inv[:, tq:])
    ms = jnp.mean(o * o, axis=0, keepdims=True)
    y = (o * lax.rsqrt(ms + EPS) * g_ref[...]) * (1.0 - lam_init)
    o_ref[...] = y.T


def _attn_a_prompt(proj, t5_bias, lq1, lk1, lq2, lk2, subln_g, lam_init):
    s = proj.shape[1]
    tq, tk = A_TQ, A_TK
    assert s % tq == 0 and tq == 2 * tk and tk % CHUNK == 0 and tk >= T5_MAX_DIST
    vec = lambda: pl.BlockSpec((1, DK_A), lambda h, i: (0, 0))
    return pl.pallas_call(
        functools.partial(_attn_a_kernel, lam_init=lam_init),
        grid=(H_A, s // tq),
        in_specs=[
            pl.BlockSpec(memory_space=pltpu.SMEM),
            vec(), vec(), vec(), vec(),
            pl.BlockSpec((DV_A, 1), lambda h, i: (0, 0)),
            pl.BlockSpec((None, tq, HEAD_W), lambda h, i: (QA, i, h)),
            pl.BlockSpec((None, s, HEAD_W), lambda h, i: (KA, 0, h)),
            pl.BlockSpec((None, s, HEAD_W), lambda h, i: (VA, 0, h)),
        ],
        out_specs=pl.BlockSpec((tq, HEAD_W), lambda h, i: (i, h)),
        out_shape=jax.ShapeDtypeStruct((s, H_A * DV_A), F32),
        scratch_shapes=[
            pltpu.VMEM((s, HEAD_W), BF16),
            pltpu.VMEM((s // tk, DV_A, tk), BF16),
            pltpu.VMEM((3, tk, tq), F32),
            pltpu.VMEM((HEAD_W, 2 * tq), BF16),
            pltpu.VMEM((1, 2 * tq), F32),
            pltpu.VMEM((1, 2 * tq), F32),
            pltpu.VMEM((DV_A, 2 * tq), F32),
            pltpu.VMEM((tk, 2 * tq), F32),
            pltpu.VMEM((tk, 2 * tq), F32),
        ],
        compiler_params=pltpu.CompilerParams(
            dimension_semantics=("arbitrary", "arbitrary"), vmem_limit_bytes=VMEM_LIMIT),
        name="attn_a_prompt",
    )(t5_bias, lq1.reshape(1, DK_A), lk1.reshape(1, DK_A), lq2.reshape(1, DK_A), lk2.reshape(1, DK_A),
      subln_g.reshape(DV_A, 1), proj, proj, proj)


def _attn_b_kernel(rb_ref, q_ref, k_ref, v_ref, o_ref,
                   kbf_ref, vt_ref, bias_ref, qt_ref, m_ref, l_ref, acc_ref):
    t = ATT_TILE
    blk = REL_CLIP
    nb = t // blk
    h = pl.program_id(0)
    i = pl.program_id(1)

    @pl.when(i == 0)
    def _():
        _stage_keys_values(k_ref, v_ref, kbf_ref, vt_ref, t)
        kk = lax.broadcasted_iota(jnp.int32, (blk, blk), 0)
        qq = lax.broadcasted_iota(jnp.int32, (blk, blk), 1)
        rel = kk - qq
        lo = jnp.full((blk, blk), rb_ref[h, 0] * LOG2E, F32)
        same = _rel_bias_tile(rel, -(blk - 1), blk - 1, rb_ref, h) * LOG2E
        prev = _rel_bias_tile(rel - blk, -(2 * blk - 1), -1, rb_ref, h) * LOG2E
        ninf = jnp.full((blk, blk), -jnp.inf, F32)
        kc = kk // CHUNK
        qc = qq // CHUNK
        for a in range(nb):
            for b in range(nb):
                if a > b:
                    own = ninf
                elif a == b:
                    own = jnp.where(kc <= qc, same, -jnp.inf)
                elif a == b - 1:
                    own = prev
                else:
                    own = lo
                bias_ref[0, a * blk:(a + 1) * blk, b * blk:(b + 1) * blk] = own
                if a < b:
                    old = ninf
                elif a == b:
                    old = jnp.where(kc >= qc, lo, -jnp.inf)
                elif a - b == nb - 1:
                    old = prev
                else:
                    old = lo
                bias_ref[1, a * blk:(a + 1) * blk, b * blk:(b + 1) * blk] = old

    qt_ref[...] = q_ref[...].T.astype(BF16)
    m_ref[...] = jnp.full(m_ref.shape, -jnp.inf, F32)
    l_ref[...] = jnp.zeros(l_ref.shape, F32)
    acc_ref[...] = jnp.zeros(acc_ref.shape, F32)

    def tile(j, bias_idx):
        r = pl.multiple_of(j * t, t)
        s = jnp.dot(kbf_ref[pl.ds(r, t), :], qt_ref[...], preferred_element_type=F32)
        s = s * (DH_B ** -0.5 * LOG2E) + bias_ref[bias_idx]
        _online_softmax_step(s, vt_ref[j], m_ref, l_ref, acc_ref)

    tile(i, 0)

    @pl.when(i >= 1)
    def _():
        tile(i - 1, 1)

    o_ref[...] = (acc_ref[...] * (1.0 / l_ref[...])).T


def _attn_b_prompt(proj, rel_bias):
    s = proj.shape[1]
    t = ATT_TILE
    assert s % t == 0 and t == BAND_PAST and t % REL_CLIP == 0 and REL_CLIP % CHUNK == 0
    return pl.pallas_call(
        _attn_b_kernel,
        grid=(H_B, s // t),
        in_specs=[
            pl.BlockSpec(memory_space=pltpu.SMEM),
            pl.BlockSpec((None, t, HEAD_W), lambda h, i: (QB, i, h)),
            pl.BlockSpec((None, s, HEAD_W), lambda h, i: (KB, 0, h)),
            pl.BlockSpec((None, s, HEAD_W), lambda h, i: (VB, 0, h)),
        ],
        out_specs=pl.BlockSpec((t, HEAD_W), lambda h, i: (i, h)),
        out_shape=jax.ShapeDtypeStruct((s, H_B * DH_B), F32),
        scratch_shapes=[
            pltpu.VMEM((s, HEAD_W), BF16),
            pltpu.VMEM((s // t, DH_B, t), BF16),
            pltpu.VMEM((2, t, t), F32),
            pltpu.VMEM((HEAD_W, t), BF16),
            pltpu.VMEM((1, t), F32),
            pltpu.VMEM((1, t), F32),
            pltpu.VMEM((DH_B, t), F32),
        ],
        compiler_params=pltpu.CompilerParams(
            dimension_semantics=("arbitrary", "arbitrary"), vmem_limit_bytes=VMEM_LIMIT),
        name="attn_b_prompt",
    )(rel_bias, proj, proj, proj)


def _dot_nt(a, b):
    return lax.dot_general(a, b, (((1,), (1,)), ((), ())), preferred_element_type=F32)


def _attn_sample_kernel(t5_ref, rb_ref, lq1, lk1, lq2, lk2, g_ref,
                        qa_ref, ka_ref, va_ref, cka_ref, cva_ref,
                        qb_ref, kb_ref, vb_ref, ckb_ref, cvb_ref,
                        oa_ref, ob_ref, kroll_ref, vroll_ref,
                        ba_ref, bb_ref, *, lam_init, past):
    n = qa_ref.shape[0]
    win = ckb_ref.shape[0]
    near = NEAR
    h = pl.program_id(0)

    @pl.when(pl.program_id(1) == 0)
    def _():
        qry = lax.broadcasted_iota(jnp.int32, (n, near + n), 0)
        key = lax.broadcasted_iota(jnp.int32, (n, near + n), 1)
        rel = key - near - qry
        ba_ref[...] = _t5_bias_tile(rel, -(near + n - 1), n - 1, t5_ref, h)
        bb_ref[...] = _rel_bias_tile(rel, -(near + n - 1), n - 1, rb_ref, h)

    q = qa_ref[...] * (DK_A ** -0.5)
    lane = lax.broadcasted_iota(jnp.int32, (n, HEAD_W), 1)
    q2 = jnp.concatenate([jnp.where(lane < DK_A, q, 0.0), jnp.where(lane >= DK_A, q, 0.0)], axis=0).astype(BF16)
    ba = ba_ref[...]
    ba2 = jnp.concatenate([ba, ba], axis=0)
    s_c = _dot_nt(q2, cka_ref[...].astype(BF16))
    s_c = jnp.concatenate([s_c[:, :past - near], s_c[:, past - near:] + ba2[:, :near]], axis=1)
    s_n = _dot_nt(q2, ka_ref[...].astype(BF16)) + ba2[:, near:]
    m = jnp.maximum(jnp.max(s_c, axis=-1, keepdims=True), jnp.max(s_n, axis=-1, keepdims=True))
    p_c = jnp.exp(s_c - m)
    p_n = jnp.exp(s_n - m)
    l = jnp.sum(p_c, axis=-1, keepdims=True) + jnp.sum(p_n, axis=-1, keepdims=True)
    o2 = (jnp.dot(p_c.astype(BF16), cva_ref[...].astype(BF16), preferred_element_type=F32)
          + jnp.dot(p_n.astype(BF16), va_ref[...].astype(BF16), preferred_element_type=F32)) * (1.0 / l)
    lam = _lambda(lq1, lk1, lq2, lk2, lam_init)
    o = o2[:n] - lam * o2[n:]
    ms = jnp.mean(o * o, axis=-1, keepdims=True)
    oa_ref[...] = (o * lax.rsqrt(ms + EPS) * g_ref[...]) * (1.0 - lam_init)

    qb = qb_ref[...].astype(BF16)
    bb = bb_ref[...]
    s_c = _dot_nt(qb, ckb_ref[...].astype(BF16)) * (DH_B ** -0.5)
    s_c = jnp.concatenate([s_c[:, :win - near] + rb_ref[h, 0], s_c[:, win - near:] + bb[:, :near]], axis=1)
    s_n = _dot_nt(qb, kb_ref[...].astype(BF16)) * (DH_B ** -0.5) + bb[:, near:]
    m = jnp.maximum(jnp.max(s_c, axis=-1, keepdims=True), jnp.max(s_n, axis=-1, keepdims=True))
    p_c = jnp.exp(s_c - m)
    p_n = jnp.exp(s_n - m)
    l = jnp.sum(p_c, axis=-1, keepdims=True) + jnp.sum(p_n, axis=-1, keepdims=True)
    ob_ref[...] = (jnp.dot(p_c.astype(BF16), cvb_ref[...].astype(BF16), preferred_element_type=F32)
                   + jnp.dot(p_n.astype(BF16), vb_ref[...].astype(BF16), preferred_element_type=F32)) * (1.0 / l)

    kroll_ref[:win - n, :] = ckb_ref[n:, :]
    kroll_ref[win - n:, :] = kb_ref[...]
    vroll_ref[:win - n, :] = cvb_ref[n:, :]
    vroll_ref[win - n:, :] = vb_ref[...]


def _attn_sample(proj, ck_a, cv_a, ck_b, cv_b, t5_bias, rel_bias, lq1, lk1, lq2, lk2, subln_g, lam_init):
    nb, past = ck_a.shape[0], ck_a.shape[1]
    win = ck_b.shape[1]
    n = proj.shape[1] // nb
    near = NEAR
    assert past % CHUNK == 0 and n <= CHUNK and win <= BAND_PAST and win <= past
    assert near % HEAD_W == 0 and near <= win and near <= past and n % 8 == 0 and T5_MAX_DIST <= REL_CLIP
    cka = ck_a.reshape(nb, past, H_A * 2 * DK_A)
    cva = cv_a.reshape(nb, past, H_A * DV_A)
    ckb = ck_b.reshape(nb, win, H_B * DH_B)
    cvb = cv_b.reshape(nb, win, H_B * DH_B)
    vec = lambda: pl.BlockSpec((1, DK_A), lambda h, b: (0, 0))
    new = lambda c: pl.BlockSpec((None, n, HEAD_W), lambda h, b: (c, b, h))
    cache = lambda rows: pl.BlockSpec((None, rows, HEAD_W), lambda h, b: (b, 0, h))
    out = pl.BlockSpec((n, HEAD_W), lambda h, b: (b, h))
    return pl.pallas_call(
        functools.partial(_attn_sample_kernel, lam_init=lam_init, past=past),
        grid=(H_A, nb),
        in_specs=[
            pl.BlockSpec(memory_space=pltpu.SMEM),
            pl.BlockSpec(memory_space=pltpu.SMEM),
            vec(), vec(), vec(), vec(),
            pl.BlockSpec((1, DV_A), lambda h, b: (0, 0)),
            new(QA), new(KA), new(VA), cache(past), cache(past),
            new(QB), new(KB), new(VB), cache(win), cache(win),
        ],
        out_specs=[out, out, cache(win), cache(win)],
        out_shape=[
            jax.ShapeDtypeStruct((nb * n, H_A * DV_A), F32),
            jax.ShapeDtypeStruct((nb * n, H_B * DH_B), F32),
            jax.ShapeDtypeStruct((nb, win, H_B * DH_B), F32),
            jax.ShapeDtypeStruct((nb, win, H_B * DH_B), F32),
        ],
        scratch_shapes=[pltpu.VMEM((n, near + n), F32), pltpu.VMEM((n, near + n), F32)],
        compiler_params=pltpu.CompilerParams(
            dimension_semantics=("arbitrary", "arbitrary"), vmem_limit_bytes=VMEM_LIMIT),
        name="attn_sample",
    )(t5_bias, rel_bias, lq1.reshape(1, DK_A), lk1.reshape(1, DK_A), lq2.reshape(1, DK_A),
      lk2.reshape(1, DK_A), subln_g.reshape(1, DV_A),
      proj, proj, proj, cka, cva, proj, proj, proj, ckb, cvb)


def _merge_kernel(x_ref, oa_ref, ob_ref, za_ref, zb_ref, ga0_ref, ga1_ref, gb0_ref, gb1_ref,
                  woa_ref, wob_ref, wout_ref, pg_ref, y_ref):
    za = za_ref[...]
    zb = zb_ref[...]
    a = (oa_ref[...] * (za * _sigmoid(za))).astype(BF16)
    b = (ob_ref[...] * (zb * _sigmoid(zb))).astype(BF16)
    ya = jnp.dot(a, woa_ref[...], preferred_element_type=F32)
    yb = jnp.dot(b, wob_ref[...], preferred_element_type=F32)
    ga = jnp.concatenate([ga0_ref[...], ga1_ref[...]], axis=1)
    gb = jnp.concatenate([gb0_ref[...], gb1_ref[...]], axis=1)
    mix = (_sigmoid(ga) * ya + _sigmoid(gb) * yb).astype(BF16)
    y = jnp.dot(mix, wout_ref[...], preferred_element_type=F32)
    ms = jnp.mean(y * y, axis=-1, keepdims=True)
    y_ref[...] = x_ref[...] + y * lax.rsqrt(ms + EPS) * pg_ref[...]


def _merge(x2d, o_a, o_b, proj, woa, wob, wout, post_g, tm):
    m, d = x2d.shape
    wa = o_a.shape[1]
    wb = o_b.shape[1]
    assert m % tm == 0 and wa == COL_BLOCK and wb == COL_BLOCK and d == 2 * COL_BLOCK
    row = lambda w: pl.BlockSpec((tm, w), lambda i: (i, 0))
    col = lambda c: pl.BlockSpec((None, tm, COL_BLOCK), lambda i: (c, i, 0))
    resident = lambda r, c: pl.BlockSpec((r, c), lambda i: (0, 0), pipeline_mode=pl.Buffered(1))
    return pl.pallas_call(
        _merge_kernel,
        grid=(m // tm,),
        in_specs=[row(d), row(wa), row(wb), col(ZA), col(ZB), col(GA0), col(GA1), col(GB0), col(GB1),
                  resident(wa, d), resident(wb, d), resident(d, d), resident(1, d)],
        out_specs=row(d),
        out_shape=jax.ShapeDtypeStruct((m, d), F32),
        compiler_params=pltpu.CompilerParams(
            dimension_semantics=("arbitrary",), vmem_limit_bytes=VMEM_LIMIT),
        name="merge",
    )(x2d, o_a, o_b, proj, proj, proj, proj, proj, proj, woa, wob, wout, post_g.reshape(1, d))


def kernel(x_prompt, x_sample, cache_k_a, cache_v_a, cache_k_b, cache_v_b, t5_bias, pre_norm, post_norm,
           w_in, lambda_q1, lambda_k1, lambda_q2, lambda_k2, subln_a, rel_bias_b, w_o_a, w_o_b, w_out):
    depth = w_in.shape[0]
    bp, sp, d = x_prompt.shape
    bs, ss, _ = x_sample.shape
    assert bp == 1 and w_in.shape[2] == 12 * COL_BLOCK
    yp = x_prompt.reshape(sp, d)
    ys = x_sample.reshape(bs * ss, d)
    tail = min(BAND_PAST, sp)
    outs = [[] for _ in range(8)]
    for l in range(depth):
        lam_init = 0.8 - 0.6 * math.exp(-0.3 * l)
        w = w_in[l].astype(BF16)
        woa = w_o_a[l].astype(BF16)
        wob = w_o_b[l].astype(BF16)
        wout = w_out[l].astype(BF16)
        lam_args = (lambda_q1[l], lambda_k1[l], lambda_q2[l], lambda_k2[l], subln_a[l], lam_init)

        pp = _in_proj(yp, pre_norm[l], w, tm=1024)
        oa = _attn_a_prompt(pp, t5_bias, *lam_args)
        ob = _attn_b_prompt(pp, rel_bias_b[l])
        yp = _merge(yp, oa, ob, pp, woa, wob, wout, post_norm[l], tm=256)
        outs[0].append(pp[KA].reshape(bp, sp, 2 * H_A, DK_A))
        outs[1].append(pp[VA].reshape(bp, sp, H_A, DV_A))
        outs[2].append(pp[KB, sp - tail:].reshape(bp, tail, H_B, DH_B))
        outs[3].append(pp[VB, sp - tail:].reshape(bp, tail, H_B, DH_B))

        ps = _in_proj(ys, pre_norm[l], w, tm=bs * ss)
        oas, obs, kroll, vroll = _attn_sample(ps, cache_k_a[l], cache_v_a[l], cache_k_b[l], cache_v_b[l],
                                              t5_bias, rel_bias_b[l], *lam_args)
        ys = _merge(ys, oas, obs, ps, woa, wob, wout, post_norm[l], tm=bs * ss)
        outs[4].append(ps[KA].reshape(bs, ss, 2 * H_A, DK_A))
        outs[5].append(ps[VA].reshape(bs, ss, H_A, DV_A))
        outs[6].append(kroll.reshape(bs, -1, H_B, DH_B))
        outs[7].append(vroll.reshape(bs, -1, H_B, DH_B))
    return (yp.reshape(bp, sp, d), ys.reshape(bs, ss, d)) + tuple(jnp.stack(o) for o in outs)
```

```python
import functools
import math

import numpy as np
import jax
import jax.numpy as jnp
from jax import lax
from jax.experimental import pallas as pl
from jax.experimental.pallas import tpu as pltpu

F32 = jnp.float32
BF16 = jnp.bfloat16

CHUNK = 64
H_A = 8
DK_A = 64
DV_A = 2 * DK_A
H_B = 8
DH_B = 128
BAND_CHUNKS = 8
BAND_PAST = BAND_CHUNKS * CHUNK
REL_CLIP = 128
T5_BUCKETS = 32
T5_MAX_DIST = 128
EPS = 1e-6

HEAD_W = 128
COL_BLOCK = 1024
A_TQ, A_TK = 512, 256
B_TQ = 256
LOG2E = math.log2(math.e)
NEAR = max(REL_CLIP, T5_MAX_DIST)
VMEM_LIMIT = 60 * 1024 * 1024

QA, KA, VA, ZA, QB, KB, VB, ZB, GA0, GA1, GB0, GB1 = range(12)


def _t5_bucket_int(rel):
    half = T5_BUCKETS // 2
    max_exact = half // 2
    n = abs(rel)
    ret = half if rel > 0 else 0
    if n < max_exact:
        return ret + n
    assert (T5_MAX_DIST // max_exact) ** 2 == 2 ** (half - max_exact)
    j = 0
    while n * n >= (max_exact * max_exact) * 2 ** (j + 1):
        j += 1
    return ret + min(max_exact + j, half - 1)


def _t5_runs(lo, hi):
    runs = []
    for r in range(lo, hi + 1):
        b = _t5_bucket_int(r)
        if not runs or runs[-1][1] != b:
            runs.append((r, b))
    return runs


T5_FAR_BUCKET = _t5_bucket_int(-T5_MAX_DIST)
assert all(_t5_bucket_int(-n) == T5_FAR_BUCKET for n in range(T5_MAX_DIST, 4 * T5_MAX_DIST))


def _t5_bias_tile(rel, lo, hi, t5_ref, h):
    runs = _t5_runs(lo, hi)
    val = jnp.full(rel.shape, t5_ref[runs[0][1], h], F32)
    for start, b in runs[1:]:
        val = jnp.where(rel >= start, t5_ref[b, h], val)
    return val - t5_ref[T5_FAR_BUCKET, h]


def _rel_bias_tile(rel, lo, hi, rb_ref, h):
    lo = max(lo, -REL_CLIP)
    hi = min(hi, REL_CLIP)
    val = jnp.full(rel.shape, rb_ref[h, lo + REL_CLIP], F32)
    for d in range(lo + 1, hi + 1):
        val = jnp.where(rel >= d, rb_ref[h, d + REL_CLIP], val)
    return val


def _sigmoid(x):
    return 1.0 / (1.0 + jnp.exp(-x))


def _lambda(lq1, lk1, lq2, lk2, lam_init):
    a = jnp.sum(lq1[...] * lk1[...], axis=-1, keepdims=True)
    b = jnp.sum(lq2[...] * lk2[...], axis=-1, keepdims=True)
    return jnp.exp(a) - jnp.exp(b) + lam_init


def _in_proj_kernel(x_ref, g_ref, w_ref, o_ref, h_ref):
    @pl.when(pl.program_id(1) == 0)
    def _():
        x = x_ref[...]
        ms = jnp.mean(x * x, axis=-1, keepdims=True)
        h_ref[...] = (x * lax.rsqrt(ms + EPS) * g_ref[...]).astype(BF16)

    o_ref[...] = jnp.dot(h_ref[...], w_ref[...], preferred_element_type=F32)


def _in_proj(x2d, pre_g, w_bf16, tm):
    m, d = x2d.shape
    n = w_bf16.shape[1]
    assert m % tm == 0 and n % COL_BLOCK == 0
    return pl.pallas_call(
        _in_proj_kernel,
        grid=(m // tm, n // COL_BLOCK),
        in_specs=[
            pl.BlockSpec((tm, d), lambda i, j: (i, 0)),
            pl.BlockSpec((1, d), lambda i, j: (0, 0)),
            pl.BlockSpec((d, COL_BLOCK), lambda i, j: (0, j)),
        ],
        out_specs=pl.BlockSpec((None, tm, COL_BLOCK), lambda i, j: (j, i, 0)),
        out_shape=jax.ShapeDtypeStruct((n // COL_BLOCK, m, COL_BLOCK), F32),
        scratch_shapes=[pltpu.VMEM((tm, d), BF16)],
        compiler_params=pltpu.CompilerParams(
            dimension_semantics=("arbitrary", "arbitrary"), vmem_limit_bytes=VMEM_LIMIT),
        name="in_proj",
    )(x2d, pre_g.reshape(1, d), w_bf16)


def _online_softmax_step(s, vt, m_ref, l_ref, acc_ref):
    m_old = m_ref[...]
    m_new = jnp.maximum(m_old, jnp.max(s, axis=0, keepdims=True))
    alpha = jnp.exp2(m_old - m_new)
    p = jnp.exp2(s - m_new)
    l_ref[...] = alpha * l_ref[...] + jnp.sum(p, axis=0, keepdims=True)
    acc_ref[...] = alpha * acc_ref[...] + jnp.dot(vt, p.astype(BF16), preferred_element_type=F32)
    m_ref[...] = m_new


def _stage_keys_values(k_ref, v_ref, kbf_ref, vt_ref, t):
    def body(j, c):
        r = pl.multiple_of(j * t, t)
        kbf_ref[pl.ds(r, t), :] = k_ref[pl.ds(r, t), :].astype(BF16)
        vt_ref[j] = v_ref[pl.ds(r, t), :].T.astype(BF16)
        return c
    lax.fori_loop(0, vt_ref.shape[0], body, 0)


def _attn_a_kernel(t5_ref, lq1, lk1, lq2, lk2, g_ref, q_ref, k_ref, v_ref, o_ref,
                   kbf_ref, vt_ref, bias_ref, qt_ref, m_ref, l_ref, acc_ref, *s_refs, lam_init):
    tq, tk = A_TQ, A_TK
    h = pl.program_id(0)
    i = pl.program_id(1)

    @pl.when(i == 0)
    def _():
        _stage_keys_values(k_ref, v_ref, kbf_ref, vt_ref, tk)
        key = lax.broadcasted_iota(jnp.int32, (tk, tq), 0)
        qry = lax.broadcasted_iota(jnp.int32, (tk, tq), 1)
        for n in range(3):
            rel = key + (n - 1) * tk - qry
            lo, hi = (n - 1) * tk - (tq - 1), min(n * tk - 1, CHUNK - 1)
            b = _t5_bias_tile(jnp.minimum(rel, hi), lo, hi, t5_ref, h) * LOG2E
            if n >= 1:
                b = jnp.where((key + (n - 1) * tk) // CHUNK <= qry // CHUNK, b, -jnp.inf)
            bias_ref[n] = b

    qt = (q_ref[...] * (DK_A ** -0.5 * LOG2E)).T
    sub = lax.broadcasted_iota(jnp.int32, (HEAD_W, tq), 0)
    qt_ref[:, :tq] = jnp.where(sub < DK_A, qt, 0.0).astype(BF16)
    qt_ref[:, tq:] = jnp.where(sub >= DK_A, qt, 0.0).astype(BF16)
    m_ref[...] = jnp.full(m_ref.shape, -jnp.inf, F32)
    l_ref[...] = jnp.zeros(l_ref.shape, F32)
    acc_ref[...] = jnp.zeros(acc_ref.shape, F32)

    def scores(j, s_ref):
        r = pl.multiple_of(j * tk, tk)
        s_ref[...] = jnp.dot(kbf_ref[pl.ds(r, tk), :], qt_ref[...], preferred_element_type=F32)

    def update(j, s_ref, bias_idx):
        s = s_ref[...]
        if bias_idx is not None:
            b = bias_ref[bias_idx]
            s = jnp.concatenate([s[:, :tq] + b, s[:, tq:] + b], axis=1)
        _online_softmax_step(s, vt_ref[j], m_ref, l_ref, acc_ref)

    scores(0, s_refs[0])
    scores(1, s_refs[1])

    odd = jnp.logical_and(i >= 2, i % 2 == 0)

    @pl.when(odd)
    def _():
        update(0, s_refs[0], None)
        scores(2, s_refs[0])
        update(1, s_refs[1], None)
        scores(3, s_refs[1])

    j0 = jnp.where(odd, 2, 0)

    def quad(u, c):
        j = j0 + 4 * u
        for n in range(4):
            scores(j + n + 2, s_refs[(n + 2) % 4])
            update(j + n, s_refs[n], None)
        return c
    lax.fori_loop(0, jnp.maximum(i - 1, 0) // 2, quad, 0)

    @pl.when(i >= 1)
    def _():
        j = 2 * i - 2
        scores(j + 2, s_refs[2])
        update(j, s_refs[0], None)
        scores(j + 3, s_refs[3])
        update(j + 1, s_refs[1], 0)
        update(j + 2, s_refs[2], 1)
        update(j + 3, s_refs[3], 2)

    @pl.when(i == 0)
    def _():
        update(0, s_refs[0], 1)
        update(1, s_refs[1], 2)

    inv = 1.0 / l_ref[...]
    acc = acc_ref[...]
    lam = _lambda(lq1, lk1, lq2, lk2, lam_init)
    o = acc[:, :tq] * inv[:, :tq] - lam * (acc[:, tq:] * inv[:, tq:])
    ms = jnp.mean(o * o, axis=0, keepdims=True)
    y = (o * lax.rsqrt(ms + EPS) * g_ref[...]) * (1.0 - lam_init)
    o_ref[...] = y.T


def _attn_a_prompt(proj, t5_bias, lq1, lk1, lq2, lk2, subln_g, lam_init):
    s = proj.shape[1]
    tq, tk = A_TQ, A_TK
    assert s % tq == 0 and tq == 2 * tk and tk % CHUNK == 0 and tk >= T5_MAX_DIST
    vec = lambda: pl.BlockSpec((1, DK_A), lambda h, i: (0, 0))
    return pl.pallas_call(
        functools.partial(_attn_a_kernel, lam_init=lam_init),
        grid=(H_A, s // tq),
        in_specs=[
            pl.BlockSpec(memory_space=pltpu.SMEM),
            vec(), vec(), vec(), vec(),
            pl.BlockSpec((DV_A, 1), lambda h, i: (0, 0)),
            pl.BlockSpec((None, tq, HEAD_W), lambda h, i: (QA, i, h)),
            pl.BlockSpec((None, s, HEAD_W), lambda h, i: (KA, 0, h)),
            pl.BlockSpec((None, s, HEAD_W), lambda h, i: (VA, 0, h)),
        ],
        out_specs=pl.BlockSpec((tq, HEAD_W), lambda h, i: (i, h)),
        out_shape=jax.ShapeDtypeStruct((s, H_A * DV_A), F32),
        scratch_shapes=[
            pltpu.VMEM((s, HEAD_W), BF16),
            pltpu.VMEM((s // tk, DV_A, tk), BF16),
            pltpu.VMEM((3, tk, tq), F32),
            pltpu.VMEM((HEAD_W, 2 * tq), BF16),
            pltpu.VMEM((1, 2 * tq), F32),
            pltpu.VMEM((1, 2 * tq), F32),
            pltpu.VMEM((DV_A, 2 * tq), F32),
        ] + [pltpu.VMEM((tk, 2 * tq), F32)] * 4,
        compiler_params=pltpu.CompilerParams(
            dimension_semantics=("arbitrary", "arbitrary"), vmem_limit_bytes=VMEM_LIMIT),
        name="attn_a_prompt",
    )(t5_bias, lq1.reshape(1, DK_A), lk1.reshape(1, DK_A), lq2.reshape(1, DK_A), lk2.reshape(1, DK_A),
      subln_g.reshape(DV_A, 1), proj, proj, proj)


def _attn_b_kernel(rb_ref, q_ref, k_ref, v_ref, o_ref, kbf_ref, vt_ref, bias_ref, *s_refs):
    t = B_TQ
    blk = REL_CLIP
    nkt = BAND_PAST // t + 1
    nq = q_ref.shape[0] // t
    h = pl.program_id(0)

    _stage_keys_values(k_ref, v_ref, kbf_ref, vt_ref, t)

    kk = lax.broadcasted_iota(jnp.int32, (blk, blk), 0)
    qq = lax.broadcasted_iota(jnp.int32, (blk, blk), 1)
    rel = kk - qq
    lo = jnp.full((blk, blk), rb_ref[h, 0] * LOG2E, F32)
    same = _rel_bias_tile(rel, -(blk - 1), blk - 1, rb_ref, h) * LOG2E
    prev = _rel_bias_tile(rel - blk, -(2 * blk - 1), -1, rb_ref, h) * LOG2E
    ninf = jnp.full((blk, blk), -jnp.inf, F32)
    kc = kk // CHUNK
    qc = qq // CHUNK
    far_blocks = BAND_PAST // blk
    for a in range(nkt * t // blk):
        for b in range(t // blk):
            e = a - b
            if e < 0 or e > far_blocks:
                tile = ninf
            elif e == 0:
                tile = jnp.where(kc >= qc, lo, -jnp.inf)
            elif e == far_blocks:
                tile = jnp.where(kc <= qc, same, -jnp.inf)
            elif e == far_blocks - 1:
                tile = prev
            else:
                tile = lo
            bias_ref[a * blk:(a + 1) * blk, b * blk:(b + 1) * blk] = tile

    def scores(g, nk, s_ref):
        q0 = pl.multiple_of(g * t, t)
        k0 = pl.multiple_of((g - (nk - 1)) * t, t)
        qt = (q_ref[pl.ds(q0, t), :] * (DH_B ** -0.5 * LOG2E)).T.astype(BF16)
        s_ref[(nkt - nk) * t:, :] = jnp.dot(kbf_ref[pl.ds(k0, nk * t), :], qt, preferred_element_type=F32)

    def finish(g, nk, s_ref):
        s = s_ref[(nkt - nk) * t:, :] + bias_ref[(nkt - nk) * t:, :]
        m = jnp.max(s, axis=0, keepdims=True)
        p = jnp.exp2(s - m)
        l = jnp.sum(p, axis=0, keepdims=True)
        pb = p.astype(BF16)
        o = jnp.dot(vt_ref[g - (nk - 1)], pb[:t], preferred_element_type=F32)
        for c in range(1, nk):
            o = o + jnp.dot(vt_ref[g - (nk - 1) + c], pb[c * t:(c + 1) * t], preferred_element_type=F32)
        o_ref[pl.ds(pl.multiple_of(g * t, t), t), :] = (o * (1.0 / l)).T

    first = nkt - 1
    for g in range(first):
        scores(g, g + 1, s_refs[g % 4])
        finish(g, g + 1, s_refs[g % 4])

    assert (nq - first - 2) % 4 == 0
    scores(first, nkt, s_refs[0])
    scores(first + 1, nkt, s_refs[1])

    def body(u, c):
        g = first + 4 * u
        for n in range(4):
            scores(g + n + 2, nkt, s_refs[(n + 2) % 4])
            finish(g + n, nkt, s_refs[n])
        return c
    lax.fori_loop(0, (nq - first - 2) // 4, body, 0)
    finish(nq - 2, nkt, s_refs[0])
    finish(nq - 1, nkt, s_refs[1])


def _attn_b_prompt(proj, rel_bias):
    s = proj.shape[1]
    t = B_TQ
    nkt = BAND_PAST // t + 1
    assert s % t == 0 and BAND_PAST % t == 0 and t % REL_CLIP == 0 and REL_CLIP % CHUNK == 0
    head = lambda c: pl.BlockSpec((None, s, HEAD_W), lambda h: (c, 0, h))
    return pl.pallas_call(
        _attn_b_kernel,
        grid=(H_B,),
        in_specs=[pl.BlockSpec(memory_space=pltpu.SMEM), head(QB), head(KB), head(VB)],
        out_specs=pl.BlockSpec((s, HEAD_W), lambda h: (0, h)),
        out_shape=jax.ShapeDtypeStruct((s, H_B * DH_B), F32),
        scratch_shapes=[
            pltpu.VMEM((s, HEAD_W), BF16),
            pltpu.VMEM((s // t, DH_B, t), BF16),
            pltpu.VMEM((nkt * t, t), F32),
        ] + [pltpu.VMEM((nkt * t, t), F32)] * 4,
        compiler_params=pltpu.CompilerParams(
            dimension_semantics=("arbitrary",), vmem_limit_bytes=VMEM_LIMIT),
        name="attn_b_prompt",
    )(rel_bias, proj, proj, proj)


def _dot_nt(a, b):
    return lax.dot_general(a, b, (((1,), (1,)), ((), ())), preferred_element_type=F32)


def _attn_sample_kernel(t5_ref, rb_ref, lq1, lk1, lq2, lk2, g_ref,
                        qa_ref, ka_ref, va_ref, cka_ref, cva_ref,
                        qb_ref, kb_ref, vb_ref, ckb_ref, cvb_ref,
                        oa_ref, ob_ref, kroll_ref, vroll_ref,
                        ba_ref, bb_ref, *, lam_init, past):
    n = qa_ref.shape[0]
    win = ckb_ref.shape[0]
    near = NEAR
    h = pl.program_id(0)

    @pl.when(pl.program_id(1) == 0)
    def _():
        qry = lax.broadcasted_iota(jnp.int32, (n, near + n), 0)
        key = lax.broadcasted_iota(jnp.int32, (n, near + n), 1)
        rel = key - near - qry
        ba_ref[...] = _t5_bias_tile(rel, -(near + n - 1), n - 1, t5_ref, h)
        bb_ref[...] = _rel_bias_tile(rel, -(near + n - 1), n - 1, rb_ref, h)

    q = qa_ref[...] * (DK_A ** -0.5)
    lane = lax.broadcasted_iota(jnp.int32, (n, HEAD_W), 1)
    q2 = jnp.concatenate([jnp.where(lane < DK_A, q, 0.0), jnp.where(lane >= DK_A, q, 0.0)], axis=0).astype(BF16)
    ba = ba_ref[...]
    ba2 = jnp.concatenate([ba, ba], axis=0)
    s_c = _dot_nt(q2, cka_ref[...].astype(BF16))
    s_c = jnp.concatenate([s_c[:, :past - near], s_c[:, past - near:] + ba2[:, :near]], axis=1)
    s_n = _dot_nt(q2, ka_ref[...].astype(BF16)) + ba2[:, near:]
    m = jnp.maximum(jnp.max(s_c, axis=-1, keepdims=True), jnp.max(s_n, axis=-1, keepdims=True))
    p_c = jnp.exp(s_c - m)
    p_n = jnp.exp(s_n - m)
    l = jnp.sum(p_c, axis=-1, keepdims=True) + jnp.sum(p_n, axis=-1, keepdims=True)
    o2 = (jnp.dot(p_c.astype(BF16), cva_ref[...].astype(BF16), preferred_element_type=F32)
          + jnp.dot(p_n.astype(BF16), va_ref[...].astype(BF16), preferred_element_type=F32)) * (1.0 / l)
    lam = _lambda(lq1, lk1, lq2, lk2, lam_init)
    o = o2[:n] - lam * o2[n:]
    ms = jnp.mean(o * o, axis=-1, keepdims=True)
    oa_ref[...] = (o * lax.rsqrt(ms + EPS) * g_ref[...]) * (1.0 - lam_init)

    qb = qb_ref[...].astype(BF16)
    bb = bb_ref[...]
    s_c = _dot_nt(qb, ckb_ref[...].astype(BF16)) * (DH_B ** -0.5)
    s_c = jnp.concatenate([s_c[:, :win - near] + rb_ref[h, 0], s_c[:, win - near:] + bb[:, :near]], axis=1)
    s_n = _dot_nt(qb, kb_ref[...].astype(BF16)) * (DH_B ** -0.5) + bb[:, near:]
    m = jnp.maximum(jnp.max(s_c, axis=-1, keepdims=True), jnp.max(s_n, axis=-1, keepdims=True))
    p_c = jnp.exp(s_c - m)
    p_n = jnp.exp(s_n - m)
    l = jnp.sum(p_c, axis=-1, keepdims=True) + jnp.sum(p_n, axis=-1, keepdims=True)
    ob_ref[...] = (jnp.dot(p_c.astype(BF16), cvb_ref[...].astype(BF16), preferred_element_type=F32)
                   + jnp.dot(p_n.astype(BF16), vb_ref[...].astype(BF16), preferred_element_type=F32)) * (1.0 / l)

    kroll_ref[:win - n, :] = ckb_ref[n:, :]
    kroll_ref[win - n:, :] = kb_ref[...]
    vroll_ref[:win - n, :] = cvb_ref[n:, :]
    vroll_ref[win - n:, :] = vb_ref[...]


def _attn_sample(proj, ck_a, cv_a, ck_b, cv_b, t5_bias, rel_bias, lq1, lk1, lq2, lk2, subln_g, lam_init):
    nb, past = ck_a.shape[0], ck_a.shape[1]
    win = ck_b.shape[1]
    n = proj.shape[1] // nb
    near = NEAR
    assert past % CHUNK == 0 and n <= CHUNK and win <= BAND_PAST and win <= past
    assert near % HEAD_W == 0 and near <= win and near <= past and n % 8 == 0 and T5_MAX_DIST <= REL_CLIP
    cka = ck_a.reshape(nb, past, H_A * 2 * DK_A)
    cva = cv_a.reshape(nb, past, H_A * DV_A)
    ckb = ck_b.reshape(nb, win, H_B * DH_B)
    cvb = cv_b.reshape(nb, win, H_B * DH_B)
    vec = lambda: pl.BlockSpec((1, DK_A), lambda h, b: (0, 0))
    new = lambda c: pl.BlockSpec((None, n, HEAD_W), lambda h, b: (c, b, h))
    cache = lambda rows: pl.BlockSpec((None, rows, HEAD_W), lambda h, b: (b, 0, h))
    out = pl.BlockSpec((n, HEAD_W), lambda h, b: (b, h))
    return pl.pallas_call(
        functools.partial(_attn_sample_kernel, lam_init=lam_init, past=past),
        grid=(H_A, nb),
        in_specs=[
            pl.BlockSpec(memory_space=pltpu.SMEM),
            pl.BlockSpec(memory_space=pltpu.SMEM),
            vec(), vec(), vec(), vec(),
            pl.BlockSpec((1, DV_A), lambda h, b: (0, 0)),
            new(QA), new(KA), new(VA), cache(past), cache(past),
            new(QB), new(KB), new(VB), cache(win), cache(win),
        ],
        out_specs=[out, out, cache(win), cache(win)],
        out_shape=[
            jax.ShapeDtypeStruct((nb * n, H_A * DV_A), F32),
            jax.ShapeDtypeStruct((nb * n, H_B * DH_B), F32),
            jax.ShapeDtypeStruct((nb, win, H_B * DH_B), F32),
            jax.ShapeDtypeStruct((nb, win, H_B * DH_B), F32),
        ],
        scratch_shapes=[pltpu.VMEM((n, near + n), F32), pltpu.VMEM((n, near + n), F32)],
        compiler_params=pltpu.CompilerParams(
            dimension_semantics=("arbitrary", "arbitrary"), vmem_limit_bytes=VMEM_LIMIT),
        name="attn_sample",
    )(t5_bias, rel_bias, lq1.reshape(1, DK_A), lk1.reshape(1, DK_A), lq2.reshape(1, DK_A),
      lk2.reshape(1, DK_A), subln_g.reshape(1, DV_A),
      proj, proj, proj, cka, cva, proj, proj, proj, ckb, cvb)


def _merge_kernel(x_ref, oa_ref, ob_ref, za_ref, zb_ref, ga0_ref, ga1_ref, gb0_ref, gb1_ref,
                  woa_ref, wob_ref, wout_ref, pg_ref, y_ref):
    za = za_ref[...]
    zb = zb_ref[...]
    a = (oa_ref[...] * (za * _sigmoid(za))).astype(BF16)
    b = (ob_ref[...] * (zb * _sigmoid(zb))).astype(BF16)
    ya = jnp.dot(a, woa_ref[...], preferred_element_type=F32)
    yb = jnp.dot(b, wob_ref[...], preferred_element_type=F32)
    ga = jnp.concatenate([ga0_ref[...], ga1_ref[...]], axis=1)
    gb = jnp.concatenate([gb0_ref[...], gb1_ref[...]], axis=1)
    mix = (_sigmoid(ga) * ya + _sigmoid(gb) * yb).astype(BF16)
    y = jnp.dot(mix, wout_ref[...], preferred_element_type=F32)
    ms = jnp.mean(y * y, axis=-1, keepdims=True)
    y_ref[...] = x_ref[...] + y * lax.rsqrt(ms + EPS) * pg_ref[...]


def _merge(x2d, o_a, o_b, proj, woa, wob, wout, post_g, tm):
    m, d = x2d.shape
    wa = o_a.shape[1]
    wb = o_b.shape[1]
    assert m % tm == 0 and wa == COL_BLOCK and wb == COL_BLOCK and d == 2 * COL_BLOCK
    row = lambda w: pl.BlockSpec((tm, w), lambda i: (i, 0))
    col = lambda c: pl.BlockSpec((None, tm, COL_BLOCK), lambda i: (c, i, 0))
    resident = lambda r, c: pl.BlockSpec((r, c), lambda i: (0, 0), pipeline_mode=pl.Buffered(1))
    return pl.pallas_call(
        _merge_kernel,
        grid=(m // tm,),
        in_specs=[row(d), row(wa), row(wb), col(ZA), col(ZB), col(GA0), col(GA1), col(GB0), col(GB1),
                  resident(wa, d), resident(wb, d), resident(d, d), resident(1, d)],
        out_specs=row(d),
        out_shape=jax.ShapeDtypeStruct((m, d), F32),
        compiler_params=pltpu.CompilerParams(
            dimension_semantics=("arbitrary",), vmem_limit_bytes=VMEM_LIMIT),
        name="merge",
    )(x2d, o_a, o_b, proj, proj, proj, proj, proj, proj, woa, wob, wout, post_g.reshape(1, d))


def kernel(x_prompt, x_sample, cache_k_a, cache_v_a, cache_k_b, cache_v_b, t5_bias, pre_norm, post_norm,
           w_in, lambda_q1, lambda_k1, lambda_q2, lambda_k2, subln_a, rel_bias_b, w_o_a, w_o_b, w_out):
    depth = w_in.shape[0]
    bp, sp, d = x_prompt.shape
    bs, ss, _ = x_sample.shape
    assert bp == 1 and w_in.shape[2] == 12 * COL_BLOCK
    yp = x_prompt.reshape(sp, d)
    ys = x_sample.reshape(bs * ss, d)
    tail = min(BAND_PAST, sp)
    outs = [[] for _ in range(8)]
    for l in range(depth):
        lam_init = 0.8 - 0.6 * math.exp(-0.3 * l)
        w = w_in[l].astype(BF16)
        woa = w_o_a[l].astype(BF16)
        wob = w_o_b[l].astype(BF16)
        wout = w_out[l].astype(BF16)
        lam_args = (lambda_q1[l], lambda_k1[l], lambda_q2[l], lambda_k2[l], subln_a[l], lam_init)

        pp = _in_proj(yp, pre_norm[l], w, tm=1024)
        oa = _attn_a_prompt(pp, t5_bias, *lam_args)
        ob = _attn_b_prompt(pp, rel_bias_b[l])
        yp = _merge(yp, oa, ob, pp, woa, wob, wout, post_norm[l], tm=256)
        outs[0].append(pp[KA].reshape(bp, sp, 2 * H_A, DK_A))
        outs[1].append(pp[VA].reshape(bp, sp, H_A, DV_A))
        outs[2].append(pp[KB, sp - tail:].reshape(bp, tail, H_B, DH_B))
        outs[3].append(pp[VB, sp - tail:].reshape(bp, tail, H_B, DH_B))

        ps = _in_proj(ys, pre_norm[l], w, tm=bs * ss)
        oas, obs, kroll, vroll = _attn_sample(ps, cache_k_a[l], cache_v_a[l], cache_k_b[l], cache_v_b[l],
                                              t5_bias, rel_bias_b[l], *lam_args)
        ys = _merge(ys, oas, obs, ps, woa, wob, wout, post_norm[l], tm=bs * ss)
        outs[4].append(ps[KA].reshape(bs, ss, 2 * H_A, DK_A))
        outs[5].append(ps[VA].reshape(bs, ss, H_A, DV_A))
        outs[6].append(kroll.reshape(bs, -1, H_B, DH_B))
        outs[7].append(vroll.reshape(bs, -1, H_B, DH_B))
    return (yp.reshape(bp, sp, d), ys.reshape(bs, ss, d)) + tuple(jnp.stack(o) for o in outs)
```

```python
import functools
import math

import numpy as np
import jax
import jax.numpy as jnp
from jax import lax
from jax.experimental import pallas as pl
from jax.experimental.pallas import tpu as pltpu

F32 = jnp.float32
BF16 = jnp.bfloat16

CHUNK = 64
H_A = 8
DK_A = 64
DV_A = 2 * DK_A
H_B = 8
DH_B = 128
BAND_CHUNKS = 8
BAND_PAST = BAND_CHUNKS * CHUNK
REL_CLIP = 128
T5_BUCKETS = 32
T5_MAX_DIST = 128
EPS = 1e-6

HEAD_W = 128
COL_BLOCK = 1024
A_TQ, A_TK = 512, 256
B_TQ = 256
LOG2E = math.log2(math.e)
NEAR = max(REL_CLIP, T5_MAX_DIST)
VMEM_LIMIT = 60 * 1024 * 1024

QA, KA, VA, ZA, QB, KB, VB, ZB, GA0, GA1, GB0, GB1 = range(12)


def _t5_bucket_int(rel):
    half = T5_BUCKETS // 2
    max_exact = half // 2
    n = abs(rel)
    ret = half if rel > 0 else 0
    if n < max_exact:
        return ret + n
    assert (T5_MAX_DIST // max_exact) ** 2 == 2 ** (half - max_exact)
    j = 0
    while n * n >= (max_exact * max_exact) * 2 ** (j + 1):
        j += 1
    return ret + min(max_exact + j, half - 1)


def _t5_runs(lo, hi):
    runs = []
    for r in range(lo, hi + 1):
        b = _t5_bucket_int(r)
        if not runs or runs[-1][1] != b:
            runs.append((r, b))
    return runs


T5_FAR_BUCKET = _t5_bucket_int(-T5_MAX_DIST)
assert all(_t5_bucket_int(-n) == T5_FAR_BUCKET for n in range(T5_MAX_DIST, 4 * T5_MAX_DIST))


def _t5_bias_tile(rel, lo, hi, t5_ref, h):
    runs = _t5_runs(lo, hi)
    val = jnp.full(rel.shape, t5_ref[runs[0][1], h], F32)
    for start, b in runs[1:]:
        val = jnp.where(rel >= start, t5_ref[b, h], val)
    return val - t5_ref[T5_FAR_BUCKET, h]


def _rel_bias_tile(rel, lo, hi, rb_ref, h):
    lo = max(lo, -REL_CLIP)
    hi = min(hi, REL_CLIP)
    val = jnp.full(rel.shape, rb_ref[h, lo + REL_CLIP], F32)
    for d in range(lo + 1, hi + 1):
        val = jnp.where(rel >= d, rb_ref[h, d + REL_CLIP], val)
    return val


def _sigmoid(x):
    return 1.0 / (1.0 + jnp.exp(-x))


def _lambda(lq1, lk1, lq2, lk2, lam_init):
    a = jnp.sum(lq1[...] * lk1[...], axis=-1, keepdims=True)
    b = jnp.sum(lq2[...] * lk2[...], axis=-1, keepdims=True)
    return jnp.exp(a) - jnp.exp(b) + lam_init


def _in_proj_kernel(x_ref, g_ref, w_ref, o_ref, h_ref):
    @pl.when(pl.program_id(1) == 0)
    def _():
        x = x_ref[...]
        ms = jnp.mean(x * x, axis=-1, keepdims=True)
        h_ref[...] = (x * lax.rsqrt(ms + EPS) * g_ref[...]).astype(BF16)

    o_ref[...] = jnp.dot(h_ref[...], w_ref[...], preferred_element_type=F32)


def _in_proj(x2d, pre_g, w_bf16, tm):
    m, d = x2d.shape
    n = w_bf16.shape[1]
    assert m % tm == 0 and n % COL_BLOCK == 0
    return pl.pallas_call(
        _in_proj_kernel,
        grid=(m // tm, n // COL_BLOCK),
        in_specs=[
            pl.BlockSpec((tm, d), lambda i, j: (i, 0)),
            pl.BlockSpec((1, d), lambda i, j: (0, 0)),
            pl.BlockSpec((d, COL_BLOCK), lambda i, j: (0, j)),
        ],
        out_specs=pl.BlockSpec((None, tm, COL_BLOCK), lambda i, j: (j, i, 0)),
        out_shape=jax.ShapeDtypeStruct((n // COL_BLOCK, m, COL_BLOCK), F32),
        scratch_shapes=[pltpu.VMEM((tm, d), BF16)],
        compiler_params=pltpu.CompilerParams(
            dimension_semantics=("arbitrary", "arbitrary"), vmem_limit_bytes=VMEM_LIMIT),
        name="in_proj",
    )(x2d, pre_g.reshape(1, d), w_bf16)


def _online_softmax_step(s, vt, m_ref, l_ref, acc_ref):
    m_old = m_ref[...]
    m_new = jnp.maximum(m_old, jnp.max(s, axis=0, keepdims=True))
    alpha = jnp.exp2(m_old - m_new)
    p = jnp.exp2(s - m_new)
    l_ref[...] = alpha * l_ref[...] + jnp.sum(p, axis=0, keepdims=True)
    acc_ref[...] = alpha * acc_ref[...] + jnp.dot(vt, p.astype(BF16), preferred_element_type=F32)
    m_ref[...] = m_new


def _stage_keys_values(k_ref, v_ref, kbf_ref, vt_ref, t):
    def body(j, c):
        r = pl.multiple_of(j * t, t)
        kbf_ref[pl.ds(r, t), :] = k_ref[pl.ds(r, t), :].astype(BF16)
        vt_ref[j] = v_ref[pl.ds(r, t), :].T.astype(BF16)
        return c
    lax.fori_loop(0, vt_ref.shape[0], body, 0)


def _attn_a_kernel(t5_ref, lq1, lk1, lq2, lk2, g_ref, q_ref, k_ref, v_ref, o_ref,
                   kbf_ref, vt_ref, bias_ref, qt_ref, m_ref, l_ref, acc_ref, *s_refs, lam_init):
    tq, tk = A_TQ, A_TK
    h = pl.program_id(0)
    i = pl.program_id(1)

    @pl.when(i == 0)
    def _():
        _stage_keys_values(k_ref, v_ref, kbf_ref, vt_ref, tk)
        key = lax.broadcasted_iota(jnp.int32, (tk, tq), 0)
        qry = lax.broadcasted_iota(jnp.int32, (tk, tq), 1)
        for n in range(3):
            rel = key + (n - 1) * tk - qry
            lo, hi = (n - 1) * tk - (tq - 1), min(n * tk - 1, CHUNK - 1)
            b = _t5_bias_tile(jnp.minimum(rel, hi), lo, hi, t5_ref, h) * LOG2E
            if n >= 1:
                b = jnp.where((key + (n - 1) * tk) // CHUNK <= qry // CHUNK, b, -jnp.inf)
            bias_ref[n] = b

    qt = (q_ref[...] * (DK_A ** -0.5 * LOG2E)).T
    sub = lax.broadcasted_iota(jnp.int32, (HEAD_W, tq), 0)
    qt_ref[:, :tq] = jnp.where(sub < DK_A, qt, 0.0).astype(BF16)
    qt_ref[:, tq:] = jnp.where(sub >= DK_A, qt, 0.0).astype(BF16)
    m_ref[...] = jnp.full(m_ref.shape, -jnp.inf, F32)
    l_ref[...] = jnp.zeros(l_ref.shape, F32)
    acc_ref[...] = jnp.zeros(acc_ref.shape, F32)

    def scores(j, s_ref):
        r = pl.multiple_of(j * tk, tk)
        s_ref[...] = jnp.dot(kbf_ref[pl.ds(r, tk), :], qt_ref[...], preferred_element_type=F32)

    def update(j, s_ref, bias_idx):
        s = s_ref[...]
        if bias_idx is not None:
            b = bias_ref[bias_idx]
            s = jnp.concatenate([s[:, :tq] + b, s[:, tq:] + b], axis=1)
        _online_softmax_step(s, vt_ref[j], m_ref, l_ref, acc_ref)

    scores(0, s_refs[0])
    scores(1, s_refs[1])

    odd = jnp.logical_and(i >= 2, i % 2 == 0)

    @pl.when(odd)
    def _():
        update(0, s_refs[0], None)
        scores(2, s_refs[0])
        update(1, s_refs[1], None)
        scores(3, s_refs[1])

    j0 = jnp.where(odd, 2, 0)

    def quad(u, c):
        j = j0 + 4 * u
        for n in range(4):
            scores(j + n + 2, s_refs[(n + 2) % 4])
            update(j + n, s_refs[n], None)
        return c
    lax.fori_loop(0, jnp.maximum(i - 1, 0) // 2, quad, 0)

    @pl.when(i >= 1)
    def _():
        j = 2 * i - 2
        scores(j + 2, s_refs[2])
        update(j, s_refs[0], None)
        scores(j + 3, s_refs[3])
        update(j + 1, s_refs[1], 0)
        update(j + 2, s_refs[2], 1)
        update(j + 3, s_refs[3], 2)

    @pl.when(i == 0)
    def _():
        update(0, s_refs[0], 1)
        update(1, s_refs[1], 2)

    inv = 1.0 / l_ref[...]
    acc = acc_ref[...]
    lam = _lambda(lq1, lk1, lq2, lk2, lam_init)
    o = acc[:, :tq] * inv[:, :tq] - lam * (acc[:, tq:] * inv[:, tq:])
    ms = jnp.mean(o * o, axis=0, keepdims=True)
    y = (o * lax.rsqrt(ms + EPS) * g_ref[...]) * (1.0 - lam_init)
    o_ref[...] = y.T


def _attn_a_prompt(proj, t5_bias, lq1, lk1, lq2, lk2, subln_g, lam_init):
    s = proj.shape[1]
    tq, tk = A_TQ, A_TK
    assert s % tq == 0 and tq == 2 * tk and tk % CHUNK == 0 and tk >= T5_MAX_DIST
    vec = lambda: pl.BlockSpec((1, DK_A), lambda h, i: (0, 0))
    return pl.pallas_call(
        functools.partial(_attn_a_kernel, lam_init=lam_init),
        grid=(H_A, s // tq),
        in_specs=[
            pl.BlockSpec(memory_space=pltpu.SMEM),
            vec(), vec(), vec(), vec(),
            pl.BlockSpec((DV_A, 1), lambda h, i: (0, 0)),
            pl.BlockSpec((None, tq, HEAD_W), lambda h, i: (QA, i, h)),
            pl.BlockSpec((None, s, HEAD_W), lambda h, i: (KA, 0, h)),
            pl.BlockSpec((None, s, HEAD_W), lambda h, i: (VA, 0, h)),
        ],
        out_specs=pl.BlockSpec((tq, HEAD_W), lambda h, i: (i, h)),
        out_shape=jax.ShapeDtypeStruct((s, H_A * DV_A), F32),
        scratch_shapes=[
            pltpu.VMEM((s, HEAD_W), BF16),
            pltpu.VMEM((s // tk, DV_A, tk), BF16),
            pltpu.VMEM((3, tk, tq), F32),
            pltpu.VMEM((HEAD_W, 2 * tq), BF16),
            pltpu.VMEM((1, 2 * tq), F32),
            pltpu.VMEM((1, 2 * tq), F32),
            pltpu.VMEM((DV_A, 2 * tq), F32),
        ] + [pltpu.VMEM((tk, 2 * tq), F32)] * 4,
        compiler_params=pltpu.CompilerParams(
            dimension_semantics=("arbitrary", "arbitrary"), vmem_limit_bytes=VMEM_LIMIT),
        name="attn_a_prompt",
    )(t5_bias, lq1.reshape(1, DK_A), lk1.reshape(1, DK_A), lq2.reshape(1, DK_A), lk2.reshape(1, DK_A),
      subln_g.reshape(DV_A, 1), proj, proj, proj)


def _attn_b_kernel(rb_ref, q_ref, k_ref, v_ref, o_ref, kbf_ref, vt_ref, bias_ref, *s_refs):
    t = B_TQ
    blk = REL_CLIP
    nkt = BAND_PAST // t + 1
    nq = q_ref.shape[0] // t
    h = pl.program_id(0)

    _stage_keys_values(k_ref, v_ref, kbf_ref, vt_ref, t)

    kk = lax.broadcasted_iota(jnp.int32, (blk, blk), 0)
    qq = lax.broadcasted_iota(jnp.int32, (blk, blk), 1)
    rel = kk - qq
    lo = jnp.full((blk, blk), rb_ref[h, 0] * LOG2E, F32)
    same = _rel_bias_tile(rel, -(blk - 1), blk - 1, rb_ref, h) * LOG2E
    prev = _rel_bias_tile(rel - blk, -(2 * blk - 1), -1, rb_ref, h) * LOG2E
    ninf = jnp.full((blk, blk), -jnp.inf, F32)
    kc = kk // CHUNK
    qc = qq // CHUNK
    far_blocks = BAND_PAST // blk
    for a in range(nkt * t // blk):
        for b in range(t // blk):
            e = a - b
            if e < 0 or e > far_blocks:
                tile = ninf
            elif e == 0:
                tile = jnp.where(kc >= qc, lo, -jnp.inf)
            elif e == far_blocks:
                tile = jnp.where(kc <= qc, same, -jnp.inf)
            elif e == far_blocks - 1:
                tile = prev
            else:
                tile = lo
            bias_ref[a * blk:(a + 1) * blk, b * blk:(b + 1) * blk] = tile

    def scores(g, nk, s_ref):
        q0 = pl.multiple_of(g * t, t)
        k0 = pl.multiple_of((g - (nk - 1)) * t, t)
        qt = (q_ref[pl.ds(q0, t), :] * (DH_B ** -0.5 * LOG2E)).T.astype(BF16)
        s_ref[(nkt - nk) * t:, :] = jnp.dot(kbf_ref[pl.ds(k0, nk * t), :], qt, preferred_element_type=F32)

    def finish(g, nk, s_ref):
        s = s_ref[(nkt - nk) * t:, :] + bias_ref[(nkt - nk) * t:, :]
        m = jnp.max(s, axis=0, keepdims=True)
        p = jnp.exp2(s - m)
        l = jnp.sum(p, axis=0, keepdims=True)
        pb = p.astype(BF16)
        o = jnp.dot(vt_ref[g - (nk - 1)], pb[:t], preferred_element_type=F32)
        for c in range(1, nk):
            o = o + jnp.dot(vt_ref[g - (nk - 1) + c], pb[c * t:(c + 1) * t], preferred_element_type=F32)
        o_ref[pl.ds(pl.multiple_of(g * t, t), t), :] = (o * (1.0 / l)).T

    first = nkt - 1
    for g in range(first):
        scores(g, g + 1, s_refs[g % 4])
        finish(g, g + 1, s_refs[g % 4])

    assert (nq - first - 2) % 4 == 0
    scores(first, nkt, s_refs[0])
    scores(first + 1, nkt, s_refs[1])

    def body(u, c):
        g = first + 4 * u
        for n in range(4):
            scores(g + n + 2, nkt, s_refs[(n + 2) % 4])
            finish(g + n, nkt, s_refs[n])
        return c
    lax.fori_loop(0, (nq - first - 2) // 4, body, 0)
    finish(nq - 2, nkt, s_refs[0])
    finish(nq - 1, nkt, s_refs[1])


def _attn_b_prompt(proj, rel_bias):
    s = proj.shape[1]
    t = B_TQ
    nkt = BAND_PAST // t + 1
    assert s % t == 0 and BAND_PAST % t == 0 and t % REL_CLIP == 0 and REL_CLIP % CHUNK == 0
    head = lambda c: pl.BlockSpec((None, s, HEAD_W), lambda h: (c, 0, h))
    return pl.pallas_call(
        _attn_b_kernel,
        grid=(H_B,),
        in_specs=[pl.BlockSpec(memory_space=pltpu.SMEM), head(QB), head(KB), head(VB)],
        out_specs=pl.BlockSpec((s, HEAD_W), lambda h: (0, h)),
        out_shape=jax.ShapeDtypeStruct((s, H_B * DH_B), F32),
        scratch_shapes=[
            pltpu.VMEM((s, HEAD_W), BF16),
            pltpu.VMEM((s // t, DH_B, t), BF16),
            pltpu.VMEM((nkt * t, t), F32),
        ] + [pltpu.VMEM((nkt * t, t), F32)] * 4,
        compiler_params=pltpu.CompilerParams(
            dimension_semantics=("arbitrary",), vmem_limit_bytes=VMEM_LIMIT),
        name="attn_b_prompt",
    )(rel_bias, proj, proj, proj)


def _dot_nt(a, b):
    return lax.dot_general(a, b, (((1,), (1,)), ((), ())), preferred_element_type=F32)


def _attn_sample_kernel(t5_ref, rb_ref, lq1, lk1, lq2, lk2, g_ref,
                        qa_ref, ka_ref, va_ref, cka_ref, cva_ref,
                        qb_ref, kb_ref, vb_ref, ckb_ref, cvb_ref, kbn_ref, vbn_ref,
                        oa_ref, ob_ref, kroll_ref, vroll_ref,
                        ba_ref, bb_ref, *, lam_init, past, win):
    n = qa_ref.shape[0]
    near = NEAR
    nh = H_A
    h = pl.program_id(1)

    @pl.when(jnp.logical_and(pl.program_id(0) == 0, h == 0))
    def _():
        qry = lax.broadcasted_iota(jnp.int32, (n, near + n), 0)
        key = lax.broadcasted_iota(jnp.int32, (n, near + n), 1)
        rel = key - near - qry
        for hh in range(nh):
            ba_ref[hh] = _t5_bias_tile(rel, -(near + n - 1), n - 1, t5_ref, hh)
            bb_ref[hh] = _rel_bias_tile(rel, -(near + n - 1), n - 1, rb_ref, hh)

    @pl.when(h == 0)
    def _():
        kroll_ref[:(win - n) * nh, :] = ckb_ref[n * nh:, :]
        vroll_ref[:(win - n) * nh, :] = cvb_ref[n * nh:, :]
        for hh in range(nh):
            rows = pl.ds((win - n) * nh + hh, n, stride=nh)
            kroll_ref[rows, :] = kbn_ref[:, hh * HEAD_W:(hh + 1) * HEAD_W]
            vroll_ref[rows, :] = vbn_ref[:, hh * HEAD_W:(hh + 1) * HEAD_W]

    q = qa_ref[...] * (DK_A ** -0.5)
    lane = lax.broadcasted_iota(jnp.int32, (n, HEAD_W), 1)
    q2 = jnp.concatenate([jnp.where(lane < DK_A, q, 0.0), jnp.where(lane >= DK_A, q, 0.0)], axis=0).astype(BF16)
    ba = ba_ref[h]
    ba2 = jnp.concatenate([ba, ba], axis=0)
    s_c = _dot_nt(q2, cka_ref[...].astype(BF16))
    s_c = jnp.concatenate([s_c[:, :past - near], s_c[:, past - near:] + ba2[:, :near]], axis=1)
    s_n = _dot_nt(q2, ka_ref[...].astype(BF16)) + ba2[:, near:]
    m = jnp.maximum(jnp.max(s_c, axis=-1, keepdims=True), jnp.max(s_n, axis=-1, keepdims=True))
    p_c = jnp.exp(s_c - m)
    p_n = jnp.exp(s_n - m)
    l = jnp.sum(p_c, axis=-1, keepdims=True) + jnp.sum(p_n, axis=-1, keepdims=True)
    cva = cva_ref[pl.ds(h, past, stride=nh), :].astype(BF16)
    o2 = (jnp.dot(p_c.astype(BF16), cva, preferred_element_type=F32)
          + jnp.dot(p_n.astype(BF16), va_ref[...].astype(BF16), preferred_element_type=F32)) * (1.0 / l)
    lam = _lambda(lq1, lk1, lq2, lk2, lam_init)
    o = o2[:n] - lam * o2[n:]
    ms = jnp.mean(o * o, axis=-1, keepdims=True)
    oa_ref[...] = (o * lax.rsqrt(ms + EPS) * g_ref[...]) * (1.0 - lam_init)

    qb = qb_ref[...].astype(BF16)
    bb = bb_ref[h]
    ckb = ckb_ref[pl.ds(h, win, stride=nh), :].astype(BF16)
    cvb = cvb_ref[pl.ds(h, win, stride=nh), :].astype(BF16)
    s_c = _dot_nt(qb, ckb) * (DH_B ** -0.5)
    s_c = jnp.concatenate([s_c[:, :win - near] + rb_ref[h, 0], s_c[:, win - near:] + bb[:, :near]], axis=1)
    s_n = _dot_nt(qb, kb_ref[...].astype(BF16)) * (DH_B ** -0.5) + bb[:, near:]
    m = jnp.maximum(jnp.max(s_c, axis=-1, keepdims=True), jnp.max(s_n, axis=-1, keepdims=True))
    p_c = jnp.exp(s_c - m)
    p_n = jnp.exp(s_n - m)
    l = jnp.sum(p_c, axis=-1, keepdims=True) + jnp.sum(p_n, axis=-1, keepdims=True)
    ob_ref[...] = (jnp.dot(p_c.astype(BF16), cvb, preferred_element_type=F32)
                   + jnp.dot(p_n.astype(BF16), vb_ref[...].astype(BF16), preferred_element_type=F32)) * (1.0 / l)


def _attn_sample(proj, ck_a, cv_a, ck_b, cv_b, t5_bias, rel_bias, lq1, lk1, lq2, lk2, subln_g, lam_init):
    nb, past = ck_a.shape[0], ck_a.shape[1]
    win = ck_b.shape[1]
    n = proj.shape[1] // nb
    near = NEAR
    assert past % CHUNK == 0 and n <= CHUNK and win <= BAND_PAST and win <= past
    assert near % HEAD_W == 0 and near <= win and near <= past and n % 8 == 0 and T5_MAX_DIST <= REL_CLIP
    assert H_A == H_B and DV_A == HEAD_W and DH_B == HEAD_W
    cka = ck_a.reshape(nb, past, H_A * 2 * DK_A)
    cva = cv_a.reshape(nb, past * H_A, DV_A)
    ckb = ck_b.reshape(nb, win * H_B, DH_B)
    cvb = cv_b.reshape(nb, win * H_B, DH_B)
    vec = lambda: pl.BlockSpec((1, DK_A), lambda b, h: (0, 0))
    new = lambda c: pl.BlockSpec((None, n, HEAD_W), lambda b, h: (c, b, h))
    new_all = lambda c: pl.BlockSpec((None, n, H_B * DH_B), lambda b, h: (c, b, 0))
    seq = lambda rows: pl.BlockSpec((None, rows, HEAD_W), lambda b, h: (b, 0, 0))
    out = pl.BlockSpec((n, HEAD_W), lambda b, h: (b, h))
    return pl.pallas_call(
        functools.partial(_attn_sample_kernel, lam_init=lam_init, past=past, win=win),
        grid=(nb, H_A),
        in_specs=[
            pl.BlockSpec(memory_space=pltpu.SMEM),
            pl.BlockSpec(memory_space=pltpu.SMEM),
            vec(), vec(), vec(), vec(),
            pl.BlockSpec((1, DV_A), lambda b, h: (0, 0)),
            new(QA), new(KA), new(VA),
            pl.BlockSpec((None, past, HEAD_W), lambda b, h: (b, 0, h)), seq(past * H_A),
            new(QB), new(KB), new(VB), seq(win * H_B), seq(win * H_B), new_all(KB), new_all(VB),
        ],
        out_specs=[out, out, seq(win * H_B), seq(win * H_B)],
        out_shape=[
            jax.ShapeDtypeStruct((nb * n, H_A * DV_A), F32),
            jax.ShapeDtypeStruct((nb * n, H_B * DH_B), F32),
            jax.ShapeDtypeStruct((nb, win * H_B, DH_B), F32),
            jax.ShapeDtypeStruct((nb, win * H_B, DH_B), F32),
        ],
        scratch_shapes=[pltpu.VMEM((H_A, n, near + n), F32), pltpu.VMEM((H_B, n, near + n), F32)],
        compiler_params=pltpu.CompilerParams(
            dimension_semantics=("arbitrary", "arbitrary"), vmem_limit_bytes=VMEM_LIMIT),
        name="attn_sample",
    )(t5_bias, rel_bias, lq1.reshape(1, DK_A), lk1.reshape(1, DK_A), lq2.reshape(1, DK_A),
      lk2.reshape(1, DK_A), subln_g.reshape(1, DV_A),
      proj, proj, proj, cka, cva, proj, proj, proj, ckb, cvb, proj, proj)


def _merge_kernel(x_ref, oa_ref, ob_ref, za_ref, zb_ref, ga0_ref, ga1_ref, gb0_ref, gb1_ref,
                  woa_ref, wob_ref, wout_ref, pg_ref, y_ref):
    za = za_ref[...]
    zb = zb_ref[...]
    a = (oa_ref[...] * (za * _sigmoid(za))).astype(BF16)
    b = (ob_ref[...] * (zb * _sigmoid(zb))).astype(BF16)
    ya = jnp.dot(a, woa_ref[...], preferred_element_type=F32)
    yb = jnp.dot(b, wob_ref[...], preferred_element_type=F32)
    ga = jnp.concatenate([ga0_ref[...], ga1_ref[...]], axis=1)
    gb = jnp.concatenate([gb0_ref[...], gb1_ref[...]], axis=1)
    mix = (_sigmoid(ga) * ya + _sigmoid(gb) * yb).astype(BF16)
    y = jnp.dot(mix, wout_ref[...], preferred_element_type=F32)
    ms = jnp.mean(y * y, axis=-1, keepdims=True)
    y_ref[...] = x_ref[...] + y * lax.rsqrt(ms + EPS) * pg_ref[...]


def _merge(x2d, o_a, o_b, proj, woa, wob, wout, post_g, tm):
    m, d = x2d.shape
    wa = o_a.shape[1]
    wb = o_b.shape[1]
    assert m % tm == 0 and wa == COL_BLOCK and wb == COL_BLOCK and d == 2 * COL_BLOCK
    row = lambda w: pl.BlockSpec((tm, w), lambda i: (i, 0))
    col = lambda c: pl.BlockSpec((None, tm, COL_BLOCK), lambda i: (c, i, 0))
    resident = lambda r, c: pl.BlockSpec((r, c), lambda i: (0, 0), pipeline_mode=pl.Buffered(1))
    return pl.pallas_call(
        _merge_kernel,
        grid=(m // tm,),
        in_specs=[row(d), row(wa), row(wb), col(ZA), col(ZB), col(GA0), col(GA1), col(GB0), col(GB1),
                  resident(wa, d), resident(wb, d), resident(d, d), resident(1, d)],
        out_specs=row(d),
        out_shape=jax.ShapeDtypeStruct((m, d), F32),
        compiler_params=pltpu.CompilerParams(
            dimension_semantics=("arbitrary",), vmem_limit_bytes=VMEM_LIMIT),
        name="merge",
    )(x2d, o_a, o_b, proj, proj, proj, proj, proj, proj, woa, wob, wout, post_g.reshape(1, d))


def kernel(x_prompt, x_sample, cache_k_a, cache_v_a, cache_k_b, cache_v_b, t5_bias, pre_norm, post_norm,
           w_in, lambda_q1, lambda_k1, lambda_q2, lambda_k2, subln_a, rel_bias_b, w_o_a, w_o_b, w_out):
    depth = w_in.shape[0]
    bp, sp, d = x_prompt.shape
    bs, ss, _ = x_sample.shape
    assert bp == 1 and w_in.shape[2] == 12 * COL_BLOCK
    yp = x_prompt.reshape(sp, d)
    ys = x_sample.reshape(bs * ss, d)
    tail = min(BAND_PAST, sp)
    outs = [[] for _ in range(8)]
    for l in range(depth):
        lam_init = 0.8 - 0.6 * math.exp(-0.3 * l)
        w = w_in[l].astype(BF16)
        woa = w_o_a[l].astype(BF16)
        wob = w_o_b[l].astype(BF16)
        wout = w_out[l].astype(BF16)
        lam_args = (lambda_q1[l], lambda_k1[l], lambda_q2[l], lambda_k2[l], subln_a[l], lam_init)

        pp = _in_proj(yp, pre_norm[l], w, tm=1024)
        oa = _attn_a_prompt(pp, t5_bias, *lam_args)
        ob = _attn_b_prompt(pp, rel_bias_b[l])
        yp = _merge(yp, oa, ob, pp, woa, wob, wout, post_norm[l], tm=256)
        outs[0].append(pp[KA].reshape(bp, sp, 2 * H_A, DK_A))
        outs[1].append(pp[VA].reshape(bp, sp, H_A, DV_A))
        outs[2].append(pp[KB, sp - tail:].reshape(bp, tail, H_B, DH_B))
        outs[3].append(pp[VB, sp - tail:].reshape(bp, tail, H_B, DH_B))

        ps = _in_proj(ys, pre_norm[l], w, tm=bs * ss)
        oas, obs, kroll, vroll = _attn_sample(ps, cache_k_a[l], cache_v_a[l], cache_k_b[l], cache_v_b[l],
                                              t5_bias, rel_bias_b[l], *lam_args)
        ys = _merge(ys, oas, obs, ps, woa, wob, wout, post_norm[l], tm=bs * ss)
        outs[4].append(ps[KA].reshape(bs, ss, 2 * H_A, DK_A))
        outs[5].append(ps[VA].reshape(bs, ss, H_A, DV_A))
        outs[6].append(kroll.reshape(bs, -1, H_B, DH_B))
        outs[7].append(vroll.reshape(bs, -1, H_B, DH_B))
    return (yp.reshape(bp, sp, d), ys.reshape(bs, ss, d)) + tuple(jnp.stack(o) for o in outs)
```

```python
import functools
import math

import numpy as np
import jax
import jax.numpy as jnp
from jax import lax
from jax.experimental import pallas as pl
from jax.experimental.pallas import tpu as pltpu

F32 = jnp.float32
BF16 = jnp.bfloat16

CHUNK = 64
H_A = 8
DK_A = 64
DV_A = 2 * DK_A
H_B = 8
DH_B = 128
BAND_CHUNKS = 8
BAND_PAST = BAND_CHUNKS * CHUNK
REL_CLIP = 128
T5_BUCKETS = 32
T5_MAX_DIST = 128
EPS = 1e-6

HEAD_W = 128
COL_BLOCK = 1024
A_TQ, A_TK = 512, 256
B_TQ = 256
LOG2E = math.log2(math.e)
NEAR = max(REL_CLIP, T5_MAX_DIST)
VMEM_LIMIT = 60 * 1024 * 1024

QA, KA, VA, ZA, QB, KB, VB, ZB, GA0, GA1, GB0, GB1 = range(12)


def _t5_bucket_int(rel):
    half = T5_BUCKETS // 2
    max_exact = half // 2
    n = abs(rel)
    ret = half if rel > 0 else 0
    if n < max_exact:
        return ret + n
    assert (T5_MAX_DIST // max_exact) ** 2 == 2 ** (half - max_exact)
    j = 0
    while n * n >= (max_exact * max_exact) * 2 ** (j + 1):
        j += 1
    return ret + min(max_exact + j, half - 1)


def _t5_runs(lo, hi):
    runs = []
    for r in range(lo, hi + 1):
        b = _t5_bucket_int(r)
        if not runs or runs[-1][1] != b:
            runs.append((r, b))
    return runs


T5_FAR_BUCKET = _t5_bucket_int(-T5_MAX_DIST)
assert all(_t5_bucket_int(-n) == T5_FAR_BUCKET for n in range(T5_MAX_DIST, 4 * T5_MAX_DIST))


def _t5_bias_tile(rel, lo, hi, t5_ref, h):
    runs = _t5_runs(lo, hi)
    val = jnp.full(rel.shape, t5_ref[runs[0][1], h], F32)
    for start, b in runs[1:]:
        val = jnp.where(rel >= start, t5_ref[b, h], val)
    return val - t5_ref[T5_FAR_BUCKET, h]


def _rel_bias_tile(rel, lo, hi, rb_ref, h):
    lo = max(lo, -REL_CLIP)
    hi = min(hi, REL_CLIP)
    val = jnp.full(rel.shape, rb_ref[h, lo + REL_CLIP], F32)
    for d in range(lo + 1, hi + 1):
        val = jnp.where(rel >= d, rb_ref[h, d + REL_CLIP], val)
    return val


def _sigmoid(x):
    return 1.0 / (1.0 + jnp.exp(-x))


def _lambda(lq1, lk1, lq2, lk2, lam_init):
    a = jnp.sum(lq1[...] * lk1[...], axis=-1, keepdims=True)
    b = jnp.sum(lq2[...] * lk2[...], axis=-1, keepdims=True)
    return jnp.exp(a) - jnp.exp(b) + lam_init


def _in_proj_kernel(x_ref, g_ref, w_ref, o_ref, h_ref):
    @pl.when(pl.program_id(1) == 0)
    def _():
        x = x_ref[...]
        ms = jnp.mean(x * x, axis=-1, keepdims=True)
        h_ref[...] = (x * lax.rsqrt(ms + EPS) * g_ref[...]).astype(BF16)

    y = jnp.dot(h_ref[...], w_ref[...], preferred_element_type=F32)
    for c in range(o_ref.shape[0]):
        o_ref[c] = y[:, c * COL_BLOCK:(c + 1) * COL_BLOCK]


def _in_proj(x2d, pre_g, w_bf16, tm, blocks_per_step):
    m, d = x2d.shape
    n = w_bf16.shape[1]
    tn = blocks_per_step * COL_BLOCK
    assert m % tm == 0 and n % tn == 0
    return pl.pallas_call(
        _in_proj_kernel,
        grid=(m // tm, n // tn),
        in_specs=[
            pl.BlockSpec((tm, d), lambda i, j: (i, 0)),
            pl.BlockSpec((1, d), lambda i, j: (0, 0)),
            pl.BlockSpec((d, tn), lambda i, j: (0, j)),
        ],
        out_specs=pl.BlockSpec((blocks_per_step, tm, COL_BLOCK), lambda i, j: (j, i, 0)),
        out_shape=jax.ShapeDtypeStruct((n // COL_BLOCK, m, COL_BLOCK), F32),
        scratch_shapes=[pltpu.VMEM((tm, d), BF16)],
        compiler_params=pltpu.CompilerParams(
            dimension_semantics=("arbitrary", "arbitrary"), vmem_limit_bytes=VMEM_LIMIT),
        name="in_proj",
    )(x2d, pre_g.reshape(1, d), w_bf16)


def _online_softmax_step(s, vt, m_ref, l_ref, acc_ref):
    m_old = m_ref[...]
    m_new = jnp.maximum(m_old, jnp.max(s, axis=0, keepdims=True))
    alpha = jnp.exp2(m_old - m_new)
    p = jnp.exp2(s - m_new)
    l_ref[...] = alpha * l_ref[...] + jnp.sum(p, axis=0, keepdims=True)
    acc_ref[...] = alpha * acc_ref[...] + jnp.dot(vt, p.astype(BF16), preferred_element_type=F32)
    m_ref[...] = m_new


def _stage_keys_values(k_ref, v_ref, kbf_ref, vt_ref, t):
    def body(j, c):
        r = pl.multiple_of(j * t, t)
        kbf_ref[pl.ds(r, t), :] = k_ref[pl.ds(r, t), :].astype(BF16)
        vt_ref[j] = v_ref[pl.ds(r, t), :].T.astype(BF16)
        return c
    lax.fori_loop(0, vt_ref.shape[0], body, 0, unroll=4)


def _attn_a_kernel(t5_ref, lq1, lk1, lq2, lk2, g_ref, q_ref, k_ref, v_ref, o_ref,
                   kbf_ref, vt_ref, bias_ref, qt_ref, m_ref, l_ref, acc_ref, *s_refs, lam_init):
    tq, tk = A_TQ, A_TK
    h = pl.program_id(0)
    i = pl.program_id(1)

    @pl.when(i == 0)
    def _():
        _stage_keys_values(k_ref, v_ref, kbf_ref, vt_ref, tk)
        key = lax.broadcasted_iota(jnp.int32, (tk, tq), 0)
        qry = lax.broadcasted_iota(jnp.int32, (tk, tq), 1)
        for n in range(3):
            rel = key + (n - 1) * tk - qry
            lo, hi = (n - 1) * tk - (tq - 1), min(n * tk - 1, CHUNK - 1)
            b = _t5_bias_tile(jnp.minimum(rel, hi), lo, hi, t5_ref, h) * LOG2E
            if n >= 1:
                b = jnp.where((key + (n - 1) * tk) // CHUNK <= qry // CHUNK, b, -jnp.inf)
            bias_ref[n] = b

    qt = (q_ref[...] * (DK_A ** -0.5 * LOG2E)).T
    sub = lax.broadcasted_iota(jnp.int32, (HEAD_W, tq), 0)
    qt_ref[:, :tq] = jnp.where(sub < DK_A, qt, 0.0).astype(BF16)
    qt_ref[:, tq:] = jnp.where(sub >= DK_A, qt, 0.0).astype(BF16)
    m_ref[...] = jnp.full(m_ref.shape, -jnp.inf, F32)
    l_ref[...] = jnp.zeros(l_ref.shape, F32)
    acc_ref[...] = jnp.zeros(acc_ref.shape, F32)

    def scores(j, s_ref):
        r = pl.multiple_of(j * tk, tk)
        s_ref[...] = jnp.dot(kbf_ref[pl.ds(r, tk), :], qt_ref[...], preferred_element_type=F32)

    def update(j, s_ref, bias_idx):
        s = s_ref[...]
        if bias_idx is not None:
            b = bias_ref[bias_idx]
            s = jnp.concatenate([s[:, :tq] + b, s[:, tq:] + b], axis=1)
        _online_softmax_step(s, vt_ref[j], m_ref, l_ref, acc_ref)

    scores(0, s_refs[0])
    scores(1, s_refs[1])

    odd = jnp.logical_and(i >= 2, i % 2 == 0)

    @pl.when(odd)
    def _():
        update(0, s_refs[0], None)
        scores(2, s_refs[0])
        update(1, s_refs[1], None)
        scores(3, s_refs[1])

    j0 = jnp.where(odd, 2, 0)

    def quad(u, c):
        j = j0 + 4 * u
        for n in range(4):
            scores(j + n + 2, s_refs[(n + 2) % 4])
            update(j + n, s_refs[n], None)
        return c
    lax.fori_loop(0, jnp.maximum(i - 1, 0) // 2, quad, 0)

    @pl.when(i >= 1)
    def _():
        j = 2 * i - 2
        scores(j + 2, s_refs[2])
        update(j, s_refs[0], None)
        scores(j + 3, s_refs[3])
        update(j + 1, s_refs[1], 0)
        update(j + 2, s_refs[2], 1)
        update(j + 3, s_refs[3], 2)

    @pl.when(i == 0)
    def _():
        update(0, s_refs[0], 1)
        update(1, s_refs[1], 2)

    inv = 1.0 / l_ref[...]
    acc = acc_ref[...]
    lam = _lambda(lq1, lk1, lq2, lk2, lam_init)
    o = acc[:, :tq] * inv[:, :tq] - lam * (acc[:, tq:] * inv[:, tq:])
    ms = jnp.mean(o * o, axis=0, keepdims=True)
    y = (o * lax.rsqrt(ms + EPS) * g_ref[...]) * (1.0 - lam_init)
    o_ref[...] = y.T


def _attn_a_prompt(proj, t5_bias, lq1, lk1, lq2, lk2, subln_g, lam_init):
    s = proj.shape[1]
    tq, tk = A_TQ, A_TK
    assert s % tq == 0 and tq == 2 * tk and tk % CHUNK == 0 and tk >= T5_MAX_DIST
    vec = lambda: pl.BlockSpec((1, DK_A), lambda h, i: (0, 0))
    return pl.pallas_call(
        functools.partial(_attn_a_kernel, lam_init=lam_init),
        grid=(H_A, s // tq),
        in_specs=[
            pl.BlockSpec(memory_space=pltpu.SMEM),
            vec(), vec(), vec(), vec(),
            pl.BlockSpec((DV_A, 1), lambda h, i: (0, 0)),
            pl.BlockSpec((None, tq, HEAD_W), lambda h, i: (QA, i, h)),
            pl.BlockSpec((None, s, HEAD_W), lambda h, i: (KA, 0, h)),
            pl.BlockSpec((None, s, HEAD_W), lambda h, i: (VA, 0, h)),
        ],
        out_specs=pl.BlockSpec((tq, HEAD_W), lambda h, i: (i, h)),
        out_shape=jax.ShapeDtypeStruct((s, H_A * DV_A), F32),
        scratch_shapes=[
            pltpu.VMEM((s, HEAD_W), BF16),
            pltpu.VMEM((s // tk, DV_A, tk), BF16),
            pltpu.VMEM((3, tk, tq), F32),
            pltpu.VMEM((HEAD_W, 2 * tq), BF16),
            pltpu.VMEM((1, 2 * tq), F32),
            pltpu.VMEM((1, 2 * tq), F32),
            pltpu.VMEM((DV_A, 2 * tq), F32),
        ] + [pltpu.VMEM((tk, 2 * tq), F32)] * 4,
        compiler_params=pltpu.CompilerParams(
            dimension_semantics=("arbitrary", "arbitrary"), vmem_limit_bytes=VMEM_LIMIT),
        name="attn_a_prompt",
    )(t5_bias, lq1.reshape(1, DK_A), lk1.reshape(1, DK_A), lq2.reshape(1, DK_A), lk2.reshape(1, DK_A),
      subln_g.reshape(DV_A, 1), proj, proj, proj)


def _attn_b_kernel(rb_ref, q_ref, k_ref, v_ref, o_ref, kbf_ref, vt_ref, bias_ref, *s_refs):
    t = B_TQ
    blk = REL_CLIP
    nkt = BAND_PAST // t + 1
    nq = q_ref.shape[0] // t
    h = pl.program_id(0)

    _stage_keys_values(k_ref, v_ref, kbf_ref, vt_ref, t)

    kk = lax.broadcasted_iota(jnp.int32, (blk, blk), 0)
    qq = lax.broadcasted_iota(jnp.int32, (blk, blk), 1)
    rel = kk - qq
    lo = jnp.full((blk, blk), rb_ref[h, 0] * LOG2E, F32)
    same = _rel_bias_tile(rel, -(blk - 1), blk - 1, rb_ref, h) * LOG2E
    prev = _rel_bias_tile(rel - blk, -(2 * blk - 1), -1, rb_ref, h) * LOG2E
    ninf = jnp.full((blk, blk), -jnp.inf, F32)
    kc = kk // CHUNK
    qc = qq // CHUNK
    far_blocks = BAND_PAST // blk
    for a in range(nkt * t // blk):
        for b in range(t // blk):
            e = a - b
            if e < 0 or e > far_blocks:
                tile = ninf
            elif e == 0:
                tile = jnp.where(kc >= qc, lo, -jnp.inf)
            elif e == far_blocks:
                tile = jnp.where(kc <= qc, same, -jnp.inf)
            elif e == far_blocks - 1:
                tile = prev
            else:
                tile = lo
            bias_ref[a * blk:(a + 1) * blk, b * blk:(b + 1) * blk] = tile

    def scores(g, nk, s_ref):
        q0 = pl.multiple_of(g * t, t)
        k0 = pl.multiple_of((g - (nk - 1)) * t, t)
        qt = (q_ref[pl.ds(q0, t), :] * (DH_B ** -0.5 * LOG2E)).T.astype(BF16)
        s_ref[(nkt - nk) * t:, :] = jnp.dot(kbf_ref[pl.ds(k0, nk * t), :], qt, preferred_element_type=F32)

    def finish(g, nk, s_ref):
        s = s_ref[(nkt - nk) * t:, :] + bias_ref[(nkt - nk) * t:, :]
        m = jnp.max(s, axis=0, keepdims=True)
        p = jnp.exp2(s - m)
        l = jnp.sum(p, axis=0, keepdims=True)
        pb = p.astype(BF16)
        o = jnp.dot(vt_ref[g - (nk - 1)], pb[:t], preferred_element_type=F32)
        for c in range(1, nk):
            o = o + jnp.dot(vt_ref[g - (nk - 1) + c], pb[c * t:(c + 1) * t], preferred_element_type=F32)
        o_ref[pl.ds(pl.multiple_of(g * t, t), t), :] = (o * (1.0 / l)).T

    first = nkt - 1
    for g in range(first):
        scores(g, g + 1, s_refs[g % 4])
        finish(g, g + 1, s_refs[g % 4])

    assert (nq - first - 2) % 4 == 0
    scores(first, nkt, s_refs[0])
    scores(first + 1, nkt, s_refs[1])

    def body(u, c):
        g = first + 4 * u
        for n in range(4):
            scores(g + n + 2, nkt, s_refs[(n + 2) % 4])
            finish(g + n, nkt, s_refs[n])
        return c
    lax.fori_loop(0, (nq - first - 2) // 4, body, 0)
    finish(nq - 2, nkt, s_refs[0])
    finish(nq - 1, nkt, s_refs[1])


def _attn_b_prompt(proj, rel_bias):
    s = proj.shape[1]
    t = B_TQ
    nkt = BAND_PAST // t + 1
    assert s % t == 0 and BAND_PAST % t == 0 and t % REL_CLIP == 0 and REL_CLIP % CHUNK == 0
    head = lambda c: pl.BlockSpec((None, s, HEAD_W), lambda h: (c, 0, h))
    return pl.pallas_call(
        _attn_b_kernel,
        grid=(H_B,),
        in_specs=[pl.BlockSpec(memory_space=pltpu.SMEM), head(QB), head(KB), head(VB)],
        out_specs=pl.BlockSpec((s, HEAD_W), lambda h: (0, h)),
        out_shape=jax.ShapeDtypeStruct((s, H_B * DH_B), F32),
        scratch_shapes=[
            pltpu.VMEM((s, HEAD_W), BF16),
            pltpu.VMEM((s // t, DH_B, t), BF16),
            pltpu.VMEM((nkt * t, t), F32),
        ] + [pltpu.VMEM((nkt * t, t), F32)] * 4,
        compiler_params=pltpu.CompilerParams(
            dimension_semantics=("arbitrary",), vmem_limit_bytes=VMEM_LIMIT),
        name="attn_b_prompt",
    )(rel_bias, proj, proj, proj)


def _dot_nt(a, b):
    return lax.dot_general(a, b, (((1,), (1,)), ((), ())), preferred_element_type=F32)


def _attn_sample_kernel(t5_ref, rb_ref, lq1, lk1, lq2, lk2, g_ref,
                        qa_ref, ka_ref, va_ref, cka_ref, cva_ref,
                        qb_ref, kb_ref, vb_ref, ckb_ref, cvb_ref,
                        oa_ref, ob_ref, kroll_ref, vroll_ref,
                        ba_ref, bb_ref, *, lam_init, past, win):
    n = qa_ref.shape[0]
    near = NEAR
    nh = H_A

    @pl.when(pl.program_id(0) == 0)
    def _():
        qry = lax.broadcasted_iota(jnp.int32, (n, near + n), 0)
        key = lax.broadcasted_iota(jnp.int32, (n, near + n), 1)
        rel = key - near - qry
        for h in range(nh):
            ba_ref[h] = _t5_bias_tile(rel, -(near + n - 1), n - 1, t5_ref, h)
            bb_ref[h] = _rel_bias_tile(rel, -(near + n - 1), n - 1, rb_ref, h)

    kroll_ref[:(win - n) * nh, :] = ckb_ref[n * nh:, :]
    vroll_ref[:(win - n) * nh, :] = cvb_ref[n * nh:, :]

    lam = _lambda(lq1, lk1, lq2, lk2, lam_init)
    lane = lax.broadcasted_iota(jnp.int32, (n, HEAD_W), 1)
    for h in range(nh):
        cols = slice(h * HEAD_W, (h + 1) * HEAD_W)
        new_rows = pl.ds((win - n) * nh + h, n, stride=nh)
        kroll_ref[new_rows, :] = kb_ref[:, cols]
        vroll_ref[new_rows, :] = vb_ref[:, cols]

        q = qa_ref[:, cols] * (DK_A ** -0.5)
        q2 = jnp.concatenate([jnp.where(lane < DK_A, q, 0.0), jnp.where(lane >= DK_A, q, 0.0)],
                             axis=0).astype(BF16)
        ba = ba_ref[h]
        ba2 = jnp.concatenate([ba, ba], axis=0)
        s_c = _dot_nt(q2, cka_ref[:, cols].astype(BF16))
        s_c = jnp.concatenate([s_c[:, :past - near], s_c[:, past - near:] + ba2[:, :near]], axis=1)
        s_n = _dot_nt(q2, ka_ref[:, cols].astype(BF16)) + ba2[:, near:]
        m = jnp.maximum(jnp.max(s_c, axis=-1, keepdims=True), jnp.max(s_n, axis=-1, keepdims=True))
        p_c = jnp.exp(s_c - m)
        p_n = jnp.exp(s_n - m)
        l = jnp.sum(p_c, axis=-1, keepdims=True) + jnp.sum(p_n, axis=-1, keepdims=True)
        cva = cva_ref[pl.ds(h, past, stride=nh), :].astype(BF16)
        o2 = (jnp.dot(p_c.astype(BF16), cva, preferred_element_type=F32)
              + jnp.dot(p_n.astype(BF16), va_ref[:, cols].astype(BF16), preferred_element_type=F32)) * (1.0 / l)
        o = o2[:n] - lam * o2[n:]
        ms = jnp.mean(o * o, axis=-1, keepdims=True)
        oa_ref[:, cols] = (o * lax.rsqrt(ms + EPS) * g_ref[...]) * (1.0 - lam_init)

        qb = qb_ref[:, cols].astype(BF16)
        bb = bb_ref[h]
        ckb = ckb_ref[pl.ds(h, win, stride=nh), :].astype(BF16)
        cvb = cvb_ref[pl.ds(h, win, stride=nh), :].astype(BF16)
        s_c = _dot_nt(qb, ckb) * (DH_B ** -0.5)
        s_c = jnp.concatenate([s_c[:, :win - near] + rb_ref[h, 0], s_c[:, win - near:] + bb[:, :near]], axis=1)
        s_n = _dot_nt(qb, kb_ref[:, cols].astype(BF16)) * (DH_B ** -0.5) + bb[:, near:]
        m = jnp.maximum(jnp.max(s_c, axis=-1, keepdims=True), jnp.max(s_n, axis=-1, keepdims=True))
        p_c = jnp.exp(s_c - m)
        p_n = jnp.exp(s_n - m)
        l = jnp.sum(p_c, axis=-1, keepdims=True) + jnp.sum(p_n, axis=-1, keepdims=True)
        ob_ref[:, cols] = (jnp.dot(p_c.astype(BF16), cvb, preferred_element_type=F32)
                           + jnp.dot(p_n.astype(BF16), vb_ref[:, cols].astype(BF16),
                                     preferred_element_type=F32)) * (1.0 / l)


def _attn_sample(proj, ck_a, cv_a, ck_b, cv_b, t5_bias, rel_bias, lq1, lk1, lq2, lk2, subln_g, lam_init):
    nb, past = ck_a.shape[0], ck_a.shape[1]
    win = ck_b.shape[1]
    n = proj.shape[1] // nb
    near = NEAR
    wide = H_A * HEAD_W
    assert past % CHUNK == 0 and n <= CHUNK and win <= BAND_PAST and win <= past
    assert near % HEAD_W == 0 and near <= win and near <= past and n % 8 == 0 and T5_MAX_DIST <= REL_CLIP
    assert H_A == H_B and DV_A == HEAD_W and DH_B == HEAD_W and 2 * DK_A == HEAD_W
    cka = ck_a.reshape(nb, past, wide)
    cva = cv_a.reshape(nb, past * H_A, DV_A)
    ckb = ck_b.reshape(nb, win * H_B, DH_B)
    cvb = cv_b.reshape(nb, win * H_B, DH_B)
    vec = lambda: pl.BlockSpec((1, DK_A), lambda b: (0, 0))
    new = lambda c: pl.BlockSpec((None, n, wide), lambda b: (c, b, 0))
    seq = lambda rows, cols: pl.BlockSpec((None, rows, cols), lambda b: (b, 0, 0))
    out = pl.BlockSpec((n, wide), lambda b: (b, 0))
    return pl.pallas_call(
        functools.partial(_attn_sample_kernel, lam_init=lam_init, past=past, win=win),
        grid=(nb,),
        in_specs=[
            pl.BlockSpec(memory_space=pltpu.SMEM),
            pl.BlockSpec(memory_space=pltpu.SMEM),
            vec(), vec(), vec(), vec(),
            pl.BlockSpec((1, DV_A), lambda b: (0, 0)),
            new(QA), new(KA), new(VA), seq(past, wide), seq(past * H_A, HEAD_W),
            new(QB), new(KB), new(VB), seq(win * H_B, HEAD_W), seq(win * H_B, HEAD_W),
        ],
        out_specs=[out, out, seq(win * H_B, HEAD_W), seq(win * H_B, HEAD_W)],
        out_shape=[
            jax.ShapeDtypeStruct((nb * n, wide), F32),
            jax.ShapeDtypeStruct((nb * n, wide), F32),
            jax.ShapeDtypeStruct((nb, win * H_B, DH_B), F32),
            jax.ShapeDtypeStruct((nb, win * H_B, DH_B), F32),
        ],
        scratch_shapes=[pltpu.VMEM((H_A, n, near + n), F32), pltpu.VMEM((H_B, n, near + n), F32)],
        compiler_params=pltpu.CompilerParams(
            dimension_semantics=("arbitrary",), vmem_limit_bytes=VMEM_LIMIT),
        name="attn_sample",
    )(t5_bias, rel_bias, lq1.reshape(1, DK_A), lk1.reshape(1, DK_A), lq2.reshape(1, DK_A),
      lk2.reshape(1, DK_A), subln_g.reshape(1, DV_A),
      proj, proj, proj, cka, cva, proj, proj, proj, ckb, cvb)


def _merge_kernel(x_ref, oa_ref, ob_ref, za_ref, zb_ref, ga0_ref, ga1_ref, gb0_ref, gb1_ref,
                  woa_ref, wob_ref, wout_ref, pg_ref, y_ref):
    za = za_ref[...]
    zb = zb_ref[...]
    a = (oa_ref[...] * (za * _sigmoid(za))).astype(BF16)
    b = (ob_ref[...] * (zb * _sigmoid(zb))).astype(BF16)
    ya = jnp.dot(a, woa_ref[...], preferred_element_type=F32)
    yb = jnp.dot(b, wob_ref[...], preferred_element_type=F32)
    ga = jnp.concatenate([ga0_ref[...], ga1_ref[...]], axis=1)
    gb = jnp.concatenate([gb0_ref[...], gb1_ref[...]], axis=1)
    mix = (_sigmoid(ga) * ya + _sigmoid(gb) * yb).astype(BF16)
    y = jnp.dot(mix, wout_ref[...], preferred_element_type=F32)
    ms = jnp.mean(y * y, axis=-1, keepdims=True)
    y_ref[...] = x_ref[...] + y * lax.rsqrt(ms + EPS) * pg_ref[...]


def _merge(x2d, o_a, o_b, proj, woa, wob, wout, post_g, tm):
    m, d = x2d.shape
    wa = o_a.shape[1]
    wb = o_b.shape[1]
    assert m % tm == 0 and wa == COL_BLOCK and wb == COL_BLOCK and d == 2 * COL_BLOCK
    row = lambda w: pl.BlockSpec((tm, w), lambda i: (i, 0))
    col = lambda c: pl.BlockSpec((None, tm, COL_BLOCK), lambda i: (c, i, 0))
    resident = lambda r, c: pl.BlockSpec((r, c), lambda i: (0, 0), pipeline_mode=pl.Buffered(1))
    return pl.pallas_call(
        _merge_kernel,
        grid=(m // tm,),
        in_specs=[row(d), row(wa), row(wb), col(ZA), col(ZB), col(GA0), col(GA1), col(GB0), col(GB1),
                  resident(wa, d), resident(wb, d), resident(d, d), resident(1, d)],
        out_specs=row(d),
        out_shape=jax.ShapeDtypeStruct((m, d), F32),
        compiler_params=pltpu.CompilerParams(
            dimension_semantics=("arbitrary",), vmem_limit_bytes=VMEM_LIMIT),
        name="merge",
    )(x2d, o_a, o_b, proj, proj, proj, proj, proj, proj, woa, wob, wout, post_g.reshape(1, d))


def kernel(x_prompt, x_sample, cache_k_a, cache_v_a, cache_k_b, cache_v_b, t5_bias, pre_norm, post_norm,
           w_in, lambda_q1, lambda_k1, lambda_q2, lambda_k2, subln_a, rel_bias_b, w_o_a, w_o_b, w_out):
    depth = w_in.shape[0]
    bp, sp, d = x_prompt.shape
    bs, ss, _ = x_sample.shape
    assert bp == 1 and w_in.shape[2] == 12 * COL_BLOCK
    yp = x_prompt.reshape(sp, d)
    ys = x_sample.reshape(bs * ss, d)
    tail = min(BAND_PAST, sp)
    outs = [[] for _ in range(8)]
    for l in range(depth):
        lam_init = 0.8 - 0.6 * math.exp(-0.3 * l)
        w = w_in[l].astype(BF16)
        woa = w_o_a[l].astype(BF16)
        wob = w_o_b[l].astype(BF16)
        wout = w_out[l].astype(BF16)
        lam_args = (lambda_q1[l], lambda_k1[l], lambda_q2[l], lambda_k2[l], subln_a[l], lam_init)

        pp = _in_proj(yp, pre_norm[l], w, tm=1024, blocks_per_step=1)
        oa = _attn_a_prompt(pp, t5_bias, *lam_args)
        ob = _attn_b_prompt(pp, rel_bias_b[l])
        yp = _merge(yp, oa, ob, pp, woa, wob, wout, post_norm[l], tm=256)
        outs[0].append(pp[KA].reshape(bp, sp, 2 * H_A, DK_A))
        outs[1].append(pp[VA].reshape(bp, sp, H_A, DV_A))
        outs[2].append(pp[KB, sp - tail:].reshape(bp, tail, H_B, DH_B))
        outs[3].append(pp[VB, sp - tail:].reshape(bp, tail, H_B, DH_B))

        ps = _in_proj(ys, pre_norm[l], w, tm=bs * ss, blocks_per_step=3)
        oas, obs, kroll, vroll = _attn_sample(ps, cache_k_a[l], cache_v_a[l], cache_k_b[l], cache_v_b[l],
                                              t5_bias, rel_bias_b[l], *lam_args)
        ys = _merge(ys, oas, obs, ps, woa, wob, wout, post_norm[l], tm=bs * ss)
        outs[4].append(ps[KA].reshape(bs, ss, 2 * H_A, DK_A))
        outs[5].append(ps[VA].reshape(bs, ss, H_A, DV_A))
        outs[6].append(kroll.reshape(bs, -1, H_B, DH_B))
        outs[7].append(vroll.reshape(bs, -1, H_B, DH_B))
    return (yp.reshape(bp, sp, d), ys.reshape(bs, ss, d)) + tuple(jnp.stack(o) for o in outs)
```

```python
import functools
import math

import numpy as np
import jax
import jax.numpy as jnp
from jax import lax
from jax.experimental import pallas as pl
from jax.experimental.pallas import tpu as pltpu

F32 = jnp.float32
BF16 = jnp.bfloat16

CHUNK = 64
H_A = 8
DK_A = 64
DV_A = 2 * DK_A
H_B = 8
DH_B = 128
BAND_CHUNKS = 8
BAND_PAST = BAND_CHUNKS * CHUNK
REL_CLIP = 128
T5_BUCKETS = 32
T5_MAX_DIST = 128
EPS = 1e-6

HEAD_W = 128
COL_BLOCK = 1024
A_TQ, A_TK = 512, 256
B_TQ = 256
LOG2E = math.log2(math.e)
NEAR = max(REL_CLIP, T5_MAX_DIST)
VMEM_LIMIT = 60 * 1024 * 1024

QA, KA, VA, ZA, QB, KB, VB, ZB, GA0, GA1, GB0, GB1 = range(12)


def _t5_bucket_int(rel):
    half = T5_BUCKETS // 2
    max_exact = half // 2
    n = abs(rel)
    ret = half if rel > 0 else 0
    if n < max_exact:
        return ret + n
    assert (T5_MAX_DIST // max_exact) ** 2 == 2 ** (half - max_exact)
    j = 0
    while n * n >= (max_exact * max_exact) * 2 ** (j + 1):
        j += 1
    return ret + min(max_exact + j, half - 1)


def _t5_runs(lo, hi):
    runs = []
    for r in range(lo, hi + 1):
        b = _t5_bucket_int(r)
        if not runs or runs[-1][1] != b:
            runs.append((r, b))
    return runs


T5_FAR_BUCKET = _t5_bucket_int(-T5_MAX_DIST)
assert all(_t5_bucket_int(-n) == T5_FAR_BUCKET for n in range(T5_MAX_DIST, 4 * T5_MAX_DIST))


def _t5_bias_tile(rel, lo, hi, t5_ref, h):
    runs = _t5_runs(lo, hi)
    val = jnp.full(rel.shape, t5_ref[runs[0][1], h], F32)
    for start, b in runs[1:]:
        val = jnp.where(rel >= start, t5_ref[b, h], val)
    return val - t5_ref[T5_FAR_BUCKET, h]


def _rel_bias_tile(rel, lo, hi, rb_ref, h):
    lo = max(lo, -REL_CLIP)
    hi = min(hi, REL_CLIP)
    val = jnp.full(rel.shape, rb_ref[h, lo + REL_CLIP], F32)
    for d in range(lo + 1, hi + 1):
        val = jnp.where(rel >= d, rb_ref[h, d + REL_CLIP], val)
    return val


def _sigmoid(x):
    return 1.0 / (1.0 + jnp.exp(-x))


def _lambda(lq1, lk1, lq2, lk2, lam_init):
    a = jnp.sum(lq1[...] * lk1[...], axis=-1, keepdims=True)
    b = jnp.sum(lq2[...] * lk2[...], axis=-1, keepdims=True)
    return jnp.exp(a) - jnp.exp(b) + lam_init


def _in_proj_kernel(x_ref, g_ref, w_ref, o_ref, ka_ref, va_ref, h_ref):
    j = pl.program_id(1)

    @pl.when(j == 0)
    def _():
        x = x_ref[...]
        ms = jnp.mean(x * x, axis=-1, keepdims=True)
        h_ref[...] = (x * lax.rsqrt(ms + EPS) * g_ref[...]).astype(BF16)

    y = jnp.dot(h_ref[...], w_ref[...], preferred_element_type=F32)
    per_step = o_ref.shape[0]
    for c in range(per_step):
        o_ref[c] = y[:, c * COL_BLOCK:(c + 1) * COL_BLOCK]
    for stream, ref in ((KA, ka_ref), (VA, va_ref)):
        c = stream % per_step

        @pl.when(j == stream // per_step)
        def _():
            ref[...] = y[:, c * COL_BLOCK:(c + 1) * COL_BLOCK]


def _in_proj(x2d, pre_g, w_bf16, tm, blocks_per_step):
    m, d = x2d.shape
    n = w_bf16.shape[1]
    tn = blocks_per_step * COL_BLOCK
    assert m % tm == 0 and n % tn == 0

    def own_copy(stream):
        first = stream // blocks_per_step
        return pl.BlockSpec((tm, COL_BLOCK), lambda i, j: (jnp.where(j >= first, i, jnp.maximum(i - 1, 0)), 0))

    return pl.pallas_call(
        _in_proj_kernel,
        grid=(m // tm, n // tn),
        in_specs=[
            pl.BlockSpec((tm, d), lambda i, j: (i, 0)),
            pl.BlockSpec((1, d), lambda i, j: (0, 0)),
            pl.BlockSpec((d, tn), lambda i, j: (0, j)),
        ],
        out_specs=[pl.BlockSpec((blocks_per_step, tm, COL_BLOCK), lambda i, j: (j, i, 0)),
                   own_copy(KA), own_copy(VA)],
        out_shape=[jax.ShapeDtypeStruct((n // COL_BLOCK, m, COL_BLOCK), F32),
                   jax.ShapeDtypeStruct((m, COL_BLOCK), F32),
                   jax.ShapeDtypeStruct((m, COL_BLOCK), F32)],
        scratch_shapes=[pltpu.VMEM((tm, d), BF16)],
        compiler_params=pltpu.CompilerParams(
            dimension_semantics=("arbitrary", "arbitrary"), vmem_limit_bytes=VMEM_LIMIT),
        name="in_proj",
    )(x2d, pre_g.reshape(1, d), w_bf16)


def _online_softmax_step(s, vt, m_ref, l_ref, acc_ref):
    m_old = m_ref[...]
    m_new = jnp.maximum(m_old, jnp.max(s, axis=0, keepdims=True))
    alpha = jnp.exp2(m_old - m_new)
    p = jnp.exp2(s - m_new)
    l_ref[...] = alpha * l_ref[...] + jnp.sum(p, axis=0, keepdims=True)
    acc_ref[...] = alpha * acc_ref[...] + jnp.dot(vt, p.astype(BF16), preferred_element_type=F32)
    m_ref[...] = m_new


def _stage_keys_values(k_ref, v_ref, kbf_ref, vt_ref, t):
    def body(j, c):
        r = pl.multiple_of(j * t, t)
        kbf_ref[pl.ds(r, t), :] = k_ref[pl.ds(r, t), :].astype(BF16)
        vt_ref[j] = v_ref[pl.ds(r, t), :].T.astype(BF16)
        return c
    lax.fori_loop(0, vt_ref.shape[0], body, 0, unroll=4)


def _attn_a_kernel(t5_ref, lq1, lk1, lq2, lk2, g_ref, q_ref, k_ref, v_ref, o_ref,
                   kbf_ref, vt_ref, bias_ref, qt_ref, m_ref, l_ref, acc_ref, *s_refs, lam_init):
    tq, tk = A_TQ, A_TK
    h = pl.program_id(0)
    i = pl.program_id(1)

    @pl.when(i == 0)
    def _():
        _stage_keys_values(k_ref, v_ref, kbf_ref, vt_ref, tk)
        key = lax.broadcasted_iota(jnp.int32, (tk, tq), 0)
        qry = lax.broadcasted_iota(jnp.int32, (tk, tq), 1)
        for n in range(3):
            rel = key + (n - 1) * tk - qry
            lo, hi = (n - 1) * tk - (tq - 1), min(n * tk - 1, CHUNK - 1)
            b = _t5_bias_tile(jnp.minimum(rel, hi), lo, hi, t5_ref, h) * LOG2E
            if n >= 1:
                b = jnp.where((key + (n - 1) * tk) // CHUNK <= qry // CHUNK, b, -jnp.inf)
            bias_ref[n] = b

    qt = (q_ref[...] * (DK_A ** -0.5 * LOG2E)).T
    sub = lax.broadcasted_iota(jnp.int32, (HEAD_W, tq), 0)
    qt_ref[:, :tq] = jnp.where(sub < DK_A, qt, 0.0).astype(BF16)
    qt_ref[:, tq:] = jnp.where(sub >= DK_A, qt, 0.0).astype(BF16)
    m_ref[...] = jnp.full(m_ref.shape, -jnp.inf, F32)
    l_ref[...] = jnp.zeros(l_ref.shape, F32)
    acc_ref[...] = jnp.zeros(acc_ref.shape, F32)

    def scores(j, s_ref):
        r = pl.multiple_of(j * tk, tk)
        s_ref[...] = jnp.dot(kbf_ref[pl.ds(r, tk), :], qt_ref[...], preferred_element_type=F32)

    def update(j, s_ref, bias_idx):
        s = s_ref[...]
        if bias_idx is not None:
            b = bias_ref[bias_idx]
            s = jnp.concatenate([s[:, :tq] + b, s[:, tq:] + b], axis=1)
        _online_softmax_step(s, vt_ref[j], m_ref, l_ref, acc_ref)

    scores(0, s_refs[0])
    scores(1, s_refs[1])

    odd = jnp.logical_and(i >= 2, i % 2 == 0)

    @pl.when(odd)
    def _():
        update(0, s_refs[0], None)
        scores(2, s_refs[0])
        update(1, s_refs[1], None)
        scores(3, s_refs[1])

    j0 = jnp.where(odd, 2, 0)

    def quad(u, c):
        j = j0 + 4 * u
        for n in range(4):
            scores(j + n + 2, s_refs[(n + 2) % 4])
            update(j + n, s_refs[n], None)
        return c
    lax.fori_loop(0, jnp.maximum(i - 1, 0) // 2, quad, 0)

    @pl.when(i >= 1)
    def _():
        j = 2 * i - 2
        scores(j + 2, s_refs[2])
        update(j, s_refs[0], None)
        scores(j + 3, s_refs[3])
        update(j + 1, s_refs[1], 0)
        update(j + 2, s_refs[2], 1)
        update(j + 3, s_refs[3], 2)

    @pl.when(i == 0)
    def _():
        update(0, s_refs[0], 1)
        update(1, s_refs[1], 2)

    inv = 1.0 / l_ref[...]
    acc = acc_ref[...]
    lam = _lambda(lq1, lk1, lq2, lk2, lam_init)
    o = acc[:, :tq] * inv[:, :tq] - lam * (acc[:, tq:] * inv[:, tq:])
    ms = jnp.mean(o * o, axis=0, keepdims=True)
    y = (o * lax.rsqrt(ms + EPS) * g_ref[...]) * (1.0 - lam_init)
    o_ref[...] = y.T


def _attn_a_prompt(proj, t5_bias, lq1, lk1, lq2, lk2, subln_g, lam_init):
    s = proj.shape[1]
    tq, tk = A_TQ, A_TK
    assert s % tq == 0 and tq == 2 * tk and tk % CHUNK == 0 and tk >= T5_MAX_DIST
    vec = lambda: pl.BlockSpec((1, DK_A), lambda h, i: (0, 0))
    return pl.pallas_call(
        functools.partial(_attn_a_kernel, lam_init=lam_init),
        grid=(H_A, s // tq),
        in_specs=[
            pl.BlockSpec(memory_space=pltpu.SMEM),
            vec(), vec(), vec(), vec(),
            pl.BlockSpec((DV_A, 1), lambda h, i: (0, 0)),
            pl.BlockSpec((None, tq, HEAD_W), lambda h, i: (QA, i, h)),
            pl.BlockSpec((None, s, HEAD_W), lambda h, i: (KA, 0, h)),
            pl.BlockSpec((None, s, HEAD_W), lambda h, i: (VA, 0, h)),
        ],
        out_specs=pl.BlockSpec((tq, HEAD_W), lambda h, i: (i, h)),
        out_shape=jax.ShapeDtypeStruct((s, H_A * DV_A), F32),
        scratch_shapes=[
            pltpu.VMEM((s, HEAD_W), BF16),
            pltpu.VMEM((s // tk, DV_A, tk), BF16),
            pltpu.VMEM((3, tk, tq), F32),
            pltpu.VMEM((HEAD_W, 2 * tq), BF16),
            pltpu.VMEM((1, 2 * tq), F32),
            pltpu.VMEM((1, 2 * tq), F32),
            pltpu.VMEM((DV_A, 2 * tq), F32),
        ] + [pltpu.VMEM((tk, 2 * tq), F32)] * 4,
        compiler_params=pltpu.CompilerParams(
            dimension_semantics=("arbitrary", "arbitrary"), vmem_limit_bytes=VMEM_LIMIT),
        name="attn_a_prompt",
    )(t5_bias, lq1.reshape(1, DK_A), lk1.reshape(1, DK_A), lq2.reshape(1, DK_A), lk2.reshape(1, DK_A),
      subln_g.reshape(DV_A, 1), proj, proj, proj)


def _attn_b_kernel(rb_ref, q_ref, k_ref, v_ref, o_ref, kbf_ref, vt_ref, bias_ref, *s_refs):
    t = B_TQ
    blk = REL_CLIP
    nkt = BAND_PAST // t + 1
    nq = q_ref.shape[0] // t
    h = pl.program_id(0)

    _stage_keys_values(k_ref, v_ref, kbf_ref, vt_ref, t)

    kk = lax.broadcasted_iota(jnp.int32, (blk, blk), 0)
    qq = lax.broadcasted_iota(jnp.int32, (blk, blk), 1)
    rel = kk - qq
    lo = jnp.full((blk, blk), rb_ref[h, 0] * LOG2E, F32)
    same = _rel_bias_tile(rel, -(blk - 1), blk - 1, rb_ref, h) * LOG2E
    prev = _rel_bias_tile(rel - blk, -(2 * blk - 1), -1, rb_ref, h) * LOG2E
    ninf = jnp.full((blk, blk), -jnp.inf, F32)
    kc = kk // CHUNK
    qc = qq // CHUNK
    far_blocks = BAND_PAST // blk
    for a in range(nkt * t // blk):
        for b in range(t // blk):
            e = a - b
            if e < 0 or e > far_blocks:
                tile = ninf
            elif e == 0:
                tile = jnp.where(kc >= qc, lo, -jnp.inf)
            elif e == far_blocks:
                tile = jnp.where(kc <= qc, same, -jnp.inf)
            elif e == far_blocks - 1:
                tile = prev
            else:
                tile = lo
            bias_ref[a * blk:(a + 1) * blk, b * blk:(b + 1) * blk] = tile

    def scores(g, nk, s_ref):
        q0 = pl.multiple_of(g * t, t)
        k0 = pl.multiple_of((g - (nk - 1)) * t, t)
        qt = (q_ref[pl.ds(q0, t), :] * (DH_B ** -0.5 * LOG2E)).T.astype(BF16)
        s_ref[(nkt - nk) * t:, :] = jnp.dot(kbf_ref[pl.ds(k0, nk * t), :], qt, preferred_element_type=F32)

    def finish(g, nk, s_ref):
        s = s_ref[(nkt - nk) * t:, :] + bias_ref[(nkt - nk) * t:, :]
        m = jnp.max(s, axis=0, keepdims=True)
        p = jnp.exp2(s - m)
        l = jnp.sum(p, axis=0, keepdims=True)
        pb = p.astype(BF16)
        o = jnp.dot(vt_ref[g - (nk - 1)], pb[:t], preferred_element_type=F32)
        for c in range(1, nk):
            o = o + jnp.dot(vt_ref[g - (nk - 1) + c], pb[c * t:(c + 1) * t], preferred_element_type=F32)
        o_ref[pl.ds(pl.multiple_of(g * t, t), t), :] = (o * (1.0 / l)).T

    first = nkt - 1
    for g in range(first):
        scores(g, g + 1, s_refs[g % 4])
        finish(g, g + 1, s_refs[g % 4])

    assert (nq - first - 2) % 4 == 0
    scores(first, nkt, s_refs[0])
    scores(first + 1, nkt, s_refs[1])

    def body(u, c):
        g = first + 4 * u
        for n in range(4):
            scores(g + n + 2, nkt, s_refs[(n + 2) % 4])
            finish(g + n, nkt, s_refs[n])
        return c
    lax.fori_loop(0, (nq - first - 2) // 4, body, 0)
    finish(nq - 2, nkt, s_refs[0])
    finish(nq - 1, nkt, s_refs[1])


def _attn_b_prompt(proj, rel_bias):
    s = proj.shape[1]
    t = B_TQ
    nkt = BAND_PAST // t + 1
    assert s % t == 0 and BAND_PAST % t == 0 and t % REL_CLIP == 0 and REL_CLIP % CHUNK == 0
    head = lambda c: pl.BlockSpec((None, s, HEAD_W), lambda h: (c, 0, h))
    return pl.pallas_call(
        _attn_b_kernel,
        grid=(H_B,),
        in_specs=[pl.BlockSpec(memory_space=pltpu.SMEM), head(QB), head(KB), head(VB)],
        out_specs=pl.BlockSpec((s, HEAD_W), lambda h: (0, h)),
        out_shape=jax.ShapeDtypeStruct((s, H_B * DH_B), F32),
        scratch_shapes=[
            pltpu.VMEM((s, HEAD_W), BF16),
            pltpu.VMEM((s // t, DH_B, t), BF16),
            pltpu.VMEM((nkt * t, t), F32),
        ] + [pltpu.VMEM((nkt * t, t), F32)] * 4,
        compiler_params=pltpu.CompilerParams(
            dimension_semantics=("arbitrary",), vmem_limit_bytes=VMEM_LIMIT),
        name="attn_b_prompt",
    )(rel_bias, proj, proj, proj)


def _dot_nt(a, b):
    return lax.dot_general(a, b, (((1,), (1,)), ((), ())), preferred_element_type=F32)


def _attn_sample_kernel(t5_ref, rb_ref, lq1, lk1, lq2, lk2, g_ref,
                        qa_ref, ka_ref, va_ref, cka_ref, cva_ref,
                        qb_ref, kb_ref, vb_ref, ckb_ref, cvb_ref,
                        oa_ref, ob_ref, kroll_ref, vroll_ref,
                        ba_ref, bb_ref, *, lam_init, past, win):
    n = qa_ref.shape[0]
    near = NEAR
    nh = H_A

    @pl.when(pl.program_id(0) == 0)
    def _():
        qry = lax.broadcasted_iota(jnp.int32, (n, near + n), 0)
        key = lax.broadcasted_iota(jnp.int32, (n, near + n), 1)
        rel = key - near - qry
        for h in range(nh):
            ba_ref[h] = _t5_bias_tile(rel, -(near + n - 1), n - 1, t5_ref, h)
            bb_ref[h] = _rel_bias_tile(rel, -(near + n - 1), n - 1, rb_ref, h)

    kroll_ref[:(win - n) * nh, :] = ckb_ref[n * nh:, :]
    vroll_ref[:(win - n) * nh, :] = cvb_ref[n * nh:, :]

    lam = _lambda(lq1, lk1, lq2, lk2, lam_init)
    lane = lax.broadcasted_iota(jnp.int32, (n, HEAD_W), 1)
    for h in range(nh):
        cols = slice(h * HEAD_W, (h + 1) * HEAD_W)
        new_rows = pl.ds((win - n) * nh + h, n, stride=nh)
        kroll_ref[new_rows, :] = kb_ref[:, cols]
        vroll_ref[new_rows, :] = vb_ref[:, cols]

        q = qa_ref[:, cols] * (DK_A ** -0.5)
        q2 = jnp.concatenate([jnp.where(lane < DK_A, q, 0.0), jnp.where(lane >= DK_A, q, 0.0)],
                             axis=0).astype(BF16)
        ba = ba_ref[h]
        ba2 = jnp.concatenate([ba, ba], axis=0)
        s_c = _dot_nt(q2, cka_ref[:, cols].astype(BF16))
        s_c = jnp.concatenate([s_c[:, :past - near], s_c[:, past - near:] + ba2[:, :near]], axis=1)
        s_n = _dot_nt(q2, ka_ref[:, cols].astype(BF16)) + ba2[:, near:]
        m = jnp.maximum(jnp.max(s_c, axis=-1, keepdims=True), jnp.max(s_n, axis=-1, keepdims=True))
        p_c = jnp.exp(s_c - m)
        p_n = jnp.exp(s_n - m)
        l = jnp.sum(p_c, axis=-1, keepdims=True) + jnp.sum(p_n, axis=-1, keepdims=True)
        cva = cva_ref[pl.ds(h, past, stride=nh), :].astype(BF16)
        o2 = (jnp.dot(p_c.astype(BF16), cva, preferred_element_type=F32)
              + jnp.dot(p_n.astype(BF16), va_ref[:, cols].astype(BF16), preferred_element_type=F32)) * (1.0 / l)
        o = o2[:n] - lam * o2[n:]
        ms = jnp.mean(o * o, axis=-1, keepdims=True)
        oa_ref[:, cols] = (o * lax.rsqrt(ms + EPS) * g_ref[...]) * (1.0 - lam_init)

        qb = qb_ref[:, cols].astype(BF16)
        bb = bb_ref[h]
        ckb = ckb_ref[pl.ds(h, win, stride=nh), :].astype(BF16)
        cvb = cvb_ref[pl.ds(h, win, stride=nh), :].astype(BF16)
        s_c = _dot_nt(qb, ckb) * (DH_B ** -0.5)
        s_c = jnp.concatenate([s_c[:, :win - near] + rb_ref[h, 0], s_c[:, win - near:] + bb[:, :near]], axis=1)
        s_n = _dot_nt(qb, kb_ref[:, cols].astype(BF16)) * (DH_B ** -0.5) + bb[:, near:]
        m = jnp.maximum(jnp.max(s_c, axis=-1, keepdims=True), jnp.max(s_n, axis=-1, keepdims=True))
        p_c = jnp.exp(s_c - m)
        p_n = jnp.exp(s_n - m)
        l = jnp.sum(p_c, axis=-1, keepdims=True) + jnp.sum(p_n, axis=-1, keepdims=True)
        ob_ref[:, cols] = (jnp.dot(p_c.astype(BF16), cvb, preferred_element_type=F32)
                           + jnp.dot(p_n.astype(BF16), vb_ref[:, cols].astype(BF16),
                                     preferred_element_type=F32)) * (1.0 / l)


def _attn_sample(proj, ck_a, cv_a, ck_b, cv_b, t5_bias, rel_bias, lq1, lk1, lq2, lk2, subln_g, lam_init):
    nb, past = ck_a.shape[0], ck_a.shape[1]
    win = ck_b.shape[1]
    n = proj.shape[1] // nb
    near = NEAR
    wide = H_A * HEAD_W
    assert past % CHUNK == 0 and n <= CHUNK and win <= BAND_PAST and win <= past
    assert near % HEAD_W == 0 and near <= win and near <= past and n % 8 == 0 and T5_MAX_DIST <= REL_CLIP
    assert H_A == H_B and DV_A == HEAD_W and DH_B == HEAD_W and 2 * DK_A == HEAD_W
    cka = ck_a.reshape(nb, past, wide)
    cva = cv_a.reshape(nb, past * H_A, DV_A)
    ckb = ck_b.reshape(nb, win * H_B, DH_B)
    cvb = cv_b.reshape(nb, win * H_B, DH_B)
    vec = lambda: pl.BlockSpec((1, DK_A), lambda b: (0, 0))
    new = lambda c: pl.BlockSpec((None, n, wide), lambda b: (c, b, 0))
    seq = lambda rows, cols: pl.BlockSpec((None, rows, cols), lambda b: (b, 0, 0))
    out = pl.BlockSpec((n, wide), lambda b: (b, 0))
    return pl.pallas_call(
        functools.partial(_attn_sample_kernel, lam_init=lam_init, past=past, win=win),
        grid=(nb,),
        in_specs=[
            pl.BlockSpec(memory_space=pltpu.SMEM),
            pl.BlockSpec(memory_space=pltpu.SMEM),
            vec(), vec(), vec(), vec(),
            pl.BlockSpec((1, DV_A), lambda b: (0, 0)),
            new(QA), new(KA), new(VA), seq(past, wide), seq(past * H_A, HEAD_W),
            new(QB), new(KB), new(VB), seq(win * H_B, HEAD_W), seq(win * H_B, HEAD_W),
        ],
        out_specs=[out, out, seq(win * H_B, HEAD_W), seq(win * H_B, HEAD_W)],
        out_shape=[
            jax.ShapeDtypeStruct((nb * n, wide), F32),
            jax.ShapeDtypeStruct((nb * n, wide), F32),
            jax.ShapeDtypeStruct((nb, win * H_B, DH_B), F32),
            jax.ShapeDtypeStruct((nb, win * H_B, DH_B), F32),
        ],
        scratch_shapes=[pltpu.VMEM((H_A, n, near + n), F32), pltpu.VMEM((H_B, n, near + n), F32)],
        compiler_params=pltpu.CompilerParams(
            dimension_semantics=("arbitrary",), vmem_limit_bytes=VMEM_LIMIT),
        name="attn_sample",
    )(t5_bias, rel_bias, lq1.reshape(1, DK_A), lk1.reshape(1, DK_A), lq2.reshape(1, DK_A),
      lk2.reshape(1, DK_A), subln_g.reshape(1, DV_A),
      proj, proj, proj, cka, cva, proj, proj, proj, ckb, cvb)


def _merge_kernel(x_ref, oa_ref, ob_ref, za_ref, zb_ref, ga0_ref, ga1_ref, gb0_ref, gb1_ref,
                  woa_ref, wob_ref, wout_ref, pg_ref, y_ref):
    za = za_ref[...]
    zb = zb_ref[...]
    a = (oa_ref[...] * (za * _sigmoid(za))).astype(BF16)
    b = (ob_ref[...] * (zb * _sigmoid(zb))).astype(BF16)
    ya = jnp.dot(a, woa_ref[...], preferred_element_type=F32)
    yb = jnp.dot(b, wob_ref[...], preferred_element_type=F32)
    ga = jnp.concatenate([ga0_ref[...], ga1_ref[...]], axis=1)
    gb = jnp.concatenate([gb0_ref[...], gb1_ref[...]], axis=1)
    mix = (_sigmoid(ga) * ya + _sigmoid(gb) * yb).astype(BF16)
    y = jnp.dot(mix, wout_ref[...], preferred_element_type=F32)
    ms = jnp.mean(y * y, axis=-1, keepdims=True)
    y_ref[...] = x_ref[...] + y * lax.rsqrt(ms + EPS) * pg_ref[...]


def _merge(x2d, o_a, o_b, proj, woa, wob, wout, post_g, tm):
    m, d = x2d.shape
    wa = o_a.shape[1]
    wb = o_b.shape[1]
    assert m % tm == 0 and wa == COL_BLOCK and wb == COL_BLOCK and d == 2 * COL_BLOCK
    row = lambda w: pl.BlockSpec((tm, w), lambda i: (i, 0))
    col = lambda c: pl.BlockSpec((None, tm, COL_BLOCK), lambda i: (c, i, 0))
    resident = lambda r, c: pl.BlockSpec((r, c), lambda i: (0, 0), pipeline_mode=pl.Buffered(1))
    return pl.pallas_call(
        _merge_kernel,
        grid=(m // tm,),
        in_specs=[row(d), row(wa), row(wb), col(ZA), col(ZB), col(GA0), col(GA1), col(GB0), col(GB1),
                  resident(wa, d), resident(wb, d), resident(d, d), resident(1, d)],
        out_specs=row(d),
        out_shape=jax.ShapeDtypeStruct((m, d), F32),
        compiler_params=pltpu.CompilerParams(
            dimension_semantics=("arbitrary",), vmem_limit_bytes=VMEM_LIMIT),
        name="merge",
    )(x2d, o_a, o_b, proj, proj, proj, proj, proj, proj, woa, wob, wout, post_g.reshape(1, d))


def kernel(x_prompt, x_sample, cache_k_a, cache_v_a, cache_k_b, cache_v_b, t5_bias, pre_norm, post_norm,
           w_in, lambda_q1, lambda_k1, lambda_q2, lambda_k2, subln_a, rel_bias_b, w_o_a, w_o_b, w_out):
    depth = w_in.shape[0]
    bp, sp, d = x_prompt.shape
    bs, ss, _ = x_sample.shape
    assert bp == 1 and w_in.shape[2] == 12 * COL_BLOCK
    yp = x_prompt.reshape(sp, d)
    ys = x_sample.reshape(bs * ss, d)
    tail = min(BAND_PAST, sp)
    outs = [[] for _ in range(8)]
    for l in range(depth):
        lam_init = 0.8 - 0.6 * math.exp(-0.3 * l)
        w = w_in[l].astype(BF16)
        woa = w_o_a[l].astype(BF16)
        wob = w_o_b[l].astype(BF16)
        wout = w_out[l].astype(BF16)
        lam_args = (lambda_q1[l], lambda_k1[l], lambda_q2[l], lambda_k2[l], subln_a[l], lam_init)

        pp, ka, va = _in_proj(yp, pre_norm[l], w, tm=1024, blocks_per_step=1)
        oa = _attn_a_prompt(pp, t5_bias, *lam_args)
        ob = _attn_b_prompt(pp, rel_bias_b[l])
        yp = _merge(yp, oa, ob, pp, woa, wob, wout, post_norm[l], tm=256)
        outs[0].append(ka.reshape(bp, sp, 2 * H_A, DK_A))
        outs[1].append(va.reshape(bp, sp, H_A, DV_A))
        outs[2].append(pp[KB, sp - tail:].reshape(bp, tail, H_B, DH_B))
        outs[3].append(pp[VB, sp - tail:].reshape(bp, tail, H_B, DH_B))

        ps, ka, va = _in_proj(ys, pre_norm[l], w, tm=bs * ss, blocks_per_step=3)
        oas, obs, kroll, vroll = _attn_sample(ps, cache_k_a[l], cache_v_a[l], cache_k_b[l], cache_v_b[l],
                                              t5_bias, rel_bias_b[l], *lam_args)
        ys = _merge(ys, oas, obs, ps, woa, wob, wout, post_norm[l], tm=bs * ss)
        outs[4].append(ka.reshape(bs, ss, 2 * H_A, DK_A))
        outs[5].append(va.reshape(bs, ss, H_A, DV_A))
        outs[6].append(kroll.reshape(bs, -1, H_B, DH_B))
        outs[7].append(vroll.reshape(bs, -1, H_B, DH_B))
    return (yp.reshape(bp, sp, d), ys.reshape(bs, ss, d)) + tuple(jnp.stack(o) for o in outs)
```

```python
import functools
import math

import numpy as np
import jax
import jax.numpy as jnp
from jax import lax
from jax.experimental import pallas as pl
from jax.experimental.pallas import tpu as pltpu

F32 = jnp.float32
BF16 = jnp.bfloat16

CHUNK = 64
H_A = 8
DK_A = 64
DV_A = 2 * DK_A
H_B = 8
DH_B = 128
BAND_CHUNKS = 8
BAND_PAST = BAND_CHUNKS * CHUNK
REL_CLIP = 128
T5_BUCKETS = 32
T5_MAX_DIST = 128
EPS = 1e-6

HEAD_W = 128
COL_BLOCK = 1024
A_TQ, A_TK = 512, 256
B_TQ = 256
LOG2E = math.log2(math.e)
NEAR = max(REL_CLIP, T5_MAX_DIST)
VMEM_LIMIT = 60 * 1024 * 1024

QA, KA, VA, ZA, QB, KB, VB, ZB, GA0, GA1, GB0, GB1 = range(12)


def _t5_bucket_int(rel):
    half = T5_BUCKETS // 2
    max_exact = half // 2
    n = abs(rel)
    ret = half if rel > 0 else 0
    if n < max_exact:
        return ret + n
    assert (T5_MAX_DIST // max_exact) ** 2 == 2 ** (half - max_exact)
    j = 0
    while n * n >= (max_exact * max_exact) * 2 ** (j + 1):
        j += 1
    return ret + min(max_exact + j, half - 1)


def _t5_runs(lo, hi):
    runs = []
    for r in range(lo, hi + 1):
        b = _t5_bucket_int(r)
        if not runs or runs[-1][1] != b:
            runs.append((r, b))
    return runs


T5_FAR_BUCKET = _t5_bucket_int(-T5_MAX_DIST)
assert all(_t5_bucket_int(-n) == T5_FAR_BUCKET for n in range(T5_MAX_DIST, 4 * T5_MAX_DIST))


def _t5_bias_tile(rel, lo, hi, t5_ref, h):
    runs = _t5_runs(lo, hi)
    val = jnp.full(rel.shape, t5_ref[runs[0][1], h], F32)
    for start, b in runs[1:]:
        val = jnp.where(rel >= start, t5_ref[b, h], val)
    return val - t5_ref[T5_FAR_BUCKET, h]


def _rel_bias_tile(rel, lo, hi, rb_ref, h):
    lo = max(lo, -REL_CLIP)
    hi = min(hi, REL_CLIP)
    val = jnp.full(rel.shape, rb_ref[h, lo + REL_CLIP], F32)
    for d in range(lo + 1, hi + 1):
        val = jnp.where(rel >= d, rb_ref[h, d + REL_CLIP], val)
    return val


def _sigmoid(x):
    return 1.0 / (1.0 + jnp.exp(-x))


def _lambda(lq1, lk1, lq2, lk2, lam_init):
    a = jnp.sum(lq1[...] * lk1[...], axis=-1, keepdims=True)
    b = jnp.sum(lq2[...] * lk2[...], axis=-1, keepdims=True)
    return jnp.exp(a) - jnp.exp(b) + lam_init


def _in_proj_kernel(x_ref, g_ref, w_ref, o_ref, ka_ref, va_ref, h_ref):
    j = pl.program_id(1)

    @pl.when(j == 0)
    def _():
        x = x_ref[...]
        ms = jnp.mean(x * x, axis=-1, keepdims=True)
        h_ref[...] = (x * lax.rsqrt(ms + EPS) * g_ref[...]).astype(BF16)

    y = jnp.dot(h_ref[...], w_ref[...], preferred_element_type=F32)
    per_step = o_ref.shape[0]
    for c in range(per_step):
        o_ref[c] = y[:, c * COL_BLOCK:(c + 1) * COL_BLOCK]
    for stream, ref in ((KA, ka_ref), (VA, va_ref)):
        c = stream % per_step

        @pl.when(j == stream // per_step)
        def _():
            ref[...] = y[:, c * COL_BLOCK:(c + 1) * COL_BLOCK]


def _in_proj(x2d, pre_g, w_bf16, tm, blocks_per_step):
    m, d = x2d.shape
    n = w_bf16.shape[1]
    tn = blocks_per_step * COL_BLOCK
    assert m % tm == 0 and n % tn == 0

    def own_copy(stream):
        first = stream // blocks_per_step
        return pl.BlockSpec((tm, COL_BLOCK), lambda i, j: (jnp.where(j >= first, i, jnp.maximum(i - 1, 0)), 0))

    return pl.pallas_call(
        _in_proj_kernel,
        grid=(m // tm, n // tn),
        in_specs=[
            pl.BlockSpec((tm, d), lambda i, j: (i, 0)),
            pl.BlockSpec((1, d), lambda i, j: (0, 0)),
            pl.BlockSpec((d, tn), lambda i, j: (0, j)),
        ],
        out_specs=[pl.BlockSpec((blocks_per_step, tm, COL_BLOCK), lambda i, j: (j, i, 0)),
                   own_copy(KA), own_copy(VA)],
        out_shape=[jax.ShapeDtypeStruct((n // COL_BLOCK, m, COL_BLOCK), F32),
                   jax.ShapeDtypeStruct((m, COL_BLOCK), F32),
                   jax.ShapeDtypeStruct((m, COL_BLOCK), F32)],
        scratch_shapes=[pltpu.VMEM((tm, d), BF16)],
        compiler_params=pltpu.CompilerParams(
            dimension_semantics=("arbitrary", "arbitrary"), vmem_limit_bytes=VMEM_LIMIT),
        name="in_proj",
    )(x2d, pre_g.reshape(1, d), w_bf16)


def _online_softmax_step(s, vt, m_ref, l_ref, acc_ref):
    m_old = m_ref[...]
    m_new = jnp.maximum(m_old, jnp.max(s, axis=0, keepdims=True))
    alpha = jnp.exp2(m_old - m_new)
    p = jnp.exp2(s - m_new)
    l_ref[...] = alpha * l_ref[...] + jnp.sum(p, axis=0, keepdims=True)
    acc_ref[...] = alpha * acc_ref[...] + jnp.dot(vt, p.astype(BF16), preferred_element_type=F32)
    m_ref[...] = m_new


def _stage_keys_values(k_ref, v_ref, kbf_ref, vt_ref, t):
    def body(j, c):
        r = pl.multiple_of(j * t, t)
        kbf_ref[pl.ds(r, t), :] = k_ref[pl.ds(r, t), :].astype(BF16)
        vt_ref[j] = v_ref[pl.ds(r, t), :].T.astype(BF16)
        return c
    lax.fori_loop(0, vt_ref.shape[0], body, 0, unroll=4)


def _attn_a_kernel(t5_ref, lq1, lk1, lq2, lk2, g_ref, q_ref, k_ref, v_ref, o_ref,
                   kbf_ref, vt_ref, bias_ref, qt_ref, m_ref, l_ref, acc_ref, *s_refs, lam_init):
    tq, tk = A_TQ, A_TK
    h = pl.program_id(0)
    nq = q_ref.shape[0] // tq

    _stage_keys_values(k_ref, v_ref, kbf_ref, vt_ref, tk)
    key = lax.broadcasted_iota(jnp.int32, (tk, tq), 0)
    qry = lax.broadcasted_iota(jnp.int32, (tk, tq), 1)
    for n in range(3):
        rel = key + (n - 1) * tk - qry
        lo, hi = (n - 1) * tk - (tq - 1), min(n * tk - 1, CHUNK - 1)
        b = _t5_bias_tile(jnp.minimum(rel, hi), lo, hi, t5_ref, h) * LOG2E
        if n >= 1:
            b = jnp.where((key + (n - 1) * tk) // CHUNK <= qry // CHUNK, b, -jnp.inf)
        bias_ref[n] = b

    lam = _lambda(lq1, lk1, lq2, lk2, lam_init)

    def q_tile(i, carry):
        q0 = pl.multiple_of(i * tq, tq)
        qt = (q_ref[pl.ds(q0, tq), :] * (DK_A ** -0.5 * LOG2E)).T
        sub = lax.broadcasted_iota(jnp.int32, (HEAD_W, tq), 0)
        qt_ref[:, :tq] = jnp.where(sub < DK_A, qt, 0.0).astype(BF16)
        qt_ref[:, tq:] = jnp.where(sub >= DK_A, qt, 0.0).astype(BF16)
        m_ref[...] = jnp.full(m_ref.shape, -jnp.inf, F32)
        l_ref[...] = jnp.zeros(l_ref.shape, F32)
        acc_ref[...] = jnp.zeros(acc_ref.shape, F32)

        def scores(j, s_ref):
            r = pl.multiple_of(j * tk, tk)
            s_ref[...] = jnp.dot(kbf_ref[pl.ds(r, tk), :], qt_ref[...], preferred_element_type=F32)

        def update(j, s_ref, bias_idx):
            s = s_ref[...]
            if bias_idx is not None:
                b = bias_ref[bias_idx]
                s = jnp.concatenate([s[:, :tq] + b, s[:, tq:] + b], axis=1)
            _online_softmax_step(s, vt_ref[j], m_ref, l_ref, acc_ref)

        scores(0, s_refs[0])
        scores(1, s_refs[1])

        odd = jnp.logical_and(i >= 2, i % 2 == 0)

        @pl.when(odd)
        def _():
            update(0, s_refs[0], None)
            scores(2, s_refs[0])
            update(1, s_refs[1], None)
            scores(3, s_refs[1])

        j0 = jnp.where(odd, 2, 0)

        def quad(u, c):
            j = j0 + 4 * u
            for n in range(4):
                scores(j + n + 2, s_refs[(n + 2) % 4])
                update(j + n, s_refs[n], None)
            return c
        lax.fori_loop(0, jnp.maximum(i - 1, 0) // 2, quad, 0)

        @pl.when(i >= 1)
        def _():
            j = 2 * i - 2
            scores(j + 2, s_refs[2])
            update(j, s_refs[0], None)
            scores(j + 3, s_refs[3])
            update(j + 1, s_refs[1], 0)
            update(j + 2, s_refs[2], 1)
            update(j + 3, s_refs[3], 2)

        @pl.when(i == 0)
        def _():
            update(0, s_refs[0], 1)
            update(1, s_refs[1], 2)

        inv = 1.0 / l_ref[...]
        acc = acc_ref[...]
        o = acc[:, :tq] * inv[:, :tq] - lam * (acc[:, tq:] * inv[:, tq:])
        ms = jnp.mean(o * o, axis=0, keepdims=True)
        y = (o * lax.rsqrt(ms + EPS) * g_ref[...]) * (1.0 - lam_init)
        o_ref[pl.ds(q0, tq), :] = y.T
        return carry

    lax.fori_loop(0, nq, q_tile, 0)


def _attn_a_prompt(proj, t5_bias, lq1, lk1, lq2, lk2, subln_g, lam_init):
    s = proj.shape[1]
    tq, tk = A_TQ, A_TK
    assert s % tq == 0 and tq == 2 * tk and tk % CHUNK == 0 and tk >= T5_MAX_DIST
    vec = lambda: pl.BlockSpec((1, DK_A), lambda h: (0, 0))
    head = lambda c: pl.BlockSpec((None, s, HEAD_W), lambda h: (c, 0, h))
    return pl.pallas_call(
        functools.partial(_attn_a_kernel, lam_init=lam_init),
        grid=(H_A,),
        in_specs=[
            pl.BlockSpec(memory_space=pltpu.SMEM),
            vec(), vec(), vec(), vec(),
            pl.BlockSpec((DV_A, 1), lambda h: (0, 0)),
            head(QA), head(KA), head(VA),
        ],
        out_specs=pl.BlockSpec((s, HEAD_W), lambda h: (0, h)),
        out_shape=jax.ShapeDtypeStruct((s, H_A * DV_A), F32),
        scratch_shapes=[
            pltpu.VMEM((s, HEAD_W), BF16),
            pltpu.VMEM((s // tk, DV_A, tk), BF16),
            pltpu.VMEM((3, tk, tq), F32),
            pltpu.VMEM((HEAD_W, 2 * tq), BF16),
            pltpu.VMEM((1, 2 * tq), F32),
            pltpu.VMEM((1, 2 * tq), F32),
            pltpu.VMEM((DV_A, 2 * tq), F32),
        ] + [pltpu.VMEM((tk, 2 * tq), F32)] * 4,
        compiler_params=pltpu.CompilerParams(
            dimension_semantics=("arbitrary",), vmem_limit_bytes=VMEM_LIMIT),
        name="attn_a_prompt",
    )(t5_bias, lq1.reshape(1, DK_A), lk1.reshape(1, DK_A), lq2.reshape(1, DK_A), lk2.reshape(1, DK_A),
      subln_g.reshape(DV_A, 1), proj, proj, proj)


def _attn_b_kernel(rb_ref, q_ref, k_ref, v_ref, o_ref, kbf_ref, vt_ref, bias_ref, *s_refs):
    t = B_TQ
    blk = REL_CLIP
    nkt = BAND_PAST // t + 1
    nq = q_ref.shape[0] // t
    h = pl.program_id(0)

    _stage_keys_values(k_ref, v_ref, kbf_ref, vt_ref, t)

    kk = lax.broadcasted_iota(jnp.int32, (blk, blk), 0)
    qq = lax.broadcasted_iota(jnp.int32, (blk, blk), 1)
    rel = kk - qq
    lo = jnp.full((blk, blk), rb_ref[h, 0] * LOG2E, F32)
    same = _rel_bias_tile(rel, -(blk - 1), blk - 1, rb_ref, h) * LOG2E
    prev = _rel_bias_tile(rel - blk, -(2 * blk - 1), -1, rb_ref, h) * LOG2E
    ninf = jnp.full((blk, blk), -jnp.inf, F32)
    kc = kk // CHUNK
    qc = qq // CHUNK
    far_blocks = BAND_PAST // blk
    for a in range(nkt * t // blk):
        for b in range(t // blk):
            e = a - b
            if e < 0 or e > far_blocks:
                tile = ninf
            elif e == 0:
                tile = jnp.where(kc >= qc, lo, -jnp.inf)
            elif e == far_blocks:
                tile = jnp.where(kc <= qc, same, -jnp.inf)
            elif e == far_blocks - 1:
                tile = prev
            else:
                tile = lo
            bias_ref[a * blk:(a + 1) * blk, b * blk:(b + 1) * blk] = tile

    def scores(g, nk, s_ref):
        q0 = pl.multiple_of(g * t, t)
        k0 = pl.multiple_of((g - (nk - 1)) * t, t)
        qt = (q_ref[pl.ds(q0, t), :] * (DH_B ** -0.5 * LOG2E)).T.astype(BF16)
        s_ref[(nkt - nk) * t:, :] = jnp.dot(kbf_ref[pl.ds(k0, nk * t), :], qt, preferred_element_type=F32)

    def finish(g, nk, s_ref):
        s = s_ref[(nkt - nk) * t:, :] + bias_ref[(nkt - nk) * t:, :]
        m = jnp.max(s, axis=0, keepdims=True)
        p = jnp.exp2(s - m)
        l = jnp.sum(p, axis=0, keepdims=True)
        pb = p.astype(BF16)
        o = jnp.dot(vt_ref[g - (nk - 1)], pb[:t], preferred_element_type=F32)
        for c in range(1, nk):
            o = o + jnp.dot(vt_ref[g - (nk - 1) + c], pb[c * t:(c + 1) * t], preferred_element_type=F32)
        o_ref[pl.ds(pl.multiple_of(g * t, t), t), :] = (o * (1.0 / l)).T

    first = nkt - 1
    for g in range(first):
        scores(g, g + 1, s_refs[g % 4])
        finish(g, g + 1, s_refs[g % 4])

    assert (nq - first - 2) % 4 == 0
    scores(first, nkt, s_refs[0])
    scores(first + 1, nkt, s_refs[1])

    def body(u, c):
        g = first + 4 * u
        for n in range(4):
            scores(g + n + 2, nkt, s_refs[(n + 2) % 4])
            finish(g + n, nkt, s_refs[n])
        return c
    lax.fori_loop(0, (nq - first - 2) // 4, body, 0)
    finish(nq - 2, nkt, s_refs[0])
    finish(nq - 1, nkt, s_refs[1])


def _attn_b_prompt(proj, rel_bias):
    s = proj.shape[1]
    t = B_TQ
    nkt = BAND_PAST // t + 1
    assert s % t == 0 and BAND_PAST % t == 0 and t % REL_CLIP == 0 and REL_CLIP % CHUNK == 0
    head = lambda c: pl.BlockSpec((None, s, HEAD_W), lambda h: (c, 0, h))
    return pl.pallas_call(
        _attn_b_kernel,
        grid=(H_B,),
        in_specs=[pl.BlockSpec(memory_space=pltpu.SMEM), head(QB), head(KB), head(VB)],
        out_specs=pl.BlockSpec((s, HEAD_W), lambda h: (0, h)),
        out_shape=jax.ShapeDtypeStruct((s, H_B * DH_B), F32),
        scratch_shapes=[
            pltpu.VMEM((s, HEAD_W), BF16),
            pltpu.VMEM((s // t, DH_B, t), BF16),
            pltpu.VMEM((nkt * t, t), F32),
        ] + [pltpu.VMEM((nkt * t, t), F32)] * 4,
        compiler_params=pltpu.CompilerParams(
            dimension_semantics=("arbitrary",), vmem_limit_bytes=VMEM_LIMIT),
        name="attn_b_prompt",
    )(rel_bias, proj, proj, proj)


def _dot_nt(a, b):
    return lax.dot_general(a, b, (((1,), (1,)), ((), ())), preferred_element_type=F32)


def _attn_sample_kernel(t5_ref, rb_ref, lq1, lk1, lq2, lk2, g_ref,
                        qa_ref, ka_ref, va_ref, cka_ref, cva_ref,
                        qb_ref, kb_ref, vb_ref, ckb_ref, cvb_ref,
                        oa_ref, ob_ref, kroll_ref, vroll_ref,
                        ba_ref, bb_ref, *, lam_init, past, win):
    n = qa_ref.shape[0]
    near = NEAR
    nh = H_A

    @pl.when(pl.program_id(0) == 0)
    def _():
        qry = lax.broadcasted_iota(jnp.int32, (n, near + n), 0)
        key = lax.broadcasted_iota(jnp.int32, (n, near + n), 1)
        rel = key - near - qry
        for h in range(nh):
            ba_ref[h] = _t5_bias_tile(rel, -(near + n - 1), n - 1, t5_ref, h)
            bb_ref[h] = _rel_bias_tile(rel, -(near + n - 1), n - 1, rb_ref, h)

    kroll_ref[:(win - n) * nh, :] = ckb_ref[n * nh:, :]
    vroll_ref[:(win - n) * nh, :] = cvb_ref[n * nh:, :]

    lam = _lambda(lq1, lk1, lq2, lk2, lam_init)
    lane = lax.broadcasted_iota(jnp.int32, (n, HEAD_W), 1)
    for h in range(nh):
        cols = slice(h * HEAD_W, (h + 1) * HEAD_W)
        new_rows = pl.ds((win - n) * nh + h, n, stride=nh)
        kroll_ref[new_rows, :] = kb_ref[:, cols]
        vroll_ref[new_rows, :] = vb_ref[:, cols]

        q = qa_ref[:, cols] * (DK_A ** -0.5)
        q2 = jnp.concatenate([jnp.where(lane < DK_A, q, 0.0), jnp.where(lane >= DK_A, q, 0.0)],
                             axis=0).astype(BF16)
        ba = ba_ref[h]
        ba2 = jnp.concatenate([ba, ba], axis=0)
        s_c = _dot_nt(q2, cka_ref[:, cols].astype(BF16))
        s_c = jnp.concatenate([s_c[:, :past - near], s_c[:, past - near:] + ba2[:, :near]], axis=1)
        s_n = _dot_nt(q2, ka_ref[:, cols].astype(BF16)) + ba2[:, near:]
        m = jnp.maximum(jnp.max(s_c, axis=-1, keepdims=True), jnp.max(s_n, axis=-1, keepdims=True))
        p_c = jnp.exp(s_c - m)
        p_n = jnp.exp(s_n - m)
        l = jnp.sum(p_c, axis=-1, keepdims=True) + jnp.sum(p_n, axis=-1, keepdims=True)
        cva = cva_ref[pl.ds(h, past, stride=nh), :].astype(BF16)
        o2 = (jnp.dot(p_c.astype(BF16), cva, preferred_element_type=F32)
              + jnp.dot(p_n.astype(BF16), va_ref[:, cols].astype(BF16), preferred_element_type=F32)) * (1.0 / l)
        o = o2[:n] - lam * o2[n:]
        ms = jnp.mean(o * o, axis=-1, keepdims=True)
        oa_ref[:, cols] = (o * lax.rsqrt(ms + EPS) * g_ref[...]) * (1.0 - lam_init)

        qb = qb_ref[:, cols].astype(BF16)
        bb = bb_ref[h]
        ckb = ckb_ref[pl.ds(h, win, stride=nh), :].astype(BF16)
        cvb = cvb_ref[pl.ds(h, win, stride=nh), :].astype(BF16)
        s_c = _dot_nt(qb, ckb) * (DH_B ** -0.5)
        s_c = jnp.concatenate([s_c[:, :win - near] + rb_ref[h, 0], s_c[:, win - near:] + bb[:, :near]], axis=1)
        s_n = _dot_nt(qb, kb_ref[:, cols].astype(BF16)) * (DH_B ** -0.5) + bb[:, near:]
        m = jnp.maximum(jnp.max(s_c, axis=-1, keepdims=True), jnp.max(s_n, axis=-1, keepdims=True))
        p_c = jnp.exp(s_c - m)
        p_n = jnp.exp(s_n - m)
        l = jnp.sum(p_c, axis=-1, keepdims=True) + jnp.sum(p_n, axis=-1, keepdims=True)
        ob_ref[:, cols] = (jnp.dot(p_c.astype(BF16), cvb, preferred_element_type=F32)
                           + jnp.dot(p_n.astype(BF16), vb_ref[:, cols].astype(BF16),
                                     preferred_element_type=F32)) * (1.0 / l)


def _attn_sample(proj, ck_a, cv_a, ck_b, cv_b, t5_bias, rel_bias, lq1, lk1, lq2, lk2, subln_g, lam_init):
    nb, past = ck_a.shape[0], ck_a.shape[1]
    win = ck_b.shape[1]
    n = proj.shape[1] // nb
    near = NEAR
    wide = H_A * HEAD_W
    assert past % CHUNK == 0 and n <= CHUNK and win <= BAND_PAST and win <= past
    assert near % HEAD_W == 0 and near <= win and near <= past and n % 8 == 0 and T5_MAX_DIST <= REL_CLIP
    assert H_A == H_B and DV_A == HEAD_W and DH_B == HEAD_W and 2 * DK_A == HEAD_W
    cka = ck_a.reshape(nb, past, wide)
    cva = cv_a.reshape(nb, past * H_A, DV_A)
    ckb = ck_b.reshape(nb, win * H_B, DH_B)
    cvb = cv_b.reshape(nb, win * H_B, DH_B)
    vec = lambda: pl.BlockSpec((1, DK_A), lambda b: (0, 0))
    new = lambda c: pl.BlockSpec((None, n, wide), lambda b: (c, b, 0))
    seq = lambda rows, cols: pl.BlockSpec((None, rows, cols), lambda b: (b, 0, 0))
    out = pl.BlockSpec((n, wide), lambda b: (b, 0))
    return pl.pallas_call(
        functools.partial(_attn_sample_kernel, lam_init=lam_init, past=past, win=win),
        grid=(nb,),
        in_specs=[
            pl.BlockSpec(memory_space=pltpu.SMEM),
            pl.BlockSpec(memory_space=pltpu.SMEM),
            vec(), vec(), vec(), vec(),
            pl.BlockSpec((1, DV_A), lambda b: (0, 0)),
            new(QA), new(KA), new(VA), seq(past, wide), seq(past * H_A, HEAD_W),
            new(QB), new(KB), new(VB), seq(win * H_B, HEAD_W), seq(win * H_B, HEAD_W),
        ],
        out_specs=[out, out, seq(win * H_B, HEAD_W), seq(win * H_B, HEAD_W)],
        out_shape=[
            jax.ShapeDtypeStruct((nb * n, wide), F32),
            jax.ShapeDtypeStruct((nb * n, wide), F32),
            jax.ShapeDtypeStruct((nb, win * H_B, DH_B), F32),
            jax.ShapeDtypeStruct((nb, win * H_B, DH_B), F32),
        ],
        scratch_shapes=[pltpu.VMEM((H_A, n, near + n), F32), pltpu.VMEM((H_B, n, near + n), F32)],
        compiler_params=pltpu.CompilerParams(
            dimension_semantics=("arbitrary",), vmem_limit_bytes=VMEM_LIMIT),
        name="attn_sample",
    )(t5_bias, rel_bias, lq1.reshape(1, DK_A), lk1.reshape(1, DK_A), lq2.reshape(1, DK_A),
      lk2.reshape(1, DK_A), subln_g.reshape(1, DV_A),
      proj, proj, proj, cka, cva, proj, proj, proj, ckb, cvb)


def _merge_kernel(x_ref, oa_ref, ob_ref, za_ref, zb_ref, ga0_ref, ga1_ref, gb0_ref, gb1_ref,
                  woa_ref, wob_ref, wout_ref, pg_ref, y_ref):
    za = za_ref[...]
    zb = zb_ref[...]
    a = (oa_ref[...] * (za * _sigmoid(za))).astype(BF16)
    b = (ob_ref[...] * (zb * _sigmoid(zb))).astype(BF16)
    ya = jnp.dot(a, woa_ref[...], preferred_element_type=F32)
    yb = jnp.dot(b, wob_ref[...], preferred_element_type=F32)
    ga = jnp.concatenate([ga0_ref[...], ga1_ref[...]], axis=1)
    gb = jnp.concatenate([gb0_ref[...], gb1_ref[...]], axis=1)
    mix = (_sigmoid(ga) * ya + _sigmoid(gb) * yb).astype(BF16)
    y = jnp.dot(mix, wout_ref[...], preferred_element_type=F32)
    ms = jnp.mean(y * y, axis=-1, keepdims=True)
    y_ref[...] = x_ref[...] + y * lax.rsqrt(ms + EPS) * pg_ref[...]


def _merge(x2d, o_a, o_b, proj, woa, wob, wout, post_g, tm):
    m, d = x2d.shape
    wa = o_a.shape[1]
    wb = o_b.shape[1]
    assert m % tm == 0 and wa == COL_BLOCK and wb == COL_BLOCK and d == 2 * COL_BLOCK
    row = lambda w: pl.BlockSpec((tm, w), lambda i: (i, 0))
    col = lambda c: pl.BlockSpec((None, tm, COL_BLOCK), lambda i: (c, i, 0))
    resident = lambda r, c: pl.BlockSpec((r, c), lambda i: (0, 0), pipeline_mode=pl.Buffered(1))
    return pl.pallas_call(
        _merge_kernel,
        grid=(m // tm,),
        in_specs=[row(d), row(wa), row(wb), col(ZA), col(ZB), col(GA0), col(GA1), col(GB0), col(GB1),
                  resident(wa, d), resident(wb, d), resident(d, d), resident(1, d)],
        out_specs=row(d),
        out_shape=jax.ShapeDtypeStruct((m, d), F32),
        compiler_params=pltpu.CompilerParams(
            dimension_semantics=("arbitrary",), vmem_limit_bytes=VMEM_LIMIT),
        name="merge",
    )(x2d, o_a, o_b, proj, proj, proj, proj, proj, proj, woa, wob, wout, post_g.reshape(1, d))


def kernel(x_prompt, x_sample, cache_k_a, cache_v_a, cache_k_b, cache_v_b, t5_bias, pre_norm, post_norm,
           w_in, lambda_q1, lambda_k1, lambda_q2, lambda_k2, subln_a, rel_bias_b, w_o_a, w_o_b, w_out):
    depth = w_in.shape[0]
    bp, sp, d = x_prompt.shape
    bs, ss, _ = x_sample.shape
    assert bp == 1 and w_in.shape[2] == 12 * COL_BLOCK
    yp = x_prompt.reshape(sp, d)
    ys = x_sample.reshape(bs * ss, d)
    tail = min(BAND_PAST, sp)
    outs = [[] for _ in range(8)]
    for l in range(depth):
        lam_init = 0.8 - 0.6 * math.exp(-0.3 * l)
        w = w_in[l].astype(BF16)
        woa = w_o_a[l].astype(BF16)
        wob = w_o_b[l].astype(BF16)
        wout = w_out[l].astype(BF16)
        lam_args = (lambda_q1[l], lambda_k1[l], lambda_q2[l], lambda_k2[l], subln_a[l], lam_init)

        pp, ka, va = _in_proj(yp, pre_norm[l], w, tm=1024, blocks_per_step=1)
        oa = _attn_a_prompt(pp, t5_bias, *lam_args)
        ob = _attn_b_prompt(pp, rel_bias_b[l])
        yp = _merge(yp, oa, ob, pp, woa, wob, wout, post_norm[l], tm=256)
        outs[0].append(ka.reshape(bp, sp, 2 * H_A, DK_A))
        outs[1].append(va.reshape(bp, sp, H_A, DV_A))
        outs[2].append(pp[KB, sp - tail:].reshape(bp, tail, H_B, DH_B))
        outs[3].append(pp[VB, sp - tail:].reshape(bp, tail, H_B, DH_B))

        ps, ka, va = _in_proj(ys, pre_norm[l], w, tm=bs * ss, blocks_per_step=3)
        oas, obs, kroll, vroll = _attn_sample(ps, cache_k_a[l], cache_v_a[l], cache_k_b[l], cache_v_b[l],
                                              t5_bias, rel_bias_b[l], *lam_args)
        ys = _merge(ys, oas, obs, ps, woa, wob, wout, post_norm[l], tm=bs * ss)
        outs[4].append(ka.reshape(bs, ss, 2 * H_A, DK_A))
        outs[5].append(va.reshape(bs, ss, H_A, DV_A))
        outs[6].append(kroll.reshape(bs, -1, H_B, DH_B))
        outs[7].append(vroll.reshape(bs, -1, H_B, DH_B))
    return (yp.reshape(bp, sp, d), ys.reshape(bs, ss, d)) + tuple(jnp.stack(o) for o in outs)
```

```python
import functools
import math

import numpy as np
import jax
import jax.numpy as jnp
from jax import lax
from jax.experimental import pallas as pl
from jax.experimental.pallas import tpu as pltpu

F32 = jnp.float32
BF16 = jnp.bfloat16

CHUNK = 64
H_A = 8
DK_A = 64
DV_A = 2 * DK_A
H_B = 8
DH_B = 128
BAND_CHUNKS = 8
BAND_PAST = BAND_CHUNKS * CHUNK
REL_CLIP = 128
T5_BUCKETS = 32
T5_MAX_DIST = 128
EPS = 1e-6

HEAD_W = 128
COL_BLOCK = 1024
A_TQ, A_TK = 512, 256
B_TQ = 256
LOG2E = math.log2(math.e)
NEAR = max(REL_CLIP, T5_MAX_DIST)
VMEM_LIMIT = 60 * 1024 * 1024

QA, ZA, QB, KB, VB, ZB, GA0, GA1, GB0, GB1 = range(10)
COL_KA, COL_VA = 1, 2
COL_SLOT = (QA, None, None, ZA, QB, KB, VB, ZB, GA0, GA1, GB0, GB1)


def _t5_bucket_int(rel):
    half = T5_BUCKETS // 2
    max_exact = half // 2
    n = abs(rel)
    ret = half if rel > 0 else 0
    if n < max_exact:
        return ret + n
    assert (T5_MAX_DIST // max_exact) ** 2 == 2 ** (half - max_exact)
    j = 0
    while n * n >= (max_exact * max_exact) * 2 ** (j + 1):
        j += 1
    return ret + min(max_exact + j, half - 1)


def _t5_runs(lo, hi):
    runs = []
    for r in range(lo, hi + 1):
        b = _t5_bucket_int(r)
        if not runs or runs[-1][1] != b:
            runs.append((r, b))
    return runs


T5_FAR_BUCKET = _t5_bucket_int(-T5_MAX_DIST)
assert all(_t5_bucket_int(-n) == T5_FAR_BUCKET for n in range(T5_MAX_DIST, 4 * T5_MAX_DIST))


def _t5_bias_tile(rel, lo, hi, t5_ref, h):
    runs = _t5_runs(lo, hi)
    val = jnp.full(rel.shape, t5_ref[runs[0][1], h], F32)
    for start, b in runs[1:]:
        val = jnp.where(rel >= start, t5_ref[b, h], val)
    return val - t5_ref[T5_FAR_BUCKET, h]


def _rel_bias_tile(rel, lo, hi, rb_ref, h):
    lo = max(lo, -REL_CLIP)
    hi = min(hi, REL_CLIP)
    val = jnp.full(rel.shape, rb_ref[h, lo + REL_CLIP], F32)
    for d in range(lo + 1, hi + 1):
        val = jnp.where(rel >= d, rb_ref[h, d + REL_CLIP], val)
    return val


def _sigmoid(x):
    return 1.0 / (1.0 + jnp.exp(-x))


def _lambda(lq1, lk1, lq2, lk2, lam_init):
    a = jnp.sum(lq1[...] * lk1[...], axis=-1, keepdims=True)
    b = jnp.sum(lq2[...] * lk2[...], axis=-1, keepdims=True)
    return jnp.exp(a) - jnp.exp(b) + lam_init


def _in_proj_kernel(x_ref, g_ref, w_ref, o_ref, ka_ref, va_ref, h_ref):
    j = pl.program_id(1)

    @pl.when(j == 0)
    def _():
        x = x_ref[...]
        ms = jnp.mean(x * x, axis=-1, keepdims=True)
        h_ref[...] = (x * lax.rsqrt(ms + EPS) * g_ref[...]).astype(BF16)

    def project(ref):
        ref[...] = jnp.dot(h_ref[...], w_ref[...], preferred_element_type=F32)

    pl.when(j == COL_KA)(functools.partial(project, ka_ref))
    pl.when(j == COL_VA)(functools.partial(project, va_ref))
    pl.when(jnp.logical_and(j != COL_KA, j != COL_VA))(functools.partial(project, o_ref))


def _in_proj(x2d, pre_g, w_bf16, tm):
    m, d = x2d.shape
    n = w_bf16.shape[1]
    assert m % tm == 0 and n == len(COL_SLOT) * COL_BLOCK

    def own(col):
        return pl.BlockSpec((tm, COL_BLOCK), lambda i, j: (jnp.where(j >= col, i, jnp.maximum(i - 1, 0)), 0))

    def slab_index(i, j):
        slot = jnp.where(j <= COL_VA, 0, j - 2)
        return slot, i, 0

    assert COL_SLOT[0] == 0 and COL_SLOT[COL_VA + 1:] == tuple(range(1, len(COL_SLOT) - 2))
    return pl.pallas_call(
        _in_proj_kernel,
        grid=(m // tm, n // COL_BLOCK),
        in_specs=[
            pl.BlockSpec((tm, d), lambda i, j: (i, 0)),
            pl.BlockSpec((1, d), lambda i, j: (0, 0)),
            pl.BlockSpec((d, COL_BLOCK), lambda i, j: (0, j)),
        ],
        out_specs=[pl.BlockSpec((None, tm, COL_BLOCK), slab_index), own(COL_KA), own(COL_VA)],
        out_shape=[jax.ShapeDtypeStruct((len(COL_SLOT) - 2, m, COL_BLOCK), F32),
                   jax.ShapeDtypeStruct((m, COL_BLOCK), F32),
                   jax.ShapeDtypeStruct((m, COL_BLOCK), F32)],
        scratch_shapes=[pltpu.VMEM((tm, d), BF16)],
        compiler_params=pltpu.CompilerParams(
            dimension_semantics=("arbitrary", "arbitrary"), vmem_limit_bytes=VMEM_LIMIT),
        name="in_proj",
    )(x2d, pre_g.reshape(1, d), w_bf16)


def _online_softmax_step(s, vt, m_ref, l_ref, acc_ref):
    m_old = m_ref[...]
    m_new = jnp.maximum(m_old, jnp.max(s, axis=0, keepdims=True))
    alpha = jnp.exp2(m_old - m_new)
    p = jnp.exp2(s - m_new)
    l_ref[...] = alpha * l_ref[...] + jnp.sum(p, axis=0, keepdims=True)
    acc_ref[...] = alpha * acc_ref[...] + jnp.dot(vt, p.astype(BF16), preferred_element_type=F32)
    m_ref[...] = m_new


def _stage_keys_values(k_ref, v_ref, kbf_ref, vt_ref, t):
    def body(j, c):
        r = pl.multiple_of(j * t, t)
        kbf_ref[pl.ds(r, t), :] = k_ref[pl.ds(r, t), :].astype(BF16)
        vt_ref[j] = v_ref[pl.ds(r, t), :].T.astype(BF16)
        return c
    lax.fori_loop(0, vt_ref.shape[0], body, 0, unroll=4)


def _attn_a_kernel(t5_ref, lq1, lk1, lq2, lk2, g_ref, q_ref, k_ref, v_ref, o_ref,
                   kbf_ref, vt_ref, bias_ref, qt_ref, m_ref, l_ref, acc_ref, *s_refs, lam_init):
    tq, tk = A_TQ, A_TK
    h = pl.program_id(0)
    nq = q_ref.shape[0] // tq

    _stage_keys_values(k_ref, v_ref, kbf_ref, vt_ref, tk)
    key = lax.broadcasted_iota(jnp.int32, (tk, tq), 0)
    qry = lax.broadcasted_iota(jnp.int32, (tk, tq), 1)
    for n in range(3):
        rel = key + (n - 1) * tk - qry
        lo, hi = (n - 1) * tk - (tq - 1), min(n * tk - 1, CHUNK - 1)
        b = _t5_bias_tile(jnp.minimum(rel, hi), lo, hi, t5_ref, h) * LOG2E
        if n >= 1:
            b = jnp.where((key + (n - 1) * tk) // CHUNK <= qry // CHUNK, b, -jnp.inf)
        bias_ref[n] = b

    lam = _lambda(lq1, lk1, lq2, lk2, lam_init)

    def q_tile(i, carry):
        q0 = pl.multiple_of(i * tq, tq)
        qt = (q_ref[pl.ds(q0, tq), :] * (DK_A ** -0.5 * LOG2E)).T
        sub = lax.broadcasted_iota(jnp.int32, (HEAD_W, tq), 0)
        qt_ref[:, :tq] = jnp.where(sub < DK_A, qt, 0.0).astype(BF16)
        qt_ref[:, tq:] = jnp.where(sub >= DK_A, qt, 0.0).astype(BF16)
        m_ref[...] = jnp.full(m_ref.shape, -jnp.inf, F32)
        l_ref[...] = jnp.zeros(l_ref.shape, F32)
        acc_ref[...] = jnp.zeros(acc_ref.shape, F32)

        def scores(j, s_ref):
            r = pl.multiple_of(j * tk, tk)
            s_ref[...] = jnp.dot(kbf_ref[pl.ds(r, tk), :], qt_ref[...], preferred_element_type=F32)

        def update(j, s_ref, bias_idx):
            s = s_ref[...]
            if bias_idx is not None:
                b = bias_ref[bias_idx]
                s = jnp.concatenate([s[:, :tq] + b, s[:, tq:] + b], axis=1)
            _online_softmax_step(s, vt_ref[j], m_ref, l_ref, acc_ref)

        scores(0, s_refs[0])
        scores(1, s_refs[1])

        odd = jnp.logical_and(i >= 2, i % 2 == 0)

        @pl.when(odd)
        def _():
            update(0, s_refs[0], None)
            scores(2, s_refs[0])
            update(1, s_refs[1], None)
            scores(3, s_refs[1])

        j0 = jnp.where(odd, 2, 0)

        def quad(u, c):
            j = j0 + 4 * u
            for n in range(4):
                scores(j + n + 2, s_refs[(n + 2) % 4])
                update(j + n, s_refs[n], None)
            return c
        lax.fori_loop(0, jnp.maximum(i - 1, 0) // 2, quad, 0)

        @pl.when(i >= 1)
        def _():
            j = 2 * i - 2
            scores(j + 2, s_refs[2])
            update(j, s_refs[0], None)
            scores(j + 3, s_refs[3])
            update(j + 1, s_refs[1], 0)
            update(j + 2, s_refs[2], 1)
            update(j + 3, s_refs[3], 2)

        @pl.when(i == 0)
        def _():
            update(0, s_refs[0], 1)
            update(1, s_refs[1], 2)

        inv = 1.0 / l_ref[...]
        acc = acc_ref[...]
        o = acc[:, :tq] * inv[:, :tq] - lam * (acc[:, tq:] * inv[:, tq:])
        ms = jnp.mean(o * o, axis=0, keepdims=True)
        y = (o * lax.rsqrt(ms + EPS) * g_ref[...]) * (1.0 - lam_init)
        o_ref[pl.ds(q0, tq), :] = y.T
        return carry

    lax.fori_loop(0, nq, q_tile, 0)


def _attn_a_prompt(proj, k_a, v_a, t5_bias, lq1, lk1, lq2, lk2, subln_g, lam_init):
    s = proj.shape[1]
    tq, tk = A_TQ, A_TK
    assert s % tq == 0 and tq == 2 * tk and tk % CHUNK == 0 and tk >= T5_MAX_DIST
    vec = lambda: pl.BlockSpec((1, DK_A), lambda h: (0, 0))
    head = lambda: pl.BlockSpec((s, HEAD_W), lambda h: (0, h))
    return pl.pallas_call(
        functools.partial(_attn_a_kernel, lam_init=lam_init),
        grid=(H_A,),
        in_specs=[
            pl.BlockSpec(memory_space=pltpu.SMEM),
            vec(), vec(), vec(), vec(),
            pl.BlockSpec((DV_A, 1), lambda h: (0, 0)),
            pl.BlockSpec((None, s, HEAD_W), lambda h: (QA, 0, h)), head(), head(),
        ],
        out_specs=pl.BlockSpec((s, HEAD_W), lambda h: (0, h)),
        out_shape=jax.ShapeDtypeStruct((s, H_A * DV_A), F32),
        scratch_shapes=[
            pltpu.VMEM((s, HEAD_W), BF16),
            pltpu.VMEM((s // tk, DV_A, tk), BF16),
            pltpu.VMEM((3, tk, tq), F32),
            pltpu.VMEM((HEAD_W, 2 * tq), BF16),
            pltpu.VMEM((1, 2 * tq), F32),
            pltpu.VMEM((1, 2 * tq), F32),
            pltpu.VMEM((DV_A, 2 * tq), F32),
        ] + [pltpu.VMEM((tk, 2 * tq), F32)] * 4,
        compiler_params=pltpu.CompilerParams(
            dimension_semantics=("arbitrary",), vmem_limit_bytes=VMEM_LIMIT),
        name="attn_a_prompt",
    )(t5_bias, lq1.reshape(1, DK_A), lk1.reshape(1, DK_A), lq2.reshape(1, DK_A), lk2.reshape(1, DK_A),
      subln_g.reshape(DV_A, 1), proj, k_a, v_a)


def _attn_b_kernel(rb_ref, q_ref, k_ref, v_ref, o_ref, kbf_ref, vt_ref, bias_ref, *s_refs):
    t = B_TQ
    blk = REL_CLIP
    nkt = BAND_PAST // t + 1
    nq = q_ref.shape[0] // t
    h = pl.program_id(0)

    _stage_keys_values(k_ref, v_ref, kbf_ref, vt_ref, t)

    kk = lax.broadcasted_iota(jnp.int32, (blk, blk), 0)
    qq = lax.broadcasted_iota(jnp.int32, (blk, blk), 1)
    rel = kk - qq
    lo = jnp.full((blk, blk), rb_ref[h, 0] * LOG2E, F32)
    same = _rel_bias_tile(rel, -(blk - 1), blk - 1, rb_ref, h) * LOG2E
    prev = _rel_bias_tile(rel - blk, -(2 * blk - 1), -1, rb_ref, h) * LOG2E
    ninf = jnp.full((blk, blk), -jnp.inf, F32)
    kc = kk // CHUNK
    qc = qq // CHUNK
    far_blocks = BAND_PAST // blk
    for a in range(nkt * t // blk):
        for b in range(t // blk):
            e = a - b
            if e < 0 or e > far_blocks:
                tile = ninf
            elif e == 0:
                tile = jnp.where(kc >= qc, lo, -jnp.inf)
            elif e == far_blocks:
                tile = jnp.where(kc <= qc, same, -jnp.inf)
            elif e == far_blocks - 1:
                tile = prev
            else:
                tile = lo
            bias_ref[a * blk:(a + 1) * blk, b * blk:(b + 1) * blk] = tile

    def scores(g, nk, s_ref):
        q0 = pl.multiple_of(g * t, t)
        k0 = pl.multiple_of((g - (nk - 1)) * t, t)
        qt = (q_ref[pl.ds(q0, t), :] * (DH_B ** -0.5 * LOG2E)).T.astype(BF16)
        s_ref[(nkt - nk) * t:, :] = jnp.dot(kbf_ref[pl.ds(k0, nk * t), :], qt, preferred_element_type=F32)

    def finish(g, nk, s_ref):
        s = s_ref[(nkt - nk) * t:, :] + bias_ref[(nkt - nk) * t:, :]
        m = jnp.max(s, axis=0, keepdims=True)
        p = jnp.exp2(s - m)
        l = jnp.sum(p, axis=0, keepdims=True)
        pb = p.astype(BF16)
        o = jnp.dot(vt_ref[g - (nk - 1)], pb[:t], preferred_element_type=F32)
        for c in range(1, nk):
            o = o + jnp.dot(vt_ref[g - (nk - 1) + c], pb[c * t:(c + 1) * t], preferred_element_type=F32)
        o_ref[pl.ds(pl.multiple_of(g * t, t), t), :] = (o * (1.0 / l)).T

    first = nkt - 1
    for g in range(first):
        scores(g, g + 1, s_refs[g % 4])
        finish(g, g + 1, s_refs[g % 4])

    assert (nq - first - 2) % 4 == 0
    scores(first, nkt, s_refs[0])
    scores(first + 1, nkt, s_refs[1])

    def body(u, c):
        g = first + 4 * u
        for n in range(4):
            scores(g + n + 2, nkt, s_refs[(n + 2) % 4])
            finish(g + n, nkt, s_refs[n])
        return c
    lax.fori_loop(0, (nq - first - 2) // 4, body, 0)
    finish(nq - 2, nkt, s_refs[0])
    finish(nq - 1, nkt, s_refs[1])


def _attn_b_prompt(proj, rel_bias):
    s = proj.shape[1]
    t = B_TQ
    nkt = BAND_PAST // t + 1
    assert s % t == 0 and BAND_PAST % t == 0 and t % REL_CLIP == 0 and REL_CLIP % CHUNK == 0
    head = lambda c: pl.BlockSpec((None, s, HEAD_W), lambda h: (c, 0, h))
    return pl.pallas_call(
        _attn_b_kernel,
        grid=(H_B,),
        in_specs=[pl.BlockSpec(memory_space=pltpu.SMEM), head(QB), head(KB), head(VB)],
        out_specs=pl.BlockSpec((s, HEAD_W), lambda h: (0, h)),
        out_shape=jax.ShapeDtypeStruct((s, H_B * DH_B), F32),
        scratch_shapes=[
            pltpu.VMEM((s, HEAD_W), BF16),
            pltpu.VMEM((s // t, DH_B, t), BF16),
            pltpu.VMEM((nkt * t, t), F32),
        ] + [pltpu.VMEM((nkt * t, t), F32)] * 4,
        compiler_params=pltpu.CompilerParams(
            dimension_semantics=("arbitrary",), vmem_limit_bytes=VMEM_LIMIT),
        name="attn_b_prompt",
    )(rel_bias, proj, proj, proj)


def _dot_nt(a, b):
    return lax.dot_general(a, b, (((1,), (1,)), ((), ())), preferred_element_type=F32)


def _attn_sample_kernel(t5_ref, rb_ref, lq1, lk1, lq2, lk2, g_ref,
                        qa_ref, ka_ref, va_ref, cka_ref, cva_ref,
                        qb_ref, kb_ref, vb_ref, ckb_ref, cvb_ref,
                        oa_ref, ob_ref, kroll_ref, vroll_ref,
                        ba_ref, bb_ref, *, lam_init, past, win):
    n = qa_ref.shape[0]
    near = NEAR
    nh = H_A

    @pl.when(pl.program_id(0) == 0)
    def _():
        qry = lax.broadcasted_iota(jnp.int32, (n, near + n), 0)
        key = lax.broadcasted_iota(jnp.int32, (n, near + n), 1)
        rel = key - near - qry
        for h in range(nh):
            ba_ref[h] = _t5_bias_tile(rel, -(near + n - 1), n - 1, t5_ref, h)
            bb_ref[h] = _rel_bias_tile(rel, -(near + n - 1), n - 1, rb_ref, h)

    kroll_ref[:(win - n) * nh, :] = ckb_ref[n * nh:, :]
    vroll_ref[:(win - n) * nh, :] = cvb_ref[n * nh:, :]

    lam = _lambda(lq1, lk1, lq2, lk2, lam_init)
    lane = lax.broadcasted_iota(jnp.int32, (n, HEAD_W), 1)
    for h in range(nh):
        cols = slice(h * HEAD_W, (h + 1) * HEAD_W)
        new_rows = pl.ds((win - n) * nh + h, n, stride=nh)
        kroll_ref[new_rows, :] = kb_ref[:, cols]
        vroll_ref[new_rows, :] = vb_ref[:, cols]

        q = qa_ref[:, cols] * (DK_A ** -0.5)
        q2 = jnp.concatenate([jnp.where(lane < DK_A, q, 0.0), jnp.where(lane >= DK_A, q, 0.0)],
                             axis=0).astype(BF16)
        ba = ba_ref[h]
        ba2 = jnp.concatenate([ba, ba], axis=0)
        s_c = _dot_nt(q2, cka_ref[:, cols].astype(BF16))
        s_c = jnp.concatenate([s_c[:, :past - near], s_c[:, past - near:] + ba2[:, :near]], axis=1)
        s_n = _dot_nt(q2, ka_ref[:, cols].astype(BF16)) + ba2[:, near:]
        m = jnp.maximum(jnp.max(s_c, axis=-1, keepdims=True), jnp.max(s_n, axis=-1, keepdims=True))
        p_c = jnp.exp(s_c - m)
        p_n = jnp.exp(s_n - m)
        l = jnp.sum(p_c, axis=-1, keepdims=True) + jnp.sum(p_n, axis=-1, keepdims=True)
        cva = cva_ref[pl.ds(h, past, stride=nh), :].astype(BF16)
        o2 = (jnp.dot(p_c.astype(BF16), cva, preferred_element_type=F32)
              + jnp.dot(p_n.astype(BF16), va_ref[:, cols].astype(BF16), preferred_element_type=F32)) * (1.0 / l)
        o = o2[:n] - lam * o2[n:]
        ms = jnp.mean(o * o, axis=-1, keepdims=True)
        oa_ref[:, cols] = (o * lax.rsqrt(ms + EPS) * g_ref[...]) * (1.0 - lam_init)

        qb = qb_ref[:, cols].astype(BF16)
        bb = bb_ref[h]
        ckb = ckb_ref[pl.ds(h, win, stride=nh), :].astype(BF16)
        cvb = cvb_ref[pl.ds(h, win, stride=nh), :].astype(BF16)
        s_c = _dot_nt(qb, ckb) * (DH_B ** -0.5)
        s_c = jnp.concatenate([s_c[:, :win - near] + rb_ref[h, 0], s_c[:, win - near:] + bb[:, :near]], axis=1)
        s_n = _dot_nt(qb, kb_ref[:, cols].astype(BF16)) * (DH_B ** -0.5) + bb[:, near:]
        m = jnp.maximum(jnp.max(s_c, axis=-1, keepdims=True), jnp.max(s_n, axis=-1, keepdims=True))
        p_c = jnp.exp(s_c - m)
        p_n = jnp.exp(s_n - m)
        l = jnp.sum(p_c, axis=-1, keepdims=True) + jnp.sum(p_n, axis=-1, keepdims=True)
        ob_ref[:, cols] = (jnp.dot(p_c.astype(BF16), cvb, preferred_element_type=F32)
                           + jnp.dot(p_n.astype(BF16), vb_ref[:, cols].astype(BF16),
                                     preferred_element_type=F32)) * (1.0 / l)


def _attn_sample(proj, k_a, v_a, ck_a, cv_a, ck_b, cv_b, t5_bias, rel_bias, lq1, lk1, lq2, lk2, subln_g,
                 lam_init):
    nb, past = ck_a.shape[0], ck_a.shape[1]
    win = ck_b.shape[1]
    n = proj.shape[1] // nb
    near = NEAR
    wide = H_A * HEAD_W
    assert past % CHUNK == 0 and n <= CHUNK and win <= BAND_PAST and win <= past
    assert near % HEAD_W == 0 and near <= win and near <= past and n % 8 == 0 and T5_MAX_DIST <= REL_CLIP
    assert H_A == H_B and DV_A == HEAD_W and DH_B == HEAD_W and 2 * DK_A == HEAD_W
    cka = ck_a.reshape(nb, past, wide)
    cva = cv_a.reshape(nb, past * H_A, DV_A)
    ckb = ck_b.reshape(nb, win * H_B, DH_B)
    cvb = cv_b.reshape(nb, win * H_B, DH_B)
    vec = lambda: pl.BlockSpec((1, DK_A), lambda b: (0, 0))
    new = lambda c: pl.BlockSpec((None, n, wide), lambda b: (c, b, 0))
    seq = lambda rows, cols: pl.BlockSpec((None, rows, cols), lambda b: (b, 0, 0))
    out = pl.BlockSpec((n, wide), lambda b: (b, 0))
    return pl.pallas_call(
        functools.partial(_attn_sample_kernel, lam_init=lam_init, past=past, win=win),
        grid=(nb,),
        in_specs=[
            pl.BlockSpec(memory_space=pltpu.SMEM),
            pl.BlockSpec(memory_space=pltpu.SMEM),
            vec(), vec(), vec(), vec(),
            pl.BlockSpec((1, DV_A), lambda b: (0, 0)),
            new(QA), out, out, seq(past, wide), seq(past * H_A, HEAD_W),
            new(QB), new(KB), new(VB), seq(win * H_B, HEAD_W), seq(win * H_B, HEAD_W),
        ],
        out_specs=[out, out, seq(win * H_B, HEAD_W), seq(win * H_B, HEAD_W)],
        out_shape=[
            jax.ShapeDtypeStruct((nb * n, wide), F32),
            jax.ShapeDtypeStruct((nb * n, wide), F32),
            jax.ShapeDtypeStruct((nb, win * H_B, DH_B), F32),
            jax.ShapeDtypeStruct((nb, win * H_B, DH_B), F32),
        ],
        scratch_shapes=[pltpu.VMEM((H_A, n, near + n), F32), pltpu.VMEM((H_B, n, near + n), F32)],
        compiler_params=pltpu.CompilerParams(
            dimension_semantics=("arbitrary",), vmem_limit_bytes=VMEM_LIMIT),
        name="attn_sample",
    )(t5_bias, rel_bias, lq1.reshape(1, DK_A), lk1.reshape(1, DK_A), lq2.reshape(1, DK_A),
      lk2.reshape(1, DK_A), subln_g.reshape(1, DV_A),
      proj, k_a, v_a, cka, cva, proj, proj, proj, ckb, cvb)


def _merge_kernel(x_ref, oa_ref, ob_ref, za_ref, zb_ref, ga0_ref, ga1_ref, gb0_ref, gb1_ref,
                  woa_ref, wob_ref, wout_ref, pg_ref, y_ref):
    za = za_ref[...]
    zb = zb_ref[...]
    a = (oa_ref[...] * (za * _sigmoid(za))).astype(BF16)
    b = (ob_ref[...] * (zb * _sigmoid(zb))).astype(BF16)
    ya = jnp.dot(a, woa_ref[...], preferred_element_type=F32)
    yb = jnp.dot(b, wob_ref[...], preferred_element_type=F32)
    ga = jnp.concatenate([ga0_ref[...], ga1_ref[...]], axis=1)
    gb = jnp.concatenate([gb0_ref[...], gb1_ref[...]], axis=1)
    mix = (_sigmoid(ga) * ya + _sigmoid(gb) * yb).astype(BF16)
    y = jnp.dot(mix, wout_ref[...], preferred_element_type=F32)
    ms = jnp.mean(y * y, axis=-1, keepdims=True)
    y_ref[...] = x_ref[...] + y * lax.rsqrt(ms + EPS) * pg_ref[...]


def _merge(x2d, o_a, o_b, proj, woa, wob, wout, post_g, tm):
    m, d = x2d.shape
    wa = o_a.shape[1]
    wb = o_b.shape[1]
    assert m % tm == 0 and wa == COL_BLOCK and wb == COL_BLOCK and d == 2 * COL_BLOCK
    row = lambda w: pl.BlockSpec((tm, w), lambda i: (i, 0))
    col = lambda c: pl.BlockSpec((None, tm, COL_BLOCK), lambda i: (c, i, 0))
    resident = lambda r, c: pl.BlockSpec((r, c), lambda i: (0, 0), pipeline_mode=pl.Buffered(1))
    return pl.pallas_call(
        _merge_kernel,
        grid=(m // tm,),
        in_specs=[row(d), row(wa), row(wb), col(ZA), col(ZB), col(GA0), col(GA1), col(GB0), col(GB1),
                  resident(wa, d), resident(wb, d), resident(d, d), resident(1, d)],
        out_specs=row(d),
        out_shape=jax.ShapeDtypeStruct((m, d), F32),
        compiler_params=pltpu.CompilerParams(
            dimension_semantics=("arbitrary",), vmem_limit_bytes=VMEM_LIMIT),
        name="merge",
    )(x2d, o_a, o_b, proj, proj, proj, proj, proj, proj, woa, wob, wout, post_g.reshape(1, d))


def kernel(x_prompt, x_sample, cache_k_a, cache_v_a, cache_k_b, cache_v_b, t5_bias, pre_norm, post_norm,
           w_in, lambda_q1, lambda_k1, lambda_q2, lambda_k2, subln_a, rel_bias_b, w_o_a, w_o_b, w_out):
    depth = w_in.shape[0]
    bp, sp, d = x_prompt.shape
    bs, ss, _ = x_sample.shape
    assert bp == 1 and w_in.shape[2] == 12 * COL_BLOCK
    yp = x_prompt.reshape(sp, d)
    ys = x_sample.reshape(bs * ss, d)
    tail = min(BAND_PAST, sp)
    outs = [[] for _ in range(8)]
    for l in range(depth):
        lam_init = 0.8 - 0.6 * math.exp(-0.3 * l)
        w = w_in[l].astype(BF16)
        woa = w_o_a[l].astype(BF16)
        wob = w_o_b[l].astype(BF16)
        wout = w_out[l].astype(BF16)
        lam_args = (lambda_q1[l], lambda_k1[l], lambda_q2[l], lambda_k2[l], subln_a[l], lam_init)

        pp, ka, va = _in_proj(yp, pre_norm[l], w, tm=1024)
        oa = _attn_a_prompt(pp, ka, va, t5_bias, *lam_args)
        ob = _attn_b_prompt(pp, rel_bias_b[l])
        yp = _merge(yp, oa, ob, pp, woa, wob, wout, post_norm[l], tm=256)
        outs[0].append(ka.reshape(bp, sp, 2 * H_A, DK_A))
        outs[1].append(va.reshape(bp, sp, H_A, DV_A))
        outs[2].append(pp[KB, sp - tail:].reshape(bp, tail, H_B, DH_B))
        outs[3].append(pp[VB, sp - tail:].reshape(bp, tail, H_B, DH_B))

        ps, ka, va = _in_proj(ys, pre_norm[l], w, tm=bs * ss)
        oas, obs, kroll, vroll = _attn_sample(ps, ka, va, cache_k_a[l], cache_v_a[l], cache_k_b[l], cache_v_b[l],
                                              t5_bias, rel_bias_b[l], *lam_args)
        ys = _merge(ys, oas, obs, ps, woa, wob, wout, post_norm[l], tm=bs * ss)
        outs[4].append(ka.reshape(bs, ss, 2 * H_A, DK_A))
        outs[5].append(va.reshape(bs, ss, H_A, DV_A))
        outs[6].append(kroll.reshape(bs, -1, H_B, DH_B))
        outs[7].append(vroll.reshape(bs, -1, H_B, DH_B))
    return (yp.reshape(bp, sp, d), ys.reshape(bs, ss, d)) + tuple(jnp.stack(o) for o in outs)
```

```python
import functools
import math

import numpy as np
import jax
import jax.numpy as jnp
from jax import lax
from jax.experimental import pallas as pl
from jax.experimental.pallas import tpu as pltpu

F32 = jnp.float32
BF16 = jnp.bfloat16

CHUNK = 64
H_A = 8
DK_A = 64
DV_A = 2 * DK_A
H_B = 8
DH_B = 128
BAND_CHUNKS = 8
BAND_PAST = BAND_CHUNKS * CHUNK
REL_CLIP = 128
T5_BUCKETS = 32
T5_MAX_DIST = 128
EPS = 1e-6

HEAD_W = 128
COL_BLOCK = 1024
A_TQ, A_TK = 512, 256
B_TQ = 256
LOG2E = math.log2(math.e)
NEAR = max(REL_CLIP, T5_MAX_DIST)
VMEM_LIMIT = 60 * 1024 * 1024

QA, ZA, QB, KB, VB, ZB, GA0, GA1, GB0, GB1 = range(10)
COL_KA, COL_VA = 1, 2
COL_SLOT = (QA, None, None, ZA, QB, KB, VB, ZB, GA0, GA1, GB0, GB1)


def _t5_bucket_int(rel):
    half = T5_BUCKETS // 2
    max_exact = half // 2
    n = abs(rel)
    ret = half if rel > 0 else 0
    if n < max_exact:
        return ret + n
    assert (T5_MAX_DIST // max_exact) ** 2 == 2 ** (half - max_exact)
    j = 0
    while n * n >= (max_exact * max_exact) * 2 ** (j + 1):
        j += 1
    return ret + min(max_exact + j, half - 1)


def _t5_runs(lo, hi):
    runs = []
    for r in range(lo, hi + 1):
        b = _t5_bucket_int(r)
        if not runs or runs[-1][1] != b:
            runs.append((r, b))
    return runs


T5_FAR_BUCKET = _t5_bucket_int(-T5_MAX_DIST)
assert all(_t5_bucket_int(-n) == T5_FAR_BUCKET for n in range(T5_MAX_DIST, 4 * T5_MAX_DIST))


def _t5_bias_tile(rel, lo, hi, t5_ref, h):
    runs = _t5_runs(lo, hi)
    val = jnp.full(rel.shape, t5_ref[runs[0][1], h], F32)
    for start, b in runs[1:]:
        val = jnp.where(rel >= start, t5_ref[b, h], val)
    return val - t5_ref[T5_FAR_BUCKET, h]


def _rel_bias_tile(rel, lo, hi, rb_ref, h):
    lo = max(lo, -REL_CLIP)
    hi = min(hi, REL_CLIP)
    val = jnp.full(rel.shape, rb_ref[h, lo + REL_CLIP], F32)
    for d in range(lo + 1, hi + 1):
        val = jnp.where(rel >= d, rb_ref[h, d + REL_CLIP], val)
    return val


def _sigmoid(x):
    return 1.0 / (1.0 + jnp.exp(-x))


def _lambda(lq1, lk1, lq2, lk2, lam_init):
    a = jnp.sum(lq1[...] * lk1[...], axis=-1, keepdims=True)
    b = jnp.sum(lq2[...] * lk2[...], axis=-1, keepdims=True)
    return jnp.exp(a) - jnp.exp(b) + lam_init


def _in_proj_kernel(x_ref, g_ref, w_ref, o_ref, ka_ref, va_ref, h_ref):
    j = pl.program_id(1)

    @pl.when(j == 0)
    def _():
        x = x_ref[...]
        ms = jnp.mean(x * x, axis=-1, keepdims=True)
        h_ref[...] = (x * lax.rsqrt(ms + EPS) * g_ref[...]).astype(BF16)

    def project(ref):
        ref[...] = jnp.dot(h_ref[...], w_ref[...], preferred_element_type=F32)

    pl.when(j == COL_KA)(functools.partial(project, ka_ref))
    pl.when(j == COL_VA)(functools.partial(project, va_ref))
    pl.when(jnp.logical_and(j != COL_KA, j != COL_VA))(functools.partial(project, o_ref))


def _in_proj(x2d, pre_g, w_bf16, tm):
    m, d = x2d.shape
    n = w_bf16.shape[1]
    assert m % tm == 0 and n == len(COL_SLOT) * COL_BLOCK

    def own(col):
        return pl.BlockSpec((tm, COL_BLOCK), lambda i, j: (jnp.where(j >= col, i, jnp.maximum(i - 1, 0)), 0))

    def slab_index(i, j):
        slot = jnp.where(j <= COL_VA, 0, j - 2)
        return slot, i, 0

    assert COL_SLOT[0] == 0 and COL_SLOT[COL_VA + 1:] == tuple(range(1, len(COL_SLOT) - 2))
    return pl.pallas_call(
        _in_proj_kernel,
        grid=(m // tm, n // COL_BLOCK),
        in_specs=[
            pl.BlockSpec((tm, d), lambda i, j: (i, 0)),
            pl.BlockSpec((1, d), lambda i, j: (0, 0)),
            pl.BlockSpec((d, COL_BLOCK), lambda i, j: (0, j)),
        ],
        out_specs=[pl.BlockSpec((None, tm, COL_BLOCK), slab_index), own(COL_KA), own(COL_VA)],
        out_shape=[jax.ShapeDtypeStruct((len(COL_SLOT) - 2, m, COL_BLOCK), F32),
                   jax.ShapeDtypeStruct((m, COL_BLOCK), F32),
                   jax.ShapeDtypeStruct((m, COL_BLOCK), F32)],
        scratch_shapes=[pltpu.VMEM((tm, d), BF16)],
        compiler_params=pltpu.CompilerParams(
            dimension_semantics=("arbitrary", "arbitrary"), vmem_limit_bytes=VMEM_LIMIT),
        name="in_proj",
    )(x2d, pre_g.reshape(1, d), w_bf16)


def _online_softmax_step(s, vt, m_ref, l_ref, acc_ref):
    m_old = m_ref[...]
    m_new = jnp.maximum(m_old, jnp.max(s, axis=0, keepdims=True))
    alpha = jnp.exp2(m_old - m_new)
    p = jnp.exp2(s - m_new)
    l_ref[...] = alpha * l_ref[...] + jnp.sum(p, axis=0, keepdims=True)
    acc_ref[...] = alpha * acc_ref[...] + jnp.dot(vt, p.astype(BF16), preferred_element_type=F32)
    m_ref[...] = m_new


def _stage_keys_values(k_ref, v_ref, kbf_ref, vt_ref, t):
    def body(j, c):
        r = pl.multiple_of(j * t, t)
        kbf_ref[pl.ds(r, t), :] = k_ref[pl.ds(r, t), :].astype(BF16)
        vt_ref[j] = v_ref[pl.ds(r, t), :].T.astype(BF16)
        return c
    lax.fori_loop(0, vt_ref.shape[0], body, 0, unroll=4)


def _attn_a_kernel(t5_ref, lq1, lk1, lq2, lk2, g_ref, q_ref, k_ref, v_ref, o_ref,
                   kbf_ref, vt_ref, bias_ref, qt_ref, m_ref, l_ref, acc_ref, *s_refs, lam_init):
    tq, tk = A_TQ, A_TK
    h = pl.program_id(0)
    nq = q_ref.shape[0] // tq

    _stage_keys_values(k_ref, v_ref, kbf_ref, vt_ref, tk)
    key = lax.broadcasted_iota(jnp.int32, (tk, tq), 0)
    qry = lax.broadcasted_iota(jnp.int32, (tk, tq), 1)
    for n in range(3):
        rel = key + (n - 1) * tk - qry
        lo, hi = (n - 1) * tk - (tq - 1), min(n * tk - 1, CHUNK - 1)
        b = _t5_bias_tile(jnp.minimum(rel, hi), lo, hi, t5_ref, h) * LOG2E
        if n >= 1:
            b = jnp.where((key + (n - 1) * tk) // CHUNK <= qry // CHUNK, b, -jnp.inf)
        bias_ref[n] = b

    lam = _lambda(lq1, lk1, lq2, lk2, lam_init)

    def start_tile(i):
        q0 = pl.multiple_of(i * tq, tq)
        qt = (q_ref[pl.ds(q0, tq), :] * (DK_A ** -0.5 * LOG2E)).T
        sub = lax.broadcasted_iota(jnp.int32, (HEAD_W, tq), 0)
        qt_ref[:, :tq] = jnp.where(sub < DK_A, qt, 0.0).astype(BF16)
        qt_ref[:, tq:] = jnp.where(sub >= DK_A, qt, 0.0).astype(BF16)

    def reset_state():
        m_ref[...] = jnp.full(m_ref.shape, -jnp.inf, F32)
        l_ref[...] = jnp.zeros(l_ref.shape, F32)
        acc_ref[...] = jnp.zeros(acc_ref.shape, F32)

    def scores(j, s_ref):
        r = pl.multiple_of(j * tk, tk)
        s_ref[...] = jnp.dot(kbf_ref[pl.ds(r, tk), :], qt_ref[...], preferred_element_type=F32)

    def update(j, s_ref, bias_idx):
        s = s_ref[...]
        if bias_idx is not None:
            b = bias_ref[bias_idx]
            s = jnp.concatenate([s[:, :tq] + b, s[:, tq:] + b], axis=1)
        _online_softmax_step(s, vt_ref[j], m_ref, l_ref, acc_ref)

    def q_tile(i, carry):
        odd = jnp.logical_and(i >= 2, i % 2 == 0)

        @pl.when(odd)
        def _():
            update(0, s_refs[0], None)
            scores(2, s_refs[0])
            update(1, s_refs[1], None)
            scores(3, s_refs[1])

        j0 = jnp.where(odd, 2, 0)

        def quad(u, c):
            j = j0 + 4 * u
            for n in range(4):
                scores(j + n + 2, s_refs[(n + 2) % 4])
                update(j + n, s_refs[n], None)
            return c
        lax.fori_loop(0, jnp.maximum(i - 1, 0) // 2, quad, 0)

        nxt = jnp.minimum(i + 1, nq - 1)

        @pl.when(i >= 1)
        def _():
            j = 2 * i - 2
            scores(j + 2, s_refs[2])
            update(j, s_refs[0], None)
            scores(j + 3, s_refs[3])
            update(j + 1, s_refs[1], 0)
            start_tile(nxt)
            scores(0, s_refs[0])
            update(j + 2, s_refs[2], 1)
            scores(1, s_refs[1])
            update(j + 3, s_refs[3], 2)

        @pl.when(i == 0)
        def _():
            update(0, s_refs[0], 1)
            start_tile(nxt)
            scores(0, s_refs[0])
            update(1, s_refs[1], 2)
            scores(1, s_refs[1])

        inv = 1.0 / l_ref[...]
        acc = acc_ref[...]
        o = acc[:, :tq] * inv[:, :tq] - lam * (acc[:, tq:] * inv[:, tq:])
        ms = jnp.mean(o * o, axis=0, keepdims=True)
        y = (o * lax.rsqrt(ms + EPS) * g_ref[...]) * (1.0 - lam_init)
        o_ref[pl.ds(pl.multiple_of(i * tq, tq), tq), :] = y.T
        reset_state()
        return carry

    start_tile(0)
    scores(0, s_refs[0])
    scores(1, s_refs[1])
    reset_state()
    lax.fori_loop(0, nq, q_tile, 0)


def _attn_a_prompt(proj, k_a, v_a, t5_bias, lq1, lk1, lq2, lk2, subln_g, lam_init):
    s = proj.shape[1]
    tq, tk = A_TQ, A_TK
    assert s % tq == 0 and tq == 2 * tk and tk % CHUNK == 0 and tk >= T5_MAX_DIST
    vec = lambda: pl.BlockSpec((1, DK_A), lambda h: (0, 0))
    head = lambda: pl.BlockSpec((s, HEAD_W), lambda h: (0, h))
    return pl.pallas_call(
        functools.partial(_attn_a_kernel, lam_init=lam_init),
        grid=(H_A,),
        in_specs=[
            pl.BlockSpec(memory_space=pltpu.SMEM),
            vec(), vec(), vec(), vec(),
            pl.BlockSpec((DV_A, 1), lambda h: (0, 0)),
            pl.BlockSpec((None, s, HEAD_W), lambda h: (QA, 0, h)), head(), head(),
        ],
        out_specs=pl.BlockSpec((s, HEAD_W), lambda h: (0, h)),
        out_shape=jax.ShapeDtypeStruct((s, H_A * DV_A), F32),
        scratch_shapes=[
            pltpu.VMEM((s, HEAD_W), BF16),
            pltpu.VMEM((s // tk, DV_A, tk), BF16),
            pltpu.VMEM((3, tk, tq), F32),
            pltpu.VMEM((HEAD_W, 2 * tq), BF16),
            pltpu.VMEM((1, 2 * tq), F32),
            pltpu.VMEM((1, 2 * tq), F32),
            pltpu.VMEM((DV_A, 2 * tq), F32),
        ] + [pltpu.VMEM((tk, 2 * tq), F32)] * 4,
        compiler_params=pltpu.CompilerParams(
            dimension_semantics=("arbitrary",), vmem_limit_bytes=VMEM_LIMIT),
        name="attn_a_prompt",
    )(t5_bias, lq1.reshape(1, DK_A), lk1.reshape(1, DK_A), lq2.reshape(1, DK_A), lk2.reshape(1, DK_A),
      subln_g.reshape(DV_A, 1), proj, k_a, v_a)


def _attn_b_kernel(rb_ref, q_ref, k_ref, v_ref, o_ref, kbf_ref, vt_ref, bias_ref, *s_refs):
    t = B_TQ
    blk = REL_CLIP
    nkt = BAND_PAST // t + 1
    nq = q_ref.shape[0] // t
    h = pl.program_id(0)

    _stage_keys_values(k_ref, v_ref, kbf_ref, vt_ref, t)

    kk = lax.broadcasted_iota(jnp.int32, (blk, blk), 0)
    qq = lax.broadcasted_iota(jnp.int32, (blk, blk), 1)
    rel = kk - qq
    lo = jnp.full((blk, blk), rb_ref[h, 0] * LOG2E, F32)
    same = _rel_bias_tile(rel, -(blk - 1), blk - 1, rb_ref, h) * LOG2E
    prev = _rel_bias_tile(rel - blk, -(2 * blk - 1), -1, rb_ref, h) * LOG2E
    ninf = jnp.full((blk, blk), -jnp.inf, F32)
    kc = kk // CHUNK
    qc = qq // CHUNK
    far_blocks = BAND_PAST // blk
    for a in range(nkt * t // blk):
        for b in range(t // blk):
            e = a - b
            if e < 0 or e > far_blocks:
                tile = ninf
            elif e == 0:
                tile = jnp.where(kc >= qc, lo, -jnp.inf)
            elif e == far_blocks:
                tile = jnp.where(kc <= qc, same, -jnp.inf)
            elif e == far_blocks - 1:
                tile = prev
            else:
                tile = lo
            bias_ref[a * blk:(a + 1) * blk, b * blk:(b + 1) * blk] = tile

    def scores(g, nk, s_ref):
        q0 = pl.multiple_of(g * t, t)
        k0 = pl.multiple_of((g - (nk - 1)) * t, t)
        qt = (q_ref[pl.ds(q0, t), :] * (DH_B ** -0.5 * LOG2E)).T.astype(BF16)
        s_ref[(nkt - nk) * t:, :] = jnp.dot(kbf_ref[pl.ds(k0, nk * t), :], qt, preferred_element_type=F32)

    def finish(g, nk, s_ref):
        s = s_ref[(nkt - nk) * t:, :] + bias_ref[(nkt - nk) * t:, :]
        m = jnp.max(s, axis=0, keepdims=True)
        p = jnp.exp2(s - m)
        l = jnp.sum(p, axis=0, keepdims=True)
        pb = p.astype(BF16)
        o = jnp.dot(vt_ref[g - (nk - 1)], pb[:t], preferred_element_type=F32)
        for c in range(1, nk):
            o = o + jnp.dot(vt_ref[g - (nk - 1) + c], pb[c * t:(c + 1) * t], preferred_element_type=F32)
        o_ref[pl.ds(pl.multiple_of(g * t, t), t), :] = (o * (1.0 / l)).T

    first = nkt - 1
    for g in range(first):
        scores(g, g + 1, s_refs[g % 4])
        finish(g, g + 1, s_refs[g % 4])

    assert (nq - first - 2) % 4 == 0
    scores(first, nkt, s_refs[0])
    scores(first + 1, nkt, s_refs[1])

    def body(u, c):
        g = first + 4 * u
        for n in range(4):
            scores(g + n + 2, nkt, s_refs[(n + 2) % 4])
            finish(g + n, nkt, s_refs[n])
        return c
    lax.fori_loop(0, (nq - first - 2) // 4, body, 0)
    finish(nq - 2, nkt, s_refs[0])
    finish(nq - 1, nkt, s_refs[1])


def _attn_b_prompt(proj, rel_bias):
    s = proj.shape[1]
    t = B_TQ
    nkt = BAND_PAST // t + 1
    assert s % t == 0 and BAND_PAST % t == 0 and t % REL_CLIP == 0 and REL_CLIP % CHUNK == 0
    head = lambda c: pl.BlockSpec((None, s, HEAD_W), lambda h: (c, 0, h))
    return pl.pallas_call(
        _attn_b_kernel,
        grid=(H_B,),
        in_specs=[pl.BlockSpec(memory_space=pltpu.SMEM), head(QB), head(KB), head(VB)],
        out_specs=pl.BlockSpec((s, HEAD_W), lambda h: (0, h)),
        out_shape=jax.ShapeDtypeStruct((s, H_B * DH_B), F32),
        scratch_shapes=[
            pltpu.VMEM((s, HEAD_W), BF16),
            pltpu.VMEM((s // t, DH_B, t), BF16),
            pltpu.VMEM((nkt * t, t), F32),
        ] + [pltpu.VMEM((nkt * t, t), F32)] * 4,
        compiler_params=pltpu.CompilerParams(
            dimension_semantics=("arbitrary",), vmem_limit_bytes=VMEM_LIMIT),
        name="attn_b_prompt",
    )(rel_bias, proj, proj, proj)


def _dot_nt(a, b):
    return lax.dot_general(a, b, (((1,), (1,)), ((), ())), preferred_element_type=F32)


def _attn_sample_kernel(t5_ref, rb_ref, lq1, lk1, lq2, lk2, g_ref,
                        qa_ref, ka_ref, va_ref, cka_ref, cva_ref,
                        qb_ref, kb_ref, vb_ref, ckb_ref, cvb_ref,
                        oa_ref, ob_ref, kroll_ref, vroll_ref,
                        ba_ref, bb_ref, *, lam_init, past, win):
    n = qa_ref.shape[0]
    near = NEAR
    nh = H_A

    @pl.when(pl.program_id(0) == 0)
    def _():
        qry = lax.broadcasted_iota(jnp.int32, (n, near + n), 0)
        key = lax.broadcasted_iota(jnp.int32, (n, near + n), 1)
        rel = key - near - qry
        for h in range(nh):
            ba_ref[h] = _t5_bias_tile(rel, -(near + n - 1), n - 1, t5_ref, h)
            bb_ref[h] = _rel_bias_tile(rel, -(near + n - 1), n - 1, rb_ref, h)

    kroll_ref[:(win - n) * nh, :] = ckb_ref[n * nh:, :]
    vroll_ref[:(win - n) * nh, :] = cvb_ref[n * nh:, :]

    lam = _lambda(lq1, lk1, lq2, lk2, lam_init)
    lane = lax.broadcasted_iota(jnp.int32, (n, HEAD_W), 1)
    for h in range(nh):
        cols = slice(h * HEAD_W, (h + 1) * HEAD_W)
        new_rows = pl.ds((win - n) * nh + h, n, stride=nh)
        kroll_ref[new_rows, :] = kb_ref[:, cols]
        vroll_ref[new_rows, :] = vb_ref[:, cols]

        q = qa_ref[:, cols] * (DK_A ** -0.5)
        q2 = jnp.concatenate([jnp.where(lane < DK_A, q, 0.0), jnp.where(lane >= DK_A, q, 0.0)],
                             axis=0).astype(BF16)
        ba = ba_ref[h]
        ba2 = jnp.concatenate([ba, ba], axis=0)
        s_c = _dot_nt(q2, cka_ref[:, cols].astype(BF16))
        s_c = jnp.concatenate([s_c[:, :past - near], s_c[:, past - near:] + ba2[:, :near]], axis=1)
        s_n = _dot_nt(q2, ka_ref[:, cols].astype(BF16)) + ba2[:, near:]
        m = jnp.maximum(jnp.max(s_c, axis=-1, keepdims=True), jnp.max(s_n, axis=-1, keepdims=True))
        p_c = jnp.exp(s_c - m)
        p_n = jnp.exp(s_n - m)
        l = jnp.sum(p_c, axis=-1, keepdims=True) + jnp.sum(p_n, axis=-1, keepdims=True)
        cva = cva_ref[pl.ds(h, past, stride=nh), :].astype(BF16)
        o2 = (jnp.dot(p_c.astype(BF16), cva, preferred_element_type=F32)
              + jnp.dot(p_n.astype(BF16), va_ref[:, cols].astype(BF16), preferred_element_type=F32)) * (1.0 / l)
        o = o2[:n] - lam * o2[n:]
        ms = jnp.mean(o * o, axis=-1, keepdims=True)
        oa_ref[:, cols] = (o * lax.rsqrt(ms + EPS) * g_ref[...]) * (1.0 - lam_init)

        qb = qb_ref[:, cols].astype(BF16)
        bb = bb_ref[h]
        ckb = ckb_ref[pl.ds(h, win, stride=nh), :].astype(BF16)
        cvb = cvb_ref[pl.ds(h, win, stride=nh), :].astype(BF16)
        s_c = _dot_nt(qb, ckb) * (DH_B ** -0.5)
        s_c = jnp.concatenate([s_c[:, :win - near] + rb_ref[h, 0], s_c[:, win - near:] + bb[:, :near]], axis=1)
        s_n = _dot_nt(qb, kb_ref[:, cols].astype(BF16)) * (DH_B ** -0.5) + bb[:, near:]
        m = jnp.maximum(jnp.max(s_c, axis=-1, keepdims=True), jnp.max(s_n, axis=-1, keepdims=True))
        p_c = jnp.exp(s_c - m)
        p_n = jnp.exp(s_n - m)
        l = jnp.sum(p_c, axis=-1, keepdims=True) + jnp.sum(p_n, axis=-1, keepdims=True)
        ob_ref[:, cols] = (jnp.dot(p_c.astype(BF16), cvb, preferred_element_type=F32)
                           + jnp.dot(p_n.astype(BF16), vb_ref[:, cols].astype(BF16),
                                     preferred_element_type=F32)) * (1.0 / l)


def _attn_sample(proj, k_a, v_a, ck_a, cv_a, ck_b, cv_b, t5_bias, rel_bias, lq1, lk1, lq2, lk2, subln_g,
                 lam_init):
    nb, past = ck_a.shape[0], ck_a.shape[1]
    win = ck_b.shape[1]
    n = proj.shape[1] // nb
    near = NEAR
    wide = H_A * HEAD_W
    assert past % CHUNK == 0 and n <= CHUNK and win <= BAND_PAST and win <= past
    assert near % HEAD_W == 0 and near <= win and near <= past and n % 8 == 0 and T5_MAX_DIST <= REL_CLIP
    assert H_A == H_B and DV_A == HEAD_W and DH_B == HEAD_W and 2 * DK_A == HEAD_W
    cka = ck_a.reshape(nb, past, wide)
    cva = cv_a.reshape(nb, past * H_A, DV_A)
    ckb = ck_b.reshape(nb, win * H_B, DH_B)
    cvb = cv_b.reshape(nb, win * H_B, DH_B)
    vec = lambda: pl.BlockSpec((1, DK_A), lambda b: (0, 0))
    new = lambda c: pl.BlockSpec((None, n, wide), lambda b: (c, b, 0))
    seq = lambda rows, cols: pl.BlockSpec((None, rows, cols), lambda b: (b, 0, 0))
    out = pl.BlockSpec((n, wide), lambda b: (b, 0))
    return pl.pallas_call(
        functools.partial(_attn_sample_kernel, lam_init=lam_init, past=past, win=win),
        grid=(nb,),
        in_specs=[
            pl.BlockSpec(memory_space=pltpu.SMEM),
            pl.BlockSpec(memory_space=pltpu.SMEM),
            vec(), vec(), vec(), vec(),
            pl.BlockSpec((1, DV_A), lambda b: (0, 0)),
            new(QA), out, out, seq(past, wide), seq(past * H_A, HEAD_W),
            new(QB), new(KB), new(VB), seq(win * H_B, HEAD_W), seq(win * H_B, HEAD_W),
        ],
        out_specs=[out, out, seq(win * H_B, HEAD_W), seq(win * H_B, HEAD_W)],
        out_shape=[
            jax.ShapeDtypeStruct((nb * n, wide), F32),
            jax.ShapeDtypeStruct((nb * n, wide), F32),
            jax.ShapeDtypeStruct((nb, win * H_B, DH_B), F32),
            jax.ShapeDtypeStruct((nb, win * H_B, DH_B), F32),
        ],
        scratch_shapes=[pltpu.VMEM((H_A, n, near + n), F32), pltpu.VMEM((H_B, n, near + n), F32)],
        compiler_params=pltpu.CompilerParams(
            dimension_semantics=("arbitrary",), vmem_limit_bytes=VMEM_LIMIT),
        name="attn_sample",
    )(t5_bias, rel_bias, lq1.reshape(1, DK_A), lk1.reshape(1, DK_A), lq2.reshape(1, DK_A),
      lk2.reshape(1, DK_A), subln_g.reshape(1, DV_A),
      proj, k_a, v_a, cka, cva, proj, proj, proj, ckb, cvb)


def _merge_kernel(x_ref, oa_ref, ob_ref, za_ref, zb_ref, ga0_ref, ga1_ref, gb0_ref, gb1_ref,
                  woa_ref, wob_ref, wout_ref, pg_ref, y_ref):
    za = za_ref[...]
    zb = zb_ref[...]
    a = (oa_ref[...] * (za * _sigmoid(za))).astype(BF16)
    b = (ob_ref[...] * (zb * _sigmoid(zb))).astype(BF16)
    ya = jnp.dot(a, woa_ref[...], preferred_element_type=F32)
    yb = jnp.dot(b, wob_ref[...], preferred_element_type=F32)
    ga = jnp.concatenate([ga0_ref[...], ga1_ref[...]], axis=1)
    gb = jnp.concatenate([gb0_ref[...], gb1_ref[...]], axis=1)
    mix = (_sigmoid(ga) * ya + _sigmoid(gb) * yb).astype(BF16)
    y = jnp.dot(mix, wout_ref[...], preferred_element_type=F32)
    ms = jnp.mean(y * y, axis=-1, keepdims=True)
    y_ref[...] = x_ref[...] + y * lax.rsqrt(ms + EPS) * pg_ref[...]


def _merge(x2d, o_a, o_b, proj, woa, wob, wout, post_g, tm):
    m, d = x2d.shape
    wa = o_a.shape[1]
    wb = o_b.shape[1]
    assert m % tm == 0 and wa == COL_BLOCK and wb == COL_BLOCK and d == 2 * COL_BLOCK
    row = lambda w: pl.BlockSpec((tm, w), lambda i: (i, 0))
    col = lambda c: pl.BlockSpec((None, tm, COL_BLOCK), lambda i: (c, i, 0))
    resident = lambda r, c: pl.BlockSpec((r, c), lambda i: (0, 0), pipeline_mode=pl.Buffered(1))
    return pl.pallas_call(
        _merge_kernel,
        grid=(m // tm,),
        in_specs=[row(d), row(wa), row(wb), col(ZA), col(ZB), col(GA0), col(GA1), col(GB0), col(GB1),
                  resident(wa, d), resident(wb, d), resident(d, d), resident(1, d)],
        out_specs=row(d),
        out_shape=jax.ShapeDtypeStruct((m, d), F32),
        compiler_params=pltpu.CompilerParams(
            dimension_semantics=("arbitrary",), vmem_limit_bytes=VMEM_LIMIT),
        name="merge",
    )(x2d, o_a, o_b, proj, proj, proj, proj, proj, proj, woa, wob, wout, post_g.reshape(1, d))


def kernel(x_prompt, x_sample, cache_k_a, cache_v_a, cache_k_b, cache_v_b, t5_bias, pre_norm, post_norm,
           w_in, lambda_q1, lambda_k1, lambda_q2, lambda_k2, subln_a, rel_bias_b, w_o_a, w_o_b, w_out):
    depth = w_in.shape[0]
    bp, sp, d = x_prompt.shape
    bs, ss, _ = x_sample.shape
    assert bp == 1 and w_in.shape[2] == 12 * COL_BLOCK
    yp = x_prompt.reshape(sp, d)
    ys = x_sample.reshape(bs * ss, d)
    tail = min(BAND_PAST, sp)
    outs = [[] for _ in range(8)]
    for l in range(depth):
        lam_init = 0.8 - 0.6 * math.exp(-0.3 * l)
        w = w_in[l].astype(BF16)
        woa = w_o_a[l].astype(BF16)
        wob = w_o_b[l].astype(BF16)
        wout = w_out[l].astype(BF16)
        lam_args = (lambda_q1[l], lambda_k1[l], lambda_q2[l], lambda_k2[l], subln_a[l], lam_init)

        pp, ka, va = _in_proj(yp, pre_norm[l], w, tm=1024)
        oa = _attn_a_prompt(pp, ka, va, t5_bias, *lam_args)
        ob = _attn_b_prompt(pp, rel_bias_b[l])
        yp = _merge(yp, oa, ob, pp, woa, wob, wout, post_norm[l], tm=256)
        outs[0].append(ka.reshape(bp, sp, 2 * H_A, DK_A))
        outs[1].append(va.reshape(bp, sp, H_A, DV_A))
        outs[2].append(pp[KB, sp - tail:].reshape(bp, tail, H_B, DH_B))
        outs[3].append(pp[VB, sp - tail:].reshape(bp, tail, H_B, DH_B))

        ps, ka, va = _in_proj(ys, pre_norm[l], w, tm=bs * ss)
        oas, obs, kroll, vroll = _attn_sample(ps, ka, va, cache_k_a[l], cache_v_a[l], cache_k_b[l], cache_v_b[l],
                                              t5_bias, rel_bias_b[l], *lam_args)
        ys = _merge(ys, oas, obs, ps, woa, wob, wout, post_norm[l], tm=bs * ss)
        outs[4].append(ka.reshape(bs, ss, 2 * H_A, DK_A))
        outs[5].append(va.reshape(bs, ss, H_A, DV_A))
        outs[6].append(kroll.reshape(bs, -1, H_B, DH_B))
        outs[7].append(vroll.reshape(bs, -1, H_B, DH_B))
    return (yp.reshape(bp, sp, d), ys.reshape(bs, ss, d)) + tuple(jnp.stack(o) for o in outs)
```

```python
import functools
import math

import numpy as np
import jax
import jax.numpy as jnp
from jax import lax
from jax.experimental import pallas as pl
from jax.experimental.pallas import tpu as pltpu

F32 = jnp.float32
BF16 = jnp.bfloat16

CHUNK = 64
H_A = 8
DK_A = 64
DV_A = 2 * DK_A
H_B = 8
DH_B = 128
BAND_CHUNKS = 8
BAND_PAST = BAND_CHUNKS * CHUNK
REL_CLIP = 128
T5_BUCKETS = 32
T5_MAX_DIST = 128
EPS = 1e-6

HEAD_W = 128
COL_BLOCK = 1024
A_TQ, A_TK = 512, 256
B_TQ = 256
LOG2E = math.log2(math.e)
NEAR = max(REL_CLIP, T5_MAX_DIST)
VMEM_LIMIT = 60 * 1024 * 1024

QA, ZA, QB, KB, VB, ZB, GA0, GA1, GB0, GB1 = range(10)
COL_KA, COL_VA = 1, 2
COL_SLOT = (QA, None, None, ZA, QB, KB, VB, ZB, GA0, GA1, GB0, GB1)


def _t5_bucket_int(rel):
    half = T5_BUCKETS // 2
    max_exact = half // 2
    n = abs(rel)
    ret = half if rel > 0 else 0
    if n < max_exact:
        return ret + n
    assert (T5_MAX_DIST // max_exact) ** 2 == 2 ** (half - max_exact)
    j = 0
    while n * n >= (max_exact * max_exact) * 2 ** (j + 1):
        j += 1
    return ret + min(max_exact + j, half - 1)


def _t5_runs(lo, hi):
    runs = []
    for r in range(lo, hi + 1):
        b = _t5_bucket_int(r)
        if not runs or runs[-1][1] != b:
            runs.append((r, b))
    return runs


T5_FAR_BUCKET = _t5_bucket_int(-T5_MAX_DIST)
assert all(_t5_bucket_int(-n) == T5_FAR_BUCKET for n in range(T5_MAX_DIST, 4 * T5_MAX_DIST))


def _t5_bias_tile(rel, lo, hi, t5_ref, h):
    runs = _t5_runs(lo, hi)
    val = jnp.full(rel.shape, t5_ref[runs[0][1], h], F32)
    for start, b in runs[1:]:
        val = jnp.where(rel >= start, t5_ref[b, h], val)
    return val - t5_ref[T5_FAR_BUCKET, h]


def _rel_bias_tile(rel, lo, hi, rb_ref, h):
    lo = max(lo, -REL_CLIP)
    hi = min(hi, REL_CLIP)
    val = jnp.full(rel.shape, rb_ref[h, lo + REL_CLIP], F32)
    for d in range(lo + 1, hi + 1):
        val = jnp.where(rel >= d, rb_ref[h, d + REL_CLIP], val)
    return val


def _sigmoid(x):
    return 1.0 / (1.0 + jnp.exp(-x))


def _lambda(lq1, lk1, lq2, lk2, lam_init):
    a = jnp.sum(lq1[...] * lk1[...], axis=-1, keepdims=True)
    b = jnp.sum(lq2[...] * lk2[...], axis=-1, keepdims=True)
    return jnp.exp(a) - jnp.exp(b) + lam_init


def _in_proj_kernel(x_ref, g_ref, w_ref, o_ref, ka_ref, va_ref, h_ref):
    j = pl.program_id(1)

    @pl.when(j == 0)
    def _():
        x = x_ref[...]
        ms = jnp.mean(x * x, axis=-1, keepdims=True)
        h_ref[...] = (x * lax.rsqrt(ms + EPS) * g_ref[...]).astype(BF16)

    def project(ref):
        ref[...] = jnp.dot(h_ref[...], w_ref[...], preferred_element_type=F32)

    pl.when(j == COL_KA)(functools.partial(project, ka_ref))
    pl.when(j == COL_VA)(functools.partial(project, va_ref))
    pl.when(jnp.logical_and(j != COL_KA, j != COL_VA))(functools.partial(project, o_ref))


def _in_proj(x2d, pre_g, w_bf16, tm):
    m, d = x2d.shape
    n = w_bf16.shape[1]
    assert m % tm == 0 and n == len(COL_SLOT) * COL_BLOCK

    def own(col):
        return pl.BlockSpec((tm, COL_BLOCK), lambda i, j: (jnp.where(j >= col, i, jnp.maximum(i - 1, 0)), 0))

    def slab_index(i, j):
        slot = jnp.where(j <= COL_VA, 0, j - 2)
        return slot, i, 0

    assert COL_SLOT[0] == 0 and COL_SLOT[COL_VA + 1:] == tuple(range(1, len(COL_SLOT) - 2))
    return pl.pallas_call(
        _in_proj_kernel,
        grid=(m // tm, n // COL_BLOCK),
        in_specs=[
            pl.BlockSpec((tm, d), lambda i, j: (i, 0)),
            pl.BlockSpec((1, d), lambda i, j: (0, 0)),
            pl.BlockSpec((d, COL_BLOCK), lambda i, j: (0, j)),
        ],
        out_specs=[pl.BlockSpec((None, tm, COL_BLOCK), slab_index), own(COL_KA), own(COL_VA)],
        out_shape=[jax.ShapeDtypeStruct((len(COL_SLOT) - 2, m, COL_BLOCK), F32),
                   jax.ShapeDtypeStruct((m, COL_BLOCK), F32),
                   jax.ShapeDtypeStruct((m, COL_BLOCK), F32)],
        scratch_shapes=[pltpu.VMEM((tm, d), BF16)],
        compiler_params=pltpu.CompilerParams(
            dimension_semantics=("arbitrary", "arbitrary"), vmem_limit_bytes=VMEM_LIMIT),
        name="in_proj",
    )(x2d, pre_g.reshape(1, d), w_bf16)


def _online_softmax_step(s, vt, m_ref, l_ref, acc_ref):
    m_old = m_ref[...]
    m_new = jnp.maximum(m_old, jnp.max(s, axis=0, keepdims=True))
    alpha = jnp.exp2(m_old - m_new)
    p = jnp.exp2(s - m_new)
    l_ref[...] = alpha * l_ref[...] + jnp.sum(p, axis=0, keepdims=True)
    acc_ref[...] = alpha * acc_ref[...] + jnp.dot(vt, p.astype(BF16), preferred_element_type=F32)
    m_ref[...] = m_new


def _stage_keys_values(k_ref, v_ref, kbf_ref, vt_ref, t):
    def body(j, c):
        r = pl.multiple_of(j * t, t)
        kbf_ref[pl.ds(r, t), :] = k_ref[pl.ds(r, t), :].astype(BF16)
        vt_ref[j] = v_ref[pl.ds(r, t), :].T.astype(BF16)
        return c
    lax.fori_loop(0, vt_ref.shape[0], body, 0, unroll=4)


def _attn_a_kernel(t5_ref, lq1, lk1, lq2, lk2, g_ref, q_ref, k_ref, v_ref, o_ref,
                   kbf_ref, vt_ref, bias_ref, qt_ref, m_ref, l_ref, acc_ref, *s_refs, lam_init):
    tq, tk = A_TQ, A_TK
    h = pl.program_id(0)
    nq = q_ref.shape[0] // tq

    _stage_keys_values(k_ref, v_ref, kbf_ref, vt_ref, tk)
    key = lax.broadcasted_iota(jnp.int32, (tk, tq), 0)
    qry = lax.broadcasted_iota(jnp.int32, (tk, tq), 1)
    for n in range(3):
        rel = key + (n - 1) * tk - qry
        lo, hi = (n - 1) * tk - (tq - 1), min(n * tk - 1, CHUNK - 1)
        b = _t5_bias_tile(jnp.minimum(rel, hi), lo, hi, t5_ref, h) * LOG2E
        if n >= 1:
            b = jnp.where((key + (n - 1) * tk) // CHUNK <= qry // CHUNK, b, -jnp.inf)
        bias_ref[n] = b

    lam = _lambda(lq1, lk1, lq2, lk2, lam_init)

    def start_tile(i):
        q0 = pl.multiple_of(i * tq, tq)
        qt = (q_ref[pl.ds(q0, tq), :] * (DK_A ** -0.5 * LOG2E)).T
        sub = lax.broadcasted_iota(jnp.int32, (HEAD_W, tq), 0)
        qt_ref[:, :tq] = jnp.where(sub < DK_A, qt, 0.0).astype(BF16)
        qt_ref[:, tq:] = jnp.where(sub >= DK_A, qt, 0.0).astype(BF16)

    def reset_state():
        m_ref[...] = jnp.full(m_ref.shape, -jnp.inf, F32)
        l_ref[...] = jnp.zeros(l_ref.shape, F32)
        acc_ref[...] = jnp.zeros(acc_ref.shape, F32)

    def scores(j, s_ref):
        r = pl.multiple_of(j * tk, tk)
        s_ref[...] = jnp.dot(kbf_ref[pl.ds(r, tk), :], qt_ref[...], preferred_element_type=F32)

    def update(j, s_ref, bias_idx):
        s = s_ref[...]
        if bias_idx is not None:
            b = bias_ref[bias_idx]
            s = jnp.concatenate([s[:, :tq] + b, s[:, tq:] + b], axis=1)
        _online_softmax_step(s, vt_ref[j], m_ref, l_ref, acc_ref)

    def q_tile(i, carry):
        odd = jnp.logical_and(i >= 2, i % 2 == 0)

        @pl.when(odd)
        def _():
            update(0, s_refs[0], None)
            scores(2, s_refs[0])
            update(1, s_refs[1], None)
            scores(3, s_refs[1])

        j0 = jnp.where(odd, 2, 0)

        def quad(j):
            for n in range(4):
                scores(j + n + 2, s_refs[(n + 2) % 4])
                update(j + n, s_refs[n], None)

        def two_quads(u, c):
            quad(j0 + 8 * u)
            quad(j0 + 8 * u + 4)
            return c
        quads = jnp.maximum(i - 1, 0) // 2
        lax.fori_loop(0, quads // 2, two_quads, 0)

        @pl.when(quads % 2 == 1)
        def _():
            quad(j0 + 4 * (quads - 1))

        nxt = jnp.minimum(i + 1, nq - 1)

        @pl.when(i >= 1)
        def _():
            j = 2 * i - 2
            scores(j + 2, s_refs[2])
            update(j, s_refs[0], None)
            scores(j + 3, s_refs[3])
            update(j + 1, s_refs[1], 0)
            start_tile(nxt)
            scores(0, s_refs[0])
            update(j + 2, s_refs[2], 1)
            scores(1, s_refs[1])
            update(j + 3, s_refs[3], 2)

        @pl.when(i == 0)
        def _():
            update(0, s_refs[0], 1)
            start_tile(nxt)
            scores(0, s_refs[0])
            update(1, s_refs[1], 2)
            scores(1, s_refs[1])

        inv = 1.0 / l_ref[...]
        acc = acc_ref[...]
        o = acc[:, :tq] * inv[:, :tq] - lam * (acc[:, tq:] * inv[:, tq:])
        ms = jnp.mean(o * o, axis=0, keepdims=True)
        y = (o * lax.rsqrt(ms + EPS) * g_ref[...]) * (1.0 - lam_init)
        o_ref[pl.ds(pl.multiple_of(i * tq, tq), tq), :] = y.T
        reset_state()
        return carry

    start_tile(0)
    scores(0, s_refs[0])
    scores(1, s_refs[1])
    reset_state()
    lax.fori_loop(0, nq, q_tile, 0)


def _attn_a_prompt(proj, k_a, v_a, t5_bias, lq1, lk1, lq2, lk2, subln_g, lam_init):
    s = proj.shape[1]
    tq, tk = A_TQ, A_TK
    assert s % tq == 0 and tq == 2 * tk and tk % CHUNK == 0 and tk >= T5_MAX_DIST
    vec = lambda: pl.BlockSpec((1, DK_A), lambda h: (0, 0))
    head = lambda: pl.BlockSpec((s, HEAD_W), lambda h: (0, h))
    return pl.pallas_call(
        functools.partial(_attn_a_kernel, lam_init=lam_init),
        grid=(H_A,),
        in_specs=[
            pl.BlockSpec(memory_space=pltpu.SMEM),
            vec(), vec(), vec(), vec(),
            pl.BlockSpec((DV_A, 1), lambda h: (0, 0)),
            pl.BlockSpec((None, s, HEAD_W), lambda h: (QA, 0, h)), head(), head(),
        ],
        out_specs=pl.BlockSpec((s, HEAD_W), lambda h: (0, h)),
        out_shape=jax.ShapeDtypeStruct((s, H_A * DV_A), F32),
        scratch_shapes=[
            pltpu.VMEM((s, HEAD_W), BF16),
            pltpu.VMEM((s // tk, DV_A, tk), BF16),
            pltpu.VMEM((3, tk, tq), F32),
            pltpu.VMEM((HEAD_W, 2 * tq), BF16),
            pltpu.VMEM((1, 2 * tq), F32),
            pltpu.VMEM((1, 2 * tq), F32),
            pltpu.VMEM((DV_A, 2 * tq), F32),
        ] + [pltpu.VMEM((tk, 2 * tq), F32)] * 4,
        compiler_params=pltpu.CompilerParams(
            dimension_semantics=("arbitrary",), vmem_limit_bytes=VMEM_LIMIT),
        name="attn_a_prompt",
    )(t5_bias, lq1.reshape(1, DK_A), lk1.reshape(1, DK_A), lq2.reshape(1, DK_A), lk2.reshape(1, DK_A),
      subln_g.reshape(DV_A, 1), proj, k_a, v_a)


def _attn_b_kernel(rb_ref, q_ref, k_ref, v_ref, o_ref, kbf_ref, vt_ref, bias_ref, *s_refs):
    t = B_TQ
    blk = REL_CLIP
    nkt = BAND_PAST // t + 1
    nq = q_ref.shape[0] // t
    h = pl.program_id(0)

    _stage_keys_values(k_ref, v_ref, kbf_ref, vt_ref, t)

    kk = lax.broadcasted_iota(jnp.int32, (blk, blk), 0)
    qq = lax.broadcasted_iota(jnp.int32, (blk, blk), 1)
    rel = kk - qq
    lo = jnp.full((blk, blk), rb_ref[h, 0] * LOG2E, F32)
    same = _rel_bias_tile(rel, -(blk - 1), blk - 1, rb_ref, h) * LOG2E
    prev = _rel_bias_tile(rel - blk, -(2 * blk - 1), -1, rb_ref, h) * LOG2E
    ninf = jnp.full((blk, blk), -jnp.inf, F32)
    kc = kk // CHUNK
    qc = qq // CHUNK
    far_blocks = BAND_PAST // blk
    for a in range(nkt * t // blk):
        for b in range(t // blk):
            e = a - b
            if e < 0 or e > far_blocks:
                tile = ninf
            elif e == 0:
                tile = jnp.where(kc >= qc, lo, -jnp.inf)
            elif e == far_blocks:
                tile = jnp.where(kc <= qc, same, -jnp.inf)
            elif e == far_blocks - 1:
                tile = prev
            else:
                tile = lo
            bias_ref[a * blk:(a + 1) * blk, b * blk:(b + 1) * blk] = tile

    def scores(g, nk, s_ref):
        q0 = pl.multiple_of(g * t, t)
        k0 = pl.multiple_of((g - (nk - 1)) * t, t)
        qt = (q_ref[pl.ds(q0, t), :] * (DH_B ** -0.5 * LOG2E)).T.astype(BF16)
        s_ref[(nkt - nk) * t:, :] = jnp.dot(kbf_ref[pl.ds(k0, nk * t), :], qt, preferred_element_type=F32)

    def finish(g, nk, s_ref):
        s = s_ref[(nkt - nk) * t:, :] + bias_ref[(nkt - nk) * t:, :]
        m = jnp.max(s, axis=0, keepdims=True)
        p = jnp.exp2(s - m)
        l = jnp.sum(p, axis=0, keepdims=True)
        pb = p.astype(BF16)
        o = jnp.dot(vt_ref[g - (nk - 1)], pb[:t], preferred_element_type=F32)
        for c in range(1, nk):
            o = o + jnp.dot(vt_ref[g - (nk - 1) + c], pb[c * t:(c + 1) * t], preferred_element_type=F32)
        o_ref[pl.ds(pl.multiple_of(g * t, t), t), :] = (o * (1.0 / l)).T

    first = nkt - 1
    for g in range(first):
        scores(g, g + 1, s_refs[g % 4])
        finish(g, g + 1, s_refs[g % 4])

    assert (nq - first - 2) % 4 == 0
    scores(first, nkt, s_refs[0])
    scores(first + 1, nkt, s_refs[1])

    def body(u, c):
        g = first + 4 * u
        for n in range(4):
            scores(g + n + 2, nkt, s_refs[(n + 2) % 4])
            finish(g + n, nkt, s_refs[n])
        return c
    lax.fori_loop(0, (nq - first - 2) // 4, body, 0)
    finish(nq - 2, nkt, s_refs[0])
    finish(nq - 1, nkt, s_refs[1])


def _attn_b_prompt(proj, rel_bias):
    s = proj.shape[1]
    t = B_TQ
    nkt = BAND_PAST // t + 1
    assert s % t == 0 and BAND_PAST % t == 0 and t % REL_CLIP == 0 and REL_CLIP % CHUNK == 0
    head = lambda c: pl.BlockSpec((None, s, HEAD_W), lambda h: (c, 0, h))
    return pl.pallas_call(
        _attn_b_kernel,
        grid=(H_B,),
        in_specs=[pl.BlockSpec(memory_space=pltpu.SMEM), head(QB), head(KB), head(VB)],
        out_specs=pl.BlockSpec((s, HEAD_W), lambda h: (0, h)),
        out_shape=jax.ShapeDtypeStruct((s, H_B * DH_B), F32),
        scratch_shapes=[
            pltpu.VMEM((s, HEAD_W), BF16),
            pltpu.VMEM((s // t, DH_B, t), BF16),
            pltpu.VMEM((nkt * t, t), F32),
        ] + [pltpu.VMEM((nkt * t, t), F32)] * 4,
        compiler_params=pltpu.CompilerParams(
            dimension_semantics=("arbitrary",), vmem_limit_bytes=VMEM_LIMIT),
        name="attn_b_prompt",
    )(rel_bias, proj, proj, proj)


def _dot_nt(a, b):
    return lax.dot_general(a, b, (((1,), (1,)), ((), ())), preferred_element_type=F32)


def _attn_sample_kernel(t5_ref, rb_ref, lq1, lk1, lq2, lk2, g_ref,
                        qa_ref, ka_ref, va_ref, cka_ref, cva_ref,
                        qb_ref, kb_ref, vb_ref, ckb_ref, cvb_ref,
                        oa_ref, ob_ref, kroll_ref, vroll_ref,
                        ba_ref, bb_ref, *, lam_init, past, win):
    n = qa_ref.shape[0]
    near = NEAR
    nh = H_A

    @pl.when(pl.program_id(0) == 0)
    def _():
        qry = lax.broadcasted_iota(jnp.int32, (n, near + n), 0)
        key = lax.broadcasted_iota(jnp.int32, (n, near + n), 1)
        rel = key - near - qry
        for h in range(nh):
            ba_ref[h] = _t5_bias_tile(rel, -(near + n - 1), n - 1, t5_ref, h)
            bb_ref[h] = _rel_bias_tile(rel, -(near + n - 1), n - 1, rb_ref, h)

    kroll_ref[:(win - n) * nh, :] = ckb_ref[n * nh:, :]
    vroll_ref[:(win - n) * nh, :] = cvb_ref[n * nh:, :]

    lam = _lambda(lq1, lk1, lq2, lk2, lam_init)
    lane = lax.broadcasted_iota(jnp.int32, (n, HEAD_W), 1)
    for h in range(nh):
        cols = slice(h * HEAD_W, (h + 1) * HEAD_W)
        new_rows = pl.ds((win - n) * nh + h, n, stride=nh)
        kroll_ref[new_rows, :] = kb_ref[:, cols]
        vroll_ref[new_rows, :] = vb_ref[:, cols]

        q = qa_ref[:, cols] * (DK_A ** -0.5)
        q2 = jnp.concatenate([jnp.where(lane < DK_A, q, 0.0), jnp.where(lane >= DK_A, q, 0.0)],
                             axis=0).astype(BF16)
        ba = ba_ref[h]
        ba2 = jnp.concatenate([ba, ba], axis=0)
        s_c = _dot_nt(q2, cka_ref[:, cols].astype(BF16))
        s_c = jnp.concatenate([s_c[:, :past - near], s_c[:, past - near:] + ba2[:, :near]], axis=1)
        s_n = _dot_nt(q2, ka_ref[:, cols].astype(BF16)) + ba2[:, near:]
        m = jnp.maximum(jnp.max(s_c, axis=-1, keepdims=True), jnp.max(s_n, axis=-1, keepdims=True))
        p_c = jnp.exp(s_c - m)
        p_n = jnp.exp(s_n - m)
        l = jnp.sum(p_c, axis=-1, keepdims=True) + jnp.sum(p_n, axis=-1, keepdims=True)
        cva = cva_ref[pl.ds(h, past, stride=nh), :].astype(BF16)
        o2 = (jnp.dot(p_c.astype(BF16), cva, preferred_element_type=F32)
              + jnp.dot(p_n.astype(BF16), va_ref[:, cols].astype(BF16), preferred_element_type=F32)) * (1.0 / l)
        o = o2[:n] - lam * o2[n:]
        ms = jnp.mean(o * o, axis=-1, keepdims=True)
        oa_ref[:, cols] = (o * lax.rsqrt(ms + EPS) * g_ref[...]) * (1.0 - lam_init)

        qb = qb_ref[:, cols].astype(BF16)
        bb = bb_ref[h]
        ckb = ckb_ref[pl.ds(h, win, stride=nh), :].astype(BF16)
        cvb = cvb_ref[pl.ds(h, win, stride=nh), :].astype(BF16)
        s_c = _dot_nt(qb, ckb) * (DH_B ** -0.5)
        s_c = jnp.concatenate([s_c[:, :win - near] + rb_ref[h, 0], s_c[:, win - near:] + bb[:, :near]], axis=1)
        s_n = _dot_nt(qb, kb_ref[:, cols].astype(BF16)) * (DH_B ** -0.5) + bb[:, near:]
        m = jnp.maximum(jnp.max(s_c, axis=-1, keepdims=True), jnp.max(s_n, axis=-1, keepdims=True))
        p_c = jnp.exp(s_c - m)
        p_n = jnp.exp(s_n - m)
        l = jnp.sum(p_c, axis=-1, keepdims=True) + jnp.sum(p_n, axis=-1, keepdims=True)
        ob_ref[:, cols] = (jnp.dot(p_c.astype(BF16), cvb, preferred_element_type=F32)
                           + jnp.dot(p_n.astype(BF16), vb_ref[:, cols].astype(BF16),
                                     preferred_element_type=F32)) * (1.0 / l)


def _attn_sample(proj, k_a, v_a, ck_a, cv_a, ck_b, cv_b, t5_bias, rel_bias, lq1, lk1, lq2, lk2, subln_g,
                 lam_init):
    nb, past = ck_a.shape[0], ck_a.shape[1]
    win = ck_b.shape[1]
    n = proj.shape[1] // nb
    near = NEAR
    wide = H_A * HEAD_W
    assert past % CHUNK == 0 and n <= CHUNK and win <= BAND_PAST and win <= past
    assert near % HEAD_W == 0 and near <= win and near <= past and n % 8 == 0 and T5_MAX_DIST <= REL_CLIP
    assert H_A == H_B and DV_A == HEAD_W and DH_B == HEAD_W and 2 * DK_A == HEAD_W
    cka = ck_a.reshape(nb, past, wide)
    cva = cv_a.reshape(nb, past * H_A, DV_A)
    ckb = ck_b.reshape(nb, win * H_B, DH_B)
    cvb = cv_b.reshape(nb, win * H_B, DH_B)
    vec = lambda: pl.BlockSpec((1, DK_A), lambda b: (0, 0))
    new = lambda c: pl.BlockSpec((None, n, wide), lambda b: (c, b, 0))
    seq = lambda rows, cols: pl.BlockSpec((None, rows, cols), lambda b: (b, 0, 0))
    out = pl.BlockSpec((n, wide), lambda b: (b, 0))
    return pl.pallas_call(
        functools.partial(_attn_sample_kernel, lam_init=lam_init, past=past, win=win),
        grid=(nb,),
        in_specs=[
            pl.BlockSpec(memory_space=pltpu.SMEM),
            pl.BlockSpec(memory_space=pltpu.SMEM),
            vec(), vec(), vec(), vec(),
            pl.BlockSpec((1, DV_A), lambda b: (0, 0)),
            new(QA), out, out, seq(past, wide), seq(past * H_A, HEAD_W),
            new(QB), new(KB), new(VB), seq(win * H_B, HEAD_W), seq(win * H_B, HEAD_W),
        ],
        out_specs=[out, out, seq(win * H_B, HEAD_W), seq(win * H_B, HEAD_W)],
        out_shape=[
            jax.ShapeDtypeStruct((nb * n, wide), F32),
            jax.ShapeDtypeStruct((nb * n, wide), F32),
            jax.ShapeDtypeStruct((nb, win * H_B, DH_B), F32),
            jax.ShapeDtypeStruct((nb, win * H_B, DH_B), F32),
        ],
        scratch_shapes=[pltpu.VMEM((H_A, n, near + n), F32), pltpu.VMEM((H_B, n, near + n), F32)],
        compiler_params=pltpu.CompilerParams(
            dimension_semantics=("arbitrary",), vmem_limit_bytes=VMEM_LIMIT),
        name="attn_sample",
    )(t5_bias, rel_bias, lq1.reshape(1, DK_A), lk1.reshape(1, DK_A), lq2.reshape(1, DK_A),
      lk2.reshape(1, DK_A), subln_g.reshape(1, DV_A),
      proj, k_a, v_a, cka, cva, proj, proj, proj, ckb, cvb)


def _merge_kernel(x_ref, oa_ref, ob_ref, za_ref, zb_ref, ga0_ref, ga1_ref, gb0_ref, gb1_ref,
                  woa_ref, wob_ref, wout_ref, pg_ref, y_ref):
    za = za_ref[...]
    zb = zb_ref[...]
    a = (oa_ref[...] * (za * _sigmoid(za))).astype(BF16)
    b = (ob_ref[...] * (zb * _sigmoid(zb))).astype(BF16)
    ya = jnp.dot(a, woa_ref[...], preferred_element_type=F32)
    yb = jnp.dot(b, wob_ref[...], preferred_element_type=F32)
    ga = jnp.concatenate([ga0_ref[...], ga1_ref[...]], axis=1)
    gb = jnp.concatenate([gb0_ref[...], gb1_ref[...]], axis=1)
    mix = (_sigmoid(ga) * ya + _sigmoid(gb) * yb).astype(BF16)
    y = jnp.dot(mix, wout_ref[...], preferred_element_type=F32)
    ms = jnp.mean(y * y, axis=-1, keepdims=True)
    y_ref[...] = x_ref[...] + y * lax.rsqrt(ms + EPS) * pg_ref[...]


def _merge(x2d, o_a, o_b, proj, woa, wob, wout, post_g, tm):
    m, d = x2d.shape
    wa = o_a.shape[1]
    wb = o_b.shape[1]
    assert m % tm == 0 and wa == COL_BLOCK and wb == COL_BLOCK and d == 2 * COL_BLOCK
    row = lambda w: pl.BlockSpec((tm, w), lambda i: (i, 0))
    col = lambda c: pl.BlockSpec((None, tm, COL_BLOCK), lambda i: (c, i, 0))
    resident = lambda r, c: pl.BlockSpec((r, c), lambda i: (0, 0), pipeline_mode=pl.Buffered(1))
    return pl.pallas_call(
        _merge_kernel,
        grid=(m // tm,),
        in_specs=[row(d), row(wa), row(wb), col(ZA), col(ZB), col(GA0), col(GA1), col(GB0), col(GB1),
                  resident(wa, d), resident(wb, d), resident(d, d), resident(1, d)],
        out_specs=row(d),
        out_shape=jax.ShapeDtypeStruct((m, d), F32),
        compiler_params=pltpu.CompilerParams(
            dimension_semantics=("arbitrary",), vmem_limit_bytes=VMEM_LIMIT),
        name="merge",
    )(x2d, o_a, o_b, proj, proj, proj, proj, proj, proj, woa, wob, wout, post_g.reshape(1, d))


def kernel(x_prompt, x_sample, cache_k_a, cache_v_a, cache_k_b, cache_v_b, t5_bias, pre_norm, post_norm,
           w_in, lambda_q1, lambda_k1, lambda_q2, lambda_k2, subln_a, rel_bias_b, w_o_a, w_o_b, w_out):
    depth = w_in.shape[0]
    bp, sp, d = x_prompt.shape
    bs, ss, _ = x_sample.shape
    assert bp == 1 and w_in.shape[2] == 12 * COL_BLOCK
    yp = x_prompt.reshape(sp, d)
    ys = x_sample.reshape(bs * ss, d)
    tail = min(BAND_PAST, sp)
    outs = [[] for _ in range(8)]
    for l in range(depth):
        lam_init = 0.8 - 0.6 * math.exp(-0.3 * l)
        w = w_in[l].astype(BF16)
        woa = w_o_a[l].astype(BF16)
        wob = w_o_b[l].astype(BF16)
        wout = w_out[l].astype(BF16)
        lam_args = (lambda_q1[l], lambda_k1[l], lambda_q2[l], lambda_k2[l], subln_a[l], lam_init)

        ps, ka_s, va_s = _in_proj(ys, pre_norm[l], w, tm=bs * ss)
        yp, ps = lax.optimization_barrier((yp, ps))
        pp, ka, va = _in_proj(yp, pre_norm[l], w, tm=1024)
        ob = _attn_b_prompt(pp, rel_bias_b[l])
        ck_a, ob = lax.optimization_barrier((cache_k_a[l], ob))
        oa = _attn_a_prompt(pp, ka, va, t5_bias, *lam_args)
        yp = _merge(yp, oa, ob, pp, woa, wob, wout, post_norm[l], tm=256)
        outs[0].append(ka.reshape(bp, sp, 2 * H_A, DK_A))
        outs[1].append(va.reshape(bp, sp, H_A, DV_A))
        outs[2].append(pp[KB, sp - tail:].reshape(bp, tail, H_B, DH_B))
        outs[3].append(pp[VB, sp - tail:].reshape(bp, tail, H_B, DH_B))

        oas, obs, kroll, vroll = _attn_sample(ps, ka_s, va_s, ck_a, cache_v_a[l], cache_k_b[l], cache_v_b[l],
                                              t5_bias, rel_bias_b[l], *lam_args)
        ys = _merge(ys, oas, obs, ps, woa, wob, wout, post_norm[l], tm=bs * ss)
        outs[4].append(ka_s.reshape(bs, ss, 2 * H_A, DK_A))
        outs[5].append(va_s.reshape(bs, ss, H_A, DV_A))
        outs[6].append(kroll.reshape(bs, -1, H_B, DH_B))
        outs[7].append(vroll.reshape(bs, -1, H_B, DH_B))
    return (yp.reshape(bp, sp, d), ys.reshape(bs, ss, d)) + tuple(jnp.stack(o) for o in outs)
```

```python
import functools
import math

import numpy as np
import jax
import jax.numpy as jnp
from jax import lax
from jax.experimental import pallas as pl
from jax.experimental.pallas import tpu as pltpu

F32 = jnp.float32
BF16 = jnp.bfloat16

CHUNK = 64
H_A = 8
DK_A = 64
DV_A = 2 * DK_A
H_B = 8
DH_B = 128
BAND_CHUNKS = 8
BAND_PAST = BAND_CHUNKS * CHUNK
REL_CLIP = 128
T5_BUCKETS = 32
T5_MAX_DIST = 128
EPS = 1e-6

HEAD_W = 128
COL_BLOCK = 1024
A_TQ, A_TK = 512, 256
B_TQ = 256
LOG2E = math.log2(math.e)
NEAR = max(REL_CLIP, T5_MAX_DIST)
VMEM_LIMIT = 60 * 1024 * 1024

QA, ZA, QB, KB, VB, ZB, GA0, GA1, GB0, GB1 = range(10)
COL_KA, COL_VA = 1, 2
COL_SLOT = (QA, None, None, ZA, QB, KB, VB, ZB, GA0, GA1, GB0, GB1)


def _t5_bucket_int(rel):
    half = T5_BUCKETS // 2
    max_exact = half // 2
    n = abs(rel)
    ret = half if rel > 0 else 0
    if n < max_exact:
        return ret + n
    assert (T5_MAX_DIST // max_exact) ** 2 == 2 ** (half - max_exact)
    j = 0
    while n * n >= (max_exact * max_exact) * 2 ** (j + 1):
        j += 1
    return ret + min(max_exact + j, half - 1)


def _t5_runs(lo, hi):
    runs = []
    for r in range(lo, hi + 1):
        b = _t5_bucket_int(r)
        if not runs or runs[-1][1] != b:
            runs.append((r, b))
    return runs


T5_FAR_BUCKET = _t5_bucket_int(-T5_MAX_DIST)
assert all(_t5_bucket_int(-n) == T5_FAR_BUCKET for n in range(T5_MAX_DIST, 4 * T5_MAX_DIST))


def _t5_bias_tile(rel, lo, hi, t5_ref, h):
    runs = _t5_runs(lo, hi)
    val = jnp.full(rel.shape, t5_ref[runs[0][1], h], F32)
    for start, b in runs[1:]:
        val = jnp.where(rel >= start, t5_ref[b, h], val)
    return val - t5_ref[T5_FAR_BUCKET, h]


def _rel_bias_tile(rel, lo, hi, rb_ref, h):
    lo = max(lo, -REL_CLIP)
    hi = min(hi, REL_CLIP)
    val = jnp.full(rel.shape, rb_ref[h, lo + REL_CLIP], F32)
    for d in range(lo + 1, hi + 1):
        val = jnp.where(rel >= d, rb_ref[h, d + REL_CLIP], val)
    return val


def _sigmoid(x):
    return 1.0 / (1.0 + jnp.exp(-x))


def _lambda(lq1, lk1, lq2, lk2, lam_init):
    a = jnp.sum(lq1[...] * lk1[...], axis=-1, keepdims=True)
    b = jnp.sum(lq2[...] * lk2[...], axis=-1, keepdims=True)
    return jnp.exp(a) - jnp.exp(b) + lam_init


def _in_proj_kernel(x_ref, g_ref, w_ref, o_ref, ka_ref, va_ref, h_ref):
    j = pl.program_id(1)

    @pl.when(j == 0)
    def _():
        x = x_ref[...]
        ms = jnp.mean(x * x, axis=-1, keepdims=True)
        h_ref[...] = (x * lax.rsqrt(ms + EPS) * g_ref[...]).astype(BF16)

    def project(ref):
        ref[...] = jnp.dot(h_ref[...], w_ref[...], preferred_element_type=F32)

    pl.when(j == COL_KA)(functools.partial(project, ka_ref))
    pl.when(j == COL_VA)(functools.partial(project, va_ref))
    pl.when(jnp.logical_and(j != COL_KA, j != COL_VA))(functools.partial(project, o_ref))


def _in_proj(x2d, pre_g, w_bf16, tm):
    m, d = x2d.shape
    n = w_bf16.shape[1]
    assert m % tm == 0 and n == len(COL_SLOT) * COL_BLOCK

    def own(col):
        return pl.BlockSpec((tm, COL_BLOCK), lambda i, j: (jnp.where(j >= col, i, jnp.maximum(i - 1, 0)), 0))

    def slab_index(i, j):
        slot = jnp.where(j <= COL_VA, 0, j - 2)
        return slot, i, 0

    assert COL_SLOT[0] == 0 and COL_SLOT[COL_VA + 1:] == tuple(range(1, len(COL_SLOT) - 2))
    return pl.pallas_call(
        _in_proj_kernel,
        grid=(m // tm, n // COL_BLOCK),
        in_specs=[
            pl.BlockSpec((tm, d), lambda i, j: (i, 0)),
            pl.BlockSpec((1, d), lambda i, j: (0, 0)),
            pl.BlockSpec((d, COL_BLOCK), lambda i, j: (0, j)),
        ],
        out_specs=[pl.BlockSpec((None, tm, COL_BLOCK), slab_index), own(COL_KA), own(COL_VA)],
        out_shape=[jax.ShapeDtypeStruct((len(COL_SLOT) - 2, m, COL_BLOCK), F32),
                   jax.ShapeDtypeStruct((m, COL_BLOCK), F32),
                   jax.ShapeDtypeStruct((m, COL_BLOCK), F32)],
        scratch_shapes=[pltpu.VMEM((tm, d), BF16)],
        compiler_params=pltpu.CompilerParams(
            dimension_semantics=("arbitrary", "arbitrary"), vmem_limit_bytes=VMEM_LIMIT),
        name="in_proj",
    )(x2d, pre_g.reshape(1, d), w_bf16)


def _online_softmax_step(s, vt, m_ref, l_ref, acc_ref):
    m_old = m_ref[...]
    m_new = jnp.maximum(m_old, jnp.max(s, axis=0, keepdims=True))
    alpha = jnp.exp2(m_old - m_new)
    p = jnp.exp2(s - m_new)
    l_ref[...] = alpha * l_ref[...] + jnp.sum(p, axis=0, keepdims=True)
    acc_ref[...] = alpha * acc_ref[...] + jnp.dot(vt, p.astype(BF16), preferred_element_type=F32)
    m_ref[...] = m_new


def _stage_keys_values(k_ref, v_ref, kbf_ref, vt_ref, t):
    def body(j, c):
        r = pl.multiple_of(j * t, t)
        kbf_ref[pl.ds(r, t), :] = k_ref[pl.ds(r, t), :].astype(BF16)
        vt_ref[j] = v_ref[pl.ds(r, t), :].T.astype(BF16)
        return c
    lax.fori_loop(0, vt_ref.shape[0], body, 0, unroll=4)


def _attn_a_kernel(t5_ref, lq1, lk1, lq2, lk2, g_ref, q_ref, k_ref, v_ref, o_ref,
                   kbf_ref, vt_ref, bias_ref, qt_ref, m_ref, l_ref, acc_ref, *s_refs, lam_init):
    tq, tk = A_TQ, A_TK
    h = pl.program_id(0)
    nq = q_ref.shape[0] // tq

    _stage_keys_values(k_ref, v_ref, kbf_ref, vt_ref, tk)
    key = lax.broadcasted_iota(jnp.int32, (tk, tq), 0)
    qry = lax.broadcasted_iota(jnp.int32, (tk, tq), 1)
    for n in range(3):
        rel = key + (n - 1) * tk - qry
        lo, hi = (n - 1) * tk - (tq - 1), min(n * tk - 1, CHUNK - 1)
        b = _t5_bias_tile(jnp.minimum(rel, hi), lo, hi, t5_ref, h) * LOG2E
        if n >= 1:
            b = jnp.where((key + (n - 1) * tk) // CHUNK <= qry // CHUNK, b, -jnp.inf)
        bias_ref[n] = b

    lam = _lambda(lq1, lk1, lq2, lk2, lam_init)

    hq = tq // 2

    def start_tile(i):
        q0 = pl.multiple_of(i * tq, tq)
        qt = (q_ref[pl.ds(q0, tq), :] * (DK_A ** -0.5 * LOG2E)).T
        sub = lax.broadcasted_iota(jnp.int32, (HEAD_W, tq), 0)
        maps = (jnp.where(sub < DK_A, qt, 0.0).astype(BF16), jnp.where(sub >= DK_A, qt, 0.0).astype(BF16))
        for c in range(4):
            qt_ref[:, c * hq:(c + 1) * hq] = maps[c % 2][:, (c // 2) * hq:(c // 2 + 1) * hq]

    def reset_state():
        m_ref[...] = jnp.full(m_ref.shape, -jnp.inf, F32)
        l_ref[...] = jnp.zeros(l_ref.shape, F32)
        acc_ref[...] = jnp.zeros(acc_ref.shape, F32)

    def scores(j, s_ref, late_only=False):
        r = pl.multiple_of(j * tk, tk)
        c0 = tq if late_only else 0
        s_ref[:, c0:] = jnp.dot(kbf_ref[pl.ds(r, tk), :], qt_ref[:, c0:], preferred_element_type=F32)

    def update(j, s_ref, bias_idx, late_only=False):
        c0 = tq if late_only else 0
        s = s_ref[:, c0:]
        if bias_idx is not None:
            b = bias_ref[bias_idx]
            halves = [b[:, (c // 2) * hq:(c // 2 + 1) * hq] for c in range(c0 // hq, 4)]
            s = jnp.concatenate([s[:, n * hq:(n + 1) * hq] + bh for n, bh in enumerate(halves)], axis=1)
        _online_softmax_step(s, vt_ref[j], m_ref.at[:, c0:], l_ref.at[:, c0:], acc_ref.at[:, c0:])

    def q_tile(i, carry):
        odd = jnp.logical_and(i >= 2, i % 2 == 0)

        @pl.when(odd)
        def _():
            update(0, s_refs[0], None)
            scores(2, s_refs[0])
            update(1, s_refs[1], None)
            scores(3, s_refs[1])

        j0 = jnp.where(odd, 2, 0)

        def quad(j):
            for n in range(4):
                scores(j + n + 2, s_refs[(n + 2) % 4])
                update(j + n, s_refs[n], None)

        def two_quads(u, c):
            quad(j0 + 8 * u)
            quad(j0 + 8 * u + 4)
            return c
        quads = jnp.maximum(i - 1, 0) // 2
        lax.fori_loop(0, quads // 2, two_quads, 0)

        @pl.when(quads % 2 == 1)
        def _():
            quad(j0 + 4 * (quads - 1))

        nxt = jnp.minimum(i + 1, nq - 1)

        @pl.when(i >= 1)
        def _():
            j = 2 * i - 2
            scores(j + 2, s_refs[2])
            update(j, s_refs[0], None)
            scores(j + 3, s_refs[3], late_only=True)
            update(j + 1, s_refs[1], 0)
            start_tile(nxt)
            scores(0, s_refs[0])
            update(j + 2, s_refs[2], 1)
            scores(1, s_refs[1])
            update(j + 3, s_refs[3], 2, late_only=True)

        @pl.when(i == 0)
        def _():
            update(0, s_refs[0], 1)
            start_tile(nxt)
            scores(0, s_refs[0])
            update(1, s_refs[1], 2, late_only=True)
            scores(1, s_refs[1])

        inv = 1.0 / l_ref[...]
        acc = acc_ref[...]
        o = jnp.concatenate(
            [acc[:, c:c + hq] * inv[:, c:c + hq] - lam * (acc[:, c + hq:c + tq] * inv[:, c + hq:c + tq])
             for c in (0, tq)], axis=1)
        ms = jnp.mean(o * o, axis=0, keepdims=True)
        y = (o * lax.rsqrt(ms + EPS) * g_ref[...]) * (1.0 - lam_init)
        o_ref[pl.ds(pl.multiple_of(i * tq, tq), tq), :] = y.T
        reset_state()
        return carry

    start_tile(0)
    scores(0, s_refs[0])
    scores(1, s_refs[1])
    reset_state()
    lax.fori_loop(0, nq, q_tile, 0)


def _attn_a_prompt(proj, k_a, v_a, t5_bias, lq1, lk1, lq2, lk2, subln_g, lam_init):
    s = proj.shape[1]
    tq, tk = A_TQ, A_TK
    assert s % tq == 0 and tq == 2 * tk and tk % CHUNK == 0 and tk >= T5_MAX_DIST
    vec = lambda: pl.BlockSpec((1, DK_A), lambda h: (0, 0))
    head = lambda: pl.BlockSpec((s, HEAD_W), lambda h: (0, h))
    return pl.pallas_call(
        functools.partial(_attn_a_kernel, lam_init=lam_init),
        grid=(H_A,),
        in_specs=[
            pl.BlockSpec(memory_space=pltpu.SMEM),
            vec(), vec(), vec(), vec(),
            pl.BlockSpec((DV_A, 1), lambda h: (0, 0)),
            pl.BlockSpec((None, s, HEAD_W), lambda h: (QA, 0, h)), head(), head(),
        ],
        out_specs=pl.BlockSpec((s, HEAD_W), lambda h: (0, h)),
        out_shape=jax.ShapeDtypeStruct((s, H_A * DV_A), F32),
        scratch_shapes=[
            pltpu.VMEM((s, HEAD_W), BF16),
            pltpu.VMEM((s // tk, DV_A, tk), BF16),
            pltpu.VMEM((3, tk, tq), F32),
            pltpu.VMEM((HEAD_W, 2 * tq), BF16),
            pltpu.VMEM((1, 2 * tq), F32),
            pltpu.VMEM((1, 2 * tq), F32),
            pltpu.VMEM((DV_A, 2 * tq), F32),
        ] + [pltpu.VMEM((tk, 2 * tq), F32)] * 4,
        compiler_params=pltpu.CompilerParams(
            dimension_semantics=("arbitrary",), vmem_limit_bytes=VMEM_LIMIT),
        name="attn_a_prompt",
    )(t5_bias, lq1.reshape(1, DK_A), lk1.reshape(1, DK_A), lq2.reshape(1, DK_A), lk2.reshape(1, DK_A),
      subln_g.reshape(DV_A, 1), proj, k_a, v_a)


def _attn_b_kernel(rb_ref, q_ref, k_ref, v_ref, o_ref, kbf_ref, vt_ref, bias_ref, *s_refs):
    t = B_TQ
    blk = REL_CLIP
    nkt = BAND_PAST // t + 1
    nq = q_ref.shape[0] // t
    h = pl.program_id(0)

    _stage_keys_values(k_ref, v_ref, kbf_ref, vt_ref, t)

    kk = lax.broadcasted_iota(jnp.int32, (blk, blk), 0)
    qq = lax.broadcasted_iota(jnp.int32, (blk, blk), 1)
    rel = kk - qq
    lo = jnp.full((blk, blk), rb_ref[h, 0] * LOG2E, F32)
    same = _rel_bias_tile(rel, -(blk - 1), blk - 1, rb_ref, h) * LOG2E
    prev = _rel_bias_tile(rel - blk, -(2 * blk - 1), -1, rb_ref, h) * LOG2E
    ninf = jnp.full((blk, blk), -jnp.inf, F32)
    kc = kk // CHUNK
    qc = qq // CHUNK
    far_blocks = BAND_PAST // blk
    for a in range(nkt * t // blk):
        for b in range(t // blk):
            e = a - b
            if e < 0 or e > far_blocks:
                tile = ninf
            elif e == 0:
                tile = jnp.where(kc >= qc, lo, -jnp.inf)
            elif e == far_blocks:
                tile = jnp.where(kc <= qc, same, -jnp.inf)
            elif e == far_blocks - 1:
                tile = prev
            else:
                tile = lo
            bias_ref[a * blk:(a + 1) * blk, b * blk:(b + 1) * blk] = tile

    def scores(g, nk, s_ref):
        q0 = pl.multiple_of(g * t, t)
        k0 = pl.multiple_of((g - (nk - 1)) * t, t)
        qt = (q_ref[pl.ds(q0, t), :] * (DH_B ** -0.5 * LOG2E)).T.astype(BF16)
        s_ref[(nkt - nk) * t:, :] = jnp.dot(kbf_ref[pl.ds(k0, nk * t), :], qt, preferred_element_type=F32)

    def finish(g, nk, s_ref):
        s = s_ref[(nkt - nk) * t:, :] + bias_ref[(nkt - nk) * t:, :]
        m = jnp.max(s, axis=0, keepdims=True)
        p = jnp.exp2(s - m)
        l = jnp.sum(p, axis=0, keepdims=True)
        pb = p.astype(BF16)
        o = jnp.dot(vt_ref[g - (nk - 1)], pb[:t], preferred_element_type=F32)
        for c in range(1, nk):
            o = o + jnp.dot(vt_ref[g - (nk - 1) + c], pb[c * t:(c + 1) * t], preferred_element_type=F32)
        o_ref[pl.ds(pl.multiple_of(g * t, t), t), :] = (o * (1.0 / l)).T

    first = nkt - 1
    for g in range(first):
        scores(g, g + 1, s_refs[g % 4])
        finish(g, g + 1, s_refs[g % 4])

    assert (nq - first - 2) % 4 == 0
    scores(first, nkt, s_refs[0])
    scores(first + 1, nkt, s_refs[1])

    def quad(g):
        for n in range(4):
            scores(g + n + 2, nkt, s_refs[(n + 2) % 4])
            finish(g + n, nkt, s_refs[n])

    def two_quads(u, c):
        quad(first + 8 * u)
        quad(first + 8 * u + 4)
        return c
    quads = (nq - first - 2) // 4
    lax.fori_loop(0, quads // 2, two_quads, 0)
    if quads % 2:
        quad(first + 4 * (quads - 1))
    finish(nq - 2, nkt, s_refs[0])
    finish(nq - 1, nkt, s_refs[1])


def _attn_b_prompt(proj, rel_bias):
    s = proj.shape[1]
    t = B_TQ
    nkt = BAND_PAST // t + 1
    assert s % t == 0 and BAND_PAST % t == 0 and t % REL_CLIP == 0 and REL_CLIP % CHUNK == 0
    head = lambda c: pl.BlockSpec((None, s, HEAD_W), lambda h: (c, 0, h))
    return pl.pallas_call(
        _attn_b_kernel,
        grid=(H_B,),
        in_specs=[pl.BlockSpec(memory_space=pltpu.SMEM), head(QB), head(KB), head(VB)],
        out_specs=pl.BlockSpec((s, HEAD_W), lambda h: (0, h)),
        out_shape=jax.ShapeDtypeStruct((s, H_B * DH_B), F32),
        scratch_shapes=[
            pltpu.VMEM((s, HEAD_W), BF16),
            pltpu.VMEM((s // t, DH_B, t), BF16),
            pltpu.VMEM((nkt * t, t), F32),
        ] + [pltpu.VMEM((nkt * t, t), F32)] * 4,
        compiler_params=pltpu.CompilerParams(
            dimension_semantics=("arbitrary",), vmem_limit_bytes=VMEM_LIMIT),
        name="attn_b_prompt",
    )(rel_bias, proj, proj, proj)


def _dot_nt(a, b):
    return lax.dot_general(a, b, (((1,), (1,)), ((), ())), preferred_element_type=F32)


def _attn_sample_kernel(t5_ref, rb_ref, lq1, lk1, lq2, lk2, g_ref,
                        qa_ref, ka_ref, va_ref, cka_ref, cva_ref,
                        qb_ref, kb_ref, vb_ref, ckb_ref, cvb_ref,
                        oa_ref, ob_ref, kroll_ref, vroll_ref,
                        ba_ref, bb_ref, *, lam_init, past, win):
    n = qa_ref.shape[0]
    near = NEAR
    nh = H_A

    @pl.when(pl.program_id(0) == 0)
    def _():
        qry = lax.broadcasted_iota(jnp.int32, (n, near + n), 0)
        key = lax.broadcasted_iota(jnp.int32, (n, near + n), 1)
        rel = key - near - qry
        for h in range(nh):
            ba_ref[h] = _t5_bias_tile(rel, -(near + n - 1), n - 1, t5_ref, h)
            bb_ref[h] = _rel_bias_tile(rel, -(near + n - 1), n - 1, rb_ref, h)

    kroll_ref[:(win - n) * nh, :] = ckb_ref[n * nh:, :]
    vroll_ref[:(win - n) * nh, :] = cvb_ref[n * nh:, :]

    lam = _lambda(lq1, lk1, lq2, lk2, lam_init)
    lane = lax.broadcasted_iota(jnp.int32, (n, HEAD_W), 1)
    for h in range(nh):
        cols = slice(h * HEAD_W, (h + 1) * HEAD_W)
        new_rows = pl.ds((win - n) * nh + h, n, stride=nh)
        kroll_ref[new_rows, :] = kb_ref[:, cols]
        vroll_ref[new_rows, :] = vb_ref[:, cols]

        q = qa_ref[:, cols] * (DK_A ** -0.5)
        q2 = jnp.concatenate([jnp.where(lane < DK_A, q, 0.0), jnp.where(lane >= DK_A, q, 0.0)],
                             axis=0).astype(BF16)
        ba = ba_ref[h]
        ba2 = jnp.concatenate([ba, ba], axis=0)
        s_c = _dot_nt(q2, cka_ref[:, cols].astype(BF16))
        s_c = jnp.concatenate([s_c[:, :past - near], s_c[:, past - near:] + ba2[:, :near]], axis=1)
        s_n = _dot_nt(q2, ka_ref[:, cols].astype(BF16)) + ba2[:, near:]
        m = jnp.maximum(jnp.max(s_c, axis=-1, keepdims=True), jnp.max(s_n, axis=-1, keepdims=True))
        p_c = jnp.exp(s_c - m)
        p_n = jnp.exp(s_n - m)
        l = jnp.sum(p_c, axis=-1, keepdims=True) + jnp.sum(p_n, axis=-1, keepdims=True)
        cva = cva_ref[pl.ds(h, past, stride=nh), :].astype(BF16)
        o2 = (jnp.dot(p_c.astype(BF16), cva, preferred_element_type=F32)
              + jnp.dot(p_n.astype(BF16), va_ref[:, cols].astype(BF16), preferred_element_type=F32)) * (1.0 / l)
        o = o2[:n] - lam * o2[n:]
        ms = jnp.mean(o * o, axis=-1, keepdims=True)
        oa_ref[:, cols] = (o * lax.rsqrt(ms + EPS) * g_ref[...]) * (1.0 - lam_init)

        qb = qb_ref[:, cols].astype(BF16)
        bb = bb_ref[h]
        ckb = ckb_ref[pl.ds(h, win, stride=nh), :].astype(BF16)
        cvb = cvb_ref[pl.ds(h, win, stride=nh), :].astype(BF16)
        s_c = _dot_nt(qb, ckb) * (DH_B ** -0.5)
        s_c = jnp.concatenate([s_c[:, :win - near] + rb_ref[h, 0], s_c[:, win - near:] + bb[:, :near]], axis=1)
        s_n = _dot_nt(qb, kb_ref[:, cols].astype(BF16)) * (DH_B ** -0.5) + bb[:, near:]
        m = jnp.maximum(jnp.max(s_c, axis=-1, keepdims=True), jnp.max(s_n, axis=-1, keepdims=True))
        p_c = jnp.exp(s_c - m)
        p_n = jnp.exp(s_n - m)
        l = jnp.sum(p_c, axis=-1, keepdims=True) + jnp.sum(p_n, axis=-1, keepdims=True)
        ob_ref[:, cols] = (jnp.dot(p_c.astype(BF16), cvb, preferred_element_type=F32)
                           + jnp.dot(p_n.astype(BF16), vb_ref[:, cols].astype(BF16),
                                     preferred_element_type=F32)) * (1.0 / l)


def _attn_sample(proj, k_a, v_a, ck_a, cv_a, ck_b, cv_b, t5_bias, rel_bias, lq1, lk1, lq2, lk2, subln_g,
                 lam_init):
    nb, past = ck_a.shape[0], ck_a.shape[1]
    win = ck_b.shape[1]
    n = proj.shape[1] // nb
    near = NEAR
    wide = H_A * HEAD_W
    assert past % CHUNK == 0 and n <= CHUNK and win <= BAND_PAST and win <= past
    assert near % HEAD_W == 0 and near <= win and near <= past and n % 8 == 0 and T5_MAX_DIST <= REL_CLIP
    assert H_A == H_B and DV_A == HEAD_W and DH_B == HEAD_W and 2 * DK_A == HEAD_W
    cka = ck_a.reshape(nb, past, wide)
    cva = cv_a.reshape(nb, past * H_A, DV_A)
    ckb = ck_b.reshape(nb, win * H_B, DH_B)
    cvb = cv_b.reshape(nb, win * H_B, DH_B)
    vec = lambda: pl.BlockSpec((1, DK_A), lambda b: (0, 0))
    new = lambda c: pl.BlockSpec((None, n, wide), lambda b: (c, b, 0))
    seq = lambda rows, cols: pl.BlockSpec((None, rows, cols), lambda b: (b, 0, 0))
    out = pl.BlockSpec((n, wide), lambda b: (b, 0))
    return pl.pallas_call(
        functools.partial(_attn_sample_kernel, lam_init=lam_init, past=past, win=win),
        grid=(nb,),
        in_specs=[
            pl.BlockSpec(memory_space=pltpu.SMEM),
            pl.BlockSpec(memory_space=pltpu.SMEM),
            vec(), vec(), vec(), vec(),
            pl.BlockSpec((1, DV_A), lambda b: (0, 0)),
            new(QA), out, out, seq(past, wide), seq(past * H_A, HEAD_W),
            new(QB), new(KB), new(VB), seq(win * H_B, HEAD_W), seq(win * H_B, HEAD_W),
        ],
        out_specs=[out, out, seq(win * H_B, HEAD_W), seq(win * H_B, HEAD_W)],
        out_shape=[
            jax.ShapeDtypeStruct((nb * n, wide), F32),
            jax.ShapeDtypeStruct((nb * n, wide), F32),
            jax.ShapeDtypeStruct((nb, win * H_B, DH_B), F32),
            jax.ShapeDtypeStruct((nb, win * H_B, DH_B), F32),
        ],
        scratch_shapes=[pltpu.VMEM((H_A, n, near + n), F32), pltpu.VMEM((H_B, n, near + n), F32)],
        compiler_params=pltpu.CompilerParams(
            dimension_semantics=("arbitrary",), vmem_limit_bytes=VMEM_LIMIT),
        name="attn_sample",
    )(t5_bias, rel_bias, lq1.reshape(1, DK_A), lk1.reshape(1, DK_A), lq2.reshape(1, DK_A),
      lk2.reshape(1, DK_A), subln_g.reshape(1, DV_A),
      proj, k_a, v_a, cka, cva, proj, proj, proj, ckb, cvb)


def _merge_kernel(x_ref, oa_ref, ob_ref, za_ref, zb_ref, ga0_ref, ga1_ref, gb0_ref, gb1_ref,
                  woa_ref, wob_ref, wout_ref, pg_ref, y_ref):
    za = za_ref[...]
    zb = zb_ref[...]
    a = (oa_ref[...] * (za * _sigmoid(za))).astype(BF16)
    b = (ob_ref[...] * (zb * _sigmoid(zb))).astype(BF16)
    ya = jnp.dot(a, woa_ref[...], preferred_element_type=F32)
    yb = jnp.dot(b, wob_ref[...], preferred_element_type=F32)
    ga = jnp.concatenate([ga0_ref[...], ga1_ref[...]], axis=1)
    gb = jnp.concatenate([gb0_ref[...], gb1_ref[...]], axis=1)
    mix = (_sigmoid(ga) * ya + _sigmoid(gb) * yb).astype(BF16)
    y = jnp.dot(mix, wout_ref[...], preferred_element_type=F32)
    ms = jnp.mean(y * y, axis=-1, keepdims=True)
    y_ref[...] = x_ref[...] + y * lax.rsqrt(ms + EPS) * pg_ref[...]


def _merge(x2d, o_a, o_b, proj, woa, wob, wout, post_g, tm):
    m, d = x2d.shape
    wa = o_a.shape[1]
    wb = o_b.shape[1]
    assert m % tm == 0 and wa == COL_BLOCK and wb == COL_BLOCK and d == 2 * COL_BLOCK
    row = lambda w: pl.BlockSpec((tm, w), lambda i: (i, 0))
    col = lambda c: pl.BlockSpec((None, tm, COL_BLOCK), lambda i: (c, i, 0))
    resident = lambda r, c: pl.BlockSpec((r, c), lambda i: (0, 0), pipeline_mode=pl.Buffered(1))
    return pl.pallas_call(
        _merge_kernel,
        grid=(m // tm,),
        in_specs=[row(d), row(wa), row(wb), col(ZA), col(ZB), col(GA0), col(GA1), col(GB0), col(GB1),
                  resident(wa, d), resident(wb, d), resident(d, d), resident(1, d)],
        out_specs=row(d),
        out_shape=jax.ShapeDtypeStruct((m, d), F32),
        compiler_params=pltpu.CompilerParams(
            dimension_semantics=("arbitrary",), vmem_limit_bytes=VMEM_LIMIT),
        name="merge",
    )(x2d, o_a, o_b, proj, proj, proj, proj, proj, proj, woa, wob, wout, post_g.reshape(1, d))


def kernel(x_prompt, x_sample, cache_k_a, cache_v_a, cache_k_b, cache_v_b, t5_bias, pre_norm, post_norm,
           w_in, lambda_q1, lambda_k1, lambda_q2, lambda_k2, subln_a, rel_bias_b, w_o_a, w_o_b, w_out):
    depth = w_in.shape[0]
    bp, sp, d = x_prompt.shape
    bs, ss, _ = x_sample.shape
    assert bp == 1 and w_in.shape[2] == 12 * COL_BLOCK
    yp = x_prompt.reshape(sp, d)
    ys = x_sample.reshape(bs * ss, d)
    tail = min(BAND_PAST, sp)
    outs = [[] for _ in range(8)]
    for l in range(depth):
        lam_init = 0.8 - 0.6 * math.exp(-0.3 * l)
        w = w_in[l].astype(BF16)
        woa = w_o_a[l].astype(BF16)
        wob = w_o_b[l].astype(BF16)
        wout = w_out[l].astype(BF16)
        lam_args = (lambda_q1[l], lambda_k1[l], lambda_q2[l], lambda_k2[l], subln_a[l], lam_init)

        ps, ka_s, va_s = _in_proj(ys, pre_norm[l], w, tm=bs * ss)
        yp, ps = lax.optimization_barrier((yp, ps))
        pp, ka, va = _in_proj(yp, pre_norm[l], w, tm=1024)
        ob = _attn_b_prompt(pp, rel_bias_b[l])
        ck_a, ob = lax.optimization_barrier((cache_k_a[l], ob))
        oa = _attn_a_prompt(pp, ka, va, t5_bias, *lam_args)
        yp = _merge(yp, oa, ob, pp, woa, wob, wout, post_norm[l], tm=256)
        outs[0].append(ka.reshape(bp, sp, 2 * H_A, DK_A))
        outs[1].append(va.reshape(bp, sp, H_A, DV_A))
        outs[2].append(pp[KB, sp - tail:].reshape(bp, tail, H_B, DH_B))
        outs[3].append(pp[VB, sp - tail:].reshape(bp, tail, H_B, DH_B))

        oas, obs, kroll, vroll = _attn_sample(ps, ka_s, va_s, ck_a, cache_v_a[l], cache_k_b[l], cache_v_b[l],
                                              t5_bias, rel_bias_b[l], *lam_args)
        ys = _merge(ys, oas, obs, ps, woa, wob, wout, post_norm[l], tm=bs * ss)
        outs[4].append(ka_s.reshape(bs, ss, 2 * H_A, DK_A))
        outs[5].append(va_s.reshape(bs, ss, H_A, DV_A))
        outs[6].append(kroll.reshape(bs, -1, H_B, DH_B))
        outs[7].append(vroll.reshape(bs, -1, H_B, DH_B))
    return (yp.reshape(bp, sp, d), ys.reshape(bs, ss, d)) + tuple(jnp.stack(o) for o in outs)
```

```python
import functools
import math

import numpy as np
import jax
import jax.numpy as jnp
from jax import lax
from jax.experimental import pallas as pl
from jax.experimental.pallas import tpu as pltpu

F32 = jnp.float32
BF16 = jnp.bfloat16

CHUNK = 64
H_A = 8
DK_A = 64
DV_A = 2 * DK_A
H_B = 8
DH_B = 128
BAND_CHUNKS = 8
BAND_PAST = BAND_CHUNKS * CHUNK
REL_CLIP = 128
T5_BUCKETS = 32
T5_MAX_DIST = 128
EPS = 1e-6

HEAD_W = 128
COL_BLOCK = 1024
A_TQ, A_TK = 512, 256
B_TQ = 256
SAMPLE_HEAD_GROUP = 4
LOG2E = math.log2(math.e)
NEAR = max(REL_CLIP, T5_MAX_DIST)
VMEM_LIMIT = 60 * 1024 * 1024

QA, ZA, QB, KB, VB, ZB, GA0, GA1, GB0, GB1 = range(10)
COL_KA, COL_VA = 1, 2
COL_SLOT = (QA, None, None, ZA, QB, KB, VB, ZB, GA0, GA1, GB0, GB1)


def _t5_bucket_int(rel):
    half = T5_BUCKETS // 2
    max_exact = half // 2
    n = abs(rel)
    ret = half if rel > 0 else 0
    if n < max_exact:
        return ret + n
    assert (T5_MAX_DIST // max_exact) ** 2 == 2 ** (half - max_exact)
    j = 0
    while n * n >= (max_exact * max_exact) * 2 ** (j + 1):
        j += 1
    return ret + min(max_exact + j, half - 1)


def _t5_runs(lo, hi):
    runs = []
    for r in range(lo, hi + 1):
        b = _t5_bucket_int(r)
        if not runs or runs[-1][1] != b:
            runs.append((r, b))
    return runs


T5_FAR_BUCKET = _t5_bucket_int(-T5_MAX_DIST)
assert all(_t5_bucket_int(-n) == T5_FAR_BUCKET for n in range(T5_MAX_DIST, 4 * T5_MAX_DIST))


def _t5_bias_tile(rel, lo, hi, t5_ref, h):
    runs = _t5_runs(lo, hi)
    val = jnp.full(rel.shape, t5_ref[runs[0][1], h], F32)
    for start, b in runs[1:]:
        val = jnp.where(rel >= start, t5_ref[b, h], val)
    return val - t5_ref[T5_FAR_BUCKET, h]


def _rel_bias_tile(rel, lo, hi, rb_ref, h):
    lo = max(lo, -REL_CLIP)
    hi = min(hi, REL_CLIP)
    val = jnp.full(rel.shape, rb_ref[h, lo + REL_CLIP], F32)
    for d in range(lo + 1, hi + 1):
        val = jnp.where(rel >= d, rb_ref[h, d + REL_CLIP], val)
    return val


def _sigmoid(x):
    return 1.0 / (1.0 + jnp.exp(-x))


def _lambda(lq1, lk1, lq2, lk2, lam_init):
    a = jnp.sum(lq1[...] * lk1[...], axis=-1, keepdims=True)
    b = jnp.sum(lq2[...] * lk2[...], axis=-1, keepdims=True)
    return jnp.exp(a) - jnp.exp(b) + lam_init


def _in_proj_kernel(x_ref, g_ref, w_ref, o_ref, ka_ref, va_ref, h_ref):
    j = pl.program_id(1)

    @pl.when(j == 0)
    def _():
        x = x_ref[...]
        ms = jnp.mean(x * x, axis=-1, keepdims=True)
        h_ref[...] = (x * lax.rsqrt(ms + EPS) * g_ref[...]).astype(BF16)

    def project(ref):
        ref[...] = jnp.dot(h_ref[...], w_ref[...], preferred_element_type=F32)

    pl.when(j == COL_KA)(functools.partial(project, ka_ref))
    pl.when(j == COL_VA)(functools.partial(project, va_ref))
    pl.when(jnp.logical_and(j != COL_KA, j != COL_VA))(functools.partial(project, o_ref))


def _in_proj(x2d, pre_g, w_bf16, tm):
    m, d = x2d.shape
    n = w_bf16.shape[1]
    assert m % tm == 0 and n == len(COL_SLOT) * COL_BLOCK

    def own(col):
        return pl.BlockSpec((tm, COL_BLOCK), lambda i, j: (jnp.where(j >= col, i, jnp.maximum(i - 1, 0)), 0))

    def slab_index(i, j):
        slot = jnp.where(j <= COL_VA, 0, j - 2)
        return slot, i, 0

    assert COL_SLOT[0] == 0 and COL_SLOT[COL_VA + 1:] == tuple(range(1, len(COL_SLOT) - 2))
    return pl.pallas_call(
        _in_proj_kernel,
        grid=(m // tm, n // COL_BLOCK),
        in_specs=[
            pl.BlockSpec((tm, d), lambda i, j: (i, 0)),
            pl.BlockSpec((1, d), lambda i, j: (0, 0)),
            pl.BlockSpec((d, COL_BLOCK), lambda i, j: (0, j)),
        ],
        out_specs=[pl.BlockSpec((None, tm, COL_BLOCK), slab_index), own(COL_KA), own(COL_VA)],
        out_shape=[jax.ShapeDtypeStruct((len(COL_SLOT) - 2, m, COL_BLOCK), F32),
                   jax.ShapeDtypeStruct((m, COL_BLOCK), F32),
                   jax.ShapeDtypeStruct((m, COL_BLOCK), F32)],
        scratch_shapes=[pltpu.VMEM((tm, d), BF16)],
        compiler_params=pltpu.CompilerParams(
            dimension_semantics=("arbitrary", "arbitrary"), vmem_limit_bytes=VMEM_LIMIT),
        name="in_proj",
    )(x2d, pre_g.reshape(1, d), w_bf16)


def _online_softmax_step(s, vt, m_ref, l_ref, acc_ref):
    m_old = m_ref[...]
    m_new = jnp.maximum(m_old, jnp.max(s, axis=0, keepdims=True))
    alpha = jnp.exp2(m_old - m_new)
    p = jnp.exp2(s - m_new)
    l_ref[...] = alpha * l_ref[...] + jnp.sum(p, axis=0, keepdims=True)
    acc_ref[...] = alpha * acc_ref[...] + jnp.dot(vt, p.astype(BF16), preferred_element_type=F32)
    m_ref[...] = m_new


def _stage_keys_values(k_ref, v_ref, kbf_ref, vt_ref, t):
    def body(j, c):
        r = pl.multiple_of(j * t, t)
        kbf_ref[pl.ds(r, t), :] = k_ref[pl.ds(r, t), :].astype(BF16)
        vt_ref[j] = v_ref[pl.ds(r, t), :].T.astype(BF16)
        return c
    lax.fori_loop(0, vt_ref.shape[0], body, 0, unroll=4)


def _attn_a_kernel(t5_ref, lq1, lk1, lq2, lk2, g_ref, q_ref, k_ref, v_ref, o_ref,
                   kbf_ref, vt_ref, bias_ref, qt_ref, m_ref, l_ref, acc_ref, *s_refs, lam_init):
    tq, tk = A_TQ, A_TK
    h = pl.program_id(0)
    nq = q_ref.shape[0] // tq

    _stage_keys_values(k_ref, v_ref, kbf_ref, vt_ref, tk)
    key = lax.broadcasted_iota(jnp.int32, (tk, tq), 0)
    qry = lax.broadcasted_iota(jnp.int32, (tk, tq), 1)
    for n in range(3):
        rel = key + (n - 1) * tk - qry
        lo, hi = (n - 1) * tk - (tq - 1), min(n * tk - 1, CHUNK - 1)
        b = _t5_bias_tile(jnp.minimum(rel, hi), lo, hi, t5_ref, h) * LOG2E
        if n >= 1:
            b = jnp.where((key + (n - 1) * tk) // CHUNK <= qry // CHUNK, b, -jnp.inf)
        bias_ref[n] = b

    lam = _lambda(lq1, lk1, lq2, lk2, lam_init)

    hq = tq // 2

    def start_tile(i):
        q0 = pl.multiple_of(i * tq, tq)
        qt = (q_ref[pl.ds(q0, tq), :] * (DK_A ** -0.5 * LOG2E)).T
        sub = lax.broadcasted_iota(jnp.int32, (HEAD_W, tq), 0)
        maps = (jnp.where(sub < DK_A, qt, 0.0).astype(BF16), jnp.where(sub >= DK_A, qt, 0.0).astype(BF16))
        for c in range(4):
            qt_ref[:, c * hq:(c + 1) * hq] = maps[c % 2][:, (c // 2) * hq:(c // 2 + 1) * hq]

    def reset_state():
        m_ref[...] = jnp.full(m_ref.shape, -jnp.inf, F32)
        l_ref[...] = jnp.zeros(l_ref.shape, F32)
        acc_ref[...] = jnp.zeros(acc_ref.shape, F32)

    def scores(j, s_ref, late_only=False):
        r = pl.multiple_of(j * tk, tk)
        c0 = tq if late_only else 0
        s_ref[:, c0:] = jnp.dot(kbf_ref[pl.ds(r, tk), :], qt_ref[:, c0:], preferred_element_type=F32)

    def update(j, s_ref, bias_idx, late_only=False):
        c0 = tq if late_only else 0
        s = s_ref[:, c0:]
        if bias_idx is not None:
            b = bias_ref[bias_idx]
            halves = [b[:, (c // 2) * hq:(c // 2 + 1) * hq] for c in range(c0 // hq, 4)]
            s = jnp.concatenate([s[:, n * hq:(n + 1) * hq] + bh for n, bh in enumerate(halves)], axis=1)
        _online_softmax_step(s, vt_ref[j], m_ref.at[:, c0:], l_ref.at[:, c0:], acc_ref.at[:, c0:])

    def q_tile(i, carry):
        odd = jnp.logical_and(i >= 2, i % 2 == 0)

        @pl.when(odd)
        def _():
            update(0, s_refs[0], None)
            scores(2, s_refs[0])
            update(1, s_refs[1], None)
            scores(3, s_refs[1])

        j0 = jnp.where(odd, 2, 0)

        def quad(j):
            for n in range(4):
                scores(j + n + 2, s_refs[(n + 2) % 4])
                update(j + n, s_refs[n], None)

        def two_quads(u, c):
            quad(j0 + 8 * u)
            quad(j0 + 8 * u + 4)
            return c
        quads = jnp.maximum(i - 1, 0) // 2
        lax.fori_loop(0, quads // 2, two_quads, 0)

        @pl.when(quads % 2 == 1)
        def _():
            quad(j0 + 4 * (quads - 1))

        nxt = jnp.minimum(i + 1, nq - 1)

        @pl.when(i >= 1)
        def _():
            j = 2 * i - 2
            scores(j + 2, s_refs[2])
            update(j, s_refs[0], None)
            scores(j + 3, s_refs[3], late_only=True)
            update(j + 1, s_refs[1], 0)
            start_tile(nxt)
            scores(0, s_refs[0])
            update(j + 2, s_refs[2], 1)
            scores(1, s_refs[1])
            update(j + 3, s_refs[3], 2, late_only=True)

        @pl.when(i == 0)
        def _():
            update(0, s_refs[0], 1)
            start_tile(nxt)
            scores(0, s_refs[0])
            update(1, s_refs[1], 2, late_only=True)
            scores(1, s_refs[1])

        inv = 1.0 / l_ref[...]
        acc = acc_ref[...]
        o = jnp.concatenate(
            [acc[:, c:c + hq] * inv[:, c:c + hq] - lam * (acc[:, c + hq:c + tq] * inv[:, c + hq:c + tq])
             for c in (0, tq)], axis=1)
        ms = jnp.mean(o * o, axis=0, keepdims=True)
        y = (o * lax.rsqrt(ms + EPS) * g_ref[...]) * (1.0 - lam_init)
        o_ref[pl.ds(pl.multiple_of(i * tq, tq), tq), :] = y.T
        reset_state()
        return carry

    start_tile(0)
    scores(0, s_refs[0])
    scores(1, s_refs[1])
    reset_state()
    lax.fori_loop(0, nq, q_tile, 0)


def _attn_a_prompt(proj, k_a, v_a, t5_bias, lq1, lk1, lq2, lk2, subln_g, lam_init):
    s = proj.shape[1]
    tq, tk = A_TQ, A_TK
    assert s % tq == 0 and tq == 2 * tk and tk % CHUNK == 0 and tk >= T5_MAX_DIST
    vec = lambda: pl.BlockSpec((1, DK_A), lambda h: (0, 0))
    head = lambda: pl.BlockSpec((s, HEAD_W), lambda h: (0, h))
    return pl.pallas_call(
        functools.partial(_attn_a_kernel, lam_init=lam_init),
        grid=(H_A,),
        in_specs=[
            pl.BlockSpec(memory_space=pltpu.SMEM),
            vec(), vec(), vec(), vec(),
            pl.BlockSpec((DV_A, 1), lambda h: (0, 0)),
            pl.BlockSpec((None, s, HEAD_W), lambda h: (QA, 0, h)), head(), head(),
        ],
        out_specs=pl.BlockSpec((s, HEAD_W), lambda h: (0, h)),
        out_shape=jax.ShapeDtypeStruct((s, H_A * DV_A), F32),
        scratch_shapes=[
            pltpu.VMEM((s, HEAD_W), BF16),
            pltpu.VMEM((s // tk, DV_A, tk), BF16),
            pltpu.VMEM((3, tk, tq), F32),
            pltpu.VMEM((HEAD_W, 2 * tq), BF16),
            pltpu.VMEM((1, 2 * tq), F32),
            pltpu.VMEM((1, 2 * tq), F32),
            pltpu.VMEM((DV_A, 2 * tq), F32),
        ] + [pltpu.VMEM((tk, 2 * tq), F32)] * 4,
        compiler_params=pltpu.CompilerParams(
            dimension_semantics=("arbitrary",), vmem_limit_bytes=VMEM_LIMIT),
        name="attn_a_prompt",
    )(t5_bias, lq1.reshape(1, DK_A), lk1.reshape(1, DK_A), lq2.reshape(1, DK_A), lk2.reshape(1, DK_A),
      subln_g.reshape(DV_A, 1), proj, k_a, v_a)


def _attn_b_kernel(rb_ref, q_ref, k_ref, v_ref, o_ref, kbf_ref, vt_ref, bias_ref, *s_refs):
    t = B_TQ
    blk = REL_CLIP
    nkt = BAND_PAST // t + 1
    nq = q_ref.shape[0] // t
    h = pl.program_id(0)

    _stage_keys_values(k_ref, v_ref, kbf_ref, vt_ref, t)

    kk = lax.broadcasted_iota(jnp.int32, (blk, blk), 0)
    qq = lax.broadcasted_iota(jnp.int32, (blk, blk), 1)
    rel = kk - qq
    lo = jnp.full((blk, blk), rb_ref[h, 0] * LOG2E, F32)
    same = _rel_bias_tile(rel, -(blk - 1), blk - 1, rb_ref, h) * LOG2E
    prev = _rel_bias_tile(rel - blk, -(2 * blk - 1), -1, rb_ref, h) * LOG2E
    ninf = jnp.full((blk, blk), -jnp.inf, F32)
    kc = kk // CHUNK
    qc = qq // CHUNK
    far_blocks = BAND_PAST // blk
    for a in range(nkt * t // blk):
        for b in range(t // blk):
            e = a - b
            if e < 0 or e > far_blocks:
                tile = ninf
            elif e == 0:
                tile = jnp.where(kc >= qc, lo, -jnp.inf)
            elif e == far_blocks:
                tile = jnp.where(kc <= qc, same, -jnp.inf)
            elif e == far_blocks - 1:
                tile = prev
            else:
                tile = lo
            bias_ref[a * blk:(a + 1) * blk, b * blk:(b + 1) * blk] = tile

    def scores(g, nk, s_ref):
        q0 = pl.multiple_of(g * t, t)
        k0 = pl.multiple_of((g - (nk - 1)) * t, t)
        qt = (q_ref[pl.ds(q0, t), :] * (DH_B ** -0.5 * LOG2E)).T.astype(BF16)
        s_ref[(nkt - nk) * t:, :] = jnp.dot(kbf_ref[pl.ds(k0, nk * t), :], qt, preferred_element_type=F32)

    def finish(g, nk, s_ref):
        s = s_ref[(nkt - nk) * t:, :] + bias_ref[(nkt - nk) * t:, :]
        m = jnp.max(s, axis=0, keepdims=True)
        p = jnp.exp2(s - m)
        l = jnp.sum(p, axis=0, keepdims=True)
        pb = p.astype(BF16)
        o = jnp.dot(vt_ref[g - (nk - 1)], pb[:t], preferred_element_type=F32)
        for c in range(1, nk):
            o = o + jnp.dot(vt_ref[g - (nk - 1) + c], pb[c * t:(c + 1) * t], preferred_element_type=F32)
        o_ref[pl.ds(pl.multiple_of(g * t, t), t), :] = (o * (1.0 / l)).T

    assert nq % 4 == 0 and nq >= 8 and nkt <= 4

    def quad(g, static_start=False, lookahead=True):
        for n in range(4):
            if lookahead or n < 2:
                scores(g + n + 2, nkt, s_refs[(n + 2) % 4])
            finish(g + n, min(n + 1, nkt) if static_start else nkt, s_refs[n])

    def two_quads(u, c):
        quad(4 + 8 * u)
        quad(8 + 8 * u)
        return c

    scores(0, 1, s_refs[0])
    scores(1, min(2, nkt), s_refs[1])
    quad(0, static_start=True)
    quads = nq // 4 - 2
    lax.fori_loop(0, quads // 2, two_quads, 0)
    if quads % 2:
        quad(4 * quads)
    quad(nq - 4, lookahead=False)


def _attn_b_prompt(proj, rel_bias):
    s = proj.shape[1]
    t = B_TQ
    nkt = BAND_PAST // t + 1
    assert s % t == 0 and BAND_PAST % t == 0 and t % REL_CLIP == 0 and REL_CLIP % CHUNK == 0
    head = lambda c: pl.BlockSpec((None, s, HEAD_W), lambda h: (c, 0, h))
    return pl.pallas_call(
        _attn_b_kernel,
        grid=(H_B,),
        in_specs=[pl.BlockSpec(memory_space=pltpu.SMEM), head(QB), head(KB), head(VB)],
        out_specs=pl.BlockSpec((s, HEAD_W), lambda h: (0, h)),
        out_shape=jax.ShapeDtypeStruct((s, H_B * DH_B), F32),
        scratch_shapes=[
            pltpu.VMEM((s, HEAD_W), BF16),
            pltpu.VMEM((s // t, DH_B, t), BF16),
            pltpu.VMEM((nkt * t, t), F32),
        ] + [pltpu.VMEM((nkt * t, t), F32)] * 4,
        compiler_params=pltpu.CompilerParams(
            dimension_semantics=("arbitrary",), vmem_limit_bytes=VMEM_LIMIT),
        name="attn_b_prompt",
    )(rel_bias, proj, proj, proj)


def _dot_nt(a, b):
    return lax.dot_general(a, b, (((1,), (1,)), ((), ())), preferred_element_type=F32)


def _attn_sample_kernel(t5_ref, rb_ref, lq1, lk1, lq2, lk2, g_ref,
                        qa_ref, ka_ref, va_ref, cka_ref, cva_ref,
                        qb_ref, kb_ref, vb_ref, ckb_ref, cvb_ref,
                        oa_ref, ob_ref, kroll_ref, vroll_ref,
                        ba_ref, bb_ref, *, lam_init, past, win):
    n = qa_ref.shape[0]
    near = NEAR
    nh = H_A

    @pl.when(pl.program_id(0) == 0)
    def _():
        qry = lax.broadcasted_iota(jnp.int32, (n, near + n), 0)
        key = lax.broadcasted_iota(jnp.int32, (n, near + n), 1)
        rel = key - near - qry
        for h in range(nh):
            ba_ref[h] = _t5_bias_tile(rel, -(near + n - 1), n - 1, t5_ref, h)
            bb_ref[h] = _rel_bias_tile(rel, -(near + n - 1), n - 1, rb_ref, h)

    kroll_ref[:(win - n) * nh, :] = ckb_ref[n * nh:, :]
    vroll_ref[:(win - n) * nh, :] = cvb_ref[n * nh:, :]

    lam = _lambda(lq1, lk1, lq2, lk2, lam_init)
    lane = lax.broadcasted_iota(jnp.int32, (n, HEAD_W), 1)

    def cols(h):
        return slice(h * HEAD_W, (h + 1) * HEAD_W)

    def scores_a(h):
        q = qa_ref[:, cols(h)] * (DK_A ** -0.5)
        q2 = jnp.concatenate([jnp.where(lane < DK_A, q, 0.0), jnp.where(lane >= DK_A, q, 0.0)],
                             axis=0).astype(BF16)
        ba = ba_ref[h]
        ba2 = jnp.concatenate([ba, ba], axis=0)
        s_c = _dot_nt(q2, cka_ref[:, cols(h)].astype(BF16))
        s_c = jnp.concatenate([s_c[:, :past - near], s_c[:, past - near:] + ba2[:, :near]], axis=1)
        s_n = _dot_nt(q2, ka_ref[:, cols(h)].astype(BF16)) + ba2[:, near:]
        return s_c, s_n

    def scores_b(h):
        qb = qb_ref[:, cols(h)].astype(BF16)
        bb = bb_ref[h]
        ckb = ckb_ref[pl.ds(h, win, stride=nh), :].astype(BF16)
        s_c = _dot_nt(qb, ckb) * (DH_B ** -0.5)
        s_c = jnp.concatenate([s_c[:, :win - near] + rb_ref[h, 0], s_c[:, win - near:] + bb[:, :near]], axis=1)
        s_n = _dot_nt(qb, kb_ref[:, cols(h)].astype(BF16)) * (DH_B ** -0.5) + bb[:, near:]
        return s_c, s_n

    def softmax(s_c, s_n):
        m = jnp.maximum(jnp.max(s_c, axis=-1, keepdims=True), jnp.max(s_n, axis=-1, keepdims=True))
        p_c = jnp.exp(s_c - m)
        p_n = jnp.exp(s_n - m)
        l = jnp.sum(p_c, axis=-1, keepdims=True) + jnp.sum(p_n, axis=-1, keepdims=True)
        return p_c.astype(BF16), p_n.astype(BF16), 1.0 / l

    def finish_a(h, p_c, p_n, inv):
        cva = cva_ref[pl.ds(h, past, stride=nh), :].astype(BF16)
        o2 = (jnp.dot(p_c, cva, preferred_element_type=F32)
              + jnp.dot(p_n, va_ref[:, cols(h)].astype(BF16), preferred_element_type=F32)) * inv
        o = o2[:n] - lam * o2[n:]
        ms = jnp.mean(o * o, axis=-1, keepdims=True)
        oa_ref[:, cols(h)] = (o * lax.rsqrt(ms + EPS) * g_ref[...]) * (1.0 - lam_init)

    def finish_b(h, p_c, p_n, inv):
        cvb = cvb_ref[pl.ds(h, win, stride=nh), :].astype(BF16)
        ob_ref[:, cols(h)] = (jnp.dot(p_c, cvb, preferred_element_type=F32)
                              + jnp.dot(p_n, vb_ref[:, cols(h)].astype(BF16), preferred_element_type=F32)) * inv

    for first in range(0, nh, SAMPLE_HEAD_GROUP):
        heads = range(first, first + SAMPLE_HEAD_GROUP)
        for h in heads:
            new_rows = pl.ds((win - n) * nh + h, n, stride=nh)
            kroll_ref[new_rows, :] = kb_ref[:, cols(h)]
            vroll_ref[new_rows, :] = vb_ref[:, cols(h)]
        s_a = [scores_a(h) for h in heads]
        s_b = [scores_b(h) for h in heads]
        p_a = [softmax(*s) for s in s_a]
        p_b = [softmax(*s) for s in s_b]
        for h, p in zip(heads, p_a):
            finish_a(h, *p)
        for h, p in zip(heads, p_b):
            finish_b(h, *p)


def _attn_sample(proj, k_a, v_a, ck_a, cv_a, ck_b, cv_b, t5_bias, rel_bias, lq1, lk1, lq2, lk2, subln_g,
                 lam_init):
    nb, past = ck_a.shape[0], ck_a.shape[1]
    win = ck_b.shape[1]
    n = proj.shape[1] // nb
    near = NEAR
    wide = H_A * HEAD_W
    assert past % CHUNK == 0 and n <= CHUNK and win <= BAND_PAST and win <= past
    assert near % HEAD_W == 0 and near <= win and near <= past and n % 8 == 0 and T5_MAX_DIST <= REL_CLIP
    assert H_A == H_B and DV_A == HEAD_W and DH_B == HEAD_W and 2 * DK_A == HEAD_W
    cka = ck_a.reshape(nb, past, wide)
    cva = cv_a.reshape(nb, past * H_A, DV_A)
    ckb = ck_b.reshape(nb, win * H_B, DH_B)
    cvb = cv_b.reshape(nb, win * H_B, DH_B)
    vec = lambda: pl.BlockSpec((1, DK_A), lambda b: (0, 0))
    new = lambda c: pl.BlockSpec((None, n, wide), lambda b: (c, b, 0))
    seq = lambda rows, cols: pl.BlockSpec((None, rows, cols), lambda b: (b, 0, 0))
    out = pl.BlockSpec((n, wide), lambda b: (b, 0))
    return pl.pallas_call(
        functools.partial(_attn_sample_kernel, lam_init=lam_init, past=past, win=win),
        grid=(nb,),
        in_specs=[
            pl.BlockSpec(memory_space=pltpu.SMEM),
            pl.BlockSpec(memory_space=pltpu.SMEM),
            vec(), vec(), vec(), vec(),
            pl.BlockSpec((1, DV_A), lambda b: (0, 0)),
            new(QA), out, out, seq(past, wide), seq(past * H_A, HEAD_W),
            new(QB), new(KB), new(VB), seq(win * H_B, HEAD_W), seq(win * H_B, HEAD_W),
        ],
        out_specs=[out, out, seq(win * H_B, HEAD_W), seq(win * H_B, HEAD_W)],
        out_shape=[
            jax.ShapeDtypeStruct((nb * n, wide), F32),
            jax.ShapeDtypeStruct((nb * n, wide), F32),
            jax.ShapeDtypeStruct((nb, win * H_B, DH_B), F32),
            jax.ShapeDtypeStruct((nb, win * H_B, DH_B), F32),
        ],
        scratch_shapes=[pltpu.VMEM((H_A, n, near + n), F32), pltpu.VMEM((H_B, n, near + n), F32)],
        compiler_params=pltpu.CompilerParams(
            dimension_semantics=("arbitrary",), vmem_limit_bytes=VMEM_LIMIT),
        name="attn_sample",
    )(t5_bias, rel_bias, lq1.reshape(1, DK_A), lk1.reshape(1, DK_A), lq2.reshape(1, DK_A),
      lk2.reshape(1, DK_A), subln_g.reshape(1, DV_A),
      proj, k_a, v_a, cka, cva, proj, proj, proj, ckb, cvb)


def _merge_kernel(x_ref, oa_ref, ob_ref, za_ref, zb_ref, ga0_ref, ga1_ref, gb0_ref, gb1_ref,
                  woa_ref, wob_ref, wout_ref, pg_ref, y_ref):
    za = za_ref[...]
    zb = zb_ref[...]
    a = (oa_ref[...] * (za * _sigmoid(za))).astype(BF16)
    b = (ob_ref[...] * (zb * _sigmoid(zb))).astype(BF16)
    ya = jnp.dot(a, woa_ref[...], preferred_element_type=F32)
    yb = jnp.dot(b, wob_ref[...], preferred_element_type=F32)
    ga = jnp.concatenate([ga0_ref[...], ga1_ref[...]], axis=1)
    gb = jnp.concatenate([gb0_ref[...], gb1_ref[...]], axis=1)
    mix = (_sigmoid(ga) * ya + _sigmoid(gb) * yb).astype(BF16)
    y = jnp.dot(mix, wout_ref[...], preferred_element_type=F32)
    ms = jnp.mean(y * y, axis=-1, keepdims=True)
    y_ref[...] = x_ref[...] + y * lax.rsqrt(ms + EPS) * pg_ref[...]


def _merge(x2d, o_a, o_b, proj, woa, wob, wout, post_g, tm):
    m, d = x2d.shape
    wa = o_a.shape[1]
    wb = o_b.shape[1]
    assert m % tm == 0 and wa == COL_BLOCK and wb == COL_BLOCK and d == 2 * COL_BLOCK
    row = lambda w: pl.BlockSpec((tm, w), lambda i: (i, 0))
    col = lambda c: pl.BlockSpec((None, tm, COL_BLOCK), lambda i: (c, i, 0))
    resident = lambda r, c: pl.BlockSpec((r, c), lambda i: (0, 0), pipeline_mode=pl.Buffered(1))
    return pl.pallas_call(
        _merge_kernel,
        grid=(m // tm,),
        in_specs=[row(d), row(wa), row(wb), col(ZA), col(ZB), col(GA0), col(GA1), col(GB0), col(GB1),
                  resident(wa, d), resident(wb, d), resident(d, d), resident(1, d)],
        out_specs=row(d),
        out_shape=jax.ShapeDtypeStruct((m, d), F32),
        compiler_params=pltpu.CompilerParams(
            dimension_semantics=("arbitrary",), vmem_limit_bytes=VMEM_LIMIT),
        name="merge",
    )(x2d, o_a, o_b, proj, proj, proj, proj, proj, proj, woa, wob, wout, post_g.reshape(1, d))


def kernel(x_prompt, x_sample, cache_k_a, cache_v_a, cache_k_b, cache_v_b, t5_bias, pre_norm, post_norm,
           w_in, lambda_q1, lambda_k1, lambda_q2, lambda_k2, subln_a, rel_bias_b, w_o_a, w_o_b, w_out):
    depth = w_in.shape[0]
    bp, sp, d = x_prompt.shape
    bs, ss, _ = x_sample.shape
    assert bp == 1 and w_in.shape[2] == 12 * COL_BLOCK
    yp = x_prompt.reshape(sp, d)
    ys = x_sample.reshape(bs * ss, d)
    tail = min(BAND_PAST, sp)
    outs = [[] for _ in range(8)]
    for l in range(depth):
        lam_init = 0.8 - 0.6 * math.exp(-0.3 * l)
        w = w_in[l].astype(BF16)
        woa = w_o_a[l].astype(BF16)
        wob = w_o_b[l].astype(BF16)
        wout = w_out[l].astype(BF16)
        lam_args = (lambda_q1[l], lambda_k1[l], lambda_q2[l], lambda_k2[l], subln_a[l], lam_init)

        ps, ka_s, va_s = _in_proj(ys, pre_norm[l], w, tm=bs * ss)
        yp, ps = lax.optimization_barrier((yp, ps))
        pp, ka, va = _in_proj(yp, pre_norm[l], w, tm=1024)
        ob = _attn_b_prompt(pp, rel_bias_b[l])
        ck_a, ob = lax.optimization_barrier((cache_k_a[l], ob))
        oa = _attn_a_prompt(pp, ka, va, t5_bias, *lam_args)
        yp = _merge(yp, oa, ob, pp, woa, wob, wout, post_norm[l], tm=256)
        outs[0].append(ka.reshape(bp, sp, 2 * H_A, DK_A))
        outs[1].append(va.reshape(bp, sp, H_A, DV_A))
        outs[2].append(pp[KB, sp - tail:].reshape(bp, tail, H_B, DH_B))
        outs[3].append(pp[VB, sp - tail:].reshape(bp, tail, H_B, DH_B))

        oas, obs, kroll, vroll = _attn_sample(ps, ka_s, va_s, ck_a, cache_v_a[l], cache_k_b[l], cache_v_b[l],
                                              t5_bias, rel_bias_b[l], *lam_args)
        ys = _merge(ys, oas, obs, ps, woa, wob, wout, post_norm[l], tm=bs * ss)
        outs[4].append(ka_s.reshape(bs, ss, 2 * H_A, DK_A))
        outs[5].append(va_s.reshape(bs, ss, H_A, DV_A))
        outs[6].append(kroll.reshape(bs, -1, H_B, DH_B))
        outs[7].append(vroll.reshape(bs, -1, H_B, DH_B))
    return (yp.reshape(bp, sp, d), ys.reshape(bs, ss, d)) + tuple(jnp.stack(o) for o in outs)
```

```python
import functools
import math

import numpy as np
import jax
import jax.numpy as jnp
from jax import lax
from jax.experimental import pallas as pl
from jax.experimental.pallas import tpu as pltpu

F32 = jnp.float32
BF16 = jnp.bfloat16

CHUNK = 64
H_A = 8
DK_A = 64
DV_A = 2 * DK_A
H_B = 8
DH_B = 128
BAND_CHUNKS = 8
BAND_PAST = BAND_CHUNKS * CHUNK
REL_CLIP = 128
T5_BUCKETS = 32
T5_MAX_DIST = 128
EPS = 1e-6

HEAD_W = 128
COL_BLOCK = 1024
A_TQ, A_TK = 512, 256
B_TQ = 256
SAMPLE_HEAD_GROUP = 4
LOG2E = math.log2(math.e)
NEAR = max(REL_CLIP, T5_MAX_DIST)
VMEM_LIMIT = 60 * 1024 * 1024

QA, ZA, QB, KB, VB, ZB, GA0, GA1, GB0, GB1 = range(10)
COL_KA, COL_VA = 1, 2
COL_SLOT = (QA, None, None, ZA, QB, KB, VB, ZB, GA0, GA1, GB0, GB1)


def _t5_bucket_int(rel):
    half = T5_BUCKETS // 2
    max_exact = half // 2
    n = abs(rel)
    ret = half if rel > 0 else 0
    if n < max_exact:
        return ret + n
    assert (T5_MAX_DIST // max_exact) ** 2 == 2 ** (half - max_exact)
    j = 0
    while n * n >= (max_exact * max_exact) * 2 ** (j + 1):
        j += 1
    return ret + min(max_exact + j, half - 1)


def _t5_runs(lo, hi):
    runs = []
    for r in range(lo, hi + 1):
        b = _t5_bucket_int(r)
        if not runs or runs[-1][1] != b:
            runs.append((r, b))
    return runs


T5_FAR_BUCKET = _t5_bucket_int(-T5_MAX_DIST)
assert all(_t5_bucket_int(-n) == T5_FAR_BUCKET for n in range(T5_MAX_DIST, 4 * T5_MAX_DIST))


def _t5_bias_tile(rel, lo, hi, t5_ref, h):
    runs = _t5_runs(lo, hi)
    val = jnp.full(rel.shape, t5_ref[runs[0][1], h], F32)
    for start, b in runs[1:]:
        val = jnp.where(rel >= start, t5_ref[b, h], val)
    return val - t5_ref[T5_FAR_BUCKET, h]


def _rel_bias_tile(rel, lo, hi, rb_ref, h):
    lo = max(lo, -REL_CLIP)
    hi = min(hi, REL_CLIP)
    val = jnp.full(rel.shape, rb_ref[h, lo + REL_CLIP], F32)
    for d in range(lo + 1, hi + 1):
        val = jnp.where(rel >= d, rb_ref[h, d + REL_CLIP], val)
    return val


def _sigmoid(x):
    return 1.0 / (1.0 + jnp.exp(-x))


def _lambda(lq1, lk1, lq2, lk2, lam_init):
    a = jnp.sum(lq1[...] * lk1[...], axis=-1, keepdims=True)
    b = jnp.sum(lq2[...] * lk2[...], axis=-1, keepdims=True)
    return jnp.exp(a) - jnp.exp(b) + lam_init


def _in_proj_kernel(x_ref, g_ref, w_ref, o_ref, ka_ref, va_ref, h_ref):
    j = pl.program_id(1)

    @pl.when(j == 0)
    def _():
        x = x_ref[...]
        ms = jnp.mean(x * x, axis=-1, keepdims=True)
        h_ref[...] = (x * lax.rsqrt(ms + EPS) * g_ref[...]).astype(BF16)

    def project(ref):
        ref[...] = jnp.dot(h_ref[...], w_ref[...], preferred_element_type=F32)

    pl.when(j == COL_KA)(functools.partial(project, ka_ref))
    pl.when(j == COL_VA)(functools.partial(project, va_ref))
    pl.when(jnp.logical_and(j != COL_KA, j != COL_VA))(functools.partial(project, o_ref))


def _in_proj(x2d, pre_g, w_bf16, tm):
    m, d = x2d.shape
    n = w_bf16.shape[1]
    assert m % tm == 0 and n == len(COL_SLOT) * COL_BLOCK

    def own(col):
        return pl.BlockSpec((tm, COL_BLOCK), lambda i, j: (jnp.where(j >= col, i, jnp.maximum(i - 1, 0)), 0))

    def slab_index(i, j):
        slot = jnp.where(j <= COL_VA, 0, j - 2)
        return slot, i, 0

    assert COL_SLOT[0] == 0 and COL_SLOT[COL_VA + 1:] == tuple(range(1, len(COL_SLOT) - 2))
    return pl.pallas_call(
        _in_proj_kernel,
        grid=(m // tm, n // COL_BLOCK),
        in_specs=[
            pl.BlockSpec((tm, d), lambda i, j: (i, 0)),
            pl.BlockSpec((1, d), lambda i, j: (0, 0)),
            pl.BlockSpec((d, COL_BLOCK), lambda i, j: (0, j)),
        ],
        out_specs=[pl.BlockSpec((None, tm, COL_BLOCK), slab_index), own(COL_KA), own(COL_VA)],
        out_shape=[jax.ShapeDtypeStruct((len(COL_SLOT) - 2, m, COL_BLOCK), F32),
                   jax.ShapeDtypeStruct((m, COL_BLOCK), F32),
                   jax.ShapeDtypeStruct((m, COL_BLOCK), F32)],
        scratch_shapes=[pltpu.VMEM((tm, d), BF16)],
        compiler_params=pltpu.CompilerParams(
            dimension_semantics=("arbitrary", "arbitrary"), vmem_limit_bytes=VMEM_LIMIT),
        name="in_proj",
    )(x2d, pre_g.reshape(1, d), w_bf16)


def _in_proj_few_rows_kernel(x_ref, g_ref, w_ref, o_ref, ka_ref, va_ref, h_ref, *, per_step):
    j = pl.program_id(0)

    @pl.when(j == 0)
    def _():
        x = x_ref[...]
        ms = jnp.mean(x * x, axis=-1, keepdims=True)
        h_ref[...] = (x * lax.rsqrt(ms + EPS) * g_ref[...]).astype(BF16)

    def project(step):
        y = jnp.dot(h_ref[...], w_ref[...], preferred_element_type=F32)
        for c in range(per_step):
            col = step * per_step + c
            block = y[:, c * COL_BLOCK:(c + 1) * COL_BLOCK]
            if col == COL_KA:
                ka_ref[...] = block
            elif col == COL_VA:
                va_ref[...] = block
            else:
                o_ref[COL_SLOT[col]] = block

    for step in range(len(COL_SLOT) // per_step):
        pl.when(j == step)(functools.partial(project, step))


def _in_proj_few_rows(x2d, pre_g, w_bf16, per_step):
    m, d = x2d.shape
    n = w_bf16.shape[1]
    assert n == len(COL_SLOT) * COL_BLOCK and len(COL_SLOT) % per_step == 0
    slots = len(COL_SLOT) - 2
    return pl.pallas_call(
        functools.partial(_in_proj_few_rows_kernel, per_step=per_step),
        grid=(len(COL_SLOT) // per_step,),
        in_specs=[
            pl.BlockSpec((m, d), lambda j: (0, 0)),
            pl.BlockSpec((1, d), lambda j: (0, 0)),
            pl.BlockSpec((d, per_step * COL_BLOCK), lambda j: (0, j)),
        ],
        out_specs=[pl.BlockSpec((slots, m, COL_BLOCK), lambda j: (0, 0, 0)),
                   pl.BlockSpec((m, COL_BLOCK), lambda j: (0, 0)),
                   pl.BlockSpec((m, COL_BLOCK), lambda j: (0, 0))],
        out_shape=[jax.ShapeDtypeStruct((slots, m, COL_BLOCK), F32),
                   jax.ShapeDtypeStruct((m, COL_BLOCK), F32),
                   jax.ShapeDtypeStruct((m, COL_BLOCK), F32)],
        scratch_shapes=[pltpu.VMEM((m, d), BF16)],
        compiler_params=pltpu.CompilerParams(
            dimension_semantics=("arbitrary",), vmem_limit_bytes=VMEM_LIMIT),
        name="in_proj_few_rows",
    )(x2d, pre_g.reshape(1, d), w_bf16)


def _online_softmax_step(s, vt, m_ref, l_ref, acc_ref):
    m_old = m_ref[...]
    m_new = jnp.maximum(m_old, jnp.max(s, axis=0, keepdims=True))
    alpha = jnp.exp2(m_old - m_new)
    p = jnp.exp2(s - m_new)
    l_ref[...] = alpha * l_ref[...] + jnp.sum(p, axis=0, keepdims=True)
    acc_ref[...] = alpha * acc_ref[...] + jnp.dot(vt, p.astype(BF16), preferred_element_type=F32)
    m_ref[...] = m_new


def _stage_keys_values(k_ref, v_ref, kbf_ref, vt_ref, t):
    def body(j, c):
        r = pl.multiple_of(j * t, t)
        kbf_ref[pl.ds(r, t), :] = k_ref[pl.ds(r, t), :].astype(BF16)
        vt_ref[j] = v_ref[pl.ds(r, t), :].T.astype(BF16)
        return c
    lax.fori_loop(0, vt_ref.shape[0], body, 0, unroll=4)


def _attn_a_kernel(t5_ref, lq1, lk1, lq2, lk2, g_ref, q_ref, k_ref, v_ref, o_ref,
                   kbf_ref, vt_ref, bias_ref, qt_ref, m_ref, l_ref, acc_ref, *s_refs, lam_init):
    tq, tk = A_TQ, A_TK
    h = pl.program_id(0)
    nq = q_ref.shape[0] // tq

    _stage_keys_values(k_ref, v_ref, kbf_ref, vt_ref, tk)
    key = lax.broadcasted_iota(jnp.int32, (tk, tq), 0)
    qry = lax.broadcasted_iota(jnp.int32, (tk, tq), 1)
    for n in range(3):
        rel = key + (n - 1) * tk - qry
        lo, hi = (n - 1) * tk - (tq - 1), min(n * tk - 1, CHUNK - 1)
        b = _t5_bias_tile(jnp.minimum(rel, hi), lo, hi, t5_ref, h) * LOG2E
        if n >= 1:
            b = jnp.where((key + (n - 1) * tk) // CHUNK <= qry // CHUNK, b, -jnp.inf)
        bias_ref[n] = b

    lam = _lambda(lq1, lk1, lq2, lk2, lam_init)

    hq = tq // 2

    def start_tile(i):
        q0 = pl.multiple_of(i * tq, tq)
        qt = (q_ref[pl.ds(q0, tq), :] * (DK_A ** -0.5 * LOG2E)).T
        sub = lax.broadcasted_iota(jnp.int32, (HEAD_W, tq), 0)
        maps = (jnp.where(sub < DK_A, qt, 0.0).astype(BF16), jnp.where(sub >= DK_A, qt, 0.0).astype(BF16))
        for c in range(4):
            qt_ref[:, c * hq:(c + 1) * hq] = maps[c % 2][:, (c // 2) * hq:(c // 2 + 1) * hq]

    def reset_state():
        m_ref[...] = jnp.full(m_ref.shape, -jnp.inf, F32)
        l_ref[...] = jnp.zeros(l_ref.shape, F32)
        acc_ref[...] = jnp.zeros(acc_ref.shape, F32)

    def scores(j, s_ref, late_only=False):
        r = pl.multiple_of(j * tk, tk)
        c0 = tq if late_only else 0
        s_ref[:, c0:] = jnp.dot(kbf_ref[pl.ds(r, tk), :], qt_ref[:, c0:], preferred_element_type=F32)

    def update(j, s_ref, bias_idx, late_only=False):
        c0 = tq if late_only else 0
        s = s_ref[:, c0:]
        if bias_idx is not None:
            b = bias_ref[bias_idx]
            halves = [b[:, (c // 2) * hq:(c // 2 + 1) * hq] for c in range(c0 // hq, 4)]
            s = jnp.concatenate([s[:, n * hq:(n + 1) * hq] + bh for n, bh in enumerate(halves)], axis=1)
        _online_softmax_step(s, vt_ref[j], m_ref.at[:, c0:], l_ref.at[:, c0:], acc_ref.at[:, c0:])

    def q_tile(i, carry):
        odd = jnp.logical_and(i >= 2, i % 2 == 0)

        @pl.when(odd)
        def _():
            update(0, s_refs[0], None)
            scores(2, s_refs[0])
            update(1, s_refs[1], None)
            scores(3, s_refs[1])

        j0 = jnp.where(odd, 2, 0)

        def quad(j):
            for n in range(4):
                scores(j + n + 2, s_refs[(n + 2) % 4])
                update(j + n, s_refs[n], None)

        def two_quads(u, c):
            quad(j0 + 8 * u)
            quad(j0 + 8 * u + 4)
            return c
        quads = jnp.maximum(i - 1, 0) // 2
        lax.fori_loop(0, quads // 2, two_quads, 0)

        @pl.when(quads % 2 == 1)
        def _():
            quad(j0 + 4 * (quads - 1))

        nxt = jnp.minimum(i + 1, nq - 1)

        @pl.when(i >= 1)
        def _():
            j = 2 * i - 2
            scores(j + 2, s_refs[2])
            update(j, s_refs[0], None)
            scores(j + 3, s_refs[3], late_only=True)
            update(j + 1, s_refs[1], 0)
            start_tile(nxt)
            scores(0, s_refs[0])
            update(j + 2, s_refs[2], 1)
            scores(1, s_refs[1])
            update(j + 3, s_refs[3], 2, late_only=True)

        @pl.when(i == 0)
        def _():
            update(0, s_refs[0], 1)
            start_tile(nxt)
            scores(0, s_refs[0])
            update(1, s_refs[1], 2, late_only=True)
            scores(1, s_refs[1])

        inv = 1.0 / l_ref[...]
        acc = acc_ref[...]
        o = jnp.concatenate(
            [acc[:, c:c + hq] * inv[:, c:c + hq] - lam * (acc[:, c + hq:c + tq] * inv[:, c + hq:c + tq])
             for c in (0, tq)], axis=1)
        ms = jnp.mean(o * o, axis=0, keepdims=True)
        y = (o * lax.rsqrt(ms + EPS) * g_ref[...]) * (1.0 - lam_init)
        o_ref[pl.ds(pl.multiple_of(i * tq, tq), tq), :] = y.T
        reset_state()
        return carry

    start_tile(0)
    scores(0, s_refs[0])
    scores(1, s_refs[1])
    reset_state()
    lax.fori_loop(0, nq, q_tile, 0)


def _attn_a_prompt(proj, k_a, v_a, t5_bias, lq1, lk1, lq2, lk2, subln_g, lam_init):
    s = proj.shape[1]
    tq, tk = A_TQ, A_TK
    assert s % tq == 0 and tq == 2 * tk and tk % CHUNK == 0 and tk >= T5_MAX_DIST
    vec = lambda: pl.BlockSpec((1, DK_A), lambda h: (0, 0))
    head = lambda: pl.BlockSpec((s, HEAD_W), lambda h: (0, h))
    return pl.pallas_call(
        functools.partial(_attn_a_kernel, lam_init=lam_init),
        grid=(H_A,),
        in_specs=[
            pl.BlockSpec(memory_space=pltpu.SMEM),
            vec(), vec(), vec(), vec(),
            pl.BlockSpec((DV_A, 1), lambda h: (0, 0)),
            pl.BlockSpec((None, s, HEAD_W), lambda h: (QA, 0, h)), head(), head(),
        ],
        out_specs=pl.BlockSpec((s, HEAD_W), lambda h: (0, h)),
        out_shape=jax.ShapeDtypeStruct((s, H_A * DV_A), F32),
        scratch_shapes=[
            pltpu.VMEM((s, HEAD_W), BF16),
            pltpu.VMEM((s // tk, DV_A, tk), BF16),
            pltpu.VMEM((3, tk, tq), F32),
            pltpu.VMEM((HEAD_W, 2 * tq), BF16),
            pltpu.VMEM((1, 2 * tq), F32),
            pltpu.VMEM((1, 2 * tq), F32),
            pltpu.VMEM((DV_A, 2 * tq), F32),
        ] + [pltpu.VMEM((tk, 2 * tq), F32)] * 4,
        compiler_params=pltpu.CompilerParams(
            dimension_semantics=("arbitrary",), vmem_limit_bytes=VMEM_LIMIT),
        name="attn_a_prompt",
    )(t5_bias, lq1.reshape(1, DK_A), lk1.reshape(1, DK_A), lq2.reshape(1, DK_A), lk2.reshape(1, DK_A),
      subln_g.reshape(DV_A, 1), proj, k_a, v_a)


def _attn_b_kernel(rb_ref, q_ref, k_ref, v_ref, o_ref, kbf_ref, vt_ref, bias_ref, *s_refs):
    t = B_TQ
    blk = REL_CLIP
    nkt = BAND_PAST // t + 1
    nq = q_ref.shape[0] // t
    h = pl.program_id(0)

    _stage_keys_values(k_ref, v_ref, kbf_ref, vt_ref, t)

    kk = lax.broadcasted_iota(jnp.int32, (blk, blk), 0)
    qq = lax.broadcasted_iota(jnp.int32, (blk, blk), 1)
    rel = kk - qq
    lo = jnp.full((blk, blk), rb_ref[h, 0] * LOG2E, F32)
    same = _rel_bias_tile(rel, -(blk - 1), blk - 1, rb_ref, h) * LOG2E
    prev = _rel_bias_tile(rel - blk, -(2 * blk - 1), -1, rb_ref, h) * LOG2E
    ninf = jnp.full((blk, blk), -jnp.inf, F32)
    kc = kk // CHUNK
    qc = qq // CHUNK
    far_blocks = BAND_PAST // blk
    for a in range(nkt * t // blk):
        for b in range(t // blk):
            e = a - b
            if e < 0 or e > far_blocks:
                tile = ninf
            elif e == 0:
                tile = jnp.where(kc >= qc, lo, -jnp.inf)
            elif e == far_blocks:
                tile = jnp.where(kc <= qc, same, -jnp.inf)
            elif e == far_blocks - 1:
                tile = prev
            else:
                tile = lo
            bias_ref[a * blk:(a + 1) * blk, b * blk:(b + 1) * blk] = tile

    def scores(g, nk, s_ref):
        q0 = pl.multiple_of(g * t, t)
        k0 = pl.multiple_of((g - (nk - 1)) * t, t)
        qt = (q_ref[pl.ds(q0, t), :] * (DH_B ** -0.5 * LOG2E)).T.astype(BF16)
        s_ref[(nkt - nk) * t:, :] = jnp.dot(kbf_ref[pl.ds(k0, nk * t), :], qt, preferred_element_type=F32)

    def finish(g, nk, s_ref):
        s = s_ref[(nkt - nk) * t:, :] + bias_ref[(nkt - nk) * t:, :]
        m = jnp.max(s, axis=0, keepdims=True)
        p = jnp.exp2(s - m)
        l = jnp.sum(p, axis=0, keepdims=True)
        pb = p.astype(BF16)
        o = jnp.dot(vt_ref[g - (nk - 1)], pb[:t], preferred_element_type=F32)
        for c in range(1, nk):
            o = o + jnp.dot(vt_ref[g - (nk - 1) + c], pb[c * t:(c + 1) * t], preferred_element_type=F32)
        o_ref[pl.ds(pl.multiple_of(g * t, t), t), :] = (o * (1.0 / l)).T

    assert nq % 4 == 0 and nq >= 8 and nkt <= 4

    def quad(g, static_start=False, lookahead=True):
        for n in range(4):
            if lookahead or n < 2:
                scores(g + n + 2, nkt, s_refs[(n + 2) % 4])
            finish(g + n, min(n + 1, nkt) if static_start else nkt, s_refs[n])

    def two_quads(u, c):
        quad(4 + 8 * u)
        quad(8 + 8 * u)
        return c

    scores(0, 1, s_refs[0])
    scores(1, min(2, nkt), s_refs[1])
    quad(0, static_start=True)
    quads = nq // 4 - 2
    lax.fori_loop(0, quads // 2, two_quads, 0)
    if quads % 2:
        quad(4 * quads)
    quad(nq - 4, lookahead=False)


def _attn_b_prompt(proj, rel_bias):
    s = proj.shape[1]
    t = B_TQ
    nkt = BAND_PAST // t + 1
    assert s % t == 0 and BAND_PAST % t == 0 and t % REL_CLIP == 0 and REL_CLIP % CHUNK == 0
    head = lambda c: pl.BlockSpec((None, s, HEAD_W), lambda h: (c, 0, h))
    return pl.pallas_call(
        _attn_b_kernel,
        grid=(H_B,),
        in_specs=[pl.BlockSpec(memory_space=pltpu.SMEM), head(QB), head(KB), head(VB)],
        out_specs=pl.BlockSpec((s, HEAD_W), lambda h: (0, h)),
        out_shape=jax.ShapeDtypeStruct((s, H_B * DH_B), F32),
        scratch_shapes=[
            pltpu.VMEM((s, HEAD_W), BF16),
            pltpu.VMEM((s // t, DH_B, t), BF16),
            pltpu.VMEM((nkt * t, t), F32),
        ] + [pltpu.VMEM((nkt * t, t), F32)] * 4,
        compiler_params=pltpu.CompilerParams(
            dimension_semantics=("arbitrary",), vmem_limit_bytes=VMEM_LIMIT),
        name="attn_b_prompt",
    )(rel_bias, proj, proj, proj)


def _dot_nt(a, b):
    return lax.dot_general(a, b, (((1,), (1,)), ((), ())), preferred_element_type=F32)


def _attn_sample_kernel(t5_ref, rb_ref, lq1, lk1, lq2, lk2, g_ref,
                        qa_ref, ka_ref, va_ref, cka_ref, cva_ref,
                        qb_ref, kb_ref, vb_ref, ckb_ref, cvb_ref,
                        oa_ref, ob_ref, kroll_ref, vroll_ref,
                        ba_ref, bb_ref, *, lam_init, past, win):
    n = qa_ref.shape[0]
    near = NEAR
    nh = H_A

    @pl.when(pl.program_id(0) == 0)
    def _():
        qry = lax.broadcasted_iota(jnp.int32, (n, near + n), 0)
        key = lax.broadcasted_iota(jnp.int32, (n, near + n), 1)
        rel = key - near - qry
        for h in range(nh):
            ba_ref[h] = _t5_bias_tile(rel, -(near + n - 1), n - 1, t5_ref, h)
            bb_ref[h] = _rel_bias_tile(rel, -(near + n - 1), n - 1, rb_ref, h)

    kroll_ref[:(win - n) * nh, :] = ckb_ref[n * nh:, :]
    vroll_ref[:(win - n) * nh, :] = cvb_ref[n * nh:, :]

    lam = _lambda(lq1, lk1, lq2, lk2, lam_init)
    lane = lax.broadcasted_iota(jnp.int32, (n, HEAD_W), 1)

    def cols(h):
        return slice(h * HEAD_W, (h + 1) * HEAD_W)

    def scores_a(h):
        q = qa_ref[:, cols(h)] * (DK_A ** -0.5)
        q2 = jnp.concatenate([jnp.where(lane < DK_A, q, 0.0), jnp.where(lane >= DK_A, q, 0.0)],
                             axis=0).astype(BF16)
        ba = ba_ref[h]
        ba2 = jnp.concatenate([ba, ba], axis=0)
        s_c = _dot_nt(q2, cka_ref[:, cols(h)].astype(BF16))
        s_c = jnp.concatenate([s_c[:, :past - near], s_c[:, past - near:] + ba2[:, :near]], axis=1)
        s_n = _dot_nt(q2, ka_ref[:, cols(h)].astype(BF16)) + ba2[:, near:]
        return s_c, s_n

    def scores_b(h):
        qb = qb_ref[:, cols(h)].astype(BF16)
        bb = bb_ref[h]
        ckb = ckb_ref[pl.ds(h, win, stride=nh), :].astype(BF16)
        s_c = _dot_nt(qb, ckb) * (DH_B ** -0.5)
        s_c = jnp.concatenate([s_c[:, :win - near] + rb_ref[h, 0], s_c[:, win - near:] + bb[:, :near]], axis=1)
        s_n = _dot_nt(qb, kb_ref[:, cols(h)].astype(BF16)) * (DH_B ** -0.5) + bb[:, near:]
        return s_c, s_n

    def softmax(s_c, s_n):
        m = jnp.maximum(jnp.max(s_c, axis=-1, keepdims=True), jnp.max(s_n, axis=-1, keepdims=True))
        p_c = jnp.exp(s_c - m)
        p_n = jnp.exp(s_n - m)
        l = jnp.sum(p_c, axis=-1, keepdims=True) + jnp.sum(p_n, axis=-1, keepdims=True)
        return p_c.astype(BF16), p_n.astype(BF16), 1.0 / l

    def finish_a(h, p_c, p_n, inv):
        cva = cva_ref[pl.ds(h, past, stride=nh), :].astype(BF16)
        o2 = (jnp.dot(p_c, cva, preferred_element_type=F32)
              + jnp.dot(p_n, va_ref[:, cols(h)].astype(BF16), preferred_element_type=F32)) * inv
        o = o2[:n] - lam * o2[n:]
        ms = jnp.mean(o * o, axis=-1, keepdims=True)
        oa_ref[:, cols(h)] = (o * lax.rsqrt(ms + EPS) * g_ref[...]) * (1.0 - lam_init)

    def finish_b(h, p_c, p_n, inv):
        cvb = cvb_ref[pl.ds(h, win, stride=nh), :].astype(BF16)
        ob_ref[:, cols(h)] = (jnp.dot(p_c, cvb, preferred_element_type=F32)
                              + jnp.dot(p_n, vb_ref[:, cols(h)].astype(BF16), preferred_element_type=F32)) * inv

    for first in range(0, nh, SAMPLE_HEAD_GROUP):
        heads = range(first, first + SAMPLE_HEAD_GROUP)
        for h in heads:
            new_rows = pl.ds((win - n) * nh + h, n, stride=nh)
            kroll_ref[new_rows, :] = kb_ref[:, cols(h)]
            vroll_ref[new_rows, :] = vb_ref[:, cols(h)]
        s_a = [scores_a(h) for h in heads]
        s_b = [scores_b(h) for h in heads]
        p_a = [softmax(*s) for s in s_a]
        p_b = [softmax(*s) for s in s_b]
        for h, p in zip(heads, p_a):
            finish_a(h, *p)
        for h, p in zip(heads, p_b):
            finish_b(h, *p)


def _attn_sample(proj, k_a, v_a, ck_a, cv_a, ck_b, cv_b, t5_bias, rel_bias, lq1, lk1, lq2, lk2, subln_g,
                 lam_init):
    nb, past = ck_a.shape[0], ck_a.shape[1]
    win = ck_b.shape[1]
    n = proj.shape[1] // nb
    near = NEAR
    wide = H_A * HEAD_W
    assert past % CHUNK == 0 and n <= CHUNK and win <= BAND_PAST and win <= past
    assert near % HEAD_W == 0 and near <= win and near <= past and n % 8 == 0 and T5_MAX_DIST <= REL_CLIP
    assert H_A == H_B and DV_A == HEAD_W and DH_B == HEAD_W and 2 * DK_A == HEAD_W
    cka = ck_a.reshape(nb, past, wide)
    cva = cv_a.reshape(nb, past * H_A, DV_A)
    ckb = ck_b.reshape(nb, win * H_B, DH_B)
    cvb = cv_b.reshape(nb, win * H_B, DH_B)
    vec = lambda: pl.BlockSpec((1, DK_A), lambda b: (0, 0))
    new = lambda c: pl.BlockSpec((None, n, wide), lambda b: (c, b, 0))
    seq = lambda rows, cols: pl.BlockSpec((None, rows, cols), lambda b: (b, 0, 0))
    out = pl.BlockSpec((n, wide), lambda b: (b, 0))
    return pl.pallas_call(
        functools.partial(_attn_sample_kernel, lam_init=lam_init, past=past, win=win),
        grid=(nb,),
        in_specs=[
            pl.BlockSpec(memory_space=pltpu.SMEM),
            pl.BlockSpec(memory_space=pltpu.SMEM),
            vec(), vec(), vec(), vec(),
            pl.BlockSpec((1, DV_A), lambda b: (0, 0)),
            new(QA), out, out, seq(past, wide), seq(past * H_A, HEAD_W),
            new(QB), new(KB), new(VB), seq(win * H_B, HEAD_W), seq(win * H_B, HEAD_W),
        ],
        out_specs=[out, out, seq(win * H_B, HEAD_W), seq(win * H_B, HEAD_W)],
        out_shape=[
            jax.ShapeDtypeStruct((nb * n, wide), F32),
            jax.ShapeDtypeStruct((nb * n, wide), F32),
            jax.ShapeDtypeStruct((nb, win * H_B, DH_B), F32),
            jax.ShapeDtypeStruct((nb, win * H_B, DH_B), F32),
        ],
        scratch_shapes=[pltpu.VMEM((H_A, n, near + n), F32), pltpu.VMEM((H_B, n, near + n), F32)],
        compiler_params=pltpu.CompilerParams(
            dimension_semantics=("arbitrary",), vmem_limit_bytes=VMEM_LIMIT),
        name="attn_sample",
    )(t5_bias, rel_bias, lq1.reshape(1, DK_A), lk1.reshape(1, DK_A), lq2.reshape(1, DK_A),
      lk2.reshape(1, DK_A), subln_g.reshape(1, DV_A),
      proj, k_a, v_a, cka, cva, proj, proj, proj, ckb, cvb)


def _merge_kernel(x_ref, oa_ref, ob_ref, za_ref, zb_ref, ga0_ref, ga1_ref, gb0_ref, gb1_ref,
                  woa_ref, wob_ref, wout_ref, pg_ref, y_ref):
    za = za_ref[...]
    zb = zb_ref[...]
    a = (oa_ref[...] * (za * _sigmoid(za))).astype(BF16)
    b = (ob_ref[...] * (zb * _sigmoid(zb))).astype(BF16)
    ya = jnp.dot(a, woa_ref[...], preferred_element_type=F32)
    yb = jnp.dot(b, wob_ref[...], preferred_element_type=F32)
    ga = jnp.concatenate([ga0_ref[...], ga1_ref[...]], axis=1)
    gb = jnp.concatenate([gb0_ref[...], gb1_ref[...]], axis=1)
    mix = (_sigmoid(ga) * ya + _sigmoid(gb) * yb).astype(BF16)
    y = jnp.dot(mix, wout_ref[...], preferred_element_type=F32)
    ms = jnp.mean(y * y, axis=-1, keepdims=True)
    y_ref[...] = x_ref[...] + y * lax.rsqrt(ms + EPS) * pg_ref[...]


def _merge(x2d, o_a, o_b, proj, woa, wob, wout, post_g, tm):
    m, d = x2d.shape
    wa = o_a.shape[1]
    wb = o_b.shape[1]
    assert m % tm == 0 and wa == COL_BLOCK and wb == COL_BLOCK and d == 2 * COL_BLOCK
    row = lambda w: pl.BlockSpec((tm, w), lambda i: (i, 0))
    col = lambda c: pl.BlockSpec((None, tm, COL_BLOCK), lambda i: (c, i, 0))
    resident = lambda r, c: pl.BlockSpec((r, c), lambda i: (0, 0), pipeline_mode=pl.Buffered(1))
    return pl.pallas_call(
        _merge_kernel,
        grid=(m // tm,),
        in_specs=[row(d), row(wa), row(wb), col(ZA), col(ZB), col(GA0), col(GA1), col(GB0), col(GB1),
                  resident(wa, d), resident(wb, d), resident(d, d), resident(1, d)],
        out_specs=row(d),
        out_shape=jax.ShapeDtypeStruct((m, d), F32),
        compiler_params=pltpu.CompilerParams(
            dimension_semantics=("arbitrary",), vmem_limit_bytes=VMEM_LIMIT),
        name="merge",
    )(x2d, o_a, o_b, proj, proj, proj, proj, proj, proj, woa, wob, wout, post_g.reshape(1, d))


def kernel(x_prompt, x_sample, cache_k_a, cache_v_a, cache_k_b, cache_v_b, t5_bias, pre_norm, post_norm,
           w_in, lambda_q1, lambda_k1, lambda_q2, lambda_k2, subln_a, rel_bias_b, w_o_a, w_o_b, w_out):
    depth = w_in.shape[0]
    bp, sp, d = x_prompt.shape
    bs, ss, _ = x_sample.shape
    assert bp == 1 and w_in.shape[2] == 12 * COL_BLOCK
    yp = x_prompt.reshape(sp, d)
    ys = x_sample.reshape(bs * ss, d)
    tail = min(BAND_PAST, sp)
    outs = [[] for _ in range(8)]
    for l in range(depth):
        lam_init = 0.8 - 0.6 * math.exp(-0.3 * l)
        w = w_in[l].astype(BF16)
        woa = w_o_a[l].astype(BF16)
        wob = w_o_b[l].astype(BF16)
        wout = w_out[l].astype(BF16)
        lam_args = (lambda_q1[l], lambda_k1[l], lambda_q2[l], lambda_k2[l], subln_a[l], lam_init)

        ps, ka_s, va_s = _in_proj_few_rows(ys, pre_norm[l], w, per_step=3)
        yp, ps = lax.optimization_barrier((yp, ps))
        pp, ka, va = _in_proj(yp, pre_norm[l], w, tm=1024)
        ob = _attn_b_prompt(pp, rel_bias_b[l])
        ck_a, ob = lax.optimization_barrier((cache_k_a[l], ob))
        oa = _attn_a_prompt(pp, ka, va, t5_bias, *lam_args)
        yp = _merge(yp, oa, ob, pp, woa, wob, wout, post_norm[l], tm=256)
        outs[0].append(ka.reshape(bp, sp, 2 * H_A, DK_A))
        outs[1].append(va.reshape(bp, sp, H_A, DV_A))
        outs[2].append(pp[KB, sp - tail:].reshape(bp, tail, H_B, DH_B))
        outs[3].append(pp[VB, sp - tail:].reshape(bp, tail, H_B, DH_B))

        oas, obs, kroll, vroll = _attn_sample(ps, ka_s, va_s, ck_a, cache_v_a[l], cache_k_b[l], cache_v_b[l],
                                              t5_bias, rel_bias_b[l], *lam_args)
        ys = _merge(ys, oas, obs, ps, woa, wob, wout, post_norm[l], tm=bs * ss)
        outs[4].append(ka_s.reshape(bs, ss, 2 * H_A, DK_A))
        outs[5].append(va_s.reshape(bs, ss, H_A, DV_A))
        outs[6].append(kroll.reshape(bs, -1, H_B, DH_B))
        outs[7].append(vroll.reshape(bs, -1, H_B, DH_B))
    return (yp.reshape(bp, sp, d), ys.reshape(bs, ss, d)) + tuple(jnp.stack(o) for o in outs)
```

```python
import functools
import math

import jax
import jax.numpy as jnp
from jax import lax
from jax.experimental import pallas as pl
from jax.experimental.pallas import tpu as pltpu

F32 = jnp.float32
BF16 = jnp.bfloat16

CHUNK = 64
H_A = 8
DK_A = 64
DV_A = 2 * DK_A
H_B = 8
DH_B = 128
BAND_CHUNKS = 8
BAND_PAST = BAND_CHUNKS * CHUNK
REL_CLIP = 128
T5_BUCKETS = 32
T5_MAX_DIST = 128
EPS = 1e-6

HEAD_W = 128
COL_BLOCK = 1024
A_TQ, A_TK = 512, 256
B_TQ = 256
SAMPLE_HEAD_GROUP = 4
LOG2E = math.log2(math.e)
NEAR = max(REL_CLIP, T5_MAX_DIST)
VMEM_LIMIT = 60 * 1024 * 1024

QA, ZA, QB, KB, VB, ZB, GA0, GA1, GB0, GB1 = range(10)
COL_KA, COL_VA = 1, 2
COL_SLOT = (QA, None, None, ZA, QB, KB, VB, ZB, GA0, GA1, GB0, GB1)


def _t5_bucket_int(rel):
    half = T5_BUCKETS // 2
    max_exact = half // 2
    n = abs(rel)
    ret = half if rel > 0 else 0
    if n < max_exact:
        return ret + n
    assert (T5_MAX_DIST // max_exact) ** 2 == 2 ** (half - max_exact)
    j = 0
    while n * n >= (max_exact * max_exact) * 2 ** (j + 1):
        j += 1
    return ret + min(max_exact + j, half - 1)


def _t5_runs(lo, hi):
    runs = []
    for r in range(lo, hi + 1):
        b = _t5_bucket_int(r)
        if not runs or runs[-1][1] != b:
            runs.append((r, b))
    return runs


T5_FAR_BUCKET = _t5_bucket_int(-T5_MAX_DIST)
assert all(_t5_bucket_int(-n) == T5_FAR_BUCKET for n in range(T5_MAX_DIST, 4 * T5_MAX_DIST))


def _t5_bias_tile(rel, lo, hi, t5_ref, h):
    runs = _t5_runs(lo, hi)
    val = jnp.full(rel.shape, t5_ref[runs[0][1], h], F32)
    for start, b in runs[1:]:
        val = jnp.where(rel >= start, t5_ref[b, h], val)
    return val - t5_ref[T5_FAR_BUCKET, h]


def _rel_bias_tile(rel, lo, hi, rb_ref, h):
    lo = max(lo, -REL_CLIP)
    hi = min(hi, REL_CLIP)
    val = jnp.full(rel.shape, rb_ref[h, lo + REL_CLIP], F32)
    for d in range(lo + 1, hi + 1):
        val = jnp.where(rel >= d, rb_ref[h, d + REL_CLIP], val)
    return val


def _sigmoid(x):
    return 1.0 / (1.0 + jnp.exp(-x))


def _lambda(lq1, lk1, lq2, lk2, lam_init):
    a = jnp.sum(lq1[...] * lk1[...], axis=-1, keepdims=True)
    b = jnp.sum(lq2[...] * lk2[...], axis=-1, keepdims=True)
    return jnp.exp(a) - jnp.exp(b) + lam_init


def _in_proj_kernel(x_ref, g_ref, w_ref, o_ref, ka_ref, va_ref, h_ref):
    j = pl.program_id(1)

    @pl.when(j == 0)
    def _():
        x = x_ref[...]
        ms = jnp.mean(x * x, axis=-1, keepdims=True)
        h_ref[...] = (x * lax.rsqrt(ms + EPS) * g_ref[...]).astype(BF16)

    def project(ref):
        ref[...] = jnp.dot(h_ref[...], w_ref[...], preferred_element_type=F32)

    pl.when(j == COL_KA)(functools.partial(project, ka_ref))
    pl.when(j == COL_VA)(functools.partial(project, va_ref))
    pl.when(jnp.logical_and(j != COL_KA, j != COL_VA))(functools.partial(project, o_ref))


def _in_proj(x2d, pre_g, w_bf16, tm):
    m, d = x2d.shape
    n = w_bf16.shape[1]
    assert m % tm == 0 and n == len(COL_SLOT) * COL_BLOCK

    def own(col):
        return pl.BlockSpec((tm, COL_BLOCK), lambda i, j: (jnp.where(j >= col, i, jnp.maximum(i - 1, 0)), 0))

    def slab_index(i, j):
        slot = jnp.where(j <= COL_VA, 0, j - 2)
        return slot, i, 0

    assert COL_SLOT[0] == 0 and COL_SLOT[COL_VA + 1:] == tuple(range(1, len(COL_SLOT) - 2))
    return pl.pallas_call(
        _in_proj_kernel,
        grid=(m // tm, n // COL_BLOCK),
        in_specs=[
            pl.BlockSpec((tm, d), lambda i, j: (i, 0)),
            pl.BlockSpec((1, d), lambda i, j: (0, 0)),
            pl.BlockSpec((d, COL_BLOCK), lambda i, j: (0, j)),
        ],
        out_specs=[pl.BlockSpec((None, tm, COL_BLOCK), slab_index), own(COL_KA), own(COL_VA)],
        out_shape=[jax.ShapeDtypeStruct((len(COL_SLOT) - 2, m, COL_BLOCK), F32),
                   jax.ShapeDtypeStruct((m, COL_BLOCK), F32),
                   jax.ShapeDtypeStruct((m, COL_BLOCK), F32)],
        scratch_shapes=[pltpu.VMEM((tm, d), BF16)],
        compiler_params=pltpu.CompilerParams(
            dimension_semantics=("arbitrary", "arbitrary"), vmem_limit_bytes=VMEM_LIMIT),
        name="in_proj",
    )(x2d, pre_g.reshape(1, d), w_bf16)


def _in_proj_few_rows_kernel(x_ref, g_ref, w_ref, o_ref, ka_ref, va_ref, h_ref, *, per_step):
    j = pl.program_id(0)

    @pl.when(j == 0)
    def _():
        x = x_ref[...]
        ms = jnp.mean(x * x, axis=-1, keepdims=True)
        h_ref[...] = (x * lax.rsqrt(ms + EPS) * g_ref[...]).astype(BF16)

    def project(step):
        y = jnp.dot(h_ref[...], w_ref[...], preferred_element_type=F32)
        for c in range(per_step):
            col = step * per_step + c
            block = y[:, c * COL_BLOCK:(c + 1) * COL_BLOCK]
            if col == COL_KA:
                ka_ref[...] = block
            elif col == COL_VA:
                va_ref[...] = block
            else:
                o_ref[COL_SLOT[col]] = block

    for step in range(len(COL_SLOT) // per_step):
        pl.when(j == step)(functools.partial(project, step))


def _in_proj_few_rows(x2d, pre_g, w_bf16, per_step):
    m, d = x2d.shape
    n = w_bf16.shape[1]
    assert n == len(COL_SLOT) * COL_BLOCK and len(COL_SLOT) % per_step == 0
    slots = len(COL_SLOT) - 2
    return pl.pallas_call(
        functools.partial(_in_proj_few_rows_kernel, per_step=per_step),
        grid=(len(COL_SLOT) // per_step,),
        in_specs=[
            pl.BlockSpec((m, d), lambda j: (0, 0)),
            pl.BlockSpec((1, d), lambda j: (0, 0)),
            pl.BlockSpec((d, per_step * COL_BLOCK), lambda j: (0, j)),
        ],
        out_specs=[pl.BlockSpec((slots, m, COL_BLOCK), lambda j: (0, 0, 0)),
                   pl.BlockSpec((m, COL_BLOCK), lambda j: (0, 0)),
                   pl.BlockSpec((m, COL_BLOCK), lambda j: (0, 0))],
        out_shape=[jax.ShapeDtypeStruct((slots, m, COL_BLOCK), F32),
                   jax.ShapeDtypeStruct((m, COL_BLOCK), F32),
                   jax.ShapeDtypeStruct((m, COL_BLOCK), F32)],
        scratch_shapes=[pltpu.VMEM((m, d), BF16)],
        compiler_params=pltpu.CompilerParams(
            dimension_semantics=("arbitrary",), vmem_limit_bytes=VMEM_LIMIT),
        name="in_proj_few_rows",
    )(x2d, pre_g.reshape(1, d), w_bf16)


def _online_softmax_step(s, vt, m_ref, l_ref, acc_ref):
    m_old = m_ref[...]
    m_new = jnp.maximum(m_old, jnp.max(s, axis=0, keepdims=True))
    alpha = jnp.exp2(m_old - m_new)
    p = jnp.exp2(s - m_new)
    l_ref[...] = alpha * l_ref[...] + jnp.sum(p, axis=0, keepdims=True)
    acc_ref[...] = alpha * acc_ref[...] + jnp.dot(vt, p.astype(BF16), preferred_element_type=F32)
    m_ref[...] = m_new


def _stage_keys_values(k_ref, v_ref, kbf_ref, vt_ref, t):
    def body(j, c):
        r = pl.multiple_of(j * t, t)
        kbf_ref[pl.ds(r, t), :] = k_ref[pl.ds(r, t), :].astype(BF16)
        vt_ref[j] = v_ref[pl.ds(r, t), :].astype(BF16).T
        return c
    lax.fori_loop(0, vt_ref.shape[0], body, 0, unroll=4)


def _attn_a_kernel(t5_ref, lq1, lk1, lq2, lk2, g_ref, q_ref, k_ref, v_ref, o_ref,
                   kbf_ref, vt_ref, bias_ref, qt_ref, m_ref, l_ref, acc_ref, *s_refs, lam_init):
    tq, tk = A_TQ, A_TK
    h = pl.program_id(0)
    nq = q_ref.shape[0] // tq

    _stage_keys_values(k_ref, v_ref, kbf_ref, vt_ref, tk)
    key = lax.broadcasted_iota(jnp.int32, (tk, tq), 0)
    qry = lax.broadcasted_iota(jnp.int32, (tk, tq), 1)
    for n in range(3):
        rel = key + (n - 1) * tk - qry
        lo, hi = (n - 1) * tk - (tq - 1), min(n * tk - 1, CHUNK - 1)
        b = _t5_bias_tile(jnp.minimum(rel, hi), lo, hi, t5_ref, h) * LOG2E
        if n >= 1:
            b = jnp.where((key + (n - 1) * tk) // CHUNK <= qry // CHUNK, b, -jnp.inf)
        bias_ref[n] = b

    lam = _lambda(lq1, lk1, lq2, lk2, lam_init)

    hq = tq // 2

    def start_tile(i):
        q0 = pl.multiple_of(i * tq, tq)
        qt = (q_ref[pl.ds(q0, tq), :] * (DK_A ** -0.5 * LOG2E)).T
        sub = lax.broadcasted_iota(jnp.int32, (HEAD_W, tq), 0)
        maps = (jnp.where(sub < DK_A, qt, 0.0).astype(BF16), jnp.where(sub >= DK_A, qt, 0.0).astype(BF16))
        for c in range(4):
            qt_ref[:, c * hq:(c + 1) * hq] = maps[c % 2][:, (c // 2) * hq:(c // 2 + 1) * hq]

    def reset_state():
        m_ref[...] = jnp.full(m_ref.shape, -jnp.inf, F32)
        l_ref[...] = jnp.zeros(l_ref.shape, F32)
        acc_ref[...] = jnp.zeros(acc_ref.shape, F32)

    def scores(j, s_ref, late_only=False):
        r = pl.multiple_of(j * tk, tk)
        c0 = tq if late_only else 0
        s_ref[:, c0:] = jnp.dot(kbf_ref[pl.ds(r, tk), :], qt_ref[:, c0:], preferred_element_type=F32)

    def update(j, s_ref, bias_idx, late_only=False):
        c0 = tq if late_only else 0
        s = s_ref[:, c0:]
        if bias_idx is not None:
            b = bias_ref[bias_idx]
            halves = [b[:, (c // 2) * hq:(c // 2 + 1) * hq] for c in range(c0 // hq, 4)]
            s = jnp.concatenate([s[:, n * hq:(n + 1) * hq] + bh for n, bh in enumerate(halves)], axis=1)
        _online_softmax_step(s, vt_ref[j], m_ref.at[:, c0:], l_ref.at[:, c0:], acc_ref.at[:, c0:])

    def q_tile(i, carry):
        odd = jnp.logical_and(i >= 2, i % 2 == 0)

        @pl.when(odd)
        def _():
            update(0, s_refs[0], None)
            scores(2, s_refs[0])
            update(1, s_refs[1], None)
            scores(3, s_refs[1])

        j0 = jnp.where(odd, 2, 0)

        def quad(j):
            for n in range(4):
                scores(j + n + 2, s_refs[(n + 2) % 4])
                update(j + n, s_refs[n], None)

        def two_quads(u, c):
            quad(j0 + 8 * u)
            quad(j0 + 8 * u + 4)
            return c
        quads = jnp.maximum(i - 1, 0) // 2
        lax.fori_loop(0, quads // 2, two_quads, 0)

        @pl.when(quads % 2 == 1)
        def _():
            quad(j0 + 4 * (quads - 1))

        nxt = jnp.minimum(i + 1, nq - 1)

        @pl.when(i >= 1)
        def _():
            j = 2 * i - 2
            scores(j + 2, s_refs[2])
            update(j, s_refs[0], None)
            scores(j + 3, s_refs[3], late_only=True)
            update(j + 1, s_refs[1], 0)
            start_tile(nxt)
            scores(0, s_refs[0])
            update(j + 2, s_refs[2], 1)
            scores(1, s_refs[1])
            update(j + 3, s_refs[3], 2, late_only=True)

        @pl.when(i == 0)
        def _():
            update(0, s_refs[0], 1)
            start_tile(nxt)
            scores(0, s_refs[0])
            update(1, s_refs[1], 2, late_only=True)
            scores(1, s_refs[1])

        inv = 1.0 / l_ref[...]
        acc = acc_ref[...]
        o = jnp.concatenate(
            [acc[:, c:c + hq] * inv[:, c:c + hq] - lam * (acc[:, c + hq:c + tq] * inv[:, c + hq:c + tq])
             for c in (0, tq)], axis=1)
        ms = jnp.mean(o * o, axis=0, keepdims=True)
        y = (o * lax.rsqrt(ms + EPS) * g_ref[...]) * (1.0 - lam_init)
        o_ref[pl.ds(pl.multiple_of(i * tq, tq), tq), :] = y.T
        reset_state()
        return carry

    start_tile(0)
    scores(0, s_refs[0])
    scores(1, s_refs[1])
    reset_state()
    lax.fori_loop(0, nq, q_tile, 0)


def _attn_a_prompt(proj, k_a, v_a, t5_bias, lq1, lk1, lq2, lk2, subln_g, lam_init):
    s = proj.shape[1]
    tq, tk = A_TQ, A_TK
    assert s % tq == 0 and tq == 2 * tk and tk % CHUNK == 0 and tk >= T5_MAX_DIST
    vec = lambda: pl.BlockSpec((1, DK_A), lambda h: (0, 0))
    head = lambda: pl.BlockSpec((s, HEAD_W), lambda h: (0, h))
    return pl.pallas_call(
        functools.partial(_attn_a_kernel, lam_init=lam_init),
        grid=(H_A,),
        in_specs=[
            pl.BlockSpec(memory_space=pltpu.SMEM),
            vec(), vec(), vec(), vec(),
            pl.BlockSpec((DV_A, 1), lambda h: (0, 0)),
            pl.BlockSpec((None, s, HEAD_W), lambda h: (QA, 0, h)), head(), head(),
        ],
        out_specs=pl.BlockSpec((s, HEAD_W), lambda h: (0, h)),
        out_shape=jax.ShapeDtypeStruct((s, H_A * DV_A), F32),
        scratch_shapes=[
            pltpu.VMEM((s, HEAD_W), BF16),
            pltpu.VMEM((s // tk, DV_A, tk), BF16),
            pltpu.VMEM((3, tk, tq), F32),
            pltpu.VMEM((HEAD_W, 2 * tq), BF16),
            pltpu.VMEM((1, 2 * tq), F32),
            pltpu.VMEM((1, 2 * tq), F32),
            pltpu.VMEM((DV_A, 2 * tq), F32),
        ] + [pltpu.VMEM((tk, 2 * tq), F32)] * 4,
        compiler_params=pltpu.CompilerParams(
            dimension_semantics=("arbitrary",), vmem_limit_bytes=VMEM_LIMIT),
        name="attn_a_prompt",
    )(t5_bias, lq1.reshape(1, DK_A), lk1.reshape(1, DK_A), lq2.reshape(1, DK_A), lk2.reshape(1, DK_A),
      subln_g.reshape(DV_A, 1), proj, k_a, v_a)


def _attn_b_kernel(rb_ref, q_ref, k_ref, v_ref, o_ref, kbf_ref, vt_ref, bias_ref, *s_refs):
    t = B_TQ
    blk = REL_CLIP
    nkt = BAND_PAST // t + 1
    nq = q_ref.shape[0] // t
    h = pl.program_id(0)

    _stage_keys_values(k_ref, v_ref, kbf_ref, vt_ref, t)

    kk = lax.broadcasted_iota(jnp.int32, (blk, blk), 0)
    qq = lax.broadcasted_iota(jnp.int32, (blk, blk), 1)
    rel = kk - qq
    lo = jnp.full((blk, blk), rb_ref[h, 0] * LOG2E, F32)
    same = _rel_bias_tile(rel, -(blk - 1), blk - 1, rb_ref, h) * LOG2E
    prev = _rel_bias_tile(rel - blk, -(2 * blk - 1), -1, rb_ref, h) * LOG2E
    ninf = jnp.full((blk, blk), -jnp.inf, F32)
    kc = kk // CHUNK
    qc = qq // CHUNK
    far_blocks = BAND_PAST // blk
    for a in range(nkt * t // blk):
        for b in range(t // blk):
            e = a - b
            if e < 0 or e > far_blocks:
                tile = ninf
            elif e == 0:
                tile = jnp.where(kc >= qc, lo, -jnp.inf)
            elif e == far_blocks:
                tile = jnp.where(kc <= qc, same, -jnp.inf)
            elif e == far_blocks - 1:
                tile = prev
            else:
                tile = lo
            bias_ref[a * blk:(a + 1) * blk, b * blk:(b + 1) * blk] = tile

    def scores(g, nk, s_ref):
        q0 = pl.multiple_of(g * t, t)
        k0 = pl.multiple_of((g - (nk - 1)) * t, t)
        qt = (q_ref[pl.ds(q0, t), :] * (DH_B ** -0.5 * LOG2E)).T.astype(BF16)
        s_ref[(nkt - nk) * t:, :] = jnp.dot(kbf_ref[pl.ds(k0, nk * t), :], qt, preferred_element_type=F32)

    def finish(g, nk, s_ref):
        s = s_ref[(nkt - nk) * t:, :] + bias_ref[(nkt - nk) * t:, :]
        m = jnp.max(s, axis=0, keepdims=True)
        p = jnp.exp2(s - m)
        l = jnp.sum(p, axis=0, keepdims=True)
        pb = p.astype(BF16)
        o = jnp.dot(vt_ref[g - (nk - 1)], pb[:t], preferred_element_type=F32)
        for c in range(1, nk):
            o = o + jnp.dot(vt_ref[g - (nk - 1) + c], pb[c * t:(c + 1) * t], preferred_element_type=F32)
        o_ref[pl.ds(pl.multiple_of(g * t, t), t), :] = (o * (1.0 / l)).T

    assert nq % 4 == 0 and nq >= 8 and nkt <= 4

    def quad(g, static_start=False, lookahead=True):
        for n in range(4):
            if lookahead or n < 2:
                scores(g + n + 2, nkt, s_refs[(n + 2) % 4])
            finish(g + n, min(n + 1, nkt) if static_start else nkt, s_refs[n])

    def two_quads(u, c):
        quad(4 + 8 * u)
        quad(8 + 8 * u)
        return c

    scores(0, 1, s_refs[0])
    scores(1, min(2, nkt), s_refs[1])
    quad(0, static_start=True)
    quads = nq // 4 - 2
    lax.fori_loop(0, quads // 2, two_quads, 0)
    if quads % 2:
        quad(4 * quads)
    quad(nq - 4, lookahead=False)


def _attn_b_prompt(proj, rel_bias):
    s = proj.shape[1]
    t = B_TQ
    nkt = BAND_PAST // t + 1
    assert s % t == 0 and BAND_PAST % t == 0 and t % REL_CLIP == 0 and REL_CLIP % CHUNK == 0
    head = lambda c: pl.BlockSpec((None, s, HEAD_W), lambda h: (c, 0, h))
    return pl.pallas_call(
        _attn_b_kernel,
        grid=(H_B,),
        in_specs=[pl.BlockSpec(memory_space=pltpu.SMEM), head(QB), head(KB), head(VB)],
        out_specs=pl.BlockSpec((s, HEAD_W), lambda h: (0, h)),
        out_shape=jax.ShapeDtypeStruct((s, H_B * DH_B), F32),
        scratch_shapes=[
            pltpu.VMEM((s, HEAD_W), BF16),
            pltpu.VMEM((s // t, DH_B, t), BF16),
            pltpu.VMEM((nkt * t, t), F32),
        ] + [pltpu.VMEM((nkt * t, t), F32)] * 4,
        compiler_params=pltpu.CompilerParams(
            dimension_semantics=("arbitrary",), vmem_limit_bytes=VMEM_LIMIT),
        name="attn_b_prompt",
    )(rel_bias, proj, proj, proj)


def _dot_nt(a, b):
    return lax.dot_general(a, b, (((1,), (1,)), ((), ())), preferred_element_type=F32)


def _attn_sample_kernel(t5_ref, rb_ref, lq1, lk1, lq2, lk2, g_ref,
                        qa_ref, ka_ref, va_ref, cka_ref, cva_ref,
                        qb_ref, kb_ref, vb_ref, ckb_ref, cvb_ref,
                        oa_ref, ob_ref, kroll_ref, vroll_ref,
                        ba_ref, bb_ref, *, lam_init, past, win):
    n = qa_ref.shape[0]
    near = NEAR
    nh = H_A

    @pl.when(pl.program_id(0) == 0)
    def _():
        qry = lax.broadcasted_iota(jnp.int32, (n, near + n), 0)
        key = lax.broadcasted_iota(jnp.int32, (n, near + n), 1)
        rel = key - near - qry
        for h in range(nh):
            ba_ref[h] = _t5_bias_tile(rel, -(near + n - 1), n - 1, t5_ref, h)
            bb_ref[h] = _rel_bias_tile(rel, -(near + n - 1), n - 1, rb_ref, h)

    kroll_ref[:(win - n) * nh, :] = ckb_ref[n * nh:, :]
    vroll_ref[:(win - n) * nh, :] = cvb_ref[n * nh:, :]

    lam = _lambda(lq1, lk1, lq2, lk2, lam_init)
    lane = lax.broadcasted_iota(jnp.int32, (n, HEAD_W), 1)

    def cols(h):
        return slice(h * HEAD_W, (h + 1) * HEAD_W)

    def scores_a(h):
        q = qa_ref[:, cols(h)] * (DK_A ** -0.5)
        q2 = jnp.concatenate([jnp.where(lane < DK_A, q, 0.0), jnp.where(lane >= DK_A, q, 0.0)],
                             axis=0).astype(BF16)
        ba = ba_ref[h]
        ba2 = jnp.concatenate([ba, ba], axis=0)
        s_c = _dot_nt(q2, cka_ref[:, cols(h)].astype(BF16))
        s_c = jnp.concatenate([s_c[:, :past - near], s_c[:, past - near:] + ba2[:, :near]], axis=1)
        s_n = _dot_nt(q2, ka_ref[:, cols(h)].astype(BF16)) + ba2[:, near:]
        return s_c, s_n

    def scores_b(h):
        qb = qb_ref[:, cols(h)].astype(BF16)
        bb = bb_ref[h]
        ckb = ckb_ref[pl.ds(h, win, stride=nh), :].astype(BF16)
        s_c = _dot_nt(qb, ckb) * (DH_B ** -0.5)
        s_c = jnp.concatenate([s_c[:, :win - near] + rb_ref[h, 0], s_c[:, win - near:] + bb[:, :near]], axis=1)
        s_n = _dot_nt(qb, kb_ref[:, cols(h)].astype(BF16)) * (DH_B ** -0.5) + bb[:, near:]
        return s_c, s_n

    def softmax(s_c, s_n):
        m = jnp.maximum(jnp.max(s_c, axis=-1, keepdims=True), jnp.max(s_n, axis=-1, keepdims=True))
        p_c = jnp.exp(s_c - m)
        p_n = jnp.exp(s_n - m)
        l = jnp.sum(p_c, axis=-1, keepdims=True) + jnp.sum(p_n, axis=-1, keepdims=True)
        return p_c.astype(BF16), p_n.astype(BF16), 1.0 / l

    def finish_a(h, p_c, p_n, inv):
        cva = cva_ref[pl.ds(h, past, stride=nh), :].astype(BF16)
        o2 = (jnp.dot(p_c, cva, preferred_element_type=F32)
              + jnp.dot(p_n, va_ref[:, cols(h)].astype(BF16), preferred_element_type=F32)) * inv
        o = o2[:n] - lam * o2[n:]
        ms = jnp.mean(o * o, axis=-1, keepdims=True)
        oa_ref[:, cols(h)] = (o * lax.rsqrt(ms + EPS) * g_ref[...]) * (1.0 - lam_init)

    def finish_b(h, p_c, p_n, inv):
        cvb = cvb_ref[pl.ds(h, win, stride=nh), :].astype(BF16)
        ob_ref[:, cols(h)] = (jnp.dot(p_c, cvb, preferred_element_type=F32)
                              + jnp.dot(p_n, vb_ref[:, cols(h)].astype(BF16), preferred_element_type=F32)) * inv

    for first in range(0, nh, SAMPLE_HEAD_GROUP):
        heads = range(first, first + SAMPLE_HEAD_GROUP)
        for h in heads:
            new_rows = pl.ds((win - n) * nh + h, n, stride=nh)
            kroll_ref[new_rows, :] = kb_ref[:, cols(h)]
            vroll_ref[new_rows, :] = vb_ref[:, cols(h)]
        s_a = [scores_a(h) for h in heads]
        s_b = [scores_b(h) for h in heads]
        p_a = [softmax(*s) for s in s_a]
        p_b = [softmax(*s) for s in s_b]
        for h, p in zip(heads, p_a):
            finish_a(h, *p)
        for h, p in zip(heads, p_b):
            finish_b(h, *p)


def _attn_sample(proj, k_a, v_a, ck_a, cv_a, ck_b, cv_b, t5_bias, rel_bias, lq1, lk1, lq2, lk2, subln_g,
                 lam_init):
    nb, past = ck_a.shape[0], ck_a.shape[1]
    win = ck_b.shape[1]
    n = proj.shape[1] // nb
    near = NEAR
    wide = H_A * HEAD_W
    assert past % CHUNK == 0 and n <= CHUNK and win <= BAND_PAST and win <= past
    assert near % HEAD_W == 0 and near <= win and near <= past and n % 8 == 0 and T5_MAX_DIST <= REL_CLIP
    assert H_A == H_B and DV_A == HEAD_W and DH_B == HEAD_W and 2 * DK_A == HEAD_W
    cka = ck_a.reshape(nb, past, wide)
    cva = cv_a.reshape(nb, past * H_A, DV_A)
    ckb = ck_b.reshape(nb, win * H_B, DH_B)
    cvb = cv_b.reshape(nb, win * H_B, DH_B)
    vec = lambda: pl.BlockSpec((1, DK_A), lambda b: (0, 0))
    new = lambda c: pl.BlockSpec((None, n, wide), lambda b: (c, b, 0))
    seq = lambda rows, cols: pl.BlockSpec((None, rows, cols), lambda b: (b, 0, 0))
    out = pl.BlockSpec((n, wide), lambda b: (b, 0))
    return pl.pallas_call(
        functools.partial(_attn_sample_kernel, lam_init=lam_init, past=past, win=win),
        grid=(nb,),
        in_specs=[
            pl.BlockSpec(memory_space=pltpu.SMEM),
            pl.BlockSpec(memory_space=pltpu.SMEM),
            vec(), vec(), vec(), vec(),
            pl.BlockSpec((1, DV_A), lambda b: (0, 0)),
            new(QA), out, out, seq(past, wide), seq(past * H_A, HEAD_W),
            new(QB), new(KB), new(VB), seq(win * H_B, HEAD_W), seq(win * H_B, HEAD_W),
        ],
        out_specs=[out, out, seq(win * H_B, HEAD_W), seq(win * H_B, HEAD_W)],
        out_shape=[
            jax.ShapeDtypeStruct((nb * n, wide), F32),
            jax.ShapeDtypeStruct((nb * n, wide), F32),
            jax.ShapeDtypeStruct((nb, win * H_B, DH_B), F32),
            jax.ShapeDtypeStruct((nb, win * H_B, DH_B), F32),
        ],
        scratch_shapes=[pltpu.VMEM((H_A, n, near + n), F32), pltpu.VMEM((H_B, n, near + n), F32)],
        compiler_params=pltpu.CompilerParams(
            dimension_semantics=("arbitrary",), vmem_limit_bytes=VMEM_LIMIT),
        name="attn_sample",
    )(t5_bias, rel_bias, lq1.reshape(1, DK_A), lk1.reshape(1, DK_A), lq2.reshape(1, DK_A),
      lk2.reshape(1, DK_A), subln_g.reshape(1, DV_A),
      proj, k_a, v_a, cka, cva, proj, proj, proj, ckb, cvb)


def _merge_kernel(x_ref, oa_ref, ob_ref, za_ref, zb_ref, ga0_ref, ga1_ref, gb0_ref, gb1_ref,
                  woa_ref, wob_ref, wout_ref, pg_ref, y_ref):
    za = za_ref[...]
    zb = zb_ref[...]
    a = (oa_ref[...] * (za * _sigmoid(za))).astype(BF16)
    b = (ob_ref[...] * (zb * _sigmoid(zb))).astype(BF16)
    ya = jnp.dot(a, woa_ref[...], preferred_element_type=F32)
    yb = jnp.dot(b, wob_ref[...], preferred_element_type=F32)
    ga = jnp.concatenate([ga0_ref[...], ga1_ref[...]], axis=1)
    gb = jnp.concatenate([gb0_ref[...], gb1_ref[...]], axis=1)
    mix = (_sigmoid(ga) * ya + _sigmoid(gb) * yb).astype(BF16)
    y = jnp.dot(mix, wout_ref[...], preferred_element_type=F32)
    ms = jnp.mean(y * y, axis=-1, keepdims=True)
    y_ref[...] = x_ref[...] + y * lax.rsqrt(ms + EPS) * pg_ref[...]


def _merge(x2d, o_a, o_b, proj, woa, wob, wout, post_g, tm):
    m, d = x2d.shape
    wa = o_a.shape[1]
    wb = o_b.shape[1]
    assert m % tm == 0 and wa == COL_BLOCK and wb == COL_BLOCK and d == 2 * COL_BLOCK
    row = lambda w: pl.BlockSpec((tm, w), lambda i: (i, 0))
    col = lambda c: pl.BlockSpec((None, tm, COL_BLOCK), lambda i: (c, i, 0))
    resident = lambda r, c: pl.BlockSpec((r, c), lambda i: (0, 0), pipeline_mode=pl.Buffered(1))
    return pl.pallas_call(
        _merge_kernel,
        grid=(m // tm,),
        in_specs=[row(d), row(wa), row(wb), col(ZA), col(ZB), col(GA0), col(GA1), col(GB0), col(GB1),
                  resident(wa, d), resident(wb, d), resident(d, d), resident(1, d)],
        out_specs=row(d),
        out_shape=jax.ShapeDtypeStruct((m, d), F32),
        compiler_params=pltpu.CompilerParams(
            dimension_semantics=("arbitrary",), vmem_limit_bytes=VMEM_LIMIT),
        name="merge",
    )(x2d, o_a, o_b, proj, proj, proj, proj, proj, proj, woa, wob, wout, post_g.reshape(1, d))


def kernel(x_prompt, x_sample, cache_k_a, cache_v_a, cache_k_b, cache_v_b, t5_bias, pre_norm, post_norm,
           w_in, lambda_q1, lambda_k1, lambda_q2, lambda_k2, subln_a, rel_bias_b, w_o_a, w_o_b, w_out):
    depth = w_in.shape[0]
    bp, sp, d = x_prompt.shape
    bs, ss, _ = x_sample.shape
    assert bp == 1 and w_in.shape[2] == 12 * COL_BLOCK
    yp = x_prompt.reshape(sp, d)
    ys = x_sample.reshape(bs * ss, d)
    tail = min(BAND_PAST, sp)
    outs = [[] for _ in range(8)]
    for l in range(depth):
        lam_init = 0.8 - 0.6 * math.exp(-0.3 * l)
        w = w_in[l].astype(BF16)
        woa = w_o_a[l].astype(BF16)
        wob = w_o_b[l].astype(BF16)
        wout = w_out[l].astype(BF16)
        lam_args = (lambda_q1[l], lambda_k1[l], lambda_q2[l], lambda_k2[l], subln_a[l], lam_init)

        ps, ka_s, va_s = _in_proj_few_rows(ys, pre_norm[l], w, per_step=3)
        yp, ps = lax.optimization_barrier((yp, ps))
        pp, ka, va = _in_proj(yp, pre_norm[l], w, tm=1024)
        ob = _attn_b_prompt(pp, rel_bias_b[l])
        ck_a, ob = lax.optimization_barrier((cache_k_a[l], ob))
        oa = _attn_a_prompt(pp, ka, va, t5_bias, *lam_args)
        yp = _merge(yp, oa, ob, pp, woa, wob, wout, post_norm[l], tm=256)
        outs[0].append(ka.reshape(bp, sp, 2 * H_A, DK_A))
        outs[1].append(va.reshape(bp, sp, H_A, DV_A))
        outs[2].append(pp[KB, sp - tail:].reshape(bp, tail, H_B, DH_B))
        outs[3].append(pp[VB, sp - tail:].reshape(bp, tail, H_B, DH_B))

        oas, obs, kroll, vroll = _attn_sample(ps, ka_s, va_s, ck_a, cache_v_a[l], cache_k_b[l], cache_v_b[l],
                                              t5_bias, rel_bias_b[l], *lam_args)
        ys = _merge(ys, oas, obs, ps, woa, wob, wout, post_norm[l], tm=bs * ss)
        outs[4].append(ka_s.reshape(bs, ss, 2 * H_A, DK_A))
        outs[5].append(va_s.reshape(bs, ss, H_A, DV_A))
        outs[6].append(kroll.reshape(bs, -1, H_B, DH_B))
        outs[7].append(vroll.reshape(bs, -1, H_B, DH_B))
    return (yp.reshape(bp, sp, d), ys.reshape(bs, ss, d)) + tuple(jnp.stack(o) for o in outs)
```

```python
import functools
import math

import jax
import jax.numpy as jnp
from jax import lax
from jax.experimental import pallas as pl
from jax.experimental.pallas import tpu as pltpu

F32 = jnp.float32
BF16 = jnp.bfloat16

CHUNK = 64
H_A = 8
DK_A = 64
DV_A = 2 * DK_A
H_B = 8
DH_B = 128
BAND_CHUNKS = 8
BAND_PAST = BAND_CHUNKS * CHUNK
REL_CLIP = 128
T5_BUCKETS = 32
T5_MAX_DIST = 128
EPS = 1e-6

HEAD_W = 128
COL_BLOCK = 1024
A_TQ, A_TK = 512, 256
B_TQ = 256
SAMPLE_HEAD_GROUP = 4
LOG2E = math.log2(math.e)
NEAR = max(REL_CLIP, T5_MAX_DIST)
VMEM_LIMIT = 60 * 1024 * 1024

QA, ZA, QB, KB, VB, ZB, GA0, GA1, GB0, GB1 = range(10)
COL_KA, COL_VA = 1, 2
COL_SLOT = (QA, None, None, ZA, QB, KB, VB, ZB, GA0, GA1, GB0, GB1)


def _t5_bucket_int(rel):
    half = T5_BUCKETS // 2
    max_exact = half // 2
    n = abs(rel)
    ret = half if rel > 0 else 0
    if n < max_exact:
        return ret + n
    assert (T5_MAX_DIST // max_exact) ** 2 == 2 ** (half - max_exact)
    j = 0
    while n * n >= (max_exact * max_exact) * 2 ** (j + 1):
        j += 1
    return ret + min(max_exact + j, half - 1)


def _t5_runs(lo, hi):
    runs = []
    for r in range(lo, hi + 1):
        b = _t5_bucket_int(r)
        if not runs or runs[-1][1] != b:
            runs.append((r, b))
    return runs


T5_FAR_BUCKET = _t5_bucket_int(-T5_MAX_DIST)
assert all(_t5_bucket_int(-n) == T5_FAR_BUCKET for n in range(T5_MAX_DIST, 4 * T5_MAX_DIST))


def _t5_bias_tile(rel, lo, hi, t5_ref, h):
    runs = _t5_runs(lo, hi)
    val = jnp.full(rel.shape, t5_ref[runs[0][1], h], F32)
    for start, b in runs[1:]:
        val = jnp.where(rel >= start, t5_ref[b, h], val)
    return val - t5_ref[T5_FAR_BUCKET, h]


def _rel_bias_tile(rel, lo, hi, rb_ref, h):
    lo = max(lo, -REL_CLIP)
    hi = min(hi, REL_CLIP)
    val = jnp.full(rel.shape, rb_ref[h, lo + REL_CLIP], F32)
    for d in range(lo + 1, hi + 1):
        val = jnp.where(rel >= d, rb_ref[h, d + REL_CLIP], val)
    return val


def _sigmoid(x):
    return 1.0 / (1.0 + jnp.exp(-x))


def _lambda(lq1, lk1, lq2, lk2, lam_init):
    a = jnp.sum(lq1[...] * lk1[...], axis=-1, keepdims=True)
    b = jnp.sum(lq2[...] * lk2[...], axis=-1, keepdims=True)
    return jnp.exp(a) - jnp.exp(b) + lam_init


def _in_proj_kernel(x_ref, g_ref, w_ref, o_ref, ka_ref, va_ref, h_ref):
    j = pl.program_id(1)

    @pl.when(j == 0)
    def _():
        x = x_ref[...]
        ms = jnp.mean(x * x, axis=-1, keepdims=True)
        h_ref[...] = (x * lax.rsqrt(ms + EPS) * g_ref[...]).astype(BF16)

    def project(ref):
        ref[...] = jnp.dot(h_ref[...], w_ref[...], preferred_element_type=F32)

    pl.when(j == COL_KA)(functools.partial(project, ka_ref))
    pl.when(j == COL_VA)(functools.partial(project, va_ref))
    pl.when(jnp.logical_and(j != COL_KA, j != COL_VA))(functools.partial(project, o_ref))


def _in_proj(x2d, pre_g, w_bf16, tm):
    m, d = x2d.shape
    n = w_bf16.shape[1]
    assert m % tm == 0 and n == len(COL_SLOT) * COL_BLOCK

    def own(col):
        return pl.BlockSpec((tm, COL_BLOCK), lambda i, j: (jnp.where(j >= col, i, jnp.maximum(i - 1, 0)), 0))

    def slab_index(i, j):
        slot = jnp.where(j <= COL_VA, 0, j - 2)
        return slot, i, 0

    assert COL_SLOT[0] == 0 and COL_SLOT[COL_VA + 1:] == tuple(range(1, len(COL_SLOT) - 2))
    return pl.pallas_call(
        _in_proj_kernel,
        grid=(m // tm, n // COL_BLOCK),
        in_specs=[
            pl.BlockSpec((tm, d), lambda i, j: (i, 0)),
            pl.BlockSpec((1, d), lambda i, j: (0, 0)),
            pl.BlockSpec((d, COL_BLOCK), lambda i, j: (0, j)),
        ],
        out_specs=[pl.BlockSpec((None, tm, COL_BLOCK), slab_index), own(COL_KA), own(COL_VA)],
        out_shape=[jax.ShapeDtypeStruct((len(COL_SLOT) - 2, m, COL_BLOCK), F32),
                   jax.ShapeDtypeStruct((m, COL_BLOCK), F32),
                   jax.ShapeDtypeStruct((m, COL_BLOCK), F32)],
        scratch_shapes=[pltpu.VMEM((tm, d), BF16)],
        compiler_params=pltpu.CompilerParams(
            dimension_semantics=("arbitrary", "arbitrary"), vmem_limit_bytes=VMEM_LIMIT),
        name="in_proj",
    )(x2d, pre_g.reshape(1, d), w_bf16)


def _in_proj_cast_kernel(x_ref, g_ref, w_ref, o_ref, ka_ref, va_ref, wbf_ref, h_ref):
    j = pl.program_id(0)

    @pl.when(j == 0)
    def _():
        x = x_ref[...]
        ms = jnp.mean(x * x, axis=-1, keepdims=True)
        h_ref[...] = (x * lax.rsqrt(ms + EPS) * g_ref[...]).astype(BF16)

    def project(ref):
        w = w_ref[...].astype(BF16)
        wbf_ref[...] = w
        ref[...] = jnp.dot(h_ref[...], w, preferred_element_type=F32)

    pl.when(j == COL_KA)(functools.partial(project, ka_ref))
    pl.when(j == COL_VA)(functools.partial(project, va_ref))
    pl.when(jnp.logical_and(j != COL_KA, j != COL_VA))(functools.partial(project, o_ref))


def _in_proj_cast(x2d, pre_g, w_f32):
    m, d = x2d.shape
    n = w_f32.shape[1]
    assert n == len(COL_SLOT) * COL_BLOCK
    slots = len(COL_SLOT) - 2
    resident = lambda: pl.BlockSpec((m, COL_BLOCK), lambda j: (0, 0))
    return pl.pallas_call(
        _in_proj_cast_kernel,
        grid=(len(COL_SLOT),),
        in_specs=[
            pl.BlockSpec((m, d), lambda j: (0, 0)),
            pl.BlockSpec((1, d), lambda j: (0, 0)),
            pl.BlockSpec((d, COL_BLOCK), lambda j: (0, j)),
        ],
        out_specs=[pl.BlockSpec((None, m, COL_BLOCK), lambda j: (jnp.where(j <= COL_VA, 0, j - 2), 0, 0)),
                   resident(), resident(),
                   pl.BlockSpec((d, COL_BLOCK), lambda j: (0, j))],
        out_shape=[jax.ShapeDtypeStruct((slots, m, COL_BLOCK), F32),
                   jax.ShapeDtypeStruct((m, COL_BLOCK), F32),
                   jax.ShapeDtypeStruct((m, COL_BLOCK), F32),
                   jax.ShapeDtypeStruct((d, n), BF16)],
        scratch_shapes=[pltpu.VMEM((m, d), BF16)],
        compiler_params=pltpu.CompilerParams(
            dimension_semantics=("arbitrary",), vmem_limit_bytes=VMEM_LIMIT),
        name="in_proj_cast",
    )(x2d, pre_g.reshape(1, d), w_f32)


def _online_softmax_step(s, vt, m_ref, l_ref, acc_ref):
    m_old = m_ref[...]
    m_new = jnp.maximum(m_old, jnp.max(s, axis=0, keepdims=True))
    alpha = jnp.exp2(m_old - m_new)
    p = jnp.exp2(s - m_new)
    l_ref[...] = alpha * l_ref[...] + jnp.sum(p, axis=0, keepdims=True)
    acc_ref[...] = alpha * acc_ref[...] + jnp.dot(vt, p.astype(BF16), preferred_element_type=F32)
    m_ref[...] = m_new


def _stage_keys_values(k_ref, v_ref, kbf_ref, vt_ref, t):
    def body(j, c):
        r = pl.multiple_of(j * t, t)
        kbf_ref[pl.ds(r, t), :] = k_ref[pl.ds(r, t), :].astype(BF16)
        vt_ref[j] = v_ref[pl.ds(r, t), :].astype(BF16).T
        return c
    lax.fori_loop(0, vt_ref.shape[0], body, 0, unroll=4)


def _attn_a_kernel(t5_ref, lq1, lk1, lq2, lk2, g_ref, q_ref, k_ref, v_ref, o_ref,
                   kbf_ref, vt_ref, bias_ref, qt_ref, m_ref, l_ref, acc_ref, *s_refs, lam_init):
    tq, tk = A_TQ, A_TK
    h = pl.program_id(0)
    nq = q_ref.shape[0] // tq

    _stage_keys_values(k_ref, v_ref, kbf_ref, vt_ref, tk)
    key = lax.broadcasted_iota(jnp.int32, (tk, tq), 0)
    qry = lax.broadcasted_iota(jnp.int32, (tk, tq), 1)
    for n in range(3):
        rel = key + (n - 1) * tk - qry
        lo, hi = (n - 1) * tk - (tq - 1), min(n * tk - 1, CHUNK - 1)
        b = _t5_bias_tile(jnp.minimum(rel, hi), lo, hi, t5_ref, h) * LOG2E
        if n >= 1:
            b = jnp.where((key + (n - 1) * tk) // CHUNK <= qry // CHUNK, b, -jnp.inf)
        bias_ref[n] = b

    lam = _lambda(lq1, lk1, lq2, lk2, lam_init)

    hq = tq // 2

    def start_tile(i):
        q0 = pl.multiple_of(i * tq, tq)
        qt = (q_ref[pl.ds(q0, tq), :] * (DK_A ** -0.5 * LOG2E)).T
        sub = lax.broadcasted_iota(jnp.int32, (HEAD_W, tq), 0)
        maps = (jnp.where(sub < DK_A, qt, 0.0).astype(BF16), jnp.where(sub >= DK_A, qt, 0.0).astype(BF16))
        for c in range(4):
            qt_ref[:, c * hq:(c + 1) * hq] = maps[c % 2][:, (c // 2) * hq:(c // 2 + 1) * hq]

    def reset_state():
        m_ref[...] = jnp.full(m_ref.shape, -jnp.inf, F32)
        l_ref[...] = jnp.zeros(l_ref.shape, F32)
        acc_ref[...] = jnp.zeros(acc_ref.shape, F32)

    def scores(j, s_ref, late_only=False):
        r = pl.multiple_of(j * tk, tk)
        c0 = tq if late_only else 0
        s_ref[:, c0:] = jnp.dot(kbf_ref[pl.ds(r, tk), :], qt_ref[:, c0:], preferred_element_type=F32)

    def update(j, s_ref, bias_idx, late_only=False):
        c0 = tq if late_only else 0
        s = s_ref[:, c0:]
        if bias_idx is not None:
            b = bias_ref[bias_idx]
            halves = [b[:, (c // 2) * hq:(c // 2 + 1) * hq] for c in range(c0 // hq, 4)]
            s = jnp.concatenate([s[:, n * hq:(n + 1) * hq] + bh for n, bh in enumerate(halves)], axis=1)
        _online_softmax_step(s, vt_ref[j], m_ref.at[:, c0:], l_ref.at[:, c0:], acc_ref.at[:, c0:])

    def q_tile(i, carry):
        odd = jnp.logical_and(i >= 2, i % 2 == 0)

        @pl.when(odd)
        def _():
            update(0, s_refs[0], None)
            scores(2, s_refs[0])
            update(1, s_refs[1], None)
            scores(3, s_refs[1])

        j0 = jnp.where(odd, 2, 0)

        def quad(j):
            for n in range(4):
                scores(j + n + 2, s_refs[(n + 2) % 4])
                update(j + n, s_refs[n], None)

        def two_quads(u, c):
            quad(j0 + 8 * u)
            quad(j0 + 8 * u + 4)
            return c
        quads = jnp.maximum(i - 1, 0) // 2
        lax.fori_loop(0, quads // 2, two_quads, 0)

        @pl.when(quads % 2 == 1)
        def _():
            quad(j0 + 4 * (quads - 1))

        nxt = jnp.minimum(i + 1, nq - 1)

        @pl.when(i >= 1)
        def _():
            j = 2 * i - 2
            scores(j + 2, s_refs[2])
            update(j, s_refs[0], None)
            scores(j + 3, s_refs[3], late_only=True)
            update(j + 1, s_refs[1], 0)
            start_tile(nxt)
            scores(0, s_refs[0])
            update(j + 2, s_refs[2], 1)
            scores(1, s_refs[1])
            update(j + 3, s_refs[3], 2, late_only=True)

        @pl.when(i == 0)
        def _():
            update(0, s_refs[0], 1)
            start_tile(nxt)
            scores(0, s_refs[0])
            update(1, s_refs[1], 2, late_only=True)
            scores(1, s_refs[1])

        inv = 1.0 / l_ref[...]
        acc = acc_ref[...]
        o = jnp.concatenate(
            [acc[:, c:c + hq] * inv[:, c:c + hq] - lam * (acc[:, c + hq:c + tq] * inv[:, c + hq:c + tq])
             for c in (0, tq)], axis=1)
        ms = jnp.mean(o * o, axis=0, keepdims=True)
        y = (o * lax.rsqrt(ms + EPS) * g_ref[...]) * (1.0 - lam_init)
        o_ref[pl.ds(pl.multiple_of(i * tq, tq), tq), :] = y.T
        reset_state()
        return carry

    start_tile(0)
    scores(0, s_refs[0])
    scores(1, s_refs[1])
    reset_state()
    lax.fori_loop(0, nq, q_tile, 0)


def _attn_a_prompt(proj, k_a, v_a, t5_bias, lq1, lk1, lq2, lk2, subln_g, lam_init):
    s = proj.shape[1]
    tq, tk = A_TQ, A_TK
    assert s % tq == 0 and tq == 2 * tk and tk % CHUNK == 0 and tk >= T5_MAX_DIST
    vec = lambda: pl.BlockSpec((1, DK_A), lambda h: (0, 0))
    head = lambda: pl.BlockSpec((s, HEAD_W), lambda h: (0, h))
    return pl.pallas_call(
        functools.partial(_attn_a_kernel, lam_init=lam_init),
        grid=(H_A,),
        in_specs=[
            pl.BlockSpec(memory_space=pltpu.SMEM),
            vec(), vec(), vec(), vec(),
            pl.BlockSpec((DV_A, 1), lambda h: (0, 0)),
            pl.BlockSpec((None, s, HEAD_W), lambda h: (QA, 0, h)), head(), head(),
        ],
        out_specs=pl.BlockSpec((s, HEAD_W), lambda h: (0, h)),
        out_shape=jax.ShapeDtypeStruct((s, H_A * DV_A), F32),
        scratch_shapes=[
            pltpu.VMEM((s, HEAD_W), BF16),
            pltpu.VMEM((s // tk, DV_A, tk), BF16),
            pltpu.VMEM((3, tk, tq), F32),
            pltpu.VMEM((HEAD_W, 2 * tq), BF16),
            pltpu.VMEM((1, 2 * tq), F32),
            pltpu.VMEM((1, 2 * tq), F32),
            pltpu.VMEM((DV_A, 2 * tq), F32),
        ] + [pltpu.VMEM((tk, 2 * tq), F32)] * 4,
        compiler_params=pltpu.CompilerParams(
            dimension_semantics=("arbitrary",), vmem_limit_bytes=VMEM_LIMIT),
        name="attn_a_prompt",
    )(t5_bias, lq1.reshape(1, DK_A), lk1.reshape(1, DK_A), lq2.reshape(1, DK_A), lk2.reshape(1, DK_A),
      subln_g.reshape(DV_A, 1), proj, k_a, v_a)


def _attn_b_kernel(rb_ref, q_ref, k_ref, v_ref, o_ref, kbf_ref, vt_ref, bias_ref, *s_refs):
    t = B_TQ
    blk = REL_CLIP
    nkt = BAND_PAST // t + 1
    nq = q_ref.shape[0] // t
    h = pl.program_id(0)

    _stage_keys_values(k_ref, v_ref, kbf_ref, vt_ref, t)

    kk = lax.broadcasted_iota(jnp.int32, (blk, blk), 0)
    qq = lax.broadcasted_iota(jnp.int32, (blk, blk), 1)
    rel = kk - qq
    lo = jnp.full((blk, blk), rb_ref[h, 0] * LOG2E, F32)
    same = _rel_bias_tile(rel, -(blk - 1), blk - 1, rb_ref, h) * LOG2E
    prev = _rel_bias_tile(rel - blk, -(2 * blk - 1), -1, rb_ref, h) * LOG2E
    ninf = jnp.full((blk, blk), -jnp.inf, F32)
    kc = kk // CHUNK
    qc = qq // CHUNK
    far_blocks = BAND_PAST // blk
    for a in range(nkt * t // blk):
        for b in range(t // blk):
            e = a - b
            if e < 0 or e > far_blocks:
                tile = ninf
            elif e == 0:
                tile = jnp.where(kc >= qc, lo, -jnp.inf)
            elif e == far_blocks:
                tile = jnp.where(kc <= qc, same, -jnp.inf)
            elif e == far_blocks - 1:
                tile = prev
            else:
                tile = lo
            bias_ref[a * blk:(a + 1) * blk, b * blk:(b + 1) * blk] = tile

    def scores(g, nk, s_ref):
        q0 = pl.multiple_of(g * t, t)
        k0 = pl.multiple_of((g - (nk - 1)) * t, t)
        qt = (q_ref[pl.ds(q0, t), :] * (DH_B ** -0.5 * LOG2E)).T.astype(BF16)
        s_ref[(nkt - nk) * t:, :] = jnp.dot(kbf_ref[pl.ds(k0, nk * t), :], qt, preferred_element_type=F32)

    def finish(g, nk, s_ref):
        s = s_ref[(nkt - nk) * t:, :] + bias_ref[(nkt - nk) * t:, :]
        m = jnp.max(s, axis=0, keepdims=True)
        p = jnp.exp2(s - m)
        l = jnp.sum(p, axis=0, keepdims=True)
        pb = p.astype(BF16)
        o = jnp.dot(vt_ref[g - (nk - 1)], pb[:t], preferred_element_type=F32)
        for c in range(1, nk):
            o = o + jnp.dot(vt_ref[g - (nk - 1) + c], pb[c * t:(c + 1) * t], preferred_element_type=F32)
        o_ref[pl.ds(pl.multiple_of(g * t, t), t), :] = (o * (1.0 / l)).T

    assert nq % 4 == 0 and nq >= 8 and nkt <= 4

    def quad(g, static_start=False, lookahead=True):
        for n in range(4):
            if lookahead or n < 2:
                scores(g + n + 2, nkt, s_refs[(n + 2) % 4])
            finish(g + n, min(n + 1, nkt) if static_start else nkt, s_refs[n])

    def two_quads(u, c):
        quad(4 + 8 * u)
        quad(8 + 8 * u)
        return c

    scores(0, 1, s_refs[0])
    scores(1, min(2, nkt), s_refs[1])
    quad(0, static_start=True)
    quads = nq // 4 - 2
    lax.fori_loop(0, quads // 2, two_quads, 0)
    if quads % 2:
        quad(4 * quads)
    quad(nq - 4, lookahead=False)


def _attn_b_prompt(proj, rel_bias):
    s = proj.shape[1]
    t = B_TQ
    nkt = BAND_PAST // t + 1
    assert s % t == 0 and BAND_PAST % t == 0 and t % REL_CLIP == 0 and REL_CLIP % CHUNK == 0
    head = lambda c: pl.BlockSpec((None, s, HEAD_W), lambda h: (c, 0, h))
    return pl.pallas_call(
        _attn_b_kernel,
        grid=(H_B,),
        in_specs=[pl.BlockSpec(memory_space=pltpu.SMEM), head(QB), head(KB), head(VB)],
        out_specs=pl.BlockSpec((s, HEAD_W), lambda h: (0, h)),
        out_shape=jax.ShapeDtypeStruct((s, H_B * DH_B), F32),
        scratch_shapes=[
            pltpu.VMEM((s, HEAD_W), BF16),
            pltpu.VMEM((s // t, DH_B, t), BF16),
            pltpu.VMEM((nkt * t, t), F32),
        ] + [pltpu.VMEM((nkt * t, t), F32)] * 4,
        compiler_params=pltpu.CompilerParams(
            dimension_semantics=("arbitrary",), vmem_limit_bytes=VMEM_LIMIT),
        name="attn_b_prompt",
    )(rel_bias, proj, proj, proj)


def _dot_nt(a, b):
    return lax.dot_general(a, b, (((1,), (1,)), ((), ())), preferred_element_type=F32)


def _attn_sample_kernel(t5_ref, rb_ref, lq1, lk1, lq2, lk2, g_ref,
                        qa_ref, ka_ref, va_ref, cka_ref, cva_ref,
                        qb_ref, kb_ref, vb_ref, ckb_ref, cvb_ref,
                        oa_ref, ob_ref, kroll_ref, vroll_ref,
                        ba_ref, bb_ref, *, lam_init, past, win):
    n = qa_ref.shape[0]
    near = NEAR
    nh = H_A

    @pl.when(pl.program_id(0) == 0)
    def _():
        qry = lax.broadcasted_iota(jnp.int32, (n, near + n), 0)
        key = lax.broadcasted_iota(jnp.int32, (n, near + n), 1)
        rel = key - near - qry
        for h in range(nh):
            ba_ref[h] = _t5_bias_tile(rel, -(near + n - 1), n - 1, t5_ref, h)
            bb_ref[h] = _rel_bias_tile(rel, -(near + n - 1), n - 1, rb_ref, h)

    kroll_ref[:(win - n) * nh, :] = ckb_ref[n * nh:, :]
    vroll_ref[:(win - n) * nh, :] = cvb_ref[n * nh:, :]

    lam = _lambda(lq1, lk1, lq2, lk2, lam_init)
    lane = lax.broadcasted_iota(jnp.int32, (n, HEAD_W), 1)

    def cols(h):
        return slice(h * HEAD_W, (h + 1) * HEAD_W)

    def scores_a(h):
        q = qa_ref[:, cols(h)] * (DK_A ** -0.5)
        q2 = jnp.concatenate([jnp.where(lane < DK_A, q, 0.0), jnp.where(lane >= DK_A, q, 0.0)],
                             axis=0).astype(BF16)
        ba = ba_ref[h]
        ba2 = jnp.concatenate([ba, ba], axis=0)
        s_c = _dot_nt(q2, cka_ref[:, cols(h)].astype(BF16))
        s_c = jnp.concatenate([s_c[:, :past - near], s_c[:, past - near:] + ba2[:, :near]], axis=1)
        s_n = _dot_nt(q2, ka_ref[:, cols(h)].astype(BF16)) + ba2[:, near:]
        return s_c, s_n

    def scores_b(h):
        qb = qb_ref[:, cols(h)].astype(BF16)
        bb = bb_ref[h]
        ckb = ckb_ref[pl.ds(h, win, stride=nh), :].astype(BF16)
        s_c = _dot_nt(qb, ckb) * (DH_B ** -0.5)
        s_c = jnp.concatenate([s_c[:, :win - near] + rb_ref[h, 0], s_c[:, win - near:] + bb[:, :near]], axis=1)
        s_n = _dot_nt(qb, kb_ref[:, cols(h)].astype(BF16)) * (DH_B ** -0.5) + bb[:, near:]
        return s_c, s_n

    def softmax(s_c, s_n):
        m = jnp.maximum(jnp.max(s_c, axis=-1, keepdims=True), jnp.max(s_n, axis=-1, keepdims=True))
        p_c = jnp.exp(s_c - m)
        p_n = jnp.exp(s_n - m)
        l = jnp.sum(p_c, axis=-1, keepdims=True) + jnp.sum(p_n, axis=-1, keepdims=True)
        return p_c.astype(BF16), p_n.astype(BF16), 1.0 / l

    def finish_a(h, p_c, p_n, inv):
        cva = cva_ref[pl.ds(h, past, stride=nh), :].astype(BF16)
        o2 = (jnp.dot(p_c, cva, preferred_element_type=F32)
              + jnp.dot(p_n, va_ref[:, cols(h)].astype(BF16), preferred_element_type=F32)) * inv
        o = o2[:n] - lam * o2[n:]
        ms = jnp.mean(o * o, axis=-1, keepdims=True)
        oa_ref[:, cols(h)] = (o * lax.rsqrt(ms + EPS) * g_ref[...]) * (1.0 - lam_init)

    def finish_b(h, p_c, p_n, inv):
        cvb = cvb_ref[pl.ds(h, win, stride=nh), :].astype(BF16)
        ob_ref[:, cols(h)] = (jnp.dot(p_c, cvb, preferred_element_type=F32)
                              + jnp.dot(p_n, vb_ref[:, cols(h)].astype(BF16), preferred_element_type=F32)) * inv

    for first in range(0, nh, SAMPLE_HEAD_GROUP):
        heads = range(first, first + SAMPLE_HEAD_GROUP)
        for h in heads:
            new_rows = pl.ds((win - n) * nh + h, n, stride=nh)
            kroll_ref[new_rows, :] = kb_ref[:, cols(h)]
            vroll_ref[new_rows, :] = vb_ref[:, cols(h)]
        s_a = [scores_a(h) for h in heads]
        s_b = [scores_b(h) for h in heads]
        p_a = [softmax(*s) for s in s_a]
        p_b = [softmax(*s) for s in s_b]
        for h, p in zip(heads, p_a):
            finish_a(h, *p)
        for h, p in zip(heads, p_b):
            finish_b(h, *p)


def _attn_sample(proj, k_a, v_a, ck_a, cv_a, ck_b, cv_b, t5_bias, rel_bias, lq1, lk1, lq2, lk2, subln_g,
                 lam_init):
    nb, past = ck_a.shape[0], ck_a.shape[1]
    win = ck_b.shape[1]
    n = proj.shape[1] // nb
    near = NEAR
    wide = H_A * HEAD_W
    assert past % CHUNK == 0 and n <= CHUNK and win <= BAND_PAST and win <= past
    assert near % HEAD_W == 0 and near <= win and near <= past and n % 8 == 0 and T5_MAX_DIST <= REL_CLIP
    assert H_A == H_B and DV_A == HEAD_W and DH_B == HEAD_W and 2 * DK_A == HEAD_W
    cka = ck_a.reshape(nb, past, wide)
    cva = cv_a.reshape(nb, past * H_A, DV_A)
    ckb = ck_b.reshape(nb, win * H_B, DH_B)
    cvb = cv_b.reshape(nb, win * H_B, DH_B)
    vec = lambda: pl.BlockSpec((1, DK_A), lambda b: (0, 0))
    new = lambda c: pl.BlockSpec((None, n, wide), lambda b: (c, b, 0))
    seq = lambda rows, cols: pl.BlockSpec((None, rows, cols), lambda b: (b, 0, 0))
    out = pl.BlockSpec((n, wide), lambda b: (b, 0))
    return pl.pallas_call(
        functools.partial(_attn_sample_kernel, lam_init=lam_init, past=past, win=win),
        grid=(nb,),
        in_specs=[
            pl.BlockSpec(memory_space=pltpu.SMEM),
            pl.BlockSpec(memory_space=pltpu.SMEM),
            vec(), vec(), vec(), vec(),
            pl.BlockSpec((1, DV_A), lambda b: (0, 0)),
            new(QA), out, out, seq(past, wide), seq(past * H_A, HEAD_W),
            new(QB), new(KB), new(VB), seq(win * H_B, HEAD_W), seq(win * H_B, HEAD_W),
        ],
        out_specs=[out, out, seq(win * H_B, HEAD_W), seq(win * H_B, HEAD_W)],
        out_shape=[
            jax.ShapeDtypeStruct((nb * n, wide), F32),
            jax.ShapeDtypeStruct((nb * n, wide), F32),
            jax.ShapeDtypeStruct((nb, win * H_B, DH_B), F32),
            jax.ShapeDtypeStruct((nb, win * H_B, DH_B), F32),
        ],
        scratch_shapes=[pltpu.VMEM((H_A, n, near + n), F32), pltpu.VMEM((H_B, n, near + n), F32)],
        compiler_params=pltpu.CompilerParams(
            dimension_semantics=("arbitrary",), vmem_limit_bytes=VMEM_LIMIT),
        name="attn_sample",
    )(t5_bias, rel_bias, lq1.reshape(1, DK_A), lk1.reshape(1, DK_A), lq2.reshape(1, DK_A),
      lk2.reshape(1, DK_A), subln_g.reshape(1, DV_A),
      proj, k_a, v_a, cka, cva, proj, proj, proj, ckb, cvb)


def _merge_kernel(x_ref, oa_ref, ob_ref, za_ref, zb_ref, ga0_ref, ga1_ref, gb0_ref, gb1_ref,
                  woa_ref, wob_ref, wout_ref, pg_ref, y_ref):
    za = za_ref[...]
    zb = zb_ref[...]
    a = (oa_ref[...] * (za * _sigmoid(za))).astype(BF16)
    b = (ob_ref[...] * (zb * _sigmoid(zb))).astype(BF16)
    ya = jnp.dot(a, woa_ref[...], preferred_element_type=F32)
    yb = jnp.dot(b, wob_ref[...], preferred_element_type=F32)
    ga = jnp.concatenate([ga0_ref[...], ga1_ref[...]], axis=1)
    gb = jnp.concatenate([gb0_ref[...], gb1_ref[...]], axis=1)
    mix = (_sigmoid(ga) * ya + _sigmoid(gb) * yb).astype(BF16)
    y = jnp.dot(mix, wout_ref[...], preferred_element_type=F32)
    ms = jnp.mean(y * y, axis=-1, keepdims=True)
    y_ref[...] = x_ref[...] + y * lax.rsqrt(ms + EPS) * pg_ref[...]


def _merge(x2d, o_a, o_b, proj, woa, wob, wout, post_g, tm):
    m, d = x2d.shape
    wa = o_a.shape[1]
    wb = o_b.shape[1]
    assert m % tm == 0 and wa == COL_BLOCK and wb == COL_BLOCK and d == 2 * COL_BLOCK
    row = lambda w: pl.BlockSpec((tm, w), lambda i: (i, 0))
    col = lambda c: pl.BlockSpec((None, tm, COL_BLOCK), lambda i: (c, i, 0))
    resident = lambda r, c: pl.BlockSpec((r, c), lambda i: (0, 0), pipeline_mode=pl.Buffered(1))
    return pl.pallas_call(
        _merge_kernel,
        grid=(m // tm,),
        in_specs=[row(d), row(wa), row(wb), col(ZA), col(ZB), col(GA0), col(GA1), col(GB0), col(GB1),
                  resident(wa, d), resident(wb, d), resident(d, d), resident(1, d)],
        out_specs=row(d),
        out_shape=jax.ShapeDtypeStruct((m, d), F32),
        compiler_params=pltpu.CompilerParams(
            dimension_semantics=("arbitrary",), vmem_limit_bytes=VMEM_LIMIT),
        name="merge",
    )(x2d, o_a, o_b, proj, proj, proj, proj, proj, proj, woa, wob, wout, post_g.reshape(1, d))


def kernel(x_prompt, x_sample, cache_k_a, cache_v_a, cache_k_b, cache_v_b, t5_bias, pre_norm, post_norm,
           w_in, lambda_q1, lambda_k1, lambda_q2, lambda_k2, subln_a, rel_bias_b, w_o_a, w_o_b, w_out):
    depth = w_in.shape[0]
    bp, sp, d = x_prompt.shape
    bs, ss, _ = x_sample.shape
    assert bp == 1 and w_in.shape[2] == 12 * COL_BLOCK
    yp = x_prompt.reshape(sp, d)
    ys = x_sample.reshape(bs * ss, d)
    tail = min(BAND_PAST, sp)
    outs = [[] for _ in range(8)]
    for l in range(depth):
        lam_init = 0.8 - 0.6 * math.exp(-0.3 * l)
        woa = w_o_a[l].astype(BF16)
        wob = w_o_b[l].astype(BF16)
        wout = w_out[l].astype(BF16)
        lam_args = (lambda_q1[l], lambda_k1[l], lambda_q2[l], lambda_k2[l], subln_a[l], lam_init)

        ps, ka_s, va_s, w = _in_proj_cast(ys, pre_norm[l], w_in[l])
        yp, ps = lax.optimization_barrier((yp, ps))
        pp, ka, va = _in_proj(yp, pre_norm[l], w, tm=1024)
        ob = _attn_b_prompt(pp, rel_bias_b[l])
        ck_a, ob = lax.optimization_barrier((cache_k_a[l], ob))
        oa = _attn_a_prompt(pp, ka, va, t5_bias, *lam_args)
        yp = _merge(yp, oa, ob, pp, woa, wob, wout, post_norm[l], tm=256)
        outs[0].append(ka.reshape(bp, sp, 2 * H_A, DK_A))
        outs[1].append(va.reshape(bp, sp, H_A, DV_A))
        outs[2].append(pp[KB, sp - tail:].reshape(bp, tail, H_B, DH_B))
        outs[3].append(pp[VB, sp - tail:].reshape(bp, tail, H_B, DH_B))

        oas, obs, kroll, vroll = _attn_sample(ps, ka_s, va_s, ck_a, cache_v_a[l], cache_k_b[l], cache_v_b[l],
                                              t5_bias, rel_bias_b[l], *lam_args)
        ys = _merge(ys, oas, obs, ps, woa, wob, wout, post_norm[l], tm=bs * ss)
        outs[4].append(ka_s.reshape(bs, ss, 2 * H_A, DK_A))
        outs[5].append(va_s.reshape(bs, ss, H_A, DV_A))
        outs[6].append(kroll.reshape(bs, -1, H_B, DH_B))
        outs[7].append(vroll.reshape(bs, -1, H_B, DH_B))
    return (yp.reshape(bp, sp, d), ys.reshape(bs, ss, d)) + tuple(jnp.stack(o) for o in outs)
```

```python
import functools
import math

import jax
import jax.numpy as jnp
from jax import lax
from jax.experimental import pallas as pl
from jax.experimental.pallas import tpu as pltpu

F32 = jnp.float32
BF16 = jnp.bfloat16

CHUNK = 64
H_A = 8
DK_A = 64
DV_A = 2 * DK_A
H_B = 8
DH_B = 128
BAND_CHUNKS = 8
BAND_PAST = BAND_CHUNKS * CHUNK
REL_CLIP = 128
T5_BUCKETS = 32
T5_MAX_DIST = 128
EPS = 1e-6

HEAD_W = 128
COL_BLOCK = 1024
A_TQ, A_TK = 512, 256
B_TQ = 256
SAMPLE_HEAD_GROUP = 4
N_MERGE_WEIGHTS = 3
LOG2E = math.log2(math.e)
NEAR = max(REL_CLIP, T5_MAX_DIST)
VMEM_LIMIT = 60 * 1024 * 1024

QA, ZA, QB, KB, VB, ZB, GA0, GA1, GB0, GB1 = range(10)
COL_KA, COL_VA = 1, 2
COL_SLOT = (QA, None, None, ZA, QB, KB, VB, ZB, GA0, GA1, GB0, GB1)


def _t5_bucket_int(rel):
    half = T5_BUCKETS // 2
    max_exact = half // 2
    n = abs(rel)
    ret = half if rel > 0 else 0
    if n < max_exact:
        return ret + n
    assert (T5_MAX_DIST // max_exact) ** 2 == 2 ** (half - max_exact)
    j = 0
    while n * n >= (max_exact * max_exact) * 2 ** (j + 1):
        j += 1
    return ret + min(max_exact + j, half - 1)


def _t5_runs(lo, hi):
    runs = []
    for r in range(lo, hi + 1):
        b = _t5_bucket_int(r)
        if not runs or runs[-1][1] != b:
            runs.append((r, b))
    return runs


T5_FAR_BUCKET = _t5_bucket_int(-T5_MAX_DIST)
assert all(_t5_bucket_int(-n) == T5_FAR_BUCKET for n in range(T5_MAX_DIST, 4 * T5_MAX_DIST))


def _t5_bias_tile(rel, lo, hi, t5_ref, h):
    runs = _t5_runs(lo, hi)
    val = jnp.full(rel.shape, t5_ref[runs[0][1], h], F32)
    for start, b in runs[1:]:
        val = jnp.where(rel >= start, t5_ref[b, h], val)
    return val - t5_ref[T5_FAR_BUCKET, h]


def _rel_bias_tile(rel, lo, hi, rb_ref, h):
    lo = max(lo, -REL_CLIP)
    hi = min(hi, REL_CLIP)
    val = jnp.full(rel.shape, rb_ref[h, lo + REL_CLIP], F32)
    for d in range(lo + 1, hi + 1):
        val = jnp.where(rel >= d, rb_ref[h, d + REL_CLIP], val)
    return val


def _sigmoid(x):
    return 1.0 / (1.0 + jnp.exp(-x))


def _lambda(lq1, lk1, lq2, lk2, lam_init):
    a = jnp.sum(lq1[...] * lk1[...], axis=-1, keepdims=True)
    b = jnp.sum(lq2[...] * lk2[...], axis=-1, keepdims=True)
    return jnp.exp(a) - jnp.exp(b) + lam_init


def _in_proj_kernel(x_ref, g_ref, w_ref, o_ref, ka_ref, va_ref, h_ref):
    j = pl.program_id(1)

    @pl.when(j == 0)
    def _():
        x = x_ref[...]
        ms = jnp.mean(x * x, axis=-1, keepdims=True)
        h_ref[...] = (x * lax.rsqrt(ms + EPS) * g_ref[...]).astype(BF16)

    def project(ref):
        ref[...] = jnp.dot(h_ref[...], w_ref[...], preferred_element_type=F32)

    pl.when(j == COL_KA)(functools.partial(project, ka_ref))
    pl.when(j == COL_VA)(functools.partial(project, va_ref))
    pl.when(jnp.logical_and(j != COL_KA, j != COL_VA))(functools.partial(project, o_ref))


def _in_proj(x2d, pre_g, w_bf16, tm):
    m, d = x2d.shape
    n = w_bf16.shape[1]
    assert m % tm == 0 and n == len(COL_SLOT) * COL_BLOCK

    def own(col):
        return pl.BlockSpec((tm, COL_BLOCK), lambda i, j: (jnp.where(j >= col, i, jnp.maximum(i - 1, 0)), 0))

    def slab_index(i, j):
        slot = jnp.where(j <= COL_VA, 0, j - 2)
        return slot, i, 0

    assert COL_SLOT[0] == 0 and COL_SLOT[COL_VA + 1:] == tuple(range(1, len(COL_SLOT) - 2))
    return pl.pallas_call(
        _in_proj_kernel,
        grid=(m // tm, n // COL_BLOCK),
        in_specs=[
            pl.BlockSpec((tm, d), lambda i, j: (i, 0)),
            pl.BlockSpec((1, d), lambda i, j: (0, 0)),
            pl.BlockSpec((d, COL_BLOCK), lambda i, j: (0, j)),
        ],
        out_specs=[pl.BlockSpec((None, tm, COL_BLOCK), slab_index), own(COL_KA), own(COL_VA)],
        out_shape=[jax.ShapeDtypeStruct((len(COL_SLOT) - 2, m, COL_BLOCK), F32),
                   jax.ShapeDtypeStruct((m, COL_BLOCK), F32),
                   jax.ShapeDtypeStruct((m, COL_BLOCK), F32)],
        scratch_shapes=[pltpu.VMEM((tm, d), BF16)],
        compiler_params=pltpu.CompilerParams(
            dimension_semantics=("arbitrary", "arbitrary"), vmem_limit_bytes=VMEM_LIMIT),
        name="in_proj",
    )(x2d, pre_g.reshape(1, d), w_bf16)


def _in_proj_cast_kernel(x_ref, g_ref, w_ref, o_ref, ka_ref, va_ref, wbf_ref, h_ref):
    j = pl.program_id(0)

    @pl.when(j == 0)
    def _():
        x = x_ref[...]
        ms = jnp.mean(x * x, axis=-1, keepdims=True)
        h_ref[...] = (x * lax.rsqrt(ms + EPS) * g_ref[...]).astype(BF16)

    def project(ref):
        w = w_ref[...].astype(BF16)
        wbf_ref[...] = w
        ref[...] = jnp.dot(h_ref[...], w, preferred_element_type=F32)

    pl.when(j == COL_KA)(functools.partial(project, ka_ref))
    pl.when(j == COL_VA)(functools.partial(project, va_ref))
    pl.when(jnp.logical_and(j != COL_KA, j != COL_VA))(functools.partial(project, o_ref))


def _in_proj_cast(x2d, pre_g, w_f32):
    m, d = x2d.shape
    n = w_f32.shape[1]
    assert n == len(COL_SLOT) * COL_BLOCK
    slots = len(COL_SLOT) - 2
    resident = lambda: pl.BlockSpec((m, COL_BLOCK), lambda j: (0, 0))
    return pl.pallas_call(
        _in_proj_cast_kernel,
        grid=(len(COL_SLOT),),
        in_specs=[
            pl.BlockSpec((m, d), lambda j: (0, 0)),
            pl.BlockSpec((1, d), lambda j: (0, 0)),
            pl.BlockSpec((d, COL_BLOCK), lambda j: (0, j)),
        ],
        out_specs=[pl.BlockSpec((None, m, COL_BLOCK), lambda j: (jnp.where(j <= COL_VA, 0, j - 2), 0, 0)),
                   resident(), resident(),
                   pl.BlockSpec((d, COL_BLOCK), lambda j: (0, j))],
        out_shape=[jax.ShapeDtypeStruct((slots, m, COL_BLOCK), F32),
                   jax.ShapeDtypeStruct((m, COL_BLOCK), F32),
                   jax.ShapeDtypeStruct((m, COL_BLOCK), F32),
                   jax.ShapeDtypeStruct((d, n), BF16)],
        scratch_shapes=[pltpu.VMEM((m, d), BF16)],
        compiler_params=pltpu.CompilerParams(
            dimension_semantics=("arbitrary",), vmem_limit_bytes=VMEM_LIMIT),
        name="in_proj_cast",
    )(x2d, pre_g.reshape(1, d), w_f32)


def _online_softmax_step(s, vt, m_ref, l_ref, acc_ref):
    m_old = m_ref[...]
    m_new = jnp.maximum(m_old, jnp.max(s, axis=0, keepdims=True))
    alpha = jnp.exp2(m_old - m_new)
    p = jnp.exp2(s - m_new)
    l_ref[...] = alpha * l_ref[...] + jnp.sum(p, axis=0, keepdims=True)
    acc_ref[...] = alpha * acc_ref[...] + jnp.dot(vt, p.astype(BF16), preferred_element_type=F32)
    m_ref[...] = m_new


def _stage_keys_values(k_ref, v_ref, kbf_ref, vt_ref, t):
    def body(j, c):
        r = pl.multiple_of(j * t, t)
        kbf_ref[pl.ds(r, t), :] = k_ref[pl.ds(r, t), :].astype(BF16)
        vt_ref[j] = v_ref[pl.ds(r, t), :].astype(BF16).T
        return c
    lax.fori_loop(0, vt_ref.shape[0], body, 0, unroll=4)


def _attn_a_kernel(t5_ref, lq1, lk1, lq2, lk2, g_ref, q_ref, k_ref, v_ref, o_ref,
                   kbf_ref, vt_ref, bias_ref, qt_ref, m_ref, l_ref, acc_ref, *s_refs, lam_init):
    tq, tk = A_TQ, A_TK
    h = pl.program_id(0)
    nq = q_ref.shape[0] // tq

    _stage_keys_values(k_ref, v_ref, kbf_ref, vt_ref, tk)
    key = lax.broadcasted_iota(jnp.int32, (tk, tq), 0)
    qry = lax.broadcasted_iota(jnp.int32, (tk, tq), 1)
    for n in range(3):
        rel = key + (n - 1) * tk - qry
        lo, hi = (n - 1) * tk - (tq - 1), min(n * tk - 1, CHUNK - 1)
        b = _t5_bias_tile(jnp.minimum(rel, hi), lo, hi, t5_ref, h) * LOG2E
        if n >= 1:
            b = jnp.where((key + (n - 1) * tk) // CHUNK <= qry // CHUNK, b, -jnp.inf)
        bias_ref[n] = b

    lam = _lambda(lq1, lk1, lq2, lk2, lam_init)

    hq = tq // 2

    def start_tile(i):
        q0 = pl.multiple_of(i * tq, tq)
        qt = (q_ref[pl.ds(q0, tq), :] * (DK_A ** -0.5 * LOG2E)).T
        sub = lax.broadcasted_iota(jnp.int32, (HEAD_W, tq), 0)
        maps = (jnp.where(sub < DK_A, qt, 0.0).astype(BF16), jnp.where(sub >= DK_A, qt, 0.0).astype(BF16))
        for c in range(4):
            qt_ref[:, c * hq:(c + 1) * hq] = maps[c % 2][:, (c // 2) * hq:(c // 2 + 1) * hq]

    def reset_state():
        m_ref[...] = jnp.full(m_ref.shape, -jnp.inf, F32)
        l_ref[...] = jnp.zeros(l_ref.shape, F32)
        acc_ref[...] = jnp.zeros(acc_ref.shape, F32)

    def scores(j, s_ref, late_only=False):
        r = pl.multiple_of(j * tk, tk)
        c0 = tq if late_only else 0
        s_ref[:, c0:] = jnp.dot(kbf_ref[pl.ds(r, tk), :], qt_ref[:, c0:], preferred_element_type=F32)

    def update(j, s_ref, bias_idx, late_only=False):
        c0 = tq if late_only else 0
        s = s_ref[:, c0:]
        if bias_idx is not None:
            b = bias_ref[bias_idx]
            halves = [b[:, (c // 2) * hq:(c // 2 + 1) * hq] for c in range(c0 // hq, 4)]
            s = jnp.concatenate([s[:, n * hq:(n + 1) * hq] + bh for n, bh in enumerate(halves)], axis=1)
        _online_softmax_step(s, vt_ref[j], m_ref.at[:, c0:], l_ref.at[:, c0:], acc_ref.at[:, c0:])

    def q_tile(i, carry):
        odd = jnp.logical_and(i >= 2, i % 2 == 0)

        @pl.when(odd)
        def _():
            update(0, s_refs[0], None)
            scores(2, s_refs[0])
            update(1, s_refs[1], None)
            scores(3, s_refs[1])

        j0 = jnp.where(odd, 2, 0)

        def quad(j):
            for n in range(4):
                scores(j + n + 2, s_refs[(n + 2) % 4])
                update(j + n, s_refs[n], None)

        def two_quads(u, c):
            quad(j0 + 8 * u)
            quad(j0 + 8 * u + 4)
            return c
        quads = jnp.maximum(i - 1, 0) // 2
        lax.fori_loop(0, quads // 2, two_quads, 0)

        @pl.when(quads % 2 == 1)
        def _():
            quad(j0 + 4 * (quads - 1))

        nxt = jnp.minimum(i + 1, nq - 1)

        @pl.when(i >= 1)
        def _():
            j = 2 * i - 2
            scores(j + 2, s_refs[2])
            update(j, s_refs[0], None)
            scores(j + 3, s_refs[3], late_only=True)
            update(j + 1, s_refs[1], 0)
            start_tile(nxt)
            scores(0, s_refs[0])
            update(j + 2, s_refs[2], 1)
            scores(1, s_refs[1])
            update(j + 3, s_refs[3], 2, late_only=True)

        @pl.when(i == 0)
        def _():
            update(0, s_refs[0], 1)
            start_tile(nxt)
            scores(0, s_refs[0])
            update(1, s_refs[1], 2, late_only=True)
            scores(1, s_refs[1])

        inv = 1.0 / l_ref[...]
        acc = acc_ref[...]
        o = jnp.concatenate(
            [acc[:, c:c + hq] * inv[:, c:c + hq] - lam * (acc[:, c + hq:c + tq] * inv[:, c + hq:c + tq])
             for c in (0, tq)], axis=1)
        ms = jnp.mean(o * o, axis=0, keepdims=True)
        y = (o * lax.rsqrt(ms + EPS) * g_ref[...]) * (1.0 - lam_init)
        o_ref[pl.ds(pl.multiple_of(i * tq, tq), tq), :] = y.T
        reset_state()
        return carry

    start_tile(0)
    scores(0, s_refs[0])
    scores(1, s_refs[1])
    reset_state()
    lax.fori_loop(0, nq, q_tile, 0)


def _attn_a_prompt(proj, k_a, v_a, t5_bias, lq1, lk1, lq2, lk2, subln_g, lam_init):
    s = proj.shape[1]
    tq, tk = A_TQ, A_TK
    assert s % tq == 0 and tq == 2 * tk and tk % CHUNK == 0 and tk >= T5_MAX_DIST
    vec = lambda: pl.BlockSpec((1, DK_A), lambda h: (0, 0))
    head = lambda: pl.BlockSpec((s, HEAD_W), lambda h: (0, h))
    return pl.pallas_call(
        functools.partial(_attn_a_kernel, lam_init=lam_init),
        grid=(H_A,),
        in_specs=[
            pl.BlockSpec(memory_space=pltpu.SMEM),
            vec(), vec(), vec(), vec(),
            pl.BlockSpec((DV_A, 1), lambda h: (0, 0)),
            pl.BlockSpec((None, s, HEAD_W), lambda h: (QA, 0, h)), head(), head(),
        ],
        out_specs=pl.BlockSpec((s, HEAD_W), lambda h: (0, h)),
        out_shape=jax.ShapeDtypeStruct((s, H_A * DV_A), F32),
        scratch_shapes=[
            pltpu.VMEM((s, HEAD_W), BF16),
            pltpu.VMEM((s // tk, DV_A, tk), BF16),
            pltpu.VMEM((3, tk, tq), F32),
            pltpu.VMEM((HEAD_W, 2 * tq), BF16),
            pltpu.VMEM((1, 2 * tq), F32),
            pltpu.VMEM((1, 2 * tq), F32),
            pltpu.VMEM((DV_A, 2 * tq), F32),
        ] + [pltpu.VMEM((tk, 2 * tq), F32)] * 4,
        compiler_params=pltpu.CompilerParams(
            dimension_semantics=("arbitrary",), vmem_limit_bytes=VMEM_LIMIT),
        name="attn_a_prompt",
    )(t5_bias, lq1.reshape(1, DK_A), lk1.reshape(1, DK_A), lq2.reshape(1, DK_A), lk2.reshape(1, DK_A),
      subln_g.reshape(DV_A, 1), proj, k_a, v_a)


def _attn_b_kernel(rb_ref, q_ref, k_ref, v_ref, *rest):
    n_w = N_MERGE_WEIGHTS
    w_refs, o_ref, wbf_refs = rest[:n_w], rest[n_w], rest[n_w + 1:2 * n_w + 1]
    kbf_ref, vt_ref, bias_ref = rest[2 * n_w + 1:2 * n_w + 4]
    s_refs = rest[2 * n_w + 4:]
    t = B_TQ
    blk = REL_CLIP
    nkt = BAND_PAST // t + 1
    nq = q_ref.shape[0] // t
    h = pl.program_id(0)

    for w_ref, wbf_ref in zip(w_refs, wbf_refs):
        wbf_ref[...] = w_ref[...].astype(BF16)

    _stage_keys_values(k_ref, v_ref, kbf_ref, vt_ref, t)

    kk = lax.broadcasted_iota(jnp.int32, (blk, blk), 0)
    qq = lax.broadcasted_iota(jnp.int32, (blk, blk), 1)
    rel = kk - qq
    lo = jnp.full((blk, blk), rb_ref[h, 0] * LOG2E, F32)
    same = _rel_bias_tile(rel, -(blk - 1), blk - 1, rb_ref, h) * LOG2E
    prev = _rel_bias_tile(rel - blk, -(2 * blk - 1), -1, rb_ref, h) * LOG2E
    ninf = jnp.full((blk, blk), -jnp.inf, F32)
    kc = kk // CHUNK
    qc = qq // CHUNK
    far_blocks = BAND_PAST // blk
    for a in range(nkt * t // blk):
        for b in range(t // blk):
            e = a - b
            if e < 0 or e > far_blocks:
                tile = ninf
            elif e == 0:
                tile = jnp.where(kc >= qc, lo, -jnp.inf)
            elif e == far_blocks:
                tile = jnp.where(kc <= qc, same, -jnp.inf)
            elif e == far_blocks - 1:
                tile = prev
            else:
                tile = lo
            bias_ref[a * blk:(a + 1) * blk, b * blk:(b + 1) * blk] = tile

    def scores(g, nk, s_ref):
        q0 = pl.multiple_of(g * t, t)
        k0 = pl.multiple_of((g - (nk - 1)) * t, t)
        qt = (q_ref[pl.ds(q0, t), :] * (DH_B ** -0.5 * LOG2E)).T.astype(BF16)
        s_ref[(nkt - nk) * t:, :] = jnp.dot(kbf_ref[pl.ds(k0, nk * t), :], qt, preferred_element_type=F32)

    def finish(g, nk, s_ref):
        s = s_ref[(nkt - nk) * t:, :] + bias_ref[(nkt - nk) * t:, :]
        m = jnp.max(s, axis=0, keepdims=True)
        p = jnp.exp2(s - m)
        l = jnp.sum(p, axis=0, keepdims=True)
        pb = p.astype(BF16)
        o = jnp.dot(vt_ref[g - (nk - 1)], pb[:t], preferred_element_type=F32)
        for c in range(1, nk):
            o = o + jnp.dot(vt_ref[g - (nk - 1) + c], pb[c * t:(c + 1) * t], preferred_element_type=F32)
        o_ref[pl.ds(pl.multiple_of(g * t, t), t), :] = (o * (1.0 / l)).T

    assert nq % 4 == 0 and nq >= 8 and nkt <= 4

    def quad(g, static_start=False, lookahead=True):
        for n in range(4):
            if lookahead or n < 2:
                scores(g + n + 2, nkt, s_refs[(n + 2) % 4])
            finish(g + n, min(n + 1, nkt) if static_start else nkt, s_refs[n])

    def two_quads(u, c):
        quad(4 + 8 * u)
        quad(8 + 8 * u)
        return c

    scores(0, 1, s_refs[0])
    scores(1, min(2, nkt), s_refs[1])
    quad(0, static_start=True)
    quads = nq // 4 - 2
    lax.fori_loop(0, quads // 2, two_quads, 0)
    if quads % 2:
        quad(4 * quads)
    quad(nq - 4, lookahead=False)


def _attn_b_prompt(proj, rel_bias, merge_weights):
    s = proj.shape[1]
    t = B_TQ
    nkt = BAND_PAST // t + 1
    assert s % t == 0 and BAND_PAST % t == 0 and t % REL_CLIP == 0 and REL_CLIP % CHUNK == 0
    head = lambda c: pl.BlockSpec((None, s, HEAD_W), lambda h: (c, 0, h))
    assert len(merge_weights) == N_MERGE_WEIGHTS and all(w.shape[0] % (8 * H_B) == 0 for w in merge_weights)
    rows = lambda w: pl.BlockSpec((w.shape[0] // H_B, w.shape[1]), lambda h: (h, 0))
    outs = pl.pallas_call(
        _attn_b_kernel,
        grid=(H_B,),
        in_specs=[pl.BlockSpec(memory_space=pltpu.SMEM), head(QB), head(KB), head(VB)]
        + [rows(w) for w in merge_weights],
        out_specs=[pl.BlockSpec((s, HEAD_W), lambda h: (0, h))] + [rows(w) for w in merge_weights],
        out_shape=[jax.ShapeDtypeStruct((s, H_B * DH_B), F32)]
        + [jax.ShapeDtypeStruct(w.shape, BF16) for w in merge_weights],
        scratch_shapes=[
            pltpu.VMEM((s, HEAD_W), BF16),
            pltpu.VMEM((s // t, DH_B, t), BF16),
            pltpu.VMEM((nkt * t, t), F32),
        ] + [pltpu.VMEM((nkt * t, t), F32)] * 4,
        compiler_params=pltpu.CompilerParams(
            dimension_semantics=("arbitrary",), vmem_limit_bytes=VMEM_LIMIT),
        name="attn_b_prompt",
    )(rel_bias, proj, proj, proj, *merge_weights)
    return outs[0], outs[1:]


def _dot_nt(a, b):
    return lax.dot_general(a, b, (((1,), (1,)), ((), ())), preferred_element_type=F32)


def _attn_sample_kernel(t5_ref, rb_ref, lq1, lk1, lq2, lk2, g_ref,
                        qa_ref, ka_ref, va_ref, cka_ref, cva_ref,
                        qb_ref, kb_ref, vb_ref, ckb_ref, cvb_ref,
                        oa_ref, ob_ref, kroll_ref, vroll_ref,
                        ba_ref, bb_ref, *, lam_init, past, win):
    n = qa_ref.shape[0]
    near = NEAR
    nh = H_A

    @pl.when(pl.program_id(0) == 0)
    def _():
        qry = lax.broadcasted_iota(jnp.int32, (n, near + n), 0)
        key = lax.broadcasted_iota(jnp.int32, (n, near + n), 1)
        rel = key - near - qry
        for h in range(nh):
            ba_ref[h] = _t5_bias_tile(rel, -(near + n - 1), n - 1, t5_ref, h)
            bb_ref[h] = _rel_bias_tile(rel, -(near + n - 1), n - 1, rb_ref, h)

    kroll_ref[:(win - n) * nh, :] = ckb_ref[n * nh:, :]
    vroll_ref[:(win - n) * nh, :] = cvb_ref[n * nh:, :]

    lam = _lambda(lq1, lk1, lq2, lk2, lam_init)
    lane = lax.broadcasted_iota(jnp.int32, (n, HEAD_W), 1)

    def cols(h):
        return slice(h * HEAD_W, (h + 1) * HEAD_W)

    def scores_a(h):
        q = qa_ref[:, cols(h)] * (DK_A ** -0.5)
        q2 = jnp.concatenate([jnp.where(lane < DK_A, q, 0.0), jnp.where(lane >= DK_A, q, 0.0)],
                             axis=0).astype(BF16)
        ba = ba_ref[h]
        ba2 = jnp.concatenate([ba, ba], axis=0)
        s_c = _dot_nt(q2, cka_ref[:, cols(h)].astype(BF16))
        s_c = jnp.concatenate([s_c[:, :past - near], s_c[:, past - near:] + ba2[:, :near]], axis=1)
        s_n = _dot_nt(q2, ka_ref[:, cols(h)].astype(BF16)) + ba2[:, near:]
        return s_c, s_n

    def scores_b(h):
        qb = qb_ref[:, cols(h)].astype(BF16)
        bb = bb_ref[h]
        ckb = ckb_ref[pl.ds(h, win, stride=nh), :].astype(BF16)
        s_c = _dot_nt(qb, ckb) * (DH_B ** -0.5)
        s_c = jnp.concatenate([s_c[:, :win - near] + rb_ref[h, 0], s_c[:, win - near:] + bb[:, :near]], axis=1)
        s_n = _dot_nt(qb, kb_ref[:, cols(h)].astype(BF16)) * (DH_B ** -0.5) + bb[:, near:]
        return s_c, s_n

    def softmax(s_c, s_n):
        m = jnp.maximum(jnp.max(s_c, axis=-1, keepdims=True), jnp.max(s_n, axis=-1, keepdims=True))
        p_c = jnp.exp(s_c - m)
        p_n = jnp.exp(s_n - m)
        l = jnp.sum(p_c, axis=-1, keepdims=True) + jnp.sum(p_n, axis=-1, keepdims=True)
        return p_c.astype(BF16), p_n.astype(BF16), 1.0 / l

    def finish_a(h, p_c, p_n, inv):
        cva = cva_ref[pl.ds(h, past, stride=nh), :].astype(BF16)
        o2 = (jnp.dot(p_c, cva, preferred_element_type=F32)
              + jnp.dot(p_n, va_ref[:, cols(h)].astype(BF16), preferred_element_type=F32)) * inv
        o = o2[:n] - lam * o2[n:]
        ms = jnp.mean(o * o, axis=-1, keepdims=True)
        oa_ref[:, cols(h)] = (o * lax.rsqrt(ms + EPS) * g_ref[...]) * (1.0 - lam_init)

    def finish_b(h, p_c, p_n, inv):
        cvb = cvb_ref[pl.ds(h, win, stride=nh), :].astype(BF16)
        ob_ref[:, cols(h)] = (jnp.dot(p_c, cvb, preferred_element_type=F32)
                              + jnp.dot(p_n, vb_ref[:, cols(h)].astype(BF16), preferred_element_type=F32)) * inv

    for first in range(0, nh, SAMPLE_HEAD_GROUP):
        heads = range(first, first + SAMPLE_HEAD_GROUP)
        for h in heads:
            new_rows = pl.ds((win - n) * nh + h, n, stride=nh)
            kroll_ref[new_rows, :] = kb_ref[:, cols(h)]
            vroll_ref[new_rows, :] = vb_ref[:, cols(h)]
        s_a = [scores_a(h) for h in heads]
        s_b = [scores_b(h) for h in heads]
        p_a = [softmax(*s) for s in s_a]
        p_b = [softmax(*s) for s in s_b]
        for h, p in zip(heads, p_a):
            finish_a(h, *p)
        for h, p in zip(heads, p_b):
            finish_b(h, *p)


def _attn_sample(proj, k_a, v_a, ck_a, cv_a, ck_b, cv_b, t5_bias, rel_bias, lq1, lk1, lq2, lk2, subln_g,
                 lam_init):
    nb, past = ck_a.shape[0], ck_a.shape[1]
    win = ck_b.shape[1]
    n = proj.shape[1] // nb
    near = NEAR
    wide = H_A * HEAD_W
    assert past % CHUNK == 0 and n <= CHUNK and win <= BAND_PAST and win <= past
    assert near % HEAD_W == 0 and near <= win and near <= past and n % 8 == 0 and T5_MAX_DIST <= REL_CLIP
    assert H_A == H_B and DV_A == HEAD_W and DH_B == HEAD_W and 2 * DK_A == HEAD_W
    cka = ck_a.reshape(nb, past, wide)
    cva = cv_a.reshape(nb, past * H_A, DV_A)
    ckb = ck_b.reshape(nb, win * H_B, DH_B)
    cvb = cv_b.reshape(nb, win * H_B, DH_B)
    vec = lambda: pl.BlockSpec((1, DK_A), lambda b: (0, 0))
    new = lambda c: pl.BlockSpec((None, n, wide), lambda b: (c, b, 0))
    seq = lambda rows, cols: pl.BlockSpec((None, rows, cols), lambda b: (b, 0, 0))
    out = pl.BlockSpec((n, wide), lambda b: (b, 0))
    return pl.pallas_call(
        functools.partial(_attn_sample_kernel, lam_init=lam_init, past=past, win=win),
        grid=(nb,),
        in_specs=[
            pl.BlockSpec(memory_space=pltpu.SMEM),
            pl.BlockSpec(memory_space=pltpu.SMEM),
            vec(), vec(), vec(), vec(),
            pl.BlockSpec((1, DV_A), lambda b: (0, 0)),
            new(QA), out, out, seq(past, wide), seq(past * H_A, HEAD_W),
            new(QB), new(KB), new(VB), seq(win * H_B, HEAD_W), seq(win * H_B, HEAD_W),
        ],
        out_specs=[out, out, seq(win * H_B, HEAD_W), seq(win * H_B, HEAD_W)],
        out_shape=[
            jax.ShapeDtypeStruct((nb * n, wide), F32),
            jax.ShapeDtypeStruct((nb * n, wide), F32),
            jax.ShapeDtypeStruct((nb, win * H_B, DH_B), F32),
            jax.ShapeDtypeStruct((nb, win * H_B, DH_B), F32),
        ],
        scratch_shapes=[pltpu.VMEM((H_A, n, near + n), F32), pltpu.VMEM((H_B, n, near + n), F32)],
        compiler_params=pltpu.CompilerParams(
            dimension_semantics=("arbitrary",), vmem_limit_bytes=VMEM_LIMIT),
        name="attn_sample",
    )(t5_bias, rel_bias, lq1.reshape(1, DK_A), lk1.reshape(1, DK_A), lq2.reshape(1, DK_A),
      lk2.reshape(1, DK_A), subln_g.reshape(1, DV_A),
      proj, k_a, v_a, cka, cva, proj, proj, proj, ckb, cvb)


def _merge_kernel(x_ref, oa_ref, ob_ref, za_ref, zb_ref, ga0_ref, ga1_ref, gb0_ref, gb1_ref,
                  woa_ref, wob_ref, wout_ref, pg_ref, y_ref):
    za = za_ref[...]
    zb = zb_ref[...]
    a = (oa_ref[...] * (za * _sigmoid(za))).astype(BF16)
    b = (ob_ref[...] * (zb * _sigmoid(zb))).astype(BF16)
    ya = jnp.dot(a, woa_ref[...], preferred_element_type=F32)
    yb = jnp.dot(b, wob_ref[...], preferred_element_type=F32)
    ga = jnp.concatenate([ga0_ref[...], ga1_ref[...]], axis=1)
    gb = jnp.concatenate([gb0_ref[...], gb1_ref[...]], axis=1)
    mix = (_sigmoid(ga) * ya + _sigmoid(gb) * yb).astype(BF16)
    y = jnp.dot(mix, wout_ref[...], preferred_element_type=F32)
    ms = jnp.mean(y * y, axis=-1, keepdims=True)
    y_ref[...] = x_ref[...] + y * lax.rsqrt(ms + EPS) * pg_ref[...]


def _merge(x2d, o_a, o_b, proj, woa, wob, wout, post_g, tm):
    m, d = x2d.shape
    wa = o_a.shape[1]
    wb = o_b.shape[1]
    assert m % tm == 0 and wa == COL_BLOCK and wb == COL_BLOCK and d == 2 * COL_BLOCK
    row = lambda w: pl.BlockSpec((tm, w), lambda i: (i, 0))
    col = lambda c: pl.BlockSpec((None, tm, COL_BLOCK), lambda i: (c, i, 0))
    resident = lambda r, c: pl.BlockSpec((r, c), lambda i: (0, 0), pipeline_mode=pl.Buffered(1))
    return pl.pallas_call(
        _merge_kernel,
        grid=(m // tm,),
        in_specs=[row(d), row(wa), row(wb), col(ZA), col(ZB), col(GA0), col(GA1), col(GB0), col(GB1),
                  resident(wa, d), resident(wb, d), resident(d, d), resident(1, d)],
        out_specs=row(d),
        out_shape=jax.ShapeDtypeStruct((m, d), F32),
        compiler_params=pltpu.CompilerParams(
            dimension_semantics=("arbitrary",), vmem_limit_bytes=VMEM_LIMIT),
        name="merge",
    )(x2d, o_a, o_b, proj, proj, proj, proj, proj, proj, woa, wob, wout, post_g.reshape(1, d))


def kernel(x_prompt, x_sample, cache_k_a, cache_v_a, cache_k_b, cache_v_b, t5_bias, pre_norm, post_norm,
           w_in, lambda_q1, lambda_k1, lambda_q2, lambda_k2, subln_a, rel_bias_b, w_o_a, w_o_b, w_out):
    depth = w_in.shape[0]
    bp, sp, d = x_prompt.shape
    bs, ss, _ = x_sample.shape
    assert bp == 1 and w_in.shape[2] == 12 * COL_BLOCK
    yp = x_prompt.reshape(sp, d)
    ys = x_sample.reshape(bs * ss, d)
    tail = min(BAND_PAST, sp)
    outs = [[] for _ in range(8)]
    for l in range(depth):
        lam_init = 0.8 - 0.6 * math.exp(-0.3 * l)
        lam_args = (lambda_q1[l], lambda_k1[l], lambda_q2[l], lambda_k2[l], subln_a[l], lam_init)

        ps, ka_s, va_s, w = _in_proj_cast(ys, pre_norm[l], w_in[l])
        yp, ps = lax.optimization_barrier((yp, ps))
        pp, ka, va = _in_proj(yp, pre_norm[l], w, tm=1024)
        ob, (woa, wob, wout) = _attn_b_prompt(pp, rel_bias_b[l], (w_o_a[l], w_o_b[l], w_out[l]))
        ck_a, ob = lax.optimization_barrier((cache_k_a[l], ob))
        oa = _attn_a_prompt(pp, ka, va, t5_bias, *lam_args)
        yp = _merge(yp, oa, ob, pp, woa, wob, wout, post_norm[l], tm=256)
        outs[0].append(ka.reshape(bp, sp, 2 * H_A, DK_A))
        outs[1].append(va.reshape(bp, sp, H_A, DV_A))
        outs[2].append(pp[KB, sp - tail:].reshape(bp, tail, H_B, DH_B))
        outs[3].append(pp[VB, sp - tail:].reshape(bp, tail, H_B, DH_B))

        oas, obs, kroll, vroll = _attn_sample(ps, ka_s, va_s, ck_a, cache_v_a[l], cache_k_b[l], cache_v_b[l],
                                              t5_bias, rel_bias_b[l], *lam_args)
        ys = _merge(ys, oas, obs, ps, woa, wob, wout, post_norm[l], tm=bs * ss)
        outs[4].append(ka_s.reshape(bs, ss, 2 * H_A, DK_A))
        outs[5].append(va_s.reshape(bs, ss, H_A, DV_A))
        outs[6].append(kroll.reshape(bs, -1, H_B, DH_B))
        outs[7].append(vroll.reshape(bs, -1, H_B, DH_B))
    return (yp.reshape(bp, sp, d), ys.reshape(bs, ss, d)) + tuple(jnp.stack(o) for o in outs)
```

```python
import functools
import math

import jax
import jax.numpy as jnp
from jax import lax
from jax.experimental import pallas as pl
from jax.experimental.pallas import tpu as pltpu

F32 = jnp.float32
BF16 = jnp.bfloat16

CHUNK = 64
H_A = 8
DK_A = 64
DV_A = 2 * DK_A
H_B = 8
DH_B = 128
BAND_CHUNKS = 8
BAND_PAST = BAND_CHUNKS * CHUNK
REL_CLIP = 128
T5_BUCKETS = 32
T5_MAX_DIST = 128
EPS = 1e-6

HEAD_W = 128
COL_BLOCK = 1024
A_TQ, A_TK = 512, 256
B_TQ = 256
SAMPLE_HEAD_GROUP = 4
N_MERGE_WEIGHTS = 3
LOG2E = math.log2(math.e)
NEAR = max(REL_CLIP, T5_MAX_DIST)
VMEM_LIMIT = 60 * 1024 * 1024

QA, ZA, QB, KB, VB, ZB, GA0, GA1, GB0, GB1 = range(10)
COL_KA, COL_VA = 1, 2
COL_SLOT = (QA, None, None, ZA, QB, KB, VB, ZB, GA0, GA1, GB0, GB1)


def _t5_bucket_int(rel):
    half = T5_BUCKETS // 2
    max_exact = half // 2
    n = abs(rel)
    ret = half if rel > 0 else 0
    if n < max_exact:
        return ret + n
    assert (T5_MAX_DIST // max_exact) ** 2 == 2 ** (half - max_exact)
    j = 0
    while n * n >= (max_exact * max_exact) * 2 ** (j + 1):
        j += 1
    return ret + min(max_exact + j, half - 1)


def _t5_runs(lo, hi):
    runs = []
    for r in range(lo, hi + 1):
        b = _t5_bucket_int(r)
        if not runs or runs[-1][1] != b:
            runs.append((r, b))
    return runs


T5_FAR_BUCKET = _t5_bucket_int(-T5_MAX_DIST)
assert all(_t5_bucket_int(-n) == T5_FAR_BUCKET for n in range(T5_MAX_DIST, 4 * T5_MAX_DIST))


def _t5_bias_tile(rel, lo, hi, t5_ref, h):
    runs = _t5_runs(lo, hi)
    val = jnp.full(rel.shape, t5_ref[runs[0][1], h], F32)
    for start, b in runs[1:]:
        val = jnp.where(rel >= start, t5_ref[b, h], val)
    return val - t5_ref[T5_FAR_BUCKET, h]


def _rel_bias_tile(rel, lo, hi, rb_ref, h):
    lo = max(lo, -REL_CLIP)
    hi = min(hi, REL_CLIP)
    val = jnp.full(rel.shape, rb_ref[h, lo + REL_CLIP], F32)
    for d in range(lo + 1, hi + 1):
        val = jnp.where(rel >= d, rb_ref[h, d + REL_CLIP], val)
    return val


def _sigmoid(x):
    return 1.0 / (1.0 + jnp.exp(-x))


def _lambda(lq1, lk1, lq2, lk2, lam_init):
    a = jnp.sum(lq1[...] * lk1[...], axis=-1, keepdims=True)
    b = jnp.sum(lq2[...] * lk2[...], axis=-1, keepdims=True)
    return jnp.exp(a) - jnp.exp(b) + lam_init


def _in_proj_kernel(x_ref, g_ref, w_ref, o_ref, ka_ref, va_ref, h_ref):
    j = pl.program_id(1)

    @pl.when(j == 0)
    def _():
        x = x_ref[...]
        ms = jnp.mean(x * x, axis=-1, keepdims=True)
        h_ref[...] = (x * lax.rsqrt(ms + EPS) * g_ref[...]).astype(BF16)

    def project(ref):
        ref[...] = jnp.dot(h_ref[...], w_ref[...], preferred_element_type=F32)

    pl.when(j == COL_KA)(functools.partial(project, ka_ref))
    pl.when(j == COL_VA)(functools.partial(project, va_ref))
    pl.when(jnp.logical_and(j != COL_KA, j != COL_VA))(functools.partial(project, o_ref))


def _in_proj(x2d, pre_g, w_bf16, tm):
    m, d = x2d.shape
    n = w_bf16.shape[1]
    assert m % tm == 0 and n == len(COL_SLOT) * COL_BLOCK

    def own(col):
        return pl.BlockSpec((tm, COL_BLOCK), lambda i, j: (jnp.where(j >= col, i, jnp.maximum(i - 1, 0)), 0))

    def slab_index(i, j):
        slot = jnp.where(j <= COL_VA, 0, j - 2)
        return slot, i, 0

    assert COL_SLOT[0] == 0 and COL_SLOT[COL_VA + 1:] == tuple(range(1, len(COL_SLOT) - 2))
    return pl.pallas_call(
        _in_proj_kernel,
        grid=(m // tm, n // COL_BLOCK),
        in_specs=[
            pl.BlockSpec((tm, d), lambda i, j: (i, 0)),
            pl.BlockSpec((1, d), lambda i, j: (0, 0)),
            pl.BlockSpec((d, COL_BLOCK), lambda i, j: (0, j)),
        ],
        out_specs=[pl.BlockSpec((None, tm, COL_BLOCK), slab_index), own(COL_KA), own(COL_VA)],
        out_shape=[jax.ShapeDtypeStruct((len(COL_SLOT) - 2, m, COL_BLOCK), F32),
                   jax.ShapeDtypeStruct((m, COL_BLOCK), F32),
                   jax.ShapeDtypeStruct((m, COL_BLOCK), F32)],
        scratch_shapes=[pltpu.VMEM((tm, d), BF16)],
        compiler_params=pltpu.CompilerParams(
            dimension_semantics=("arbitrary", "arbitrary"), vmem_limit_bytes=VMEM_LIMIT),
        name="in_proj",
    )(x2d, pre_g.reshape(1, d), w_bf16)


def _in_proj_cast_kernel(x_ref, g_ref, w_ref, o_ref, ka_ref, va_ref, wbf_ref, h_ref):
    j = pl.program_id(0)

    @pl.when(j == 0)
    def _():
        x = x_ref[...]
        ms = jnp.mean(x * x, axis=-1, keepdims=True)
        h_ref[...] = (x * lax.rsqrt(ms + EPS) * g_ref[...]).astype(BF16)

    def project(ref):
        w = w_ref[...].astype(BF16)
        wbf_ref[...] = w
        ref[...] = jnp.dot(h_ref[...], w, preferred_element_type=F32)

    pl.when(j == COL_KA)(functools.partial(project, ka_ref))
    pl.when(j == COL_VA)(functools.partial(project, va_ref))
    pl.when(jnp.logical_and(j != COL_KA, j != COL_VA))(functools.partial(project, o_ref))


def _in_proj_cast(x2d, pre_g, w_f32):
    m, d = x2d.shape
    n = w_f32.shape[1]
    assert n == len(COL_SLOT) * COL_BLOCK
    slots = len(COL_SLOT) - 2
    resident = lambda: pl.BlockSpec((m, COL_BLOCK), lambda j: (0, 0))
    return pl.pallas_call(
        _in_proj_cast_kernel,
        grid=(len(COL_SLOT),),
        in_specs=[
            pl.BlockSpec((m, d), lambda j: (0, 0)),
            pl.BlockSpec((1, d), lambda j: (0, 0)),
            pl.BlockSpec((d, COL_BLOCK), lambda j: (0, j)),
        ],
        out_specs=[pl.BlockSpec((None, m, COL_BLOCK), lambda j: (jnp.where(j <= COL_VA, 0, j - 2), 0, 0)),
                   resident(), resident(),
                   pl.BlockSpec((d, COL_BLOCK), lambda j: (0, j))],
        out_shape=[jax.ShapeDtypeStruct((slots, m, COL_BLOCK), F32),
                   jax.ShapeDtypeStruct((m, COL_BLOCK), F32),
                   jax.ShapeDtypeStruct((m, COL_BLOCK), F32),
                   jax.ShapeDtypeStruct((d, n), BF16)],
        scratch_shapes=[pltpu.VMEM((m, d), BF16)],
        compiler_params=pltpu.CompilerParams(
            dimension_semantics=("arbitrary",), vmem_limit_bytes=VMEM_LIMIT),
        name="in_proj_cast",
    )(x2d, pre_g.reshape(1, d), w_f32)


def _online_softmax_step(s, vt, m_ref, l_ref, acc_ref):
    m_old = m_ref[...]
    m_new = jnp.maximum(m_old, jnp.max(s, axis=0, keepdims=True))
    alpha = jnp.exp2(m_old - m_new)
    p = jnp.exp2(s - m_new)
    l_ref[...] = alpha * l_ref[...] + jnp.sum(p, axis=0, keepdims=True)
    acc_ref[...] = alpha * acc_ref[...] + jnp.dot(vt, p.astype(BF16), preferred_element_type=F32)
    m_ref[...] = m_new


def _stage_keys_values(k_ref, v_ref, kbf_ref, vt_ref, t):
    def body(j, c):
        r = pl.multiple_of(j * t, t)
        kbf_ref[pl.ds(r, t), :] = k_ref[pl.ds(r, t), :].astype(BF16)
        vt_ref[j] = v_ref[pl.ds(r, t), :].astype(BF16).T
        return c
    lax.fori_loop(0, vt_ref.shape[0], body, 0, unroll=4)


def _attn_a_kernel(t5_ref, lq1, lk1, lq2, lk2, g_ref, q_ref, k_ref, v_ref, o_ref,
                   kbf_ref, vt_ref, bias_ref, qt_ref, m_ref, l_ref, acc_ref, *s_refs, lam_init):
    tq, tk = A_TQ, A_TK
    h = pl.program_id(0)
    nq = q_ref.shape[0] // tq

    _stage_keys_values(k_ref, v_ref, kbf_ref, vt_ref, tk)
    key = lax.broadcasted_iota(jnp.int32, (tk, tq), 0)
    qry = lax.broadcasted_iota(jnp.int32, (tk, tq), 1)
    for n in range(3):
        rel = key + (n - 1) * tk - qry
        lo, hi = (n - 1) * tk - (tq - 1), min(n * tk - 1, CHUNK - 1)
        b = _t5_bias_tile(jnp.minimum(rel, hi), lo, hi, t5_ref, h) * LOG2E
        if n >= 1:
            b = jnp.where((key + (n - 1) * tk) // CHUNK <= qry // CHUNK, b, -jnp.inf)
        bias_ref[n] = b

    lam = _lambda(lq1, lk1, lq2, lk2, lam_init)

    hq = tq // 2

    def start_tile(i):
        q0 = pl.multiple_of(i * tq, tq)
        qt = (q_ref[pl.ds(q0, tq), :] * (DK_A ** -0.5 * LOG2E)).T
        sub = lax.broadcasted_iota(jnp.int32, (HEAD_W, tq), 0)
        maps = (jnp.where(sub < DK_A, qt, 0.0).astype(BF16), jnp.where(sub >= DK_A, qt, 0.0).astype(BF16))
        for c in range(4):
            qt_ref[:, c * hq:(c + 1) * hq] = maps[c % 2][:, (c // 2) * hq:(c // 2 + 1) * hq]

    def reset_state():
        m_ref[...] = jnp.full(m_ref.shape, -jnp.inf, F32)
        l_ref[...] = jnp.zeros(l_ref.shape, F32)
        acc_ref[...] = jnp.zeros(acc_ref.shape, F32)

    def scores(j, s_ref, late_only=False):
        r = pl.multiple_of(j * tk, tk)
        c0 = tq if late_only else 0
        s_ref[:, c0:] = jnp.dot(kbf_ref[pl.ds(r, tk), :], qt_ref[:, c0:], preferred_element_type=F32)

    def update(j, s_ref, bias_idx, late_only=False):
        c0 = tq if late_only else 0
        s = s_ref[:, c0:]
        if bias_idx is not None:
            b = bias_ref[bias_idx]
            halves = [b[:, (c // 2) * hq:(c // 2 + 1) * hq] for c in range(c0 // hq, 4)]
            s = jnp.concatenate([s[:, n * hq:(n + 1) * hq] + bh for n, bh in enumerate(halves)], axis=1)
        _online_softmax_step(s, vt_ref[j], m_ref.at[:, c0:], l_ref.at[:, c0:], acc_ref.at[:, c0:])

    def q_tile(i, carry):
        odd = jnp.logical_and(i >= 2, i % 2 == 0)

        @pl.when(odd)
        def _():
            update(0, s_refs[0], None)
            scores(2, s_refs[0])
            update(1, s_refs[1], None)
            scores(3, s_refs[1])

        j0 = jnp.where(odd, 2, 0)

        def quad(j):
            for n in range(4):
                scores(j + n + 2, s_refs[(n + 2) % 4])
                update(j + n, s_refs[n], None)

        def two_quads(u, c):
            quad(j0 + 8 * u)
            quad(j0 + 8 * u + 4)
            return c
        quads = jnp.maximum(i - 1, 0) // 2
        lax.fori_loop(0, quads // 2, two_quads, 0)

        @pl.when(quads % 2 == 1)
        def _():
            quad(j0 + 4 * (quads - 1))

        nxt = jnp.minimum(i + 1, nq - 1)

        @pl.when(i >= 1)
        def _():
            j = 2 * i - 2
            scores(j + 2, s_refs[2])
            update(j, s_refs[0], None)
            scores(j + 3, s_refs[3], late_only=True)
            update(j + 1, s_refs[1], 0)
            start_tile(nxt)
            scores(0, s_refs[0])
            update(j + 2, s_refs[2], 1)
            scores(1, s_refs[1])
            update(j + 3, s_refs[3], 2, late_only=True)

        @pl.when(i == 0)
        def _():
            update(0, s_refs[0], 1)
            start_tile(nxt)
            scores(0, s_refs[0])
            update(1, s_refs[1], 2, late_only=True)
            scores(1, s_refs[1])

        inv = 1.0 / l_ref[...]
        acc = acc_ref[...]
        o = jnp.concatenate(
            [acc[:, c:c + hq] * inv[:, c:c + hq] - lam * (acc[:, c + hq:c + tq] * inv[:, c + hq:c + tq])
             for c in (0, tq)], axis=1)
        ms = jnp.mean(o * o, axis=0, keepdims=True)
        y = (o * lax.rsqrt(ms + EPS) * g_ref[...]) * (1.0 - lam_init)
        o_ref[pl.ds(pl.multiple_of(i * tq, tq), tq), :] = y.T
        reset_state()
        return carry

    start_tile(0)
    scores(0, s_refs[0])
    scores(1, s_refs[1])
    reset_state()
    lax.fori_loop(0, nq, q_tile, 0)


def _attn_a_prompt(proj, k_a, v_a, t5_bias, lq1, lk1, lq2, lk2, subln_g, lam_init):
    s = proj.shape[1]
    tq, tk = A_TQ, A_TK
    assert s % tq == 0 and tq == 2 * tk and tk % CHUNK == 0 and tk >= T5_MAX_DIST
    vec = lambda: pl.BlockSpec((1, DK_A), lambda h: (0, 0))
    head = lambda: pl.BlockSpec((s, HEAD_W), lambda h: (0, h))
    return pl.pallas_call(
        functools.partial(_attn_a_kernel, lam_init=lam_init),
        grid=(H_A,),
        in_specs=[
            pl.BlockSpec(memory_space=pltpu.SMEM),
            vec(), vec(), vec(), vec(),
            pl.BlockSpec((DV_A, 1), lambda h: (0, 0)),
            pl.BlockSpec((None, s, HEAD_W), lambda h: (QA, 0, h)), head(), head(),
        ],
        out_specs=pl.BlockSpec((s, HEAD_W), lambda h: (0, h)),
        out_shape=jax.ShapeDtypeStruct((s, H_A * DV_A), F32),
        scratch_shapes=[
            pltpu.VMEM((s, HEAD_W), BF16),
            pltpu.VMEM((s // tk, DV_A, tk), BF16),
            pltpu.VMEM((3, tk, tq), F32),
            pltpu.VMEM((HEAD_W, 2 * tq), BF16),
            pltpu.VMEM((1, 2 * tq), F32),
            pltpu.VMEM((1, 2 * tq), F32),
            pltpu.VMEM((DV_A, 2 * tq), F32),
        ] + [pltpu.VMEM((tk, 2 * tq), F32)] * 4,
        compiler_params=pltpu.CompilerParams(
            dimension_semantics=("arbitrary",), vmem_limit_bytes=VMEM_LIMIT),
        name="attn_a_prompt",
    )(t5_bias, lq1.reshape(1, DK_A), lk1.reshape(1, DK_A), lq2.reshape(1, DK_A), lk2.reshape(1, DK_A),
      subln_g.reshape(DV_A, 1), proj, k_a, v_a)


def _attn_b_kernel(rb_ref, q_ref, k_ref, v_ref, *rest):
    n_w = N_MERGE_WEIGHTS
    w_refs, (o_ref, ktail_ref, vtail_ref), wbf_refs = rest[:n_w], rest[n_w:n_w + 3], rest[n_w + 3:2 * n_w + 3]
    kbf_ref, vt_ref, bias_ref = rest[2 * n_w + 3:2 * n_w + 6]
    s_refs = rest[2 * n_w + 6:]
    t = B_TQ
    blk = REL_CLIP
    nkt = BAND_PAST // t + 1
    nq = q_ref.shape[0] // t
    h = pl.program_id(0)

    for w_ref, wbf_ref in zip(w_refs, wbf_refs):
        wbf_ref[...] = w_ref[...].astype(BF16)
    tail = ktail_ref.shape[0]
    ktail_ref[...] = k_ref[k_ref.shape[0] - tail:, :]
    vtail_ref[...] = v_ref[v_ref.shape[0] - tail:, :]

    _stage_keys_values(k_ref, v_ref, kbf_ref, vt_ref, t)

    kk = lax.broadcasted_iota(jnp.int32, (blk, blk), 0)
    qq = lax.broadcasted_iota(jnp.int32, (blk, blk), 1)
    rel = kk - qq
    lo = jnp.full((blk, blk), rb_ref[h, 0] * LOG2E, F32)
    same = _rel_bias_tile(rel, -(blk - 1), blk - 1, rb_ref, h) * LOG2E
    prev = _rel_bias_tile(rel - blk, -(2 * blk - 1), -1, rb_ref, h) * LOG2E
    ninf = jnp.full((blk, blk), -jnp.inf, F32)
    kc = kk // CHUNK
    qc = qq // CHUNK
    far_blocks = BAND_PAST // blk
    for a in range(nkt * t // blk):
        for b in range(t // blk):
            e = a - b
            if e < 0 or e > far_blocks:
                tile = ninf
            elif e == 0:
                tile = jnp.where(kc >= qc, lo, -jnp.inf)
            elif e == far_blocks:
                tile = jnp.where(kc <= qc, same, -jnp.inf)
            elif e == far_blocks - 1:
                tile = prev
            else:
                tile = lo
            bias_ref[a * blk:(a + 1) * blk, b * blk:(b + 1) * blk] = tile

    def scores(g, nk, s_ref):
        q0 = pl.multiple_of(g * t, t)
        k0 = pl.multiple_of((g - (nk - 1)) * t, t)
        qt = (q_ref[pl.ds(q0, t), :] * (DH_B ** -0.5 * LOG2E)).T.astype(BF16)
        s_ref[(nkt - nk) * t:, :] = jnp.dot(kbf_ref[pl.ds(k0, nk * t), :], qt, preferred_element_type=F32)

    def finish(g, nk, s_ref):
        s = s_ref[(nkt - nk) * t:, :] + bias_ref[(nkt - nk) * t:, :]
        m = jnp.max(s, axis=0, keepdims=True)
        p = jnp.exp2(s - m)
        l = jnp.sum(p, axis=0, keepdims=True)
        pb = p.astype(BF16)
        o = jnp.dot(vt_ref[g - (nk - 1)], pb[:t], preferred_element_type=F32)
        for c in range(1, nk):
            o = o + jnp.dot(vt_ref[g - (nk - 1) + c], pb[c * t:(c + 1) * t], preferred_element_type=F32)
        o_ref[pl.ds(pl.multiple_of(g * t, t), t), :] = (o * (1.0 / l)).T

    assert nq % 4 == 0 and nq >= 8 and nkt <= 4

    def quad(g, static_start=False, lookahead=True):
        for n in range(4):
            if lookahead or n < 2:
                scores(g + n + 2, nkt, s_refs[(n + 2) % 4])
            finish(g + n, min(n + 1, nkt) if static_start else nkt, s_refs[n])

    def two_quads(u, c):
        quad(4 + 8 * u)
        quad(8 + 8 * u)
        return c

    scores(0, 1, s_refs[0])
    scores(1, min(2, nkt), s_refs[1])
    quad(0, static_start=True)
    quads = nq // 4 - 2
    lax.fori_loop(0, quads // 2, two_quads, 0)
    if quads % 2:
        quad(4 * quads)
    quad(nq - 4, lookahead=False)


def _attn_b_prompt(proj, rel_bias, merge_weights, tail):
    s = proj.shape[1]
    assert tail % 8 == 0 and tail <= s
    t = B_TQ
    nkt = BAND_PAST // t + 1
    assert s % t == 0 and BAND_PAST % t == 0 and t % REL_CLIP == 0 and REL_CLIP % CHUNK == 0
    head = lambda c: pl.BlockSpec((None, s, HEAD_W), lambda h: (c, 0, h))
    assert len(merge_weights) == N_MERGE_WEIGHTS and all(w.shape[0] % (8 * H_B) == 0 for w in merge_weights)
    rows = lambda w: pl.BlockSpec((w.shape[0] // H_B, w.shape[1]), lambda h: (h, 0))
    outs = pl.pallas_call(
        _attn_b_kernel,
        grid=(H_B,),
        in_specs=[pl.BlockSpec(memory_space=pltpu.SMEM), head(QB), head(KB), head(VB)]
        + [rows(w) for w in merge_weights],
        out_specs=[pl.BlockSpec((s, HEAD_W), lambda h: (0, h))]
        + [pl.BlockSpec((tail, HEAD_W), lambda h: (0, h))] * 2 + [rows(w) for w in merge_weights],
        out_shape=[jax.ShapeDtypeStruct((s, H_B * DH_B), F32)]
        + [jax.ShapeDtypeStruct((tail, H_B * DH_B), F32)] * 2
        + [jax.ShapeDtypeStruct(w.shape, BF16) for w in merge_weights],
        scratch_shapes=[
            pltpu.VMEM((s, HEAD_W), BF16),
            pltpu.VMEM((s // t, DH_B, t), BF16),
            pltpu.VMEM((nkt * t, t), F32),
        ] + [pltpu.VMEM((nkt * t, t), F32)] * 4,
        compiler_params=pltpu.CompilerParams(
            dimension_semantics=("arbitrary",), vmem_limit_bytes=VMEM_LIMIT),
        name="attn_b_prompt",
    )(rel_bias, proj, proj, proj, *merge_weights)
    return outs[0], outs[1:3], outs[3:]


def _dot_nt(a, b):
    return lax.dot_general(a, b, (((1,), (1,)), ((), ())), preferred_element_type=F32)


def _attn_sample_kernel(t5_ref, rb_ref, lq1, lk1, lq2, lk2, g_ref,
                        qa_ref, ka_ref, va_ref, cka_ref, cva_ref,
                        qb_ref, kb_ref, vb_ref, ckb_ref, cvb_ref,
                        oa_ref, ob_ref, kroll_ref, vroll_ref,
                        ba_ref, bb_ref, *, lam_init, past, win):
    n = qa_ref.shape[0]
    near = NEAR
    nh = H_A

    @pl.when(pl.program_id(0) == 0)
    def _():
        qry = lax.broadcasted_iota(jnp.int32, (n, near + n), 0)
        key = lax.broadcasted_iota(jnp.int32, (n, near + n), 1)
        rel = key - near - qry
        for h in range(nh):
            ba_ref[h] = _t5_bias_tile(rel, -(near + n - 1), n - 1, t5_ref, h)
            bb_ref[h] = _rel_bias_tile(rel, -(near + n - 1), n - 1, rb_ref, h)

    kroll_ref[:(win - n) * nh, :] = ckb_ref[n * nh:, :]
    vroll_ref[:(win - n) * nh, :] = cvb_ref[n * nh:, :]

    lam = _lambda(lq1, lk1, lq2, lk2, lam_init)
    lane = lax.broadcasted_iota(jnp.int32, (n, HEAD_W), 1)

    def cols(h):
        return slice(h * HEAD_W, (h + 1) * HEAD_W)

    def scores_a(h):
        q = qa_ref[:, cols(h)] * (DK_A ** -0.5)
        q2 = jnp.concatenate([jnp.where(lane < DK_A, q, 0.0), jnp.where(lane >= DK_A, q, 0.0)],
                             axis=0).astype(BF16)
        ba = ba_ref[h]
        ba2 = jnp.concatenate([ba, ba], axis=0)
        s_c = _dot_nt(q2, cka_ref[:, cols(h)].astype(BF16))
        s_c = jnp.concatenate([s_c[:, :past - near], s_c[:, past - near:] + ba2[:, :near]], axis=1)
        s_n = _dot_nt(q2, ka_ref[:, cols(h)].astype(BF16)) + ba2[:, near:]
        return s_c, s_n

    def scores_b(h):
        qb = qb_ref[:, cols(h)].astype(BF16)
        bb = bb_ref[h]
        ckb = ckb_ref[pl.ds(h, win, stride=nh), :].astype(BF16)
        s_c = _dot_nt(qb, ckb) * (DH_B ** -0.5)
        s_c = jnp.concatenate([s_c[:, :win - near] + rb_ref[h, 0], s_c[:, win - near:] + bb[:, :near]], axis=1)
        s_n = _dot_nt(qb, kb_ref[:, cols(h)].astype(BF16)) * (DH_B ** -0.5) + bb[:, near:]
        return s_c, s_n

    def softmax(s_c, s_n):
        m = jnp.maximum(jnp.max(s_c, axis=-1, keepdims=True), jnp.max(s_n, axis=-1, keepdims=True))
        p_c = jnp.exp(s_c - m)
        p_n = jnp.exp(s_n - m)
        l = jnp.sum(p_c, axis=-1, keepdims=True) + jnp.sum(p_n, axis=-1, keepdims=True)
        return p_c.astype(BF16), p_n.astype(BF16), 1.0 / l

    def finish_a(h, p_c, p_n, inv):
        cva = cva_ref[pl.ds(h, past, stride=nh), :].astype(BF16)
        o2 = (jnp.dot(p_c, cva, preferred_element_type=F32)
              + jnp.dot(p_n, va_ref[:, cols(h)].astype(BF16), preferred_element_type=F32)) * inv
        o = o2[:n] - lam * o2[n:]
        ms = jnp.mean(o * o, axis=-1, keepdims=True)
        oa_ref[:, cols(h)] = (o * lax.rsqrt(ms + EPS) * g_ref[...]) * (1.0 - lam_init)

    def finish_b(h, p_c, p_n, inv):
        cvb = cvb_ref[pl.ds(h, win, stride=nh), :].astype(BF16)
        ob_ref[:, cols(h)] = (jnp.dot(p_c, cvb, preferred_element_type=F32)
                              + jnp.dot(p_n, vb_ref[:, cols(h)].astype(BF16), preferred_element_type=F32)) * inv

    for first in range(0, nh, SAMPLE_HEAD_GROUP):
        heads = range(first, first + SAMPLE_HEAD_GROUP)
        for h in heads:
            new_rows = pl.ds((win - n) * nh + h, n, stride=nh)
            kroll_ref[new_rows, :] = kb_ref[:, cols(h)]
            vroll_ref[new_rows, :] = vb_ref[:, cols(h)]
        s_a = [scores_a(h) for h in heads]
        s_b = [scores_b(h) for h in heads]
        p_a = [softmax(*s) for s in s_a]
        p_b = [softmax(*s) for s in s_b]
        for h, p in zip(heads, p_a):
            finish_a(h, *p)
        for h, p in zip(heads, p_b):
            finish_b(h, *p)


def _attn_sample(proj, k_a, v_a, ck_a, cv_a, ck_b, cv_b, t5_bias, rel_bias, lq1, lk1, lq2, lk2, subln_g,
                 lam_init):
    nb, past = ck_a.shape[0], ck_a.shape[1]
    win = ck_b.shape[1]
    n = proj.shape[1] // nb
    near = NEAR
    wide = H_A * HEAD_W
    assert past % CHUNK == 0 and n <= CHUNK and win <= BAND_PAST and win <= past
    assert near % HEAD_W == 0 and near <= win and near <= past and n % 8 == 0 and T5_MAX_DIST <= REL_CLIP
    assert H_A == H_B and DV_A == HEAD_W and DH_B == HEAD_W and 2 * DK_A == HEAD_W
    cka = ck_a.reshape(nb, past, wide)
    cva = cv_a.reshape(nb, past * H_A, DV_A)
    ckb = ck_b.reshape(nb, win * H_B, DH_B)
    cvb = cv_b.reshape(nb, win * H_B, DH_B)
    vec = lambda: pl.BlockSpec((1, DK_A), lambda b: (0, 0))
    new = lambda c: pl.BlockSpec((None, n, wide), lambda b: (c, b, 0))
    seq = lambda rows, cols: pl.BlockSpec((None, rows, cols), lambda b: (b, 0, 0))
    out = pl.BlockSpec((n, wide), lambda b: (b, 0))
    return pl.pallas_call(
        functools.partial(_attn_sample_kernel, lam_init=lam_init, past=past, win=win),
        grid=(nb,),
        in_specs=[
            pl.BlockSpec(memory_space=pltpu.SMEM),
            pl.BlockSpec(memory_space=pltpu.SMEM),
            vec(), vec(), vec(), vec(),
            pl.BlockSpec((1, DV_A), lambda b: (0, 0)),
            new(QA), out, out, seq(past, wide), seq(past * H_A, HEAD_W),
            new(QB), new(KB), new(VB), seq(win * H_B, HEAD_W), seq(win * H_B, HEAD_W),
        ],
        out_specs=[out, out, seq(win * H_B, HEAD_W), seq(win * H_B, HEAD_W)],
        out_shape=[
            jax.ShapeDtypeStruct((nb * n, wide), F32),
            jax.ShapeDtypeStruct((nb * n, wide), F32),
            jax.ShapeDtypeStruct((nb, win * H_B, DH_B), F32),
            jax.ShapeDtypeStruct((nb, win * H_B, DH_B), F32),
        ],
        scratch_shapes=[pltpu.VMEM((H_A, n, near + n), F32), pltpu.VMEM((H_B, n, near + n), F32)],
        compiler_params=pltpu.CompilerParams(
            dimension_semantics=("arbitrary",), vmem_limit_bytes=VMEM_LIMIT),
        name="attn_sample",
    )(t5_bias, rel_bias, lq1.reshape(1, DK_A), lk1.reshape(1, DK_A), lq2.reshape(1, DK_A),
      lk2.reshape(1, DK_A), subln_g.reshape(1, DV_A),
      proj, k_a, v_a, cka, cva, proj, proj, proj, ckb, cvb)


def _merge_kernel(x_ref, oa_ref, ob_ref, za_ref, zb_ref, ga0_ref, ga1_ref, gb0_ref, gb1_ref,
                  woa_ref, wob_ref, wout_ref, pg_ref, y_ref):
    za = za_ref[...]
    zb = zb_ref[...]
    a = (oa_ref[...] * (za * _sigmoid(za))).astype(BF16)
    b = (ob_ref[...] * (zb * _sigmoid(zb))).astype(BF16)
    ya = jnp.dot(a, woa_ref[...], preferred_element_type=F32)
    yb = jnp.dot(b, wob_ref[...], preferred_element_type=F32)
    ga = jnp.concatenate([ga0_ref[...], ga1_ref[...]], axis=1)
    gb = jnp.concatenate([gb0_ref[...], gb1_ref[...]], axis=1)
    mix = (_sigmoid(ga) * ya + _sigmoid(gb) * yb).astype(BF16)
    y = jnp.dot(mix, wout_ref[...], preferred_element_type=F32)
    ms = jnp.mean(y * y, axis=-1, keepdims=True)
    y_ref[...] = x_ref[...] + y * lax.rsqrt(ms + EPS) * pg_ref[...]


def _merge(x2d, o_a, o_b, proj, woa, wob, wout, post_g, tm):
    m, d = x2d.shape
    wa = o_a.shape[1]
    wb = o_b.shape[1]
    assert m % tm == 0 and wa == COL_BLOCK and wb == COL_BLOCK and d == 2 * COL_BLOCK
    row = lambda w: pl.BlockSpec((tm, w), lambda i: (i, 0))
    col = lambda c: pl.BlockSpec((None, tm, COL_BLOCK), lambda i: (c, i, 0))
    resident = lambda r, c: pl.BlockSpec((r, c), lambda i: (0, 0), pipeline_mode=pl.Buffered(1))
    return pl.pallas_call(
        _merge_kernel,
        grid=(m // tm,),
        in_specs=[row(d), row(wa), row(wb), col(ZA), col(ZB), col(GA0), col(GA1), col(GB0), col(GB1),
                  resident(wa, d), resident(wb, d), resident(d, d), resident(1, d)],
        out_specs=row(d),
        out_shape=jax.ShapeDtypeStruct((m, d), F32),
        compiler_params=pltpu.CompilerParams(
            dimension_semantics=("arbitrary",), vmem_limit_bytes=VMEM_LIMIT),
        name="merge",
    )(x2d, o_a, o_b, proj, proj, proj, proj, proj, proj, woa, wob, wout, post_g.reshape(1, d))


def kernel(x_prompt, x_sample, cache_k_a, cache_v_a, cache_k_b, cache_v_b, t5_bias, pre_norm, post_norm,
           w_in, lambda_q1, lambda_k1, lambda_q2, lambda_k2, subln_a, rel_bias_b, w_o_a, w_o_b, w_out):
    depth = w_in.shape[0]
    bp, sp, d = x_prompt.shape
    bs, ss, _ = x_sample.shape
    assert bp == 1 and w_in.shape[2] == 12 * COL_BLOCK
    yp = x_prompt.reshape(sp, d)
    ys = x_sample.reshape(bs * ss, d)
    tail = min(BAND_PAST, sp)
    outs = [[] for _ in range(8)]
    for l in range(depth):
        lam_init = 0.8 - 0.6 * math.exp(-0.3 * l)
        lam_args = (lambda_q1[l], lambda_k1[l], lambda_q2[l], lambda_k2[l], subln_a[l], lam_init)

        ps, ka_s, va_s, w = _in_proj_cast(ys, pre_norm[l], w_in[l])
        yp, ps = lax.optimization_barrier((yp, ps))
        pp, ka, va = _in_proj(yp, pre_norm[l], w, tm=1024)
        ob, (kb_tail, vb_tail), (woa, wob, wout) = _attn_b_prompt(
            pp, rel_bias_b[l], (w_o_a[l], w_o_b[l], w_out[l]), tail)
        ck_a, ob = lax.optimization_barrier((cache_k_a[l], ob))
        oa = _attn_a_prompt(pp, ka, va, t5_bias, *lam_args)
        yp = _merge(yp, oa, ob, pp, woa, wob, wout, post_norm[l], tm=256)
        outs[0].append(ka.reshape(bp, sp, 2 * H_A, DK_A))
        outs[1].append(va.reshape(bp, sp, H_A, DV_A))
        outs[2].append(kb_tail.reshape(bp, tail, H_B, DH_B))
        outs[3].append(vb_tail.reshape(bp, tail, H_B, DH_B))

        oas, obs, kroll, vroll = _attn_sample(ps, ka_s, va_s, ck_a, cache_v_a[l], cache_k_b[l], cache_v_b[l],
                                              t5_bias, rel_bias_b[l], *lam_args)
        ys = _merge(ys, oas, obs, ps, woa, wob, wout, post_norm[l], tm=bs * ss)
        outs[4].append(ka_s.reshape(bs, ss, 2 * H_A, DK_A))
        outs[5].append(va_s.reshape(bs, ss, H_A, DV_A))
        outs[6].append(kroll.reshape(bs, -1, H_B, DH_B))
        outs[7].append(vroll.reshape(bs, -1, H_B, DH_B))
    return (yp.reshape(bp, sp, d), ys.reshape(bs, ss, d)) + tuple(jnp.stack(o) for o in outs)
```

```python
import functools
import math

import jax
import jax.numpy as jnp
from jax import lax
from jax.experimental import pallas as pl
from jax.experimental.pallas import tpu as pltpu

F32 = jnp.float32
BF16 = jnp.bfloat16

CHUNK = 64
H_A = 8
DK_A = 64
DV_A = 2 * DK_A
H_B = 8
DH_B = 128
BAND_CHUNKS = 8
BAND_PAST = BAND_CHUNKS * CHUNK
REL_CLIP = 128
T5_BUCKETS = 32
T5_MAX_DIST = 128
EPS = 1e-6

HEAD_W = 128
COL_BLOCK = 1024
A_TQ, A_TK = 512, 256
B_TQ = 256
SAMPLE_HEAD_GROUP = 4
N_MERGE_WEIGHTS = 3
LOG2E = math.log2(math.e)
NEAR = max(REL_CLIP, T5_MAX_DIST)
VMEM_LIMIT = 60 * 1024 * 1024

QA, ZA, QB, KB, VB, ZB, GA0, GA1, GB0, GB1 = range(10)
COL_KA, COL_VA = 1, 2
N_OWN_COLS = 2
COL_SLOT = (QA, None, None, ZA, QB, KB, VB, ZB, GA0, GA1, GB0, GB1)


def _t5_bucket_int(rel):
    half = T5_BUCKETS // 2
    max_exact = half // 2
    n = abs(rel)
    ret = half if rel > 0 else 0
    if n < max_exact:
        return ret + n
    assert (T5_MAX_DIST // max_exact) ** 2 == 2 ** (half - max_exact)
    j = 0
    while n * n >= (max_exact * max_exact) * 2 ** (j + 1):
        j += 1
    return ret + min(max_exact + j, half - 1)


def _t5_runs(lo, hi):
    runs = []
    for r in range(lo, hi + 1):
        b = _t5_bucket_int(r)
        if not runs or runs[-1][1] != b:
            runs.append((r, b))
    return runs


T5_FAR_BUCKET = _t5_bucket_int(-T5_MAX_DIST)
assert all(_t5_bucket_int(-n) == T5_FAR_BUCKET for n in range(T5_MAX_DIST, 4 * T5_MAX_DIST))


def _t5_bias_tile(rel, lo, hi, t5_ref, h):
    runs = _t5_runs(lo, hi)
    val = jnp.full(rel.shape, t5_ref[runs[0][1], h], F32)
    for start, b in runs[1:]:
        val = jnp.where(rel >= start, t5_ref[b, h], val)
    return val - t5_ref[T5_FAR_BUCKET, h]


def _rel_bias_tile(rel, lo, hi, rb_ref, h):
    lo = max(lo, -REL_CLIP)
    hi = min(hi, REL_CLIP)
    val = jnp.full(rel.shape, rb_ref[h, lo + REL_CLIP], F32)
    for d in range(lo + 1, hi + 1):
        val = jnp.where(rel >= d, rb_ref[h, d + REL_CLIP], val)
    return val


def _sigmoid(x):
    return 1.0 / (1.0 + jnp.exp(-x))


def _lambda(lq1, lk1, lq2, lk2, lam_init):
    a = jnp.sum(lq1[...] * lk1[...], axis=-1, keepdims=True)
    b = jnp.sum(lq2[...] * lk2[...], axis=-1, keepdims=True)
    return jnp.exp(a) - jnp.exp(b) + lam_init


def _in_proj_kernel(x_ref, g_ref, w_ref, o_ref, ka_ref, va_ref, h_ref):
    j = pl.program_id(1)

    @pl.when(j == 0)
    def _():
        x = x_ref[...]
        ms = jnp.mean(x * x, axis=-1, keepdims=True)
        h_ref[...] = (x * lax.rsqrt(ms + EPS) * g_ref[...]).astype(BF16)

    def project(ref):
        ref[...] = jnp.dot(h_ref[...], w_ref[...], preferred_element_type=F32)

    pl.when(j == COL_KA)(functools.partial(project, ka_ref))
    pl.when(j == COL_VA)(functools.partial(project, va_ref))
    pl.when(jnp.logical_and(j != COL_KA, j != COL_VA))(functools.partial(project, o_ref))


def _in_proj(x2d, pre_g, w_bf16, tm):
    m, d = x2d.shape
    n = w_bf16.shape[1]
    assert m % tm == 0 and n == len(COL_SLOT) * COL_BLOCK

    def own(col):
        return pl.BlockSpec((tm, COL_BLOCK), lambda i, j: (jnp.where(j >= col, i, jnp.maximum(i - 1, 0)), 0))

    def slab_index(i, j):
        slot = jnp.where(j <= COL_VA, 0, j - N_OWN_COLS)
        return slot, i, 0

    assert COL_SLOT[0] == 0 and COL_SLOT[COL_VA + 1:] == tuple(range(1, len(COL_SLOT) - N_OWN_COLS))
    return pl.pallas_call(
        _in_proj_kernel,
        grid=(m // tm, n // COL_BLOCK),
        in_specs=[
            pl.BlockSpec((tm, d), lambda i, j: (i, 0)),
            pl.BlockSpec((1, d), lambda i, j: (0, 0)),
            pl.BlockSpec((d, COL_BLOCK), lambda i, j: (0, j)),
        ],
        out_specs=[pl.BlockSpec((None, tm, COL_BLOCK), slab_index), own(COL_KA), own(COL_VA)],
        out_shape=[jax.ShapeDtypeStruct((len(COL_SLOT) - N_OWN_COLS, m, COL_BLOCK), F32),
                   jax.ShapeDtypeStruct((m, COL_BLOCK), F32),
                   jax.ShapeDtypeStruct((m, COL_BLOCK), F32)],
        scratch_shapes=[pltpu.VMEM((tm, d), BF16)],
        compiler_params=pltpu.CompilerParams(
            dimension_semantics=("arbitrary", "arbitrary"), vmem_limit_bytes=VMEM_LIMIT),
        name="in_proj",
    )(x2d, pre_g.reshape(1, d), w_bf16)


def _in_proj_cast_kernel(x_ref, g_ref, w_ref, o_ref, ka_ref, va_ref, wbf_ref, h_ref):
    j = pl.program_id(0)

    @pl.when(j == 0)
    def _():
        x = x_ref[...]
        ms = jnp.mean(x * x, axis=-1, keepdims=True)
        h_ref[...] = (x * lax.rsqrt(ms + EPS) * g_ref[...]).astype(BF16)

    def project(ref):
        w = w_ref[...].astype(BF16)
        wbf_ref[...] = w
        ref[...] = jnp.dot(h_ref[...], w, preferred_element_type=F32)

    pl.when(j == COL_KA)(functools.partial(project, ka_ref))
    pl.when(j == COL_VA)(functools.partial(project, va_ref))
    pl.when(jnp.logical_and(j != COL_KA, j != COL_VA))(functools.partial(project, o_ref))


def _in_proj_cast(x2d, pre_g, w_f32):
    m, d = x2d.shape
    n = w_f32.shape[1]
    assert n == len(COL_SLOT) * COL_BLOCK
    slots = len(COL_SLOT) - N_OWN_COLS
    resident = lambda: pl.BlockSpec((m, COL_BLOCK), lambda j: (0, 0))
    return pl.pallas_call(
        _in_proj_cast_kernel,
        grid=(len(COL_SLOT),),
        in_specs=[
            pl.BlockSpec((m, d), lambda j: (0, 0)),
            pl.BlockSpec((1, d), lambda j: (0, 0)),
            pl.BlockSpec((d, COL_BLOCK), lambda j: (0, j)),
        ],
        out_specs=[pl.BlockSpec((None, m, COL_BLOCK), lambda j: (jnp.where(j <= COL_VA, 0, j - N_OWN_COLS), 0, 0)),
                   resident(), resident(),
                   pl.BlockSpec((d, COL_BLOCK), lambda j: (0, j))],
        out_shape=[jax.ShapeDtypeStruct((slots, m, COL_BLOCK), F32),
                   jax.ShapeDtypeStruct((m, COL_BLOCK), F32),
                   jax.ShapeDtypeStruct((m, COL_BLOCK), F32),
                   jax.ShapeDtypeStruct((d, n), BF16)],
        scratch_shapes=[pltpu.VMEM((m, d), BF16)],
        compiler_params=pltpu.CompilerParams(
            dimension_semantics=("arbitrary",), vmem_limit_bytes=VMEM_LIMIT),
        name="in_proj_cast",
    )(x2d, pre_g.reshape(1, d), w_f32)


def _online_softmax_step(s, vt, m_ref, l_ref, acc_ref):
    m_old = m_ref[...]
    m_new = jnp.maximum(m_old, jnp.max(s, axis=0, keepdims=True))
    alpha = jnp.exp2(m_old - m_new)
    p = jnp.exp2(s - m_new)
    l_ref[...] = alpha * l_ref[...] + jnp.sum(p, axis=0, keepdims=True)
    acc_ref[...] = alpha * acc_ref[...] + jnp.dot(vt, p.astype(BF16), preferred_element_type=F32)
    m_ref[...] = m_new


def _stage_keys_values(k_ref, v_ref, kbf_ref, vt_ref, t):
    def body(j, c):
        r = pl.multiple_of(j * t, t)
        kbf_ref[pl.ds(r, t), :] = k_ref[pl.ds(r, t), :].astype(BF16)
        vt_ref[j] = v_ref[pl.ds(r, t), :].astype(BF16).T
        return c
    lax.fori_loop(0, vt_ref.shape[0], body, 0, unroll=4)


def _attn_a_kernel(t5_ref, lq1, lk1, lq2, lk2, g_ref, q_ref, k_ref, v_ref, o_ref,
                   kbf_ref, vt_ref, bias_ref, qt_ref, m_ref, l_ref, acc_ref, *s_refs, lam_init):
    tq, tk = A_TQ, A_TK
    h = pl.program_id(0)
    nq = q_ref.shape[0] // tq

    _stage_keys_values(k_ref, v_ref, kbf_ref, vt_ref, tk)
    key = lax.broadcasted_iota(jnp.int32, (tk, tq), 0)
    qry = lax.broadcasted_iota(jnp.int32, (tk, tq), 1)
    for n in range(3):
        rel = key + (n - 1) * tk - qry
        lo, hi = (n - 1) * tk - (tq - 1), min(n * tk - 1, CHUNK - 1)
        b = _t5_bias_tile(jnp.minimum(rel, hi), lo, hi, t5_ref, h) * LOG2E
        if n >= 1:
            b = jnp.where((key + (n - 1) * tk) // CHUNK <= qry // CHUNK, b, -jnp.inf)
        bias_ref[n] = b

    lam = _lambda(lq1, lk1, lq2, lk2, lam_init)

    hq = tq // 2

    def start_tile(i):
        q0 = pl.multiple_of(i * tq, tq)
        qt = (q_ref[pl.ds(q0, tq), :] * (DK_A ** -0.5 * LOG2E)).T
        sub = lax.broadcasted_iota(jnp.int32, (HEAD_W, tq), 0)
        maps = (jnp.where(sub < DK_A, qt, 0.0).astype(BF16), jnp.where(sub >= DK_A, qt, 0.0).astype(BF16))
        for c in range(4):
            qt_ref[:, c * hq:(c + 1) * hq] = maps[c % 2][:, (c // 2) * hq:(c // 2 + 1) * hq]

    def reset_state():
        m_ref[...] = jnp.full(m_ref.shape, -jnp.inf, F32)
        l_ref[...] = jnp.zeros(l_ref.shape, F32)
        acc_ref[...] = jnp.zeros(acc_ref.shape, F32)

    def scores(j, s_ref, late_only=False):
        r = pl.multiple_of(j * tk, tk)
        c0 = tq if late_only else 0
        s_ref[:, c0:] = jnp.dot(kbf_ref[pl.ds(r, tk), :], qt_ref[:, c0:], preferred_element_type=F32)

    def update(j, s_ref, bias_idx, late_only=False):
        c0 = tq if late_only else 0
        s = s_ref[:, c0:]
        if bias_idx is not None:
            b = bias_ref[bias_idx]
            halves = [b[:, (c // 2) * hq:(c // 2 + 1) * hq] for c in range(c0 // hq, 4)]
            s = jnp.concatenate([s[:, n * hq:(n + 1) * hq] + bh for n, bh in enumerate(halves)], axis=1)
        _online_softmax_step(s, vt_ref[j], m_ref.at[:, c0:], l_ref.at[:, c0:], acc_ref.at[:, c0:])

    def q_tile(i, carry):
        odd = jnp.logical_and(i >= 2, i % 2 == 0)

        @pl.when(odd)
        def _():
            update(0, s_refs[0], None)
            scores(2, s_refs[0])
            update(1, s_refs[1], None)
            scores(3, s_refs[1])

        j0 = jnp.where(odd, 2, 0)

        def quad(j):
            for n in range(4):
                scores(j + n + 2, s_refs[(n + 2) % 4])
                update(j + n, s_refs[n], None)

        def two_quads(u, c):
            quad(j0 + 8 * u)
            quad(j0 + 8 * u + 4)
            return c
        quads = jnp.maximum(i - 1, 0) // 2
        lax.fori_loop(0, quads // 2, two_quads, 0)

        @pl.when(quads % 2 == 1)
        def _():
            quad(j0 + 4 * (quads - 1))

        nxt = jnp.minimum(i + 1, nq - 1)

        @pl.when(i >= 1)
        def _():
            j = 2 * i - 2
            scores(j + 2, s_refs[2])
            update(j, s_refs[0], None)
            scores(j + 3, s_refs[3], late_only=True)
            update(j + 1, s_refs[1], 0)
            start_tile(nxt)
            scores(0, s_refs[0])
            update(j + 2, s_refs[2], 1)
            scores(1, s_refs[1])
            update(j + 3, s_refs[3], 2, late_only=True)

        @pl.when(i == 0)
        def _():
            update(0, s_refs[0], 1)
            start_tile(nxt)
            scores(0, s_refs[0])
            update(1, s_refs[1], 2, late_only=True)
            scores(1, s_refs[1])

        inv = 1.0 / l_ref[...]
        acc = acc_ref[...]
        o = jnp.concatenate(
            [acc[:, c:c + hq] * inv[:, c:c + hq] - lam * (acc[:, c + hq:c + tq] * inv[:, c + hq:c + tq])
             for c in (0, tq)], axis=1)
        ms = jnp.mean(o * o, axis=0, keepdims=True)
        y = (o * lax.rsqrt(ms + EPS) * g_ref[...]) * (1.0 - lam_init)
        o_ref[pl.ds(pl.multiple_of(i * tq, tq), tq), :] = y.T
        reset_state()
        return carry

    start_tile(0)
    scores(0, s_refs[0])
    scores(1, s_refs[1])
    reset_state()
    lax.fori_loop(0, nq, q_tile, 0)


def _attn_a_prompt(proj, k_a, v_a, t5_bias, lq1, lk1, lq2, lk2, subln_g, lam_init):
    s = proj.shape[1]
    tq, tk = A_TQ, A_TK
    assert s % tq == 0 and tq == 2 * tk and tk % CHUNK == 0 and tk >= T5_MAX_DIST
    vec = lambda: pl.BlockSpec((1, DK_A), lambda h: (0, 0))
    head = lambda: pl.BlockSpec((s, HEAD_W), lambda h: (0, h))
    return pl.pallas_call(
        functools.partial(_attn_a_kernel, lam_init=lam_init),
        grid=(H_A,),
        in_specs=[
            pl.BlockSpec(memory_space=pltpu.SMEM),
            vec(), vec(), vec(), vec(),
            pl.BlockSpec((DV_A, 1), lambda h: (0, 0)),
            pl.BlockSpec((None, s, HEAD_W), lambda h: (QA, 0, h)), head(), head(),
        ],
        out_specs=pl.BlockSpec((s, HEAD_W), lambda h: (0, h)),
        out_shape=jax.ShapeDtypeStruct((s, H_A * DV_A), F32),
        scratch_shapes=[
            pltpu.VMEM((s, HEAD_W), BF16),
            pltpu.VMEM((s // tk, DV_A, tk), BF16),
            pltpu.VMEM((3, tk, tq), F32),
            pltpu.VMEM((HEAD_W, 2 * tq), BF16),
            pltpu.VMEM((1, 2 * tq), F32),
            pltpu.VMEM((1, 2 * tq), F32),
            pltpu.VMEM((DV_A, 2 * tq), F32),
        ] + [pltpu.VMEM((tk, 2 * tq), F32)] * 4,
        compiler_params=pltpu.CompilerParams(
            dimension_semantics=("arbitrary",), vmem_limit_bytes=VMEM_LIMIT),
        name="attn_a_prompt",
    )(t5_bias, lq1.reshape(1, DK_A), lk1.reshape(1, DK_A), lq2.reshape(1, DK_A), lk2.reshape(1, DK_A),
      subln_g.reshape(DV_A, 1), proj, k_a, v_a)


def _attn_b_kernel(rb_ref, q_ref, k_ref, v_ref, *rest):
    n_w = N_MERGE_WEIGHTS
    w_refs, o_ref, wbf_refs = rest[:n_w], rest[n_w], rest[n_w + 1:2 * n_w + 1]
    kbf_ref, vt_ref, bias_ref = rest[2 * n_w + 1:2 * n_w + 4]
    s_refs = rest[2 * n_w + 4:]
    t = B_TQ
    blk = REL_CLIP
    nkt = BAND_PAST // t + 1
    nq = q_ref.shape[0] // t
    h = pl.program_id(0)

    for w_ref, wbf_ref in zip(w_refs, wbf_refs):
        wbf_ref[...] = w_ref[...].astype(BF16)

    _stage_keys_values(k_ref, v_ref, kbf_ref, vt_ref, t)

    kk = lax.broadcasted_iota(jnp.int32, (blk, blk), 0)
    qq = lax.broadcasted_iota(jnp.int32, (blk, blk), 1)
    rel = kk - qq
    lo = jnp.full((blk, blk), rb_ref[h, 0] * LOG2E, F32)
    same = _rel_bias_tile(rel, -(blk - 1), blk - 1, rb_ref, h) * LOG2E
    prev = _rel_bias_tile(rel - blk, -(2 * blk - 1), -1, rb_ref, h) * LOG2E
    ninf = jnp.full((blk, blk), -jnp.inf, F32)
    kc = kk // CHUNK
    qc = qq // CHUNK
    far_blocks = BAND_PAST // blk
    for a in range(nkt * t // blk):
        for b in range(t // blk):
            e = a - b
            if e < 0 or e > far_blocks:
                tile = ninf
            elif e == 0:
                tile = jnp.where(kc >= qc, lo, -jnp.inf)
            elif e == far_blocks:
                tile = jnp.where(kc <= qc, same, -jnp.inf)
            elif e == far_blocks - 1:
                tile = prev
            else:
                tile = lo
            bias_ref[a * blk:(a + 1) * blk, b * blk:(b + 1) * blk] = tile

    def scores(g, nk, s_ref):
        q0 = pl.multiple_of(g * t, t)
        k0 = pl.multiple_of((g - (nk - 1)) * t, t)
        qt = (q_ref[pl.ds(q0, t), :] * (DH_B ** -0.5 * LOG2E)).T.astype(BF16)
        s_ref[(nkt - nk) * t:, :] = jnp.dot(kbf_ref[pl.ds(k0, nk * t), :], qt, preferred_element_type=F32)

    def finish(g, nk, s_ref):
        s = s_ref[(nkt - nk) * t:, :] + bias_ref[(nkt - nk) * t:, :]
        m = jnp.max(s, axis=0, keepdims=True)
        p = jnp.exp2(s - m)
        l = jnp.sum(p, axis=0, keepdims=True)
        pb = p.astype(BF16)
        o = jnp.dot(vt_ref[g - (nk - 1)], pb[:t], preferred_element_type=F32)
        for c in range(1, nk):
            o = o + jnp.dot(vt_ref[g - (nk - 1) + c], pb[c * t:(c + 1) * t], preferred_element_type=F32)
        o_ref[pl.ds(pl.multiple_of(g * t, t), t), :] = (o * (1.0 / l)).T

    assert nq % 4 == 0 and nq >= 8 and nkt <= 4

    def quad(g, static_start=False, lookahead=True):
        for n in range(4):
            if lookahead or n < 2:
                scores(g + n + 2, nkt, s_refs[(n + 2) % 4])
            finish(g + n, min(n + 1, nkt) if static_start else nkt, s_refs[n])

    def two_quads(u, c):
        quad(4 + 8 * u)
        quad(8 + 8 * u)
        return c

    scores(0, 1, s_refs[0])
    scores(1, min(2, nkt), s_refs[1])
    quad(0, static_start=True)
    quads = nq // 4 - 2
    lax.fori_loop(0, quads // 2, two_quads, 0)
    if quads % 2:
        quad(4 * quads)
    quad(nq - 4, lookahead=False)


def _attn_b_prompt(proj, rel_bias, merge_weights):
    s = proj.shape[1]
    t = B_TQ
    nkt = BAND_PAST // t + 1
    assert s % t == 0 and BAND_PAST % t == 0 and t % REL_CLIP == 0 and REL_CLIP % CHUNK == 0
    head = lambda c: pl.BlockSpec((None, s, HEAD_W), lambda h: (c, 0, h))
    assert len(merge_weights) == N_MERGE_WEIGHTS and all(w.shape[0] % (8 * H_B) == 0 for w in merge_weights)
    rows = lambda w: pl.BlockSpec((w.shape[0] // H_B, w.shape[1]), lambda h: (h, 0))
    outs = pl.pallas_call(
        _attn_b_kernel,
        grid=(H_B,),
        in_specs=[pl.BlockSpec(memory_space=pltpu.SMEM), head(QB), head(KB), head(VB)]
        + [rows(w) for w in merge_weights],
        out_specs=[pl.BlockSpec((s, HEAD_W), lambda h: (0, h))] + [rows(w) for w in merge_weights],
        out_shape=[jax.ShapeDtypeStruct((s, H_B * DH_B), F32)]
        + [jax.ShapeDtypeStruct(w.shape, BF16) for w in merge_weights],
        scratch_shapes=[
            pltpu.VMEM((s, HEAD_W), BF16),
            pltpu.VMEM((s // t, DH_B, t), BF16),
            pltpu.VMEM((nkt * t, t), F32),
        ] + [pltpu.VMEM((nkt * t, t), F32)] * 4,
        compiler_params=pltpu.CompilerParams(
            dimension_semantics=("arbitrary",), vmem_limit_bytes=VMEM_LIMIT),
        name="attn_b_prompt",
    )(rel_bias, proj, proj, proj, *merge_weights)
    return outs[0], outs[1:]


def _dot_nt(a, b):
    return lax.dot_general(a, b, (((1,), (1,)), ((), ())), preferred_element_type=F32)


def _attn_sample_kernel(t5_ref, rb_ref, lq1, lk1, lq2, lk2, g_ref,
                        qa_ref, ka_ref, va_ref, cka_ref, cva_ref,
                        qb_ref, kb_ref, vb_ref, ckb_ref, cvb_ref,
                        oa_ref, ob_ref, kroll_ref, vroll_ref,
                        ba_ref, bb_ref, *, lam_init, past, win):
    n = qa_ref.shape[0]
    near = NEAR
    nh = H_A

    @pl.when(pl.program_id(0) == 0)
    def _():
        qry = lax.broadcasted_iota(jnp.int32, (n, near + n), 0)
        key = lax.broadcasted_iota(jnp.int32, (n, near + n), 1)
        rel = key - near - qry
        for h in range(nh):
            ba_ref[h] = _t5_bias_tile(rel, -(near + n - 1), n - 1, t5_ref, h)
            bb_ref[h] = _rel_bias_tile(rel, -(near + n - 1), n - 1, rb_ref, h)

    kroll_ref[:(win - n) * nh, :] = ckb_ref[n * nh:, :]
    vroll_ref[:(win - n) * nh, :] = cvb_ref[n * nh:, :]

    lam = _lambda(lq1, lk1, lq2, lk2, lam_init)
    lane = lax.broadcasted_iota(jnp.int32, (n, HEAD_W), 1)

    def cols(h):
        return slice(h * HEAD_W, (h + 1) * HEAD_W)

    def scores_a(h):
        q = qa_ref[:, cols(h)] * (DK_A ** -0.5)
        q2 = jnp.concatenate([jnp.where(lane < DK_A, q, 0.0), jnp.where(lane >= DK_A, q, 0.0)],
                             axis=0).astype(BF16)
        ba = ba_ref[h]
        ba2 = jnp.concatenate([ba, ba], axis=0)
        s_c = _dot_nt(q2, cka_ref[:, cols(h)].astype(BF16))
        s_c = jnp.concatenate([s_c[:, :past - near], s_c[:, past - near:] + ba2[:, :near]], axis=1)
        s_n = _dot_nt(q2, ka_ref[:, cols(h)].astype(BF16)) + ba2[:, near:]
        return s_c, s_n

    def scores_b(h):
        qb = qb_ref[:, cols(h)].astype(BF16)
        bb = bb_ref[h]
        ckb = ckb_ref[pl.ds(h, win, stride=nh), :].astype(BF16)
        s_c = _dot_nt(qb, ckb) * (DH_B ** -0.5)
        s_c = jnp.concatenate([s_c[:, :win - near] + rb_ref[h, 0], s_c[:, win - near:] + bb[:, :near]], axis=1)
        s_n = _dot_nt(qb, kb_ref[:, cols(h)].astype(BF16)) * (DH_B ** -0.5) + bb[:, near:]
        return s_c, s_n

    def softmax(s_c, s_n):
        m = jnp.maximum(jnp.max(s_c, axis=-1, keepdims=True), jnp.max(s_n, axis=-1, keepdims=True))
        p_c = jnp.exp(s_c - m)
        p_n = jnp.exp(s_n - m)
        l = jnp.sum(p_c, axis=-1, keepdims=True) + jnp.sum(p_n, axis=-1, keepdims=True)
        return p_c.astype(BF16), p_n.astype(BF16), 1.0 / l

    def finish_a(h, p_c, p_n, inv):
        cva = cva_ref[pl.ds(h, past, stride=nh), :].astype(BF16)
        o2 = (jnp.dot(p_c, cva, preferred_element_type=F32)
              + jnp.dot(p_n, va_ref[:, cols(h)].astype(BF16), preferred_element_type=F32)) * inv
        o = o2[:n] - lam * o2[n:]
        ms = jnp.mean(o * o, axis=-1, keepdims=True)
        oa_ref[:, cols(h)] = (o * lax.rsqrt(ms + EPS) * g_ref[...]) * (1.0 - lam_init)

    def finish_b(h, p_c, p_n, inv):
        cvb = cvb_ref[pl.ds(h, win, stride=nh), :].astype(BF16)
        ob_ref[:, cols(h)] = (jnp.dot(p_c, cvb, preferred_element_type=F32)
                              + jnp.dot(p_n, vb_ref[:, cols(h)].astype(BF16), preferred_element_type=F32)) * inv

    for first in range(0, nh, SAMPLE_HEAD_GROUP):
        heads = range(first, first + SAMPLE_HEAD_GROUP)
        for h in heads:
            new_rows = pl.ds((win - n) * nh + h, n, stride=nh)
            kroll_ref[new_rows, :] = kb_ref[:, cols(h)]
            vroll_ref[new_rows, :] = vb_ref[:, cols(h)]
        s_a = [scores_a(h) for h in heads]
        s_b = [scores_b(h) for h in heads]
        p_a = [softmax(*s) for s in s_a]
        p_b = [softmax(*s) for s in s_b]
        for h, p in zip(heads, p_a):
            finish_a(h, *p)
        for h, p in zip(heads, p_b):
            finish_b(h, *p)


def _attn_sample(proj, k_a, v_a, ck_a, cv_a, ck_b, cv_b, t5_bias, rel_bias, lq1, lk1, lq2, lk2, subln_g,
                 lam_init):
    nb, past = ck_a.shape[0], ck_a.shape[1]
    win = ck_b.shape[1]
    n = proj.shape[1] // nb
    near = NEAR
    wide = H_A * HEAD_W
    assert past % CHUNK == 0 and n <= CHUNK and win <= BAND_PAST and win <= past
    assert near % HEAD_W == 0 and near <= win and near <= past and n % 8 == 0 and T5_MAX_DIST <= REL_CLIP
    assert H_A == H_B and DV_A == HEAD_W and DH_B == HEAD_W and 2 * DK_A == HEAD_W
    cka = ck_a.reshape(nb, past, wide)
    cva = cv_a.reshape(nb, past * H_A, DV_A)
    ckb = ck_b.reshape(nb, win * H_B, DH_B)
    cvb = cv_b.reshape(nb, win * H_B, DH_B)
    vec = lambda: pl.BlockSpec((1, DK_A), lambda b: (0, 0))
    new = lambda c: pl.BlockSpec((None, n, wide), lambda b: (c, b, 0))
    seq = lambda rows, cols: pl.BlockSpec((None, rows, cols), lambda b: (b, 0, 0))
    out = pl.BlockSpec((n, wide), lambda b: (b, 0))
    return pl.pallas_call(
        functools.partial(_attn_sample_kernel, lam_init=lam_init, past=past, win=win),
        grid=(nb,),
        in_specs=[
            pl.BlockSpec(memory_space=pltpu.SMEM),
            pl.BlockSpec(memory_space=pltpu.SMEM),
            vec(), vec(), vec(), vec(),
            pl.BlockSpec((1, DV_A), lambda b: (0, 0)),
            new(QA), out, out, seq(past, wide), seq(past * H_A, HEAD_W),
            new(QB), new(KB), new(VB), seq(win * H_B, HEAD_W), seq(win * H_B, HEAD_W),
        ],
        out_specs=[out, out, seq(win * H_B, HEAD_W), seq(win * H_B, HEAD_W)],
        out_shape=[
            jax.ShapeDtypeStruct((nb * n, wide), F32),
            jax.ShapeDtypeStruct((nb * n, wide), F32),
            jax.ShapeDtypeStruct((nb, win * H_B, DH_B), F32),
            jax.ShapeDtypeStruct((nb, win * H_B, DH_B), F32),
        ],
        scratch_shapes=[pltpu.VMEM((H_A, n, near + n), F32), pltpu.VMEM((H_B, n, near + n), F32)],
        compiler_params=pltpu.CompilerParams(
            dimension_semantics=("arbitrary",), vmem_limit_bytes=VMEM_LIMIT),
        name="attn_sample",
    )(t5_bias, rel_bias, lq1.reshape(1, DK_A), lk1.reshape(1, DK_A), lq2.reshape(1, DK_A),
      lk2.reshape(1, DK_A), subln_g.reshape(1, DV_A),
      proj, k_a, v_a, cka, cva, proj, proj, proj, ckb, cvb)


def _merge_kernel(x_ref, oa_ref, ob_ref, za_ref, zb_ref, ga0_ref, ga1_ref, gb0_ref, gb1_ref,
                  woa_ref, wob_ref, wout_ref, pg_ref, y_ref):
    za = za_ref[...]
    zb = zb_ref[...]
    a = (oa_ref[...] * (za * _sigmoid(za))).astype(BF16)
    b = (ob_ref[...] * (zb * _sigmoid(zb))).astype(BF16)
    ya = jnp.dot(a, woa_ref[...], preferred_element_type=F32)
    yb = jnp.dot(b, wob_ref[...], preferred_element_type=F32)
    ga = jnp.concatenate([ga0_ref[...], ga1_ref[...]], axis=1)
    gb = jnp.concatenate([gb0_ref[...], gb1_ref[...]], axis=1)
    mix = (_sigmoid(ga) * ya + _sigmoid(gb) * yb).astype(BF16)
    y = jnp.dot(mix, wout_ref[...], preferred_element_type=F32)
    ms = jnp.mean(y * y, axis=-1, keepdims=True)
    y_ref[...] = x_ref[...] + y * lax.rsqrt(ms + EPS) * pg_ref[...]


def _merge(x2d, o_a, o_b, proj, woa, wob, wout, post_g, tm):
    m, d = x2d.shape
    wa = o_a.shape[1]
    wb = o_b.shape[1]
    assert m % tm == 0 and wa == COL_BLOCK and wb == COL_BLOCK and d == 2 * COL_BLOCK
    row = lambda w: pl.BlockSpec((tm, w), lambda i: (i, 0))
    col = lambda c: pl.BlockSpec((None, tm, COL_BLOCK), lambda i: (c, i, 0))
    resident = lambda r, c: pl.BlockSpec((r, c), lambda i: (0, 0), pipeline_mode=pl.Buffered(1))
    return pl.pallas_call(
        _merge_kernel,
        grid=(m // tm,),
        in_specs=[row(d), row(wa), row(wb), col(ZA), col(ZB), col(GA0), col(GA1), col(GB0), col(GB1),
                  resident(wa, d), resident(wb, d), resident(d, d), resident(1, d)],
        out_specs=row(d),
        out_shape=jax.ShapeDtypeStruct((m, d), F32),
        compiler_params=pltpu.CompilerParams(
            dimension_semantics=("arbitrary",), vmem_limit_bytes=VMEM_LIMIT),
        name="merge",
    )(x2d, o_a, o_b, proj, proj, proj, proj, proj, proj, woa, wob, wout, post_g.reshape(1, d))


def kernel(x_prompt, x_sample, cache_k_a, cache_v_a, cache_k_b, cache_v_b, t5_bias, pre_norm, post_norm,
           w_in, lambda_q1, lambda_k1, lambda_q2, lambda_k2, subln_a, rel_bias_b, w_o_a, w_o_b, w_out):
    depth = w_in.shape[0]
    bp, sp, d = x_prompt.shape
    bs, ss, _ = x_sample.shape
    assert bp == 1 and w_in.shape[2] == 12 * COL_BLOCK
    yp = x_prompt.reshape(sp, d)
    ys = x_sample.reshape(bs * ss, d)
    tail = min(BAND_PAST, sp)
    outs = [[] for _ in range(8)]
    for l in range(depth):
        lam_init = 0.8 - 0.6 * math.exp(-0.3 * l)
        lam_args = (lambda_q1[l], lambda_k1[l], lambda_q2[l], lambda_k2[l], subln_a[l], lam_init)

        ps, ka_s, va_s, w = _in_proj_cast(ys, pre_norm[l], w_in[l])
        yp, ps = lax.optimization_barrier((yp, ps))
        pp, ka, va = _in_proj(yp, pre_norm[l], w, tm=1024)
        ob, (woa, wob, wout) = _attn_b_prompt(pp, rel_bias_b[l], (w_o_a[l], w_o_b[l], w_out[l]))
        ck_a, ob = lax.optimization_barrier((cache_k_a[l], ob))
        oa = _attn_a_prompt(pp, ka, va, t5_bias, *lam_args)
        yp = _merge(yp, oa, ob, pp, woa, wob, wout, post_norm[l], tm=256)
        outs[0].append(ka.reshape(bp, sp, 2 * H_A, DK_A))
        outs[1].append(va.reshape(bp, sp, H_A, DV_A))
        outs[2].append(pp[KB, sp - tail:].reshape(bp, tail, H_B, DH_B))
        outs[3].append(pp[VB, sp - tail:].reshape(bp, tail, H_B, DH_B))

        oas, obs, kroll, vroll = _attn_sample(ps, ka_s, va_s, ck_a, cache_v_a[l], cache_k_b[l], cache_v_b[l],
                                              t5_bias, rel_bias_b[l], *lam_args)
        ys = _merge(ys, oas, obs, ps, woa, wob, wout, post_norm[l], tm=bs * ss)
        outs[4].append(ka_s.reshape(bs, ss, 2 * H_A, DK_A))
        outs[5].append(va_s.reshape(bs, ss, H_A, DV_A))
        outs[6].append(kroll.reshape(bs, -1, H_B, DH_B))
        outs[7].append(vroll.reshape(bs, -1, H_B, DH_B))
    return (yp.reshape(bp, sp, d), ys.reshape(bs, ss, d)) + tuple(jnp.stack(o) for o in outs)
```

```python
import functools
import math

import jax
import jax.numpy as jnp
from jax import lax
from jax.experimental import pallas as pl
from jax.experimental.pallas import tpu as pltpu

F32 = jnp.float32
BF16 = jnp.bfloat16

CHUNK = 64
H_A = 8
DK_A = 64
DV_A = 2 * DK_A
H_B = 8
DH_B = 128
BAND_CHUNKS = 8
BAND_PAST = BAND_CHUNKS * CHUNK
REL_CLIP = 128
T5_BUCKETS = 32
T5_MAX_DIST = 128
EPS = 1e-6

HEAD_W = 128
COL_BLOCK = 1024
A_TQ, A_TK = 512, 256
B_TQ = 256
SAMPLE_HEAD_GROUP = 4
N_MERGE_WEIGHTS = 3
LOG2E = math.log2(math.e)
NEAR = max(REL_CLIP, T5_MAX_DIST)
VMEM_LIMIT = 60 * 1024 * 1024

QA, ZA, QB, KB, VB, ZB, GA0, GA1, GB0, GB1 = range(10)
COL_KA, COL_VA = 1, 2
N_OWN_COLS = 2
COL_SLOT = (QA, None, None, ZA, QB, KB, VB, ZB, GA0, GA1, GB0, GB1)


def _t5_bucket_int(rel):
    half = T5_BUCKETS // 2
    max_exact = half // 2
    n = abs(rel)
    ret = half if rel > 0 else 0
    if n < max_exact:
        return ret + n
    assert (T5_MAX_DIST // max_exact) ** 2 == 2 ** (half - max_exact)
    j = 0
    while n * n >= (max_exact * max_exact) * 2 ** (j + 1):
        j += 1
    return ret + min(max_exact + j, half - 1)


def _t5_runs(lo, hi):
    runs = []
    for r in range(lo, hi + 1):
        b = _t5_bucket_int(r)
        if not runs or runs[-1][1] != b:
            runs.append((r, b))
    return runs


T5_FAR_BUCKET = _t5_bucket_int(-T5_MAX_DIST)
assert all(_t5_bucket_int(-n) == T5_FAR_BUCKET for n in range(T5_MAX_DIST, 4 * T5_MAX_DIST))


def _t5_bias_tile(rel, lo, hi, t5_ref, h):
    runs = _t5_runs(lo, hi)
    val = jnp.full(rel.shape, t5_ref[runs[0][1], h], F32)
    for start, b in runs[1:]:
        val = jnp.where(rel >= start, t5_ref[b, h], val)
    return val - t5_ref[T5_FAR_BUCKET, h]


def _rel_bias_tile(rel, lo, hi, rb_ref, h):
    lo = max(lo, -REL_CLIP)
    hi = min(hi, REL_CLIP)
    val = jnp.full(rel.shape, rb_ref[h, lo + REL_CLIP], F32)
    for d in range(lo + 1, hi + 1):
        val = jnp.where(rel >= d, rb_ref[h, d + REL_CLIP], val)
    return val


def _sigmoid(x):
    return 1.0 / (1.0 + jnp.exp(-x))


def _lambda(lq1, lk1, lq2, lk2, lam_init):
    a = jnp.sum(lq1[...] * lk1[...], axis=-1, keepdims=True)
    b = jnp.sum(lq2[...] * lk2[...], axis=-1, keepdims=True)
    return jnp.exp(a) - jnp.exp(b) + lam_init


def _in_proj_kernel(x_ref, g_ref, w_ref, o_ref, ka_ref, va_ref, h_ref):
    j = pl.program_id(1)

    @pl.when(j == 0)
    def _():
        x = x_ref[...]
        ms = jnp.mean(x * x, axis=-1, keepdims=True)
        h_ref[...] = (x * lax.rsqrt(ms + EPS) * g_ref[...]).astype(BF16)

    def project(ref):
        ref[...] = jnp.dot(h_ref[...], w_ref[...], preferred_element_type=F32)

    pl.when(j == COL_KA)(functools.partial(project, ka_ref))
    pl.when(j == COL_VA)(functools.partial(project, va_ref))
    pl.when(jnp.logical_and(j != COL_KA, j != COL_VA))(functools.partial(project, o_ref))


def _in_proj(x2d, pre_g, w_bf16, tm):
    m, d = x2d.shape
    n = w_bf16.shape[1]
    assert m % tm == 0 and n == len(COL_SLOT) * COL_BLOCK

    def own(col):
        return pl.BlockSpec((tm, COL_BLOCK), lambda i, j: (jnp.where(j >= col, i, jnp.maximum(i - 1, 0)), 0))

    def slab_index(i, j):
        slot = jnp.where(j <= COL_VA, 0, j - N_OWN_COLS)
        return slot, i, 0

    assert COL_SLOT[0] == 0 and COL_SLOT[COL_VA + 1:] == tuple(range(1, len(COL_SLOT) - N_OWN_COLS))
    return pl.pallas_call(
        _in_proj_kernel,
        grid=(m // tm, n // COL_BLOCK),
        in_specs=[
            pl.BlockSpec((tm, d), lambda i, j: (i, 0)),
            pl.BlockSpec((1, d), lambda i, j: (0, 0)),
            pl.BlockSpec((d, COL_BLOCK), lambda i, j: (0, j)),
        ],
        out_specs=[pl.BlockSpec((None, tm, COL_BLOCK), slab_index), own(COL_KA), own(COL_VA)],
        out_shape=[jax.ShapeDtypeStruct((len(COL_SLOT) - N_OWN_COLS, m, COL_BLOCK), F32),
                   jax.ShapeDtypeStruct((m, COL_BLOCK), F32),
                   jax.ShapeDtypeStruct((m, COL_BLOCK), F32)],
        scratch_shapes=[pltpu.VMEM((tm, d), BF16)],
        compiler_params=pltpu.CompilerParams(
            dimension_semantics=("arbitrary", "arbitrary"), vmem_limit_bytes=VMEM_LIMIT),
        name="in_proj",
    )(x2d, pre_g.reshape(1, d), w_bf16)


def _in_proj_cast_kernel(x_ref, g_ref, w_ref, o_ref, ka_ref, va_ref, wbf_ref, h_ref):
    j = pl.program_id(0)

    @pl.when(j == 0)
    def _():
        x = x_ref[...]
        ms = jnp.mean(x * x, axis=-1, keepdims=True)
        h_ref[...] = (x * lax.rsqrt(ms + EPS) * g_ref[...]).astype(BF16)

    def project(ref):
        w = w_ref[...].astype(BF16)
        wbf_ref[...] = w
        ref[...] = jnp.dot(h_ref[...], w, preferred_element_type=F32)

    pl.when(j == COL_KA)(functools.partial(project, ka_ref))
    pl.when(j == COL_VA)(functools.partial(project, va_ref))
    pl.when(jnp.logical_and(j != COL_KA, j != COL_VA))(functools.partial(project, o_ref))


def _in_proj_cast(x2d, pre_g, w_f32):
    m, d = x2d.shape
    n = w_f32.shape[1]
    assert n == len(COL_SLOT) * COL_BLOCK
    slots = len(COL_SLOT) - N_OWN_COLS
    resident = lambda: pl.BlockSpec((m, COL_BLOCK), lambda j: (0, 0))
    return pl.pallas_call(
        _in_proj_cast_kernel,
        grid=(len(COL_SLOT),),
        in_specs=[
            pl.BlockSpec((m, d), lambda j: (0, 0)),
            pl.BlockSpec((1, d), lambda j: (0, 0)),
            pl.BlockSpec((d, COL_BLOCK), lambda j: (0, j)),
        ],
        out_specs=[pl.BlockSpec((None, m, COL_BLOCK), lambda j: (jnp.where(j <= COL_VA, 0, j - N_OWN_COLS), 0, 0)),
                   resident(), resident(),
                   pl.BlockSpec((d, COL_BLOCK), lambda j: (0, j))],
        out_shape=[jax.ShapeDtypeStruct((slots, m, COL_BLOCK), F32),
                   jax.ShapeDtypeStruct((m, COL_BLOCK), F32),
                   jax.ShapeDtypeStruct((m, COL_BLOCK), F32),
                   jax.ShapeDtypeStruct((d, n), BF16)],
        scratch_shapes=[pltpu.VMEM((m, d), BF16)],
        compiler_params=pltpu.CompilerParams(
            dimension_semantics=("arbitrary",), vmem_limit_bytes=VMEM_LIMIT),
        name="in_proj_cast",
    )(x2d, pre_g.reshape(1, d), w_f32)


def _online_softmax_step(s, vt, m_ref, l_ref, acc_ref):
    m_old = m_ref[...]
    m_new = jnp.maximum(m_old, jnp.max(s, axis=0, keepdims=True))
    alpha = jnp.exp2(m_old - m_new)
    p = jnp.exp2(s - m_new)
    l_ref[...] = alpha * l_ref[...] + jnp.sum(p, axis=0, keepdims=True)
    acc_ref[...] = alpha * acc_ref[...] + jnp.dot(vt, p.astype(BF16), preferred_element_type=F32)
    m_ref[...] = m_new


def _stage_keys_values(k_ref, v_ref, kbf_ref, vt_ref, t):
    def body(j, c):
        r = pl.multiple_of(j * t, t)
        kbf_ref[pl.ds(r, t), :] = k_ref[pl.ds(r, t), :].astype(BF16)
        vt_ref[j] = v_ref[pl.ds(r, t), :].astype(BF16).T
        return c
    lax.fori_loop(0, vt_ref.shape[0], body, 0, unroll=4)


def _attn_a_kernel(t5_ref, lq1, lk1, lq2, lk2, g_ref, q_ref, k_ref, v_ref, *rest, lam_init):
    n_w = N_MERGE_WEIGHTS
    w_refs, o_ref, wbf_refs = rest[:n_w], rest[n_w], rest[n_w + 1:2 * n_w + 1]
    kbf_ref, vt_ref, bias_ref, qt_ref, m_ref, l_ref, acc_ref = rest[2 * n_w + 1:2 * n_w + 8]
    s_refs = rest[2 * n_w + 8:]
    tq, tk = A_TQ, A_TK
    h = pl.program_id(0)
    nq = q_ref.shape[0] // tq

    for w_ref, wbf_ref in zip(w_refs, wbf_refs):
        wbf_ref[...] = w_ref[...].astype(BF16)

    _stage_keys_values(k_ref, v_ref, kbf_ref, vt_ref, tk)
    key = lax.broadcasted_iota(jnp.int32, (tk, tq), 0)
    qry = lax.broadcasted_iota(jnp.int32, (tk, tq), 1)
    for n in range(3):
        rel = key + (n - 1) * tk - qry
        lo, hi = (n - 1) * tk - (tq - 1), min(n * tk - 1, CHUNK - 1)
        b = _t5_bias_tile(jnp.minimum(rel, hi), lo, hi, t5_ref, h) * LOG2E
        if n >= 1:
            b = jnp.where((key + (n - 1) * tk) // CHUNK <= qry // CHUNK, b, -jnp.inf)
        bias_ref[n] = b

    lam = _lambda(lq1, lk1, lq2, lk2, lam_init)

    hq = tq // 2

    def start_tile(i):
        q0 = pl.multiple_of(i * tq, tq)
        qt = (q_ref[pl.ds(q0, tq), :] * (DK_A ** -0.5 * LOG2E)).T
        sub = lax.broadcasted_iota(jnp.int32, (HEAD_W, tq), 0)
        maps = (jnp.where(sub < DK_A, qt, 0.0).astype(BF16), jnp.where(sub >= DK_A, qt, 0.0).astype(BF16))
        for c in range(4):
            qt_ref[:, c * hq:(c + 1) * hq] = maps[c % 2][:, (c // 2) * hq:(c // 2 + 1) * hq]

    def reset_state():
        m_ref[...] = jnp.full(m_ref.shape, -jnp.inf, F32)
        l_ref[...] = jnp.zeros(l_ref.shape, F32)
        acc_ref[...] = jnp.zeros(acc_ref.shape, F32)

    def scores(j, s_ref, late_only=False):
        r = pl.multiple_of(j * tk, tk)
        c0 = tq if late_only else 0
        s_ref[:, c0:] = jnp.dot(kbf_ref[pl.ds(r, tk), :], qt_ref[:, c0:], preferred_element_type=F32)

    def update(j, s_ref, bias_idx, late_only=False):
        c0 = tq if late_only else 0
        s = s_ref[:, c0:]
        if bias_idx is not None:
            b = bias_ref[bias_idx]
            halves = [b[:, (c // 2) * hq:(c // 2 + 1) * hq] for c in range(c0 // hq, 4)]
            s = jnp.concatenate([s[:, n * hq:(n + 1) * hq] + bh for n, bh in enumerate(halves)], axis=1)
        _online_softmax_step(s, vt_ref[j], m_ref.at[:, c0:], l_ref.at[:, c0:], acc_ref.at[:, c0:])

    def q_tile(i, carry):
        odd = jnp.logical_and(i >= 2, i % 2 == 0)

        @pl.when(odd)
        def _():
            update(0, s_refs[0], None)
            scores(2, s_refs[0])
            update(1, s_refs[1], None)
            scores(3, s_refs[1])

        j0 = jnp.where(odd, 2, 0)

        def quad(j):
            for n in range(4):
                scores(j + n + 2, s_refs[(n + 2) % 4])
                update(j + n, s_refs[n], None)

        def two_quads(u, c):
            quad(j0 + 8 * u)
            quad(j0 + 8 * u + 4)
            return c
        quads = jnp.maximum(i - 1, 0) // 2
        lax.fori_loop(0, quads // 2, two_quads, 0)

        @pl.when(quads % 2 == 1)
        def _():
            quad(j0 + 4 * (quads - 1))

        nxt = jnp.minimum(i + 1, nq - 1)

        @pl.when(i >= 1)
        def _():
            j = 2 * i - 2
            scores(j + 2, s_refs[2])
            update(j, s_refs[0], None)
            scores(j + 3, s_refs[3], late_only=True)
            update(j + 1, s_refs[1], 0)
            start_tile(nxt)
            scores(0, s_refs[0])
            update(j + 2, s_refs[2], 1)
            scores(1, s_refs[1])
            update(j + 3, s_refs[3], 2, late_only=True)

        @pl.when(i == 0)
        def _():
            update(0, s_refs[0], 1)
            start_tile(nxt)
            scores(0, s_refs[0])
            update(1, s_refs[1], 2, late_only=True)
            scores(1, s_refs[1])

        inv = 1.0 / l_ref[...]
        acc = acc_ref[...]
        o = jnp.concatenate(
            [acc[:, c:c + hq] * inv[:, c:c + hq] - lam * (acc[:, c + hq:c + tq] * inv[:, c + hq:c + tq])
             for c in (0, tq)], axis=1)
        ms = jnp.mean(o * o, axis=0, keepdims=True)
        y = (o * lax.rsqrt(ms + EPS) * g_ref[...]) * (1.0 - lam_init)
        o_ref[pl.ds(pl.multiple_of(i * tq, tq), tq), :] = y.T
        reset_state()
        return carry

    start_tile(0)
    scores(0, s_refs[0])
    scores(1, s_refs[1])
    reset_state()
    lax.fori_loop(0, nq, q_tile, 0)


def _attn_a_prompt(proj, k_a, v_a, merge_weights, t5_bias, lq1, lk1, lq2, lk2, subln_g, lam_init):
    s = proj.shape[1]
    tq, tk = A_TQ, A_TK
    assert s % tq == 0 and tq == 2 * tk and tk % CHUNK == 0 and tk >= T5_MAX_DIST
    vec = lambda: pl.BlockSpec((1, DK_A), lambda h: (0, 0))
    head = lambda: pl.BlockSpec((s, HEAD_W), lambda h: (0, h))
    assert len(merge_weights) == N_MERGE_WEIGHTS and all(w.shape[0] % (8 * H_A) == 0 for w in merge_weights)
    rows = lambda w: pl.BlockSpec((w.shape[0] // H_A, w.shape[1]), lambda h: (h, 0))
    outs = pl.pallas_call(
        functools.partial(_attn_a_kernel, lam_init=lam_init),
        grid=(H_A,),
        in_specs=[
            pl.BlockSpec(memory_space=pltpu.SMEM),
            vec(), vec(), vec(), vec(),
            pl.BlockSpec((DV_A, 1), lambda h: (0, 0)),
            pl.BlockSpec((None, s, HEAD_W), lambda h: (QA, 0, h)), head(), head(),
        ] + [rows(w) for w in merge_weights],
        out_specs=[pl.BlockSpec((s, HEAD_W), lambda h: (0, h))] + [rows(w) for w in merge_weights],
        out_shape=[jax.ShapeDtypeStruct((s, H_A * DV_A), F32)]
        + [jax.ShapeDtypeStruct(w.shape, BF16) for w in merge_weights],
        scratch_shapes=[
            pltpu.VMEM((s, HEAD_W), BF16),
            pltpu.VMEM((s // tk, DV_A, tk), BF16),
            pltpu.VMEM((3, tk, tq), F32),
            pltpu.VMEM((HEAD_W, 2 * tq), BF16),
            pltpu.VMEM((1, 2 * tq), F32),
            pltpu.VMEM((1, 2 * tq), F32),
            pltpu.VMEM((DV_A, 2 * tq), F32),
        ] + [pltpu.VMEM((tk, 2 * tq), F32)] * 4,
        compiler_params=pltpu.CompilerParams(
            dimension_semantics=("arbitrary",), vmem_limit_bytes=VMEM_LIMIT),
        name="attn_a_prompt",
    )(t5_bias, lq1.reshape(1, DK_A), lk1.reshape(1, DK_A), lq2.reshape(1, DK_A), lk2.reshape(1, DK_A),
      subln_g.reshape(DV_A, 1), proj, k_a, v_a, *merge_weights)
    return outs[0], outs[1:]


def _attn_b_kernel(rb_ref, q_ref, k_ref, v_ref, o_ref, kbf_ref, vt_ref, bias_ref, *s_refs):
    t = B_TQ
    blk = REL_CLIP
    nkt = BAND_PAST // t + 1
    nq = q_ref.shape[0] // t
    h = pl.program_id(0)

    _stage_keys_values(k_ref, v_ref, kbf_ref, vt_ref, t)

    kk = lax.broadcasted_iota(jnp.int32, (blk, blk), 0)
    qq = lax.broadcasted_iota(jnp.int32, (blk, blk), 1)
    rel = kk - qq
    lo = jnp.full((blk, blk), rb_ref[h, 0] * LOG2E, F32)
    same = _rel_bias_tile(rel, -(blk - 1), blk - 1, rb_ref, h) * LOG2E
    prev = _rel_bias_tile(rel - blk, -(2 * blk - 1), -1, rb_ref, h) * LOG2E
    ninf = jnp.full((blk, blk), -jnp.inf, F32)
    kc = kk // CHUNK
    qc = qq // CHUNK
    far_blocks = BAND_PAST // blk
    for a in range(nkt * t // blk):
        for b in range(t // blk):
            e = a - b
            if e < 0 or e > far_blocks:
                tile = ninf
            elif e == 0:
                tile = jnp.where(kc >= qc, lo, -jnp.inf)
            elif e == far_blocks:
                tile = jnp.where(kc <= qc, same, -jnp.inf)
            elif e == far_blocks - 1:
                tile = prev
            else:
                tile = lo
            bias_ref[a * blk:(a + 1) * blk, b * blk:(b + 1) * blk] = tile

    def scores(g, nk, s_ref):
        q0 = pl.multiple_of(g * t, t)
        k0 = pl.multiple_of((g - (nk - 1)) * t, t)
        qt = (q_ref[pl.ds(q0, t), :] * (DH_B ** -0.5 * LOG2E)).T.astype(BF16)
        s_ref[(nkt - nk) * t:, :] = jnp.dot(kbf_ref[pl.ds(k0, nk * t), :], qt, preferred_element_type=F32)

    def finish(g, nk, s_ref):
        s = s_ref[(nkt - nk) * t:, :] + bias_ref[(nkt - nk) * t:, :]
        m = jnp.max(s, axis=0, keepdims=True)
        p = jnp.exp2(s - m)
        l = jnp.sum(p, axis=0, keepdims=True)
        pb = p.astype(BF16)
        o = jnp.dot(vt_ref[g - (nk - 1)], pb[:t], preferred_element_type=F32)
        for c in range(1, nk):
            o = o + jnp.dot(vt_ref[g - (nk - 1) + c], pb[c * t:(c + 1) * t], preferred_element_type=F32)
        o_ref[pl.ds(pl.multiple_of(g * t, t), t), :] = (o * (1.0 / l)).T

    assert nq % 4 == 0 and nq >= 8 and nkt <= 4

    def quad(g, static_start=False, lookahead=True):
        for n in range(4):
            if lookahead or n < 2:
                scores(g + n + 2, nkt, s_refs[(n + 2) % 4])
            finish(g + n, min(n + 1, nkt) if static_start else nkt, s_refs[n])

    def two_quads(u, c):
        quad(4 + 8 * u)
        quad(8 + 8 * u)
        return c

    scores(0, 1, s_refs[0])
    scores(1, min(2, nkt), s_refs[1])
    quad(0, static_start=True)
    quads = nq // 4 - 2
    lax.fori_loop(0, quads // 2, two_quads, 0)
    if quads % 2:
        quad(4 * quads)
    quad(nq - 4, lookahead=False)


def _attn_b_prompt(proj, rel_bias):
    s = proj.shape[1]
    t = B_TQ
    nkt = BAND_PAST // t + 1
    assert s % t == 0 and BAND_PAST % t == 0 and t % REL_CLIP == 0 and REL_CLIP % CHUNK == 0
    head = lambda c: pl.BlockSpec((None, s, HEAD_W), lambda h: (c, 0, h))
    return pl.pallas_call(
        _attn_b_kernel,
        grid=(H_B,),
        in_specs=[pl.BlockSpec(memory_space=pltpu.SMEM), head(QB), head(KB), head(VB)],
        out_specs=pl.BlockSpec((s, HEAD_W), lambda h: (0, h)),
        out_shape=jax.ShapeDtypeStruct((s, H_B * DH_B), F32),
        scratch_shapes=[
            pltpu.VMEM((s, HEAD_W), BF16),
            pltpu.VMEM((s // t, DH_B, t), BF16),
            pltpu.VMEM((nkt * t, t), F32),
        ] + [pltpu.VMEM((nkt * t, t), F32)] * 4,
        compiler_params=pltpu.CompilerParams(
            dimension_semantics=("arbitrary",), vmem_limit_bytes=VMEM_LIMIT),
        name="attn_b_prompt",
    )(rel_bias, proj, proj, proj)


def _dot_nt(a, b):
    return lax.dot_general(a, b, (((1,), (1,)), ((), ())), preferred_element_type=F32)


def _attn_sample_kernel(t5_ref, rb_ref, lq1, lk1, lq2, lk2, g_ref,
                        qa_ref, ka_ref, va_ref, cka_ref, cva_ref,
                        qb_ref, kb_ref, vb_ref, ckb_ref, cvb_ref,
                        oa_ref, ob_ref, kroll_ref, vroll_ref,
                        ba_ref, bb_ref, *, lam_init, past, win):
    n = qa_ref.shape[0]
    near = NEAR
    nh = H_A

    @pl.when(pl.program_id(0) == 0)
    def _():
        qry = lax.broadcasted_iota(jnp.int32, (n, near + n), 0)
        key = lax.broadcasted_iota(jnp.int32, (n, near + n), 1)
        rel = key - near - qry
        for h in range(nh):
            ba_ref[h] = _t5_bias_tile(rel, -(near + n - 1), n - 1, t5_ref, h)
            bb_ref[h] = _rel_bias_tile(rel, -(near + n - 1), n - 1, rb_ref, h)

    kroll_ref[:(win - n) * nh, :] = ckb_ref[n * nh:, :]
    vroll_ref[:(win - n) * nh, :] = cvb_ref[n * nh:, :]

    lam = _lambda(lq1, lk1, lq2, lk2, lam_init)
    lane = lax.broadcasted_iota(jnp.int32, (n, HEAD_W), 1)

    def cols(h):
        return slice(h * HEAD_W, (h + 1) * HEAD_W)

    def scores_a(h):
        q = qa_ref[:, cols(h)] * (DK_A ** -0.5)
        q2 = jnp.concatenate([jnp.where(lane < DK_A, q, 0.0), jnp.where(lane >= DK_A, q, 0.0)],
                             axis=0).astype(BF16)
        ba = ba_ref[h]
        ba2 = jnp.concatenate([ba, ba], axis=0)
        s_c = _dot_nt(q2, cka_ref[:, cols(h)].astype(BF16))
        s_c = jnp.concatenate([s_c[:, :past - near], s_c[:, past - near:] + ba2[:, :near]], axis=1)
        s_n = _dot_nt(q2, ka_ref[:, cols(h)].astype(BF16)) + ba2[:, near:]
        return s_c, s_n

    def scores_b(h):
        qb = qb_ref[:, cols(h)].astype(BF16)
        bb = bb_ref[h]
        ckb = ckb_ref[pl.ds(h, win, stride=nh), :].astype(BF16)
        s_c = _dot_nt(qb, ckb) * (DH_B ** -0.5)
        s_c = jnp.concatenate([s_c[:, :win - near] + rb_ref[h, 0], s_c[:, win - near:] + bb[:, :near]], axis=1)
        s_n = _dot_nt(qb, kb_ref[:, cols(h)].astype(BF16)) * (DH_B ** -0.5) + bb[:, near:]
        return s_c, s_n

    def softmax(s_c, s_n):
        m = jnp.maximum(jnp.max(s_c, axis=-1, keepdims=True), jnp.max(s_n, axis=-1, keepdims=True))
        p_c = jnp.exp(s_c - m)
        p_n = jnp.exp(s_n - m)
        l = jnp.sum(p_c, axis=-1, keepdims=True) + jnp.sum(p_n, axis=-1, keepdims=True)
        return p_c.astype(BF16), p_n.astype(BF16), 1.0 / l

    def finish_a(h, p_c, p_n, inv):
        cva = cva_ref[pl.ds(h, past, stride=nh), :].astype(BF16)
        o2 = (jnp.dot(p_c, cva, preferred_element_type=F32)
              + jnp.dot(p_n, va_ref[:, cols(h)].astype(BF16), preferred_element_type=F32)) * inv
        o = o2[:n] - lam * o2[n:]
        ms = jnp.mean(o * o, axis=-1, keepdims=True)
        oa_ref[:, cols(h)] = (o * lax.rsqrt(ms + EPS) * g_ref[...]) * (1.0 - lam_init)

    def finish_b(h, p_c, p_n, inv):
        cvb = cvb_ref[pl.ds(h, win, stride=nh), :].astype(BF16)
        ob_ref[:, cols(h)] = (jnp.dot(p_c, cvb, preferred_element_type=F32)
                              + jnp.dot(p_n, vb_ref[:, cols(h)].astype(BF16), preferred_element_type=F32)) * inv

    for first in range(0, nh, SAMPLE_HEAD_GROUP):
        heads = range(first, first + SAMPLE_HEAD_GROUP)
        for h in heads:
            new_rows = pl.ds((win - n) * nh + h, n, stride=nh)
            kroll_ref[new_rows, :] = kb_ref[:, cols(h)]
            vroll_ref[new_rows, :] = vb_ref[:, cols(h)]
        s_a = [scores_a(h) for h in heads]
        s_b = [scores_b(h) for h in heads]
        p_a = [softmax(*s) for s in s_a]
        p_b = [softmax(*s) for s in s_b]
        for h, p in zip(heads, p_a):
            finish_a(h, *p)
        for h, p in zip(heads, p_b):
            finish_b(h, *p)


def _attn_sample(proj, k_a, v_a, ck_a, cv_a, ck_b, cv_b, t5_bias, rel_bias, lq1, lk1, lq2, lk2, subln_g,
                 lam_init):
    nb, past = ck_a.shape[0], ck_a.shape[1]
    win = ck_b.shape[1]
    n = proj.shape[1] // nb
    near = NEAR
    wide = H_A * HEAD_W
    assert past % CHUNK == 0 and n <= CHUNK and win <= BAND_PAST and win <= past
    assert near % HEAD_W == 0 and near <= win and near <= past and n % 8 == 0 and T5_MAX_DIST <= REL_CLIP
    assert H_A == H_B and DV_A == HEAD_W and DH_B == HEAD_W and 2 * DK_A == HEAD_W
    cka = ck_a.reshape(nb, past, wide)
    cva = cv_a.reshape(nb, past * H_A, DV_A)
    ckb = ck_b.reshape(nb, win * H_B, DH_B)
    cvb = cv_b.reshape(nb, win * H_B, DH_B)
    vec = lambda: pl.BlockSpec((1, DK_A), lambda b: (0, 0))
    new = lambda c: pl.BlockSpec((None, n, wide), lambda b: (c, b, 0))
    seq = lambda rows, cols: pl.BlockSpec((None, rows, cols), lambda b: (b, 0, 0))
    out = pl.BlockSpec((n, wide), lambda b: (b, 0))
    return pl.pallas_call(
        functools.partial(_attn_sample_kernel, lam_init=lam_init, past=past, win=win),
        grid=(nb,),
        in_specs=[
            pl.BlockSpec(memory_space=pltpu.SMEM),
            pl.BlockSpec(memory_space=pltpu.SMEM),
            vec(), vec(), vec(), vec(),
            pl.BlockSpec((1, DV_A), lambda b: (0, 0)),
            new(QA), out, out, seq(past, wide), seq(past * H_A, HEAD_W),
            new(QB), new(KB), new(VB), seq(win * H_B, HEAD_W), seq(win * H_B, HEAD_W),
        ],
        out_specs=[out, out, seq(win * H_B, HEAD_W), seq(win * H_B, HEAD_W)],
        out_shape=[
            jax.ShapeDtypeStruct((nb * n, wide), F32),
            jax.ShapeDtypeStruct((nb * n, wide), F32),
            jax.ShapeDtypeStruct((nb, win * H_B, DH_B), F32),
            jax.ShapeDtypeStruct((nb, win * H_B, DH_B), F32),
        ],
        scratch_shapes=[pltpu.VMEM((H_A, n, near + n), F32), pltpu.VMEM((H_B, n, near + n), F32)],
        compiler_params=pltpu.CompilerParams(
            dimension_semantics=("arbitrary",), vmem_limit_bytes=VMEM_LIMIT),
        name="attn_sample",
    )(t5_bias, rel_bias, lq1.reshape(1, DK_A), lk1.reshape(1, DK_A), lq2.reshape(1, DK_A),
      lk2.reshape(1, DK_A), subln_g.reshape(1, DV_A),
      proj, k_a, v_a, cka, cva, proj, proj, proj, ckb, cvb)


def _merge_kernel(x_ref, oa_ref, ob_ref, za_ref, zb_ref, ga0_ref, ga1_ref, gb0_ref, gb1_ref,
                  woa_ref, wob_ref, wout_ref, pg_ref, y_ref):
    za = za_ref[...]
    zb = zb_ref[...]
    a = (oa_ref[...] * (za * _sigmoid(za))).astype(BF16)
    b = (ob_ref[...] * (zb * _sigmoid(zb))).astype(BF16)
    ya = jnp.dot(a, woa_ref[...], preferred_element_type=F32)
    yb = jnp.dot(b, wob_ref[...], preferred_element_type=F32)
    ga = jnp.concatenate([ga0_ref[...], ga1_ref[...]], axis=1)
    gb = jnp.concatenate([gb0_ref[...], gb1_ref[...]], axis=1)
    mix = (_sigmoid(ga) * ya + _sigmoid(gb) * yb).astype(BF16)
    y = jnp.dot(mix, wout_ref[...], preferred_element_type=F32)
    ms = jnp.mean(y * y, axis=-1, keepdims=True)
    y_ref[...] = x_ref[...] + y * lax.rsqrt(ms + EPS) * pg_ref[...]


def _merge(x2d, o_a, o_b, proj, woa, wob, wout, post_g, tm):
    m, d = x2d.shape
    wa = o_a.shape[1]
    wb = o_b.shape[1]
    assert m % tm == 0 and wa == COL_BLOCK and wb == COL_BLOCK and d == 2 * COL_BLOCK
    row = lambda w: pl.BlockSpec((tm, w), lambda i: (i, 0))
    col = lambda c: pl.BlockSpec((None, tm, COL_BLOCK), lambda i: (c, i, 0))
    resident = lambda r, c: pl.BlockSpec((r, c), lambda i: (0, 0), pipeline_mode=pl.Buffered(1))
    return pl.pallas_call(
        _merge_kernel,
        grid=(m // tm,),
        in_specs=[row(d), row(wa), row(wb), col(ZA), col(ZB), col(GA0), col(GA1), col(GB0), col(GB1),
                  resident(wa, d), resident(wb, d), resident(d, d), resident(1, d)],
        out_specs=row(d),
        out_shape=jax.ShapeDtypeStruct((m, d), F32),
        compiler_params=pltpu.CompilerParams(
            dimension_semantics=("arbitrary",), vmem_limit_bytes=VMEM_LIMIT),
        name="merge",
    )(x2d, o_a, o_b, proj, proj, proj, proj, proj, proj, woa, wob, wout, post_g.reshape(1, d))


def kernel(x_prompt, x_sample, cache_k_a, cache_v_a, cache_k_b, cache_v_b, t5_bias, pre_norm, post_norm,
           w_in, lambda_q1, lambda_k1, lambda_q2, lambda_k2, subln_a, rel_bias_b, w_o_a, w_o_b, w_out):
    depth = w_in.shape[0]
    bp, sp, d = x_prompt.shape
    bs, ss, _ = x_sample.shape
    assert bp == 1 and w_in.shape[2] == 12 * COL_BLOCK
    yp = x_prompt.reshape(sp, d)
    ys = x_sample.reshape(bs * ss, d)
    tail = min(BAND_PAST, sp)
    outs = [[] for _ in range(8)]
    for l in range(depth):
        lam_init = 0.8 - 0.6 * math.exp(-0.3 * l)
        lam_args = (lambda_q1[l], lambda_k1[l], lambda_q2[l], lambda_k2[l], subln_a[l], lam_init)

        ps, ka_s, va_s, w = _in_proj_cast(ys, pre_norm[l], w_in[l])
        yp, ps = lax.optimization_barrier((yp, ps))
        pp, ka, va = _in_proj(yp, pre_norm[l], w, tm=1024)
        ob = _attn_b_prompt(pp, rel_bias_b[l])
        ck_a, ob = lax.optimization_barrier((cache_k_a[l], ob))
        oa, (woa, wob, wout) = _attn_a_prompt(pp, ka, va, (w_o_a[l], w_o_b[l], w_out[l]), t5_bias, *lam_args)
        yp = _merge(yp, oa, ob, pp, woa, wob, wout, post_norm[l], tm=256)
        outs[0].append(ka.reshape(bp, sp, 2 * H_A, DK_A))
        outs[1].append(va.reshape(bp, sp, H_A, DV_A))
        outs[2].append(pp[KB, sp - tail:].reshape(bp, tail, H_B, DH_B))
        outs[3].append(pp[VB, sp - tail:].reshape(bp, tail, H_B, DH_B))

        oas, obs, kroll, vroll = _attn_sample(ps, ka_s, va_s, ck_a, cache_v_a[l], cache_k_b[l], cache_v_b[l],
                                              t5_bias, rel_bias_b[l], *lam_args)
        ys = _merge(ys, oas, obs, ps, woa, wob, wout, post_norm[l], tm=bs * ss)
        outs[4].append(ka_s.reshape(bs, ss, 2 * H_A, DK_A))
        outs[5].append(va_s.reshape(bs, ss, H_A, DV_A))
        outs[6].append(kroll.reshape(bs, -1, H_B, DH_B))
        outs[7].append(vroll.reshape(bs, -1, H_B, DH_B))
    return (yp.reshape(bp, sp, d), ys.reshape(bs, ss, d)) + tuple(jnp.stack(o) for o in outs)
```

```python
import functools
import math

import jax
import jax.numpy as jnp
from jax import lax
from jax.experimental import pallas as pl
from jax.experimental.pallas import tpu as pltpu

F32 = jnp.float32
BF16 = jnp.bfloat16

CHUNK = 64
H_A = 8
DK_A = 64
DV_A = 2 * DK_A
H_B = 8
DH_B = 128
BAND_CHUNKS = 8
BAND_PAST = BAND_CHUNKS * CHUNK
REL_CLIP = 128
T5_BUCKETS = 32
T5_MAX_DIST = 128
EPS = 1e-6

HEAD_W = 128
COL_BLOCK = 1024
NORM_ROWS = 256
A_TQ, A_TK = 512, 256
B_TQ = 256
SAMPLE_HEAD_GROUP = 4
N_MERGE_WEIGHTS = 3
LOG2E = math.log2(math.e)
NEAR = max(REL_CLIP, T5_MAX_DIST)
VMEM_LIMIT = 60 * 1024 * 1024

QA, ZA, QB, KB, VB, ZB, GA0, GA1, GB0, GB1 = range(10)
COL_KA, COL_VA = 1, 2
N_OWN_COLS = 2
COL_SLOT = (QA, None, None, ZA, QB, KB, VB, ZB, GA0, GA1, GB0, GB1)


def _t5_bucket_int(rel):
    half = T5_BUCKETS // 2
    max_exact = half // 2
    n = abs(rel)
    ret = half if rel > 0 else 0
    if n < max_exact:
        return ret + n
    assert (T5_MAX_DIST // max_exact) ** 2 == 2 ** (half - max_exact)
    j = 0
    while n * n >= (max_exact * max_exact) * 2 ** (j + 1):
        j += 1
    return ret + min(max_exact + j, half - 1)


def _t5_runs(lo, hi):
    runs = []
    for r in range(lo, hi + 1):
        b = _t5_bucket_int(r)
        if not runs or runs[-1][1] != b:
            runs.append((r, b))
    return runs


T5_FAR_BUCKET = _t5_bucket_int(-T5_MAX_DIST)
assert all(_t5_bucket_int(-n) == T5_FAR_BUCKET for n in range(T5_MAX_DIST, 4 * T5_MAX_DIST))


def _t5_bias_tile(rel, lo, hi, t5_ref, h):
    runs = _t5_runs(lo, hi)
    val = jnp.full(rel.shape, t5_ref[runs[0][1], h], F32)
    for start, b in runs[1:]:
        val = jnp.where(rel >= start, t5_ref[b, h], val)
    return val - t5_ref[T5_FAR_BUCKET, h]


def _rel_bias_tile(rel, lo, hi, rb_ref, h):
    lo = max(lo, -REL_CLIP)
    hi = min(hi, REL_CLIP)
    val = jnp.full(rel.shape, rb_ref[h, lo + REL_CLIP], F32)
    for d in range(lo + 1, hi + 1):
        val = jnp.where(rel >= d, rb_ref[h, d + REL_CLIP], val)
    return val


def _sigmoid(x):
    return 1.0 / (1.0 + jnp.exp(-x))


def _lambda(lq1, lk1, lq2, lk2, lam_init):
    a = jnp.sum(lq1[...] * lk1[...], axis=-1, keepdims=True)
    b = jnp.sum(lq2[...] * lk2[...], axis=-1, keepdims=True)
    return jnp.exp(a) - jnp.exp(b) + lam_init


def _in_proj_kernel(x_ref, g_ref, w_ref, o_ref, ka_ref, va_ref, h_ref):
    j = pl.program_id(1)

    def normalize_and_project():
        for r in range(0, x_ref.shape[0], NORM_ROWS):
            x = x_ref[r:r + NORM_ROWS, :]
            ms = jnp.mean(x * x, axis=-1, keepdims=True)
            h = (x * lax.rsqrt(ms + EPS) * g_ref[...]).astype(BF16)
            h_ref[r:r + NORM_ROWS, :] = h
            o_ref[r:r + NORM_ROWS, :] = jnp.dot(h, w_ref[...], preferred_element_type=F32)

    def project(ref):
        ref[...] = jnp.dot(h_ref[...], w_ref[...], preferred_element_type=F32)

    assert COL_SLOT[0] is not None
    pl.when(j == 0)(normalize_and_project)
    pl.when(j == COL_KA)(functools.partial(project, ka_ref))
    pl.when(j == COL_VA)(functools.partial(project, va_ref))
    pl.when(j > COL_VA)(functools.partial(project, o_ref))


def _in_proj(x2d, pre_g, w_bf16, tm):
    m, d = x2d.shape
    n = w_bf16.shape[1]
    assert m % tm == 0 and tm % NORM_ROWS == 0 and n == len(COL_SLOT) * COL_BLOCK

    def own(col):
        return pl.BlockSpec((tm, COL_BLOCK), lambda i, j: (jnp.where(j >= col, i, jnp.maximum(i - 1, 0)), 0))

    def slab_index(i, j):
        slot = jnp.where(j <= COL_VA, 0, j - N_OWN_COLS)
        return slot, i, 0

    assert COL_SLOT[0] == 0 and COL_SLOT[COL_VA + 1:] == tuple(range(1, len(COL_SLOT) - N_OWN_COLS))
    return pl.pallas_call(
        _in_proj_kernel,
        grid=(m // tm, n // COL_BLOCK),
        in_specs=[
            pl.BlockSpec((tm, d), lambda i, j: (i, 0)),
            pl.BlockSpec((1, d), lambda i, j: (0, 0)),
            pl.BlockSpec((d, COL_BLOCK), lambda i, j: (0, j)),
        ],
        out_specs=[pl.BlockSpec((None, tm, COL_BLOCK), slab_index), own(COL_KA), own(COL_VA)],
        out_shape=[jax.ShapeDtypeStruct((len(COL_SLOT) - N_OWN_COLS, m, COL_BLOCK), F32),
                   jax.ShapeDtypeStruct((m, COL_BLOCK), F32),
                   jax.ShapeDtypeStruct((m, COL_BLOCK), F32)],
        scratch_shapes=[pltpu.VMEM((tm, d), BF16)],
        compiler_params=pltpu.CompilerParams(
            dimension_semantics=("arbitrary", "arbitrary"), vmem_limit_bytes=VMEM_LIMIT),
        name="in_proj",
    )(x2d, pre_g.reshape(1, d), w_bf16)


def _in_proj_cast_kernel(x_ref, g_ref, w_ref, o_ref, ka_ref, va_ref, wbf_ref, h_ref):
    j = pl.program_id(0)

    @pl.when(j == 0)
    def _():
        x = x_ref[...]
        ms = jnp.mean(x * x, axis=-1, keepdims=True)
        h_ref[...] = (x * lax.rsqrt(ms + EPS) * g_ref[...]).astype(BF16)

    def project(ref):
        w = w_ref[...].astype(BF16)
        wbf_ref[...] = w
        ref[...] = jnp.dot(h_ref[...], w, preferred_element_type=F32)

    pl.when(j == COL_KA)(functools.partial(project, ka_ref))
    pl.when(j == COL_VA)(functools.partial(project, va_ref))
    pl.when(jnp.logical_and(j != COL_KA, j != COL_VA))(functools.partial(project, o_ref))


def _in_proj_cast(x2d, pre_g, w_f32):
    m, d = x2d.shape
    n = w_f32.shape[1]
    assert n == len(COL_SLOT) * COL_BLOCK
    slots = len(COL_SLOT) - N_OWN_COLS
    resident = lambda: pl.BlockSpec((m, COL_BLOCK), lambda j: (0, 0))
    return pl.pallas_call(
        _in_proj_cast_kernel,
        grid=(len(COL_SLOT),),
        in_specs=[
            pl.BlockSpec((m, d), lambda j: (0, 0)),
            pl.BlockSpec((1, d), lambda j: (0, 0)),
            pl.BlockSpec((d, COL_BLOCK), lambda j: (0, j)),
        ],
        out_specs=[pl.BlockSpec((None, m, COL_BLOCK), lambda j: (jnp.where(j <= COL_VA, 0, j - N_OWN_COLS), 0, 0)),
                   resident(), resident(),
                   pl.BlockSpec((d, COL_BLOCK), lambda j: (0, j))],
        out_shape=[jax.ShapeDtypeStruct((slots, m, COL_BLOCK), F32),
                   jax.ShapeDtypeStruct((m, COL_BLOCK), F32),
                   jax.ShapeDtypeStruct((m, COL_BLOCK), F32),
                   jax.ShapeDtypeStruct((d, n), BF16)],
        scratch_shapes=[pltpu.VMEM((m, d), BF16)],
        compiler_params=pltpu.CompilerParams(
            dimension_semantics=("arbitrary",), vmem_limit_bytes=VMEM_LIMIT),
        name="in_proj_cast",
    )(x2d, pre_g.reshape(1, d), w_f32)


def _online_softmax_step(s, vt, m_ref, l_ref, acc_ref):
    m_old = m_ref[...]
    m_new = jnp.maximum(m_old, jnp.max(s, axis=0, keepdims=True))
    alpha = jnp.exp2(m_old - m_new)
    p = jnp.exp2(s - m_new)
    l_ref[...] = alpha * l_ref[...] + jnp.sum(p, axis=0, keepdims=True)
    acc_ref[...] = alpha * acc_ref[...] + jnp.dot(vt, p.astype(BF16), preferred_element_type=F32)
    m_ref[...] = m_new


def _stage_keys_values(k_ref, v_ref, kbf_ref, vt_ref, t):
    def body(j, c):
        r = pl.multiple_of(j * t, t)
        kbf_ref[pl.ds(r, t), :] = k_ref[pl.ds(r, t), :].astype(BF16)
        vt_ref[j] = v_ref[pl.ds(r, t), :].astype(BF16).T
        return c
    lax.fori_loop(0, vt_ref.shape[0], body, 0, unroll=4)


def _attn_a_kernel(t5_ref, lq1, lk1, lq2, lk2, g_ref, q_ref, k_ref, v_ref, *rest, lam_init):
    n_w = N_MERGE_WEIGHTS
    w_refs, o_ref, wbf_refs = rest[:n_w], rest[n_w], rest[n_w + 1:2 * n_w + 1]
    kbf_ref, vt_ref, bias_ref, qt_ref, m_ref, l_ref, acc_ref = rest[2 * n_w + 1:2 * n_w + 8]
    s_refs = rest[2 * n_w + 8:]
    tq, tk = A_TQ, A_TK
    h = pl.program_id(0)
    nq = q_ref.shape[0] // tq

    for w_ref, wbf_ref in zip(w_refs, wbf_refs):
        wbf_ref[...] = w_ref[...].astype(BF16)

    _stage_keys_values(k_ref, v_ref, kbf_ref, vt_ref, tk)
    key = lax.broadcasted_iota(jnp.int32, (tk, tq), 0)
    qry = lax.broadcasted_iota(jnp.int32, (tk, tq), 1)
    for n in range(3):
        rel = key + (n - 1) * tk - qry
        lo, hi = (n - 1) * tk - (tq - 1), min(n * tk - 1, CHUNK - 1)
        b = _t5_bias_tile(jnp.minimum(rel, hi), lo, hi, t5_ref, h) * LOG2E
        if n >= 1:
            b = jnp.where((key + (n - 1) * tk) // CHUNK <= qry // CHUNK, b, -jnp.inf)
        bias_ref[n] = b

    lam = _lambda(lq1, lk1, lq2, lk2, lam_init)

    hq = tq // 2

    def start_tile(i):
        q0 = pl.multiple_of(i * tq, tq)
        qt = (q_ref[pl.ds(q0, tq), :] * (DK_A ** -0.5 * LOG2E)).T
        sub = lax.broadcasted_iota(jnp.int32, (HEAD_W, tq), 0)
        maps = (jnp.where(sub < DK_A, qt, 0.0).astype(BF16), jnp.where(sub >= DK_A, qt, 0.0).astype(BF16))
        for c in range(4):
            qt_ref[:, c * hq:(c + 1) * hq] = maps[c % 2][:, (c // 2) * hq:(c // 2 + 1) * hq]

    def reset_state():
        m_ref[...] = jnp.full(m_ref.shape, -jnp.inf, F32)
        l_ref[...] = jnp.zeros(l_ref.shape, F32)
        acc_ref[...] = jnp.zeros(acc_ref.shape, F32)

    def scores(j, s_ref, late_only=False):
        r = pl.multiple_of(j * tk, tk)
        c0 = tq if late_only else 0
        s_ref[:, c0:] = jnp.dot(kbf_ref[pl.ds(r, tk), :], qt_ref[:, c0:], preferred_element_type=F32)

    def update(j, s_ref, bias_idx, late_only=False):
        c0 = tq if late_only else 0
        s = s_ref[:, c0:]
        if bias_idx is not None:
            b = bias_ref[bias_idx]
            halves = [b[:, (c // 2) * hq:(c // 2 + 1) * hq] for c in range(c0 // hq, 4)]
            s = jnp.concatenate([s[:, n * hq:(n + 1) * hq] + bh for n, bh in enumerate(halves)], axis=1)
        _online_softmax_step(s, vt_ref[j], m_ref.at[:, c0:], l_ref.at[:, c0:], acc_ref.at[:, c0:])

    def q_tile(i, carry):
        odd = jnp.logical_and(i >= 2, i % 2 == 0)

        @pl.when(odd)
        def _():
            update(0, s_refs[0], None)
            scores(2, s_refs[0])
            update(1, s_refs[1], None)
            scores(3, s_refs[1])

        j0 = jnp.where(odd, 2, 0)

        def quad(j):
            for n in range(4):
                scores(j + n + 2, s_refs[(n + 2) % 4])
                update(j + n, s_refs[n], None)

        def two_quads(u, c):
            quad(j0 + 8 * u)
            quad(j0 + 8 * u + 4)
            return c
        quads = jnp.maximum(i - 1, 0) // 2
        lax.fori_loop(0, quads // 2, two_quads, 0)

        @pl.when(quads % 2 == 1)
        def _():
            quad(j0 + 4 * (quads - 1))

        nxt = jnp.minimum(i + 1, nq - 1)

        @pl.when(i >= 1)
        def _():
            j = 2 * i - 2
            scores(j + 2, s_refs[2])
            update(j, s_refs[0], None)
            scores(j + 3, s_refs[3], late_only=True)
            update(j + 1, s_refs[1], 0)
            start_tile(nxt)
            scores(0, s_refs[0])
            update(j + 2, s_refs[2], 1)
            scores(1, s_refs[1])
            update(j + 3, s_refs[3], 2, late_only=True)

        @pl.when(i == 0)
        def _():
            update(0, s_refs[0], 1)
            start_tile(nxt)
            scores(0, s_refs[0])
            update(1, s_refs[1], 2, late_only=True)
            scores(1, s_refs[1])

        inv = 1.0 / l_ref[...]
        acc = acc_ref[...]
        o = jnp.concatenate(
            [acc[:, c:c + hq] * inv[:, c:c + hq] - lam * (acc[:, c + hq:c + tq] * inv[:, c + hq:c + tq])
             for c in (0, tq)], axis=1)
        ms = jnp.mean(o * o, axis=0, keepdims=True)
        y = (o * lax.rsqrt(ms + EPS) * g_ref[...]) * (1.0 - lam_init)
        o_ref[pl.ds(pl.multiple_of(i * tq, tq), tq), :] = y.T
        reset_state()
        return carry

    start_tile(0)
    scores(0, s_refs[0])
    scores(1, s_refs[1])
    reset_state()
    lax.fori_loop(0, nq, q_tile, 0)


def _attn_a_prompt(proj, k_a, v_a, merge_weights, t5_bias, lq1, lk1, lq2, lk2, subln_g, lam_init):
    s = proj.shape[1]
    tq, tk = A_TQ, A_TK
    assert s % tq == 0 and tq == 2 * tk and tk % CHUNK == 0 and tk >= T5_MAX_DIST
    vec = lambda: pl.BlockSpec((1, DK_A), lambda h: (0, 0))
    head = lambda: pl.BlockSpec((s, HEAD_W), lambda h: (0, h))
    assert len(merge_weights) == N_MERGE_WEIGHTS and all(w.shape[0] % (8 * H_A) == 0 for w in merge_weights)
    rows = lambda w: pl.BlockSpec((w.shape[0] // H_A, w.shape[1]), lambda h: (h, 0))
    outs = pl.pallas_call(
        functools.partial(_attn_a_kernel, lam_init=lam_init),
        grid=(H_A,),
        in_specs=[
            pl.BlockSpec(memory_space=pltpu.SMEM),
            vec(), vec(), vec(), vec(),
            pl.BlockSpec((DV_A, 1), lambda h: (0, 0)),
            pl.BlockSpec((None, s, HEAD_W), lambda h: (QA, 0, h)), head(), head(),
        ] + [rows(w) for w in merge_weights],
        out_specs=[pl.BlockSpec((s, HEAD_W), lambda h: (0, h))] + [rows(w) for w in merge_weights],
        out_shape=[jax.ShapeDtypeStruct((s, H_A * DV_A), F32)]
        + [jax.ShapeDtypeStruct(w.shape, BF16) for w in merge_weights],
        scratch_shapes=[
            pltpu.VMEM((s, HEAD_W), BF16),
            pltpu.VMEM((s // tk, DV_A, tk), BF16),
            pltpu.VMEM((3, tk, tq), F32),
            pltpu.VMEM((HEAD_W, 2 * tq), BF16),
            pltpu.VMEM((1, 2 * tq), F32),
            pltpu.VMEM((1, 2 * tq), F32),
            pltpu.VMEM((DV_A, 2 * tq), F32),
        ] + [pltpu.VMEM((tk, 2 * tq), F32)] * 4,
        compiler_params=pltpu.CompilerParams(
            dimension_semantics=("arbitrary",), vmem_limit_bytes=VMEM_LIMIT),
        name="attn_a_prompt",
    )(t5_bias, lq1.reshape(1, DK_A), lk1.reshape(1, DK_A), lq2.reshape(1, DK_A), lk2.reshape(1, DK_A),
      subln_g.reshape(DV_A, 1), proj, k_a, v_a, *merge_weights)
    return outs[0], outs[1:]


def _attn_b_kernel(rb_ref, q_ref, k_ref, v_ref, o_ref, kbf_ref, vt_ref, bias_ref, *s_refs):
    t = B_TQ
    blk = REL_CLIP
    nkt = BAND_PAST // t + 1
    nq = q_ref.shape[0] // t
    h = pl.program_id(0)

    _stage_keys_values(k_ref, v_ref, kbf_ref, vt_ref, t)

    kk = lax.broadcasted_iota(jnp.int32, (blk, blk), 0)
    qq = lax.broadcasted_iota(jnp.int32, (blk, blk), 1)
    rel = kk - qq
    lo = jnp.full((blk, blk), rb_ref[h, 0] * LOG2E, F32)
    same = _rel_bias_tile(rel, -(blk - 1), blk - 1, rb_ref, h) * LOG2E
    prev = _rel_bias_tile(rel - blk, -(2 * blk - 1), -1, rb_ref, h) * LOG2E
    ninf = jnp.full((blk, blk), -jnp.inf, F32)
    kc = kk // CHUNK
    qc = qq // CHUNK
    far_blocks = BAND_PAST // blk
    for a in range(nkt * t // blk):
        for b in range(t // blk):
            e = a - b
            if e < 0 or e > far_blocks:
                tile = ninf
            elif e == 0:
                tile = jnp.where(kc >= qc, lo, -jnp.inf)
            elif e == far_blocks:
                tile = jnp.where(kc <= qc, same, -jnp.inf)
            elif e == far_blocks - 1:
                tile = prev
            else:
                tile = lo
            bias_ref[a * blk:(a + 1) * blk, b * blk:(b + 1) * blk] = tile

    def scores(g, nk, s_ref):
        q0 = pl.multiple_of(g * t, t)
        k0 = pl.multiple_of((g - (nk - 1)) * t, t)
        qt = (q_ref[pl.ds(q0, t), :] * (DH_B ** -0.5 * LOG2E)).T.astype(BF16)
        s_ref[(nkt - nk) * t:, :] = jnp.dot(kbf_ref[pl.ds(k0, nk * t), :], qt, preferred_element_type=F32)

    def finish(g, nk, s_ref):
        s = s_ref[(nkt - nk) * t:, :] + bias_ref[(nkt - nk) * t:, :]
        m = jnp.max(s, axis=0, keepdims=True)
        p = jnp.exp2(s - m)
        l = jnp.sum(p, axis=0, keepdims=True)
        pb = p.astype(BF16)
        o = jnp.dot(vt_ref[g - (nk - 1)], pb[:t], preferred_element_type=F32)
        for c in range(1, nk):
            o = o + jnp.dot(vt_ref[g - (nk - 1) + c], pb[c * t:(c + 1) * t], preferred_element_type=F32)
        o_ref[pl.ds(pl.multiple_of(g * t, t), t), :] = (o * (1.0 / l)).T

    assert nq % 4 == 0 and nq >= 8 and nkt <= 4

    def quad(g, static_start=False, lookahead=True):
        for n in range(4):
            if lookahead or n < 2:
                scores(g + n + 2, nkt, s_refs[(n + 2) % 4])
            finish(g + n, min(n + 1, nkt) if static_start else nkt, s_refs[n])

    def two_quads(u, c):
        quad(4 + 8 * u)
        quad(8 + 8 * u)
        return c

    scores(0, 1, s_refs[0])
    scores(1, min(2, nkt), s_refs[1])
    quad(0, static_start=True)
    quads = nq // 4 - 2
    lax.fori_loop(0, quads // 2, two_quads, 0)
    if quads % 2:
        quad(4 * quads)
    quad(nq - 4, lookahead=False)


def _attn_b_prompt(proj, rel_bias):
    s = proj.shape[1]
    t = B_TQ
    nkt = BAND_PAST // t + 1
    assert s % t == 0 and BAND_PAST % t == 0 and t % REL_CLIP == 0 and REL_CLIP % CHUNK == 0
    head = lambda c: pl.BlockSpec((None, s, HEAD_W), lambda h: (c, 0, h))
    return pl.pallas_call(
        _attn_b_kernel,
        grid=(H_B,),
        in_specs=[pl.BlockSpec(memory_space=pltpu.SMEM), head(QB), head(KB), head(VB)],
        out_specs=pl.BlockSpec((s, HEAD_W), lambda h: (0, h)),
        out_shape=jax.ShapeDtypeStruct((s, H_B * DH_B), F32),
        scratch_shapes=[
            pltpu.VMEM((s, HEAD_W), BF16),
            pltpu.VMEM((s // t, DH_B, t), BF16),
            pltpu.VMEM((nkt * t, t), F32),
        ] + [pltpu.VMEM((nkt * t, t), F32)] * 4,
        compiler_params=pltpu.CompilerParams(
            dimension_semantics=("arbitrary",), vmem_limit_bytes=VMEM_LIMIT),
        name="attn_b_prompt",
    )(rel_bias, proj, proj, proj)


def _dot_nt(a, b):
    return lax.dot_general(a, b, (((1,), (1,)), ((), ())), preferred_element_type=F32)


def _attn_sample_kernel(t5_ref, rb_ref, lq1, lk1, lq2, lk2, g_ref,
                        qa_ref, ka_ref, va_ref, cka_ref, cva_ref,
                        qb_ref, kb_ref, vb_ref, ckb_ref, cvb_ref,
                        oa_ref, ob_ref, kroll_ref, vroll_ref,
                        ba_ref, bb_ref, *, lam_init, past, win):
    n = qa_ref.shape[0]
    near = NEAR
    nh = H_A

    @pl.when(pl.program_id(0) == 0)
    def _():
        qry = lax.broadcasted_iota(jnp.int32, (n, near + n), 0)
        key = lax.broadcasted_iota(jnp.int32, (n, near + n), 1)
        rel = key - near - qry
        for h in range(nh):
            ba_ref[h] = _t5_bias_tile(rel, -(near + n - 1), n - 1, t5_ref, h)
            bb_ref[h] = _rel_bias_tile(rel, -(near + n - 1), n - 1, rb_ref, h)

    kroll_ref[:(win - n) * nh, :] = ckb_ref[n * nh:, :]
    vroll_ref[:(win - n) * nh, :] = cvb_ref[n * nh:, :]

    lam = _lambda(lq1, lk1, lq2, lk2, lam_init)
    lane = lax.broadcasted_iota(jnp.int32, (n, HEAD_W), 1)

    def cols(h):
        return slice(h * HEAD_W, (h + 1) * HEAD_W)

    def scores_a(h):
        q = qa_ref[:, cols(h)] * (DK_A ** -0.5)
        q2 = jnp.concatenate([jnp.where(lane < DK_A, q, 0.0), jnp.where(lane >= DK_A, q, 0.0)],
                             axis=0).astype(BF16)
        ba = ba_ref[h]
        ba2 = jnp.concatenate([ba, ba], axis=0)
        s_c = _dot_nt(q2, cka_ref[:, cols(h)].astype(BF16))
        s_c = jnp.concatenate([s_c[:, :past - near], s_c[:, past - near:] + ba2[:, :near]], axis=1)
        s_n = _dot_nt(q2, ka_ref[:, cols(h)].astype(BF16)) + ba2[:, near:]
        return s_c, s_n

    def scores_b(h):
        qb = qb_ref[:, cols(h)].astype(BF16)
        bb = bb_ref[h]
        ckb = ckb_ref[pl.ds(h, win, stride=nh), :].astype(BF16)
        s_c = _dot_nt(qb, ckb) * (DH_B ** -0.5)
        s_c = jnp.concatenate([s_c[:, :win - near] + rb_ref[h, 0], s_c[:, win - near:] + bb[:, :near]], axis=1)
        s_n = _dot_nt(qb, kb_ref[:, cols(h)].astype(BF16)) * (DH_B ** -0.5) + bb[:, near:]
        return s_c, s_n

    def softmax(s_c, s_n):
        m = jnp.maximum(jnp.max(s_c, axis=-1, keepdims=True), jnp.max(s_n, axis=-1, keepdims=True))
        p_c = jnp.exp(s_c - m)
        p_n = jnp.exp(s_n - m)
        l = jnp.sum(p_c, axis=-1, keepdims=True) + jnp.sum(p_n, axis=-1, keepdims=True)
        return p_c.astype(BF16), p_n.astype(BF16), 1.0 / l

    def finish_a(h, p_c, p_n, inv):
        cva = cva_ref[pl.ds(h, past, stride=nh), :].astype(BF16)
        o2 = (jnp.dot(p_c, cva, preferred_element_type=F32)
              + jnp.dot(p_n, va_ref[:, cols(h)].astype(BF16), preferred_element_type=F32)) * inv
        o = o2[:n] - lam * o2[n:]
        ms = jnp.mean(o * o, axis=-1, keepdims=True)
        oa_ref[:, cols(h)] = (o * lax.rsqrt(ms + EPS) * g_ref[...]) * (1.0 - lam_init)

    def finish_b(h, p_c, p_n, inv):
        cvb = cvb_ref[pl.ds(h, win, stride=nh), :].astype(BF16)
        ob_ref[:, cols(h)] = (jnp.dot(p_c, cvb, preferred_element_type=F32)
                              + jnp.dot(p_n, vb_ref[:, cols(h)].astype(BF16), preferred_element_type=F32)) * inv

    for first in range(0, nh, SAMPLE_HEAD_GROUP):
        heads = range(first, first + SAMPLE_HEAD_GROUP)
        for h in heads:
            new_rows = pl.ds((win - n) * nh + h, n, stride=nh)
            kroll_ref[new_rows, :] = kb_ref[:, cols(h)]
            vroll_ref[new_rows, :] = vb_ref[:, cols(h)]
        s_a = [scores_a(h) for h in heads]
        s_b = [scores_b(h) for h in heads]
        p_a = [softmax(*s) for s in s_a]
        p_b = [softmax(*s) for s in s_b]
        for h, p in zip(heads, p_a):
            finish_a(h, *p)
        for h, p in zip(heads, p_b):
            finish_b(h, *p)


def _attn_sample(proj, k_a, v_a, ck_a, cv_a, ck_b, cv_b, t5_bias, rel_bias, lq1, lk1, lq2, lk2, subln_g,
                 lam_init):
    nb, past = ck_a.shape[0], ck_a.shape[1]
    win = ck_b.shape[1]
    n = proj.shape[1] // nb
    near = NEAR
    wide = H_A * HEAD_W
    assert past % CHUNK == 0 and n <= CHUNK and win <= BAND_PAST and win <= past
    assert near % HEAD_W == 0 and near <= win and near <= past and n % 8 == 0 and T5_MAX_DIST <= REL_CLIP
    assert H_A == H_B and DV_A == HEAD_W and DH_B == HEAD_W and 2 * DK_A == HEAD_W
    cka = ck_a.reshape(nb, past, wide)
    cva = cv_a.reshape(nb, past * H_A, DV_A)
    ckb = ck_b.reshape(nb, win * H_B, DH_B)
    cvb = cv_b.reshape(nb, win * H_B, DH_B)
    vec = lambda: pl.BlockSpec((1, DK_A), lambda b: (0, 0))
    new = lambda c: pl.BlockSpec((None, n, wide), lambda b: (c, b, 0))
    seq = lambda rows, cols: pl.BlockSpec((None, rows, cols), lambda b: (b, 0, 0))
    out = pl.BlockSpec((n, wide), lambda b: (b, 0))
    return pl.pallas_call(
        functools.partial(_attn_sample_kernel, lam_init=lam_init, past=past, win=win),
        grid=(nb,),
        in_specs=[
            pl.BlockSpec(memory_space=pltpu.SMEM),
            pl.BlockSpec(memory_space=pltpu.SMEM),
            vec(), vec(), vec(), vec(),
            pl.BlockSpec((1, DV_A), lambda b: (0, 0)),
            new(QA), out, out, seq(past, wide), seq(past * H_A, HEAD_W),
            new(QB), new(KB), new(VB), seq(win * H_B, HEAD_W), seq(win * H_B, HEAD_W),
        ],
        out_specs=[out, out, seq(win * H_B, HEAD_W), seq(win * H_B, HEAD_W)],
        out_shape=[
            jax.ShapeDtypeStruct((nb * n, wide), F32),
            jax.ShapeDtypeStruct((nb * n, wide), F32),
            jax.ShapeDtypeStruct((nb, win * H_B, DH_B), F32),
            jax.ShapeDtypeStruct((nb, win * H_B, DH_B), F32),
        ],
        scratch_shapes=[pltpu.VMEM((H_A, n, near + n), F32), pltpu.VMEM((H_B, n, near + n), F32)],
        compiler_params=pltpu.CompilerParams(
            dimension_semantics=("arbitrary",), vmem_limit_bytes=VMEM_LIMIT),
        name="attn_sample",
    )(t5_bias, rel_bias, lq1.reshape(1, DK_A), lk1.reshape(1, DK_A), lq2.reshape(1, DK_A),
      lk2.reshape(1, DK_A), subln_g.reshape(1, DV_A),
      proj, k_a, v_a, cka, cva, proj, proj, proj, ckb, cvb)


def _merge_kernel(x_ref, oa_ref, ob_ref, za_ref, zb_ref, ga0_ref, ga1_ref, gb0_ref, gb1_ref,
                  woa_ref, wob_ref, wout_ref, pg_ref, y_ref):
    za = za_ref[...]
    zb = zb_ref[...]
    a = (oa_ref[...] * (za * _sigmoid(za))).astype(BF16)
    b = (ob_ref[...] * (zb * _sigmoid(zb))).astype(BF16)
    ya = jnp.dot(a, woa_ref[...], preferred_element_type=F32)
    yb = jnp.dot(b, wob_ref[...], preferred_element_type=F32)
    ga = jnp.concatenate([ga0_ref[...], ga1_ref[...]], axis=1)
    gb = jnp.concatenate([gb0_ref[...], gb1_ref[...]], axis=1)
    mix = (_sigmoid(ga) * ya + _sigmoid(gb) * yb).astype(BF16)
    y = jnp.dot(mix, wout_ref[...], preferred_element_type=F32)
    ms = jnp.mean(y * y, axis=-1, keepdims=True)
    y_ref[...] = x_ref[...] + y * lax.rsqrt(ms + EPS) * pg_ref[...]


def _merge(x2d, o_a, o_b, proj, woa, wob, wout, post_g, tm):
    m, d = x2d.shape
    wa = o_a.shape[1]
    wb = o_b.shape[1]
    assert m % tm == 0 and wa == COL_BLOCK and wb == COL_BLOCK and d == 2 * COL_BLOCK
    row = lambda w: pl.BlockSpec((tm, w), lambda i: (i, 0))
    col = lambda c: pl.BlockSpec((None, tm, COL_BLOCK), lambda i: (c, i, 0))
    resident = lambda r, c: pl.BlockSpec((r, c), lambda i: (0, 0), pipeline_mode=pl.Buffered(1))
    return pl.pallas_call(
        _merge_kernel,
        grid=(m // tm,),
        in_specs=[row(d), row(wa), row(wb), col(ZA), col(ZB), col(GA0), col(GA1), col(GB0), col(GB1),
                  resident(wa, d), resident(wb, d), resident(d, d), resident(1, d)],
        out_specs=row(d),
        out_shape=jax.ShapeDtypeStruct((m, d), F32),
        compiler_params=pltpu.CompilerParams(
            dimension_semantics=("arbitrary",), vmem_limit_bytes=VMEM_LIMIT),
        name="merge",
    )(x2d, o_a, o_b, proj, proj, proj, proj, proj, proj, woa, wob, wout, post_g.reshape(1, d))


def kernel(x_prompt, x_sample, cache_k_a, cache_v_a, cache_k_b, cache_v_b, t5_bias, pre_norm, post_norm,
           w_in, lambda_q1, lambda_k1, lambda_q2, lambda_k2, subln_a, rel_bias_b, w_o_a, w_o_b, w_out):
    depth = w_in.shape[0]
    bp, sp, d = x_prompt.shape
    bs, ss, _ = x_sample.shape
    assert bp == 1 and w_in.shape[2] == 12 * COL_BLOCK
    yp = x_prompt.reshape(sp, d)
    ys = x_sample.reshape(bs * ss, d)
    tail = min(BAND_PAST, sp)
    outs = [[] for _ in range(8)]
    for l in range(depth):
        lam_init = 0.8 - 0.6 * math.exp(-0.3 * l)
        lam_args = (lambda_q1[l], lambda_k1[l], lambda_q2[l], lambda_k2[l], subln_a[l], lam_init)

        ps, ka_s, va_s, w = _in_proj_cast(ys, pre_norm[l], w_in[l])
        yp, ps = lax.optimization_barrier((yp, ps))
        pp, ka, va = _in_proj(yp, pre_norm[l], w, tm=1024)
        ob = _attn_b_prompt(pp, rel_bias_b[l])
        ck_a, ob = lax.optimization_barrier((cache_k_a[l], ob))
        oa, (woa, wob, wout) = _attn_a_prompt(pp, ka, va, (w_o_a[l], w_o_b[l], w_out[l]), t5_bias, *lam_args)
        yp = _merge(yp, oa, ob, pp, woa, wob, wout, post_norm[l], tm=256)
        outs[0].append(ka.reshape(bp, sp, 2 * H_A, DK_A))
        outs[1].append(va.reshape(bp, sp, H_A, DV_A))
        outs[2].append(pp[KB, sp - tail:].reshape(bp, tail, H_B, DH_B))
        outs[3].append(pp[VB, sp - tail:].reshape(bp, tail, H_B, DH_B))

        oas, obs, kroll, vroll = _attn_sample(ps, ka_s, va_s, ck_a, cache_v_a[l], cache_k_b[l], cache_v_b[l],
                                              t5_bias, rel_bias_b[l], *lam_args)
        ys = _merge(ys, oas, obs, ps, woa, wob, wout, post_norm[l], tm=bs * ss)
        outs[4].append(ka_s.reshape(bs, ss, 2 * H_A, DK_A))
        outs[5].append(va_s.reshape(bs, ss, H_A, DV_A))
        outs[6].append(kroll.reshape(bs, -1, H_B, DH_B))
        outs[7].append(vroll.reshape(bs, -1, H_B, DH_B))
    return (yp.reshape(bp, sp, d), ys.reshape(bs, ss, d)) + tuple(jnp.stack(o) for o in outs)
```
